```python
import math
import jax
import jax.numpy as jnp
from jax import lax
import numpy as np

D_MODEL = 1024
BATCH = 2
SEQ = 8192
DEPTH = 2

N_MIXERS = 4
GROUP_W = D_MODEL // N_MIXERS
HEAD_DIM = 64
N_GROUP_HEADS = GROUP_W // HEAD_DIM
ML_CHUNK = 128
MLA_Q_RANK = 224
MLA_KV_RANK = 128
MLA_NOPE = 64
MLA_ROPE = 32
MLA_V = 64
ROPE_THETA = 10000.0
ATTN_Q_BLOCK = 128
HY_ORDER = 2
HY_BANDS = 8
HY_EMB = 1 + 2 * HY_BANDS
HY_FFN = 64
HY_N_FILT = HY_ORDER * 2 * GROUP_W
CONV_W = 3
N_EXPERTS = 16
EC_CAPACITY = 2
EXPERT_FF = 2 * D_MODEL
NORM_EPS = 1e-6
D_IN = 4 * GROUP_W + 4 * N_GROUP_HEADS + MLA_Q_RANK + MLA_KV_RANK + MLA_ROPE + 6 * GROUP_W
F32 = jnp.float32

kernel_name = 'hybrid_group_mixer_ec_moe_encoder'


def rmsnorm(x, g):
    xf = x.astype(F32)
    y = xf * lax.rsqrt(jnp.mean(xf * xf, axis=-1, keepdims=True) + NORM_EPS)
    return (y * g.astype(F32)).astype(x.dtype)


def head_rmsnorm(y, g):
    b, s, d = y.shape
    yf = y.astype(F32).reshape(b, s, d // HEAD_DIM, HEAD_DIM)
    yf = yf * lax.rsqrt(jnp.mean(yf * yf, axis=-1, keepdims=True) + NORM_EPS)
    return (yf.reshape(b, s, d) * g.astype(F32)).astype(y.dtype)


def dwconv3(x, w, b=None):
    ch = x.shape[-1]
    rhs = jnp.transpose(w)[:, None, :].astype(x.dtype)
    y = lax.conv_general_dilated(x, rhs, window_strides=(1,), padding='SAME',
                                 dimension_numbers=('NWC', 'WIO', 'NWC'),
                                 feature_group_count=ch)
    return y if b is None else y + b.astype(x.dtype)


def rope(x, pos):
    half = x.shape[-1] // 2
    inv = ROPE_THETA ** (-jnp.arange(half, dtype=F32) / half)
    ang = pos.astype(F32)[..., None] * inv
    ang = ang.reshape(ang.shape[:2] + (1,) * (x.ndim - 3) + (half,))
    cos, sin = jnp.cos(ang), jnp.sin(ang)
    xf = x.astype(F32)
    x1, x2 = xf[..., :half], xf[..., half:]
    return jnp.concatenate([x1 * cos - x2 * sin, x2 * cos + x1 * sin], axis=-1).astype(x.dtype)


def mlstm_scan(q, k, v, i_pre, f_pre):
    b, h, s, dh = q.shape
    nc = s // ML_CHUNK
    q, k, v = (t.reshape(b, h, nc, ML_CHUNK, dh) for t in (q, k, v))
    ig = i_pre.reshape(b, h, nc, ML_CHUNK)
    a = jnp.cumsum(jax.nn.log_sigmoid(f_pre).reshape(b, h, nc, ML_CHUNK), axis=-1)
    a_end = a[..., -1]
    w_st = a_end[..., None] - a + ig
    m_loc = jnp.max(w_st, axis=-1)
    e_st = jnp.exp(w_st - m_loc[..., None])
    c_loc = jnp.einsum('bhnl,bhnld,bhnle->bhnde', e_st, k, v)
    n_loc = jnp.einsum('bhnl,bhnld->bhnd', e_st, k)

    def step(carry, inp):
        c_st, n_st, m_st = carry
        cl, nl, ml, ae = inp
        m_new = jnp.maximum(ae + m_st, ml)
        sp = jnp.exp(ae + m_st - m_new)
        sl = jnp.exp(ml - m_new)
        c_new = sp[..., None, None] * c_st + sl[..., None, None] * cl
        n_new = sp[..., None] * n_st + sl[..., None] * nl
        return (c_new, n_new, m_new), (c_st, n_st, m_st)

    init = (jnp.zeros((b, h, dh, dh), q.dtype), jnp.zeros((b, h, dh), q.dtype), jnp.zeros((b, h), q.dtype))
    xs = tuple(jnp.moveaxis(t, 2, 0) for t in (c_loc, n_loc, m_loc, a_end))
    _, (c_prev, n_prev, m_prev) = lax.scan(step, init, xs)
    c_prev = jnp.moveaxis(c_prev, 0, 2)
    n_prev = jnp.moveaxis(n_prev, 0, 2)
    m_prev = jnp.moveaxis(m_prev, 0, 2)
    tri = jnp.tril(jnp.ones((ML_CHUNK, ML_CHUNK), dtype=bool))
    d_log = jnp.where(tri, a[..., :, None] - a[..., None, :] + ig[..., None, :], -jnp.inf)
    inter_log = a + m_prev[..., None]
    m_t = jnp.maximum(inter_log, jnp.max(d_log, axis=-1))
    p_intra = jnp.exp(d_log - m_t[..., None]) * jnp.einsum('bhnld,bhnsd->bhnls', q, k)
    sc_inter = jnp.exp(inter_log - m_t)
    num = (jnp.einsum('bhnls,bhnsd->bhnld', p_intra, v)
           + sc_inter[..., None] * jnp.einsum('bhnld,bhnde->bhnle', q, c_prev))
    den = jnp.sum(p_intra, axis=-1) + sc_inter * jnp.einsum('bhnld,bhnd->bhnl', q, n_prev)
    out = num / jnp.maximum(jnp.abs(den), jnp.exp(-m_t))[..., None]
    return out.reshape(b, h, s, dh)


def mlstm_mixer(q, k, v, o, g, gate_b):
    b, s, _ = q.shape
    hh = N_GROUP_HEADS
    heads = lambda t: jnp.transpose(t.astype(F32).reshape(b, s, hh, HEAD_DIM), (0, 2, 1, 3))
    qh, kh, vh = heads(q), heads(k) * (HEAD_DIM ** -0.5), heads(v)
    gates = jnp.transpose(g.astype(F32).reshape(b, s, 4, hh) + gate_b.astype(F32), (2, 0, 3, 1))
    h_fwd = mlstm_scan(qh, kh, vh, gates[0], gates[1])
    fl = lambda t: jnp.flip(t, axis=2)
    h_bwd = fl(mlstm_scan(fl(qh), fl(kh), fl(vh), jnp.flip(gates[2], -1), jnp.flip(gates[3], -1)))
    hsum = jnp.transpose(h_fwd + h_bwd, (0, 2, 1, 3)).reshape(b, s, GROUP_W)
    return (jax.nn.sigmoid(o.astype(F32)) * hsum).astype(q.dtype)


def mla_mixer(cq, ckv, kr, pos, q_norm, kv_norm, w_uq, w_uk, w_uv):
    b, s, _ = cq.shape
    hh = N_GROUP_HEADS
    q = (rmsnorm(cq, q_norm) @ w_uq).reshape(b, s, hh, MLA_NOPE + MLA_ROPE)
    c_kv = rmsnorm(ckv, kv_norm)
    k_nope = (c_kv @ w_uk).reshape(b, s, hh, MLA_NOPE)
    v = (c_kv @ w_uv).reshape(b, s, hh, MLA_V)
    q = jnp.concatenate([q[..., :MLA_NOPE], rope(q[..., MLA_NOPE:], pos)], axis=-1)
    k_rope = jnp.broadcast_to(rope(kr, pos)[:, :, None, :], (b, s, hh, MLA_ROPE))
    k = jnp.concatenate([k_nope, k_rope], axis=-1)
    scale = (MLA_NOPE + MLA_ROPE) ** -0.5
    nb = s // ATTN_Q_BLOCK
    qb = jnp.moveaxis(q.reshape(b, nb, ATTN_Q_BLOCK, hh, MLA_NOPE + MLA_ROPE), 1, 0)

    def attend(qblk):
        sc = jnp.einsum('bqhd,bkhd->bhqk', qblk, k).astype(F32) * scale
        p = jax.nn.softmax(sc, axis=-1).astype(v.dtype)
        return jnp.einsum('bhqk,bkhd->bqhd', p, v)

    out = lax.map(attend, qb)
    return jnp.moveaxis(out, 0, 1).reshape(b, s, hh * MLA_V)


def hyena_filters(length, w1, b1, fr1, w2, b2, fr2, w3, b3, log_decay):
    w1, b1, fr1, w2, b2, fr2, w3, b3, log_decay = (t.astype(F32) for t in (w1, b1, fr1, w2, b2, fr2, w3, b3, log_decay))
    t = jnp.arange(length, dtype=F32) / length
    bands = jnp.arange(1, HY_BANDS + 1, dtype=F32)
    ang = (2.0 * math.pi) * t[:, None] * bands
    z = jnp.concatenate([t[:, None], jnp.sin(ang), jnp.cos(ang)], axis=-1)
    hid = jnp.sin(fr1 * (z @ w1 + b1))
    hid = jnp.sin(fr2 * (hid @ w2 + b2))
    filt = (hid @ w3 + b3) * jnp.exp(-t[:, None] * jnp.exp(log_decay))
    filt = filt.reshape(length, HY_ORDER, 2, GROUP_W)
    filt = filt / jnp.sum(jnp.abs(filt), axis=(0, 2), keepdims=True)
    k_full = jnp.concatenate([filt[:, :, 0], filt[::-1, :, 1]], axis=0)
    return jnp.fft.rfft(k_full, axis=0)


def hyena_mixer(hy_u, conv_w, conv_b, w1, b1, fr1, w2, b2, fr2, w3, b3, log_decay, bias):
    b, length, _ = hy_u.shape
    proj = dwconv3(hy_u, conv_w, conv_b).astype(F32)
    x1, x2, z = jnp.split(proj, 3, axis=-1)
    kf = hyena_filters(length, w1, b1, fr1, w2, b2, fr2, w3, b3, log_decay)
    bias = bias.astype(F32)
    for o, gate in enumerate((x1, x2)):
        zf = jnp.fft.rfft(z, n=2 * length, axis=1)
        y = jnp.fft.irfft(zf * kf[None, :, o], n=2 * length, axis=1)[:, :length]
        z = gate * (y + z * bias[o])
    return z.astype(hy_u.dtype)


def shortconv_mixer(sc_u, w):
    bg, cg, xin = jnp.split(sc_u, 3, axis=-1)
    return bg * dwconv3(cg * xin, w)


def hybrid_mixer(h, pos, w_in, ml_gate_b, q_norm, kv_norm, w_uq, w_uk, w_uv,
                 hy_conv_w, hy_conv_b, hy_w1, hy_b1, hy_fr1, hy_w2, hy_b2, hy_fr2, hy_w3, hy_b3,
                 hy_log_decay, hy_bias, sc_conv_w, out_g, w_out):
    u = h @ w_in
    sizes = (GROUP_W, GROUP_W, GROUP_W, GROUP_W, 4 * N_GROUP_HEADS, MLA_Q_RANK, MLA_KV_RANK,
             MLA_ROPE, 3 * GROUP_W, 3 * GROUP_W)
    cuts, acc = [], 0
    for sz in sizes[:-1]:
        acc += sz
        cuts.append(acc)
    ml_q, ml_k, ml_v, ml_o, ml_g, cq, ckv, kr, hy_u, sc_u = jnp.split(u, cuts, axis=-1)
    y_a = mlstm_mixer(ml_q, ml_k, ml_v, ml_o, ml_g, ml_gate_b)
    y_b = mla_mixer(cq, ckv, kr, pos, q_norm, kv_norm, w_uq, w_uk, w_uv)
    y_c = hyena_mixer(hy_u, hy_conv_w, hy_conv_b, hy_w1, hy_b1, hy_fr1, hy_w2, hy_b2, hy_fr2,
                      hy_w3, hy_b3, hy_log_decay, hy_bias)
    y_d = shortconv_mixer(sc_u, sc_conv_w)
    y = jnp.concatenate([y_a, y_b.astype(y_a.dtype), y_c.astype(y_a.dtype), y_d.astype(y_a.dtype)], axis=-1)
    return head_rmsnorm(y, out_g) @ w_out


def ec_moe(h, router_w, w_gate, w_up, w_down):
    b, s, d = h.shape
    cap = EC_CAPACITY * s // N_EXPERTS
    aff = jax.nn.softmax(jnp.einsum('bsd,de->bse', h, router_w).astype(F32), axis=-1)
    g, idx = lax.top_k(jnp.swapaxes(aff, 1, 2), cap)
    xe = jax.vmap(lambda hb, ib: hb[ib])(h, idx)
    a = jnp.einsum('becd,edf->becf', xe, w_gate)
    up = jnp.einsum('becd,edf->becf', xe, w_up)
    ye = jnp.einsum('becf,efd->becd', jax.nn.silu(a) * up, w_down) * g.astype(h.dtype)[..., None]
    flat = (jnp.arange(b, dtype=jnp.int32)[:, None, None] * s + idx).reshape(-1)
    out = jnp.zeros((b * s, d), ye.dtype).at[flat].add(ye.reshape(-1, d))
    return out.reshape(b, s, d)


def setup_inputs(seed: int = 0) -> dict:
    key = jax.random.key(seed)
    ks = iter(jax.random.split(key, 48))
    L = DEPTH
    hh = N_GROUP_HEADS

    def nrm(shape, scale):
        return jax.random.normal(next(ks), shape, F32) * scale

    def gain(shape):
        return 1.0 + nrm(shape, 0.05)

    x = nrm((BATCH, SEQ, D_MODEL), 1.0)
    c = nrm((BATCH, D_MODEL), 1.0)
    positions = (jnp.arange(SEQ, dtype=jnp.int32)[None, :]
                 + jax.random.randint(next(ks), (BATCH, 1), 0, SEQ, dtype=jnp.int32))
    ada_w = nrm((L, D_MODEL, 6 * D_MODEL), 0.5 * D_MODEL ** -0.5)
    ada_b = nrm((L, 6 * D_MODEL), 0.02)
    mix_pre_g = gain((L, D_MODEL))
    mix_post_g = gain((L, D_MODEL))
    ffn_pre_g = gain((L, D_MODEL))
    ffn_post_g = gain((L, D_MODEL))
    w_in = nrm((L, D_MODEL, D_IN), D_MODEL ** -0.5)
    ib = nrm((L, 2, hh), 0.1)
    fb = jax.random.uniform(next(ks), (L, 2, hh), F32, 3.0, 6.0)
    ml_gate_b = jnp.stack([ib[:, 0], fb[:, 0], ib[:, 1], fb[:, 1]], axis=1)
    mla_q_norm = gain((L, MLA_Q_RANK))
    mla_kv_norm = gain((L, MLA_KV_RANK))
    mla_w_uq = nrm((L, MLA_Q_RANK, hh * (MLA_NOPE + MLA_ROPE)), MLA_Q_RANK ** -0.5)
    mla_w_uk = nrm((L, MLA_KV_RANK, hh * MLA_NOPE), MLA_KV_RANK ** -0.5)
    mla_w_uv = nrm((L, MLA_KV_RANK, hh * MLA_V), MLA_KV_RANK ** -0.5)
    hy_conv_w = nrm((L, 3 * GROUP_W, CONV_W), CONV_W ** -0.5)
    hy_conv_b = nrm((L, 3 * GROUP_W), 0.02)
    hy_w1 = nrm((L, HY_EMB, HY_FFN), HY_EMB ** -0.5)
    hy_b1 = nrm((L, HY_FFN), 0.5)
    hy_fr1 = 1.0 + nrm((L, HY_FFN), 0.1)
    hy_w2 = nrm((L, HY_FFN, HY_FFN), HY_FFN ** -0.5)
    hy_b2 = nrm((L, HY_FFN), 0.5)
    hy_fr2 = 1.0 + nrm((L, HY_FFN), 0.1)
    hy_w3 = nrm((L, HY_FFN, HY_N_FILT), HY_FFN ** -0.5)
    hy_b3 = nrm((L, HY_N_FILT), 0.02)
    hy_log_decay = jnp.log(jax.random.uniform(next(ks), (L, HY_N_FILT), F32, 3.0, 15.0))
    hy_bias = nrm((L, HY_ORDER, GROUP_W), 0.1)
    sc_conv_w = nrm((L, GROUP_W, CONV_W), CONV_W ** -0.5)
    mix_out_g = gain((L, D_MODEL))
    w_out = nrm((L, D_MODEL, D_MODEL), D_MODEL ** -0.5)
    router_w = nrm((L, D_MODEL, N_EXPERTS), D_MODEL ** -0.5)
    exp_w_gate = nrm((L, N_EXPERTS, D_MODEL, EXPERT_FF), D_MODEL ** -0.5)
    exp_w_up = nrm((L, N_EXPERTS, D_MODEL, EXPERT_FF), D_MODEL ** -0.5)
    exp_w_down = nrm((L, N_EXPERTS, EXPERT_FF, D_MODEL), EXPERT_FF ** -0.5)
    return {'x': x, 'c': c, 'positions': positions, 'ada_w': ada_w, 'ada_b': ada_b,
            'mix_pre_g': mix_pre_g, 'mix_post_g': mix_post_g, 'ffn_pre_g': ffn_pre_g,
            'ffn_post_g': ffn_post_g, 'w_in': w_in, 'ml_gate_b': ml_gate_b,
            'mla_q_norm': mla_q_norm, 'mla_kv_norm': mla_kv_norm, 'mla_w_uq': mla_w_uq,
            'mla_w_uk': mla_w_uk, 'mla_w_uv': mla_w_uv, 'hy_conv_w': hy_conv_w,
            'hy_conv_b': hy_conv_b, 'hy_w1': hy_w1, 'hy_b1': hy_b1, 'hy_fr1': hy_fr1,
            'hy_w2': hy_w2, 'hy_b2': hy_b2, 'hy_fr2': hy_fr2, 'hy_w3': hy_w3, 'hy_b3': hy_b3,
            'hy_log_decay': hy_log_decay, 'hy_bias': hy_bias, 'sc_conv_w': sc_conv_w,
            'mix_out_g': mix_out_g, 'w_out': w_out, 'router_w': router_w,
            'exp_w_gate': exp_w_gate, 'exp_w_up': exp_w_up, 'exp_w_down': exp_w_down}


def reference(x, c, positions, ada_w, ada_b, mix_pre_g, mix_post_g, ffn_pre_g, ffn_post_g,
              w_in, ml_gate_b, mla_q_norm, mla_kv_norm, mla_w_uq, mla_w_uk, mla_w_uv,
              hy_conv_w, hy_conv_b, hy_w1, hy_b1, hy_fr1, hy_w2, hy_b2, hy_fr2, hy_w3, hy_b3,
              hy_log_decay, hy_bias, sc_conv_w, mix_out_g, w_out, router_w,
              exp_w_gate, exp_w_up, exp_w_down):
    cs = jax.nn.silu(c)
    for l in range(DEPTH):
        mod = cs @ ada_w[l] + ada_b[l]
        sh1, sc1, g1, sh2, sc2, g2 = (m[:, None, :] for m in jnp.split(mod, 6, axis=-1))
        h = rmsnorm(x, mix_pre_g[l]) * (1.0 + sc1) + sh1
        y = hybrid_mixer(h, positions, w_in[l], ml_gate_b[l], mla_q_norm[l], mla_kv_norm[l],
                         mla_w_uq[l], mla_w_uk[l], mla_w_uv[l], hy_conv_w[l], hy_conv_b[l],
                         hy_w1[l], hy_b1[l], hy_fr1[l], hy_w2[l], hy_b2[l], hy_fr2[l], hy_w3[l],
                         hy_b3[l], hy_log_decay[l], hy_bias[l], sc_conv_w[l], mix_out_g[l], w_out[l])
        x = x + g1 * rmsnorm(y, mix_post_g[l])
        h = rmsnorm(x, ffn_pre_g[l]) * (1.0 + sc2) + sh2
        y = ec_moe(h, router_w[l], exp_w_gate[l], exp_w_up[l], exp_w_down[l])
        x = x + g2 * rmsnorm(y, ffn_post_g[l])
    return x
```

```python
import functools
import math

import numpy as np
import jax
import jax.numpy as jnp
from jax import lax
from jax.experimental import pallas as pl
from jax.experimental.pallas import tpu as pltpu

F32 = jnp.float32
BF16 = jnp.bfloat16
HIGHEST = lax.Precision.HIGHEST

GROUP_W = 256
HEAD_DIM = 64
N_HEADS = 4
ML_CHUNK = 128
MLA_Q_RANK = 224
MLA_KV_RANK = 128
MLA_NOPE = 64
MLA_ROPE = 32
ROPE_HALF = MLA_ROPE // 2
ROPE_THETA = 10000.0
HY_BANDS = 8
HY_FFN = 64
N_EXPERTS = 16
EC_CAPACITY = 2
NORM_EPS = 1e-6
LANES = 128
SLOT_BLOCK = 128
TOKEN_TILE = 256
VMEM_LIMIT = 56 * 1024 * 1024


def _params(*sem):
    return pltpu.CompilerParams(dimension_semantics=sem, vmem_limit_bytes=VMEM_LIMIT)


def _dot(a, b):
    return jnp.dot(a, b, preferred_element_type=F32)


def _dot_hi(a, b):
    return jnp.dot(a, b, precision=HIGHEST, preferred_element_type=F32)


def _dot_nt(a, b):
    return lax.dot_general(a, b, (((1,), (1,)), ((), ())), preferred_element_type=F32)


def _dot_nt_hi(a, b):
    return lax.dot_general(a, b, (((1,), (1,)), ((), ())), precision=HIGHEST,
                           preferred_element_type=F32)


def _rms(x, n):
    ms = jnp.sum(x * x, axis=-1, keepdims=True) * (1.0 / n)
    return x * lax.rsqrt(ms + NORM_EPS)


def _log_sigmoid(x):
    return jnp.minimum(x, 0.0) - jnp.log(1.0 + jnp.exp(-jnp.abs(x)))


def _sigmoid(x):
    return 1.0 / (1.0 + jnp.exp(-x))


def _ada_kernel(c_ref, w_ref, b_ref, o_ref):
    c = c_ref[...]
    cs = c * _sigmoid(c)
    o_ref[...] = _dot_hi(cs, w_ref[...]) + b_ref[...]


def ada_mod(c, ada_w, ada_b):
    depth, d, n6 = ada_w.shape
    b = c.shape[0]
    bp = 8
    cp = jnp.zeros((bp, d), F32).at[:b].set(c)
    tn = 1536
    out = pl.pallas_call(
        _ada_kernel,
        grid=(depth, n6 // tn),
        in_specs=[pl.BlockSpec((bp, d), lambda l, j: (0, 0)),
                  pl.BlockSpec((None, d, tn), lambda l, j: (l, 0, j)),
                  pl.BlockSpec((None, 1, tn), lambda l, j: (l, 0, j))],
        out_specs=pl.BlockSpec((None, bp, tn), lambda l, j: (l, 0, j)),
        out_shape=jax.ShapeDtypeStruct((depth, bp, n6), F32),
        compiler_params=_params("parallel", "parallel"),
        name="ada_mod",
    )(cp, ada_w, ada_b.reshape(depth, 1, n6))
    return out[:, :b]


_U_COLS = (("q", 256), ("v", 256), ("o", 256), ("g", 128), ("cq", 256), ("ckv", 128),
           ("kr", 128), ("hy", 768), ("sc", 768))
_U_TOTAL = sum(w for _, w in _U_COLS)
_UT_ROWS = 256 + 16


def _inproj_kernel(x_ref, gain_ref, sc_ref, sh_ref, w_ref, wt_ref,
                   q_ref, v_ref, o_ref, g_ref, cq_ref, ckv_ref, kr_ref, hy_ref, scu_ref,
                   kt_ref, gt_ref):
    x = x_ref[...]
    d = x.shape[-1]
    h = _rms(x, d) * gain_ref[...] * (1.0 + sc_ref[...]) + sh_ref[...]
    hb = h.astype(BF16)
    u = _dot(hb, w_ref[...])
    off = 0
    for ref, (_, width) in zip((q_ref, v_ref, o_ref, g_ref, cq_ref, ckv_ref, kr_ref, hy_ref, scu_ref),
                               _U_COLS):
        ref[...] = u[:, off:off + width]
        off += width
    ut = _dot_nt(wt_ref[...], hb)
    kt_ref[...] = ut[:256]
    gt_ref[...] = ut[256:]


def in_proj(x, gain, scale, shift, w_in):
    b, s, d = x.shape
    tm = min(512, s)
    cuts = np.cumsum([0, 256, 256, 256, 256, 16, MLA_Q_RANK, MLA_KV_RANK, MLA_ROPE, 768, 768])
    wq, wk, wv, wo, wg, wcq, wckv, wkr, why, wsc = (w_in[:, cuts[i]:cuts[i + 1]] for i in range(10))
    pad = lambda w, n: jnp.pad(w, ((0, 0), (0, n - w.shape[1])))
    wkr_p = jnp.pad(wkr, ((0, 0), (MLA_NOPE, LANES - MLA_NOPE - MLA_ROPE)))
    w1 = jnp.concatenate([wq, wv, wo, pad(wg, 128), pad(wcq, 256), wckv, wkr_p, why, wsc],
                         axis=1).astype(BF16)
    w2t = jnp.concatenate([wk, wg], axis=1).T.astype(BF16)
    row = lambda nm, w: pl.BlockSpec((None, tm, w), lambda i, j: (i, j, 0))
    out_shapes = [jax.ShapeDtypeStruct((b, s, w), F32) for _, w in _U_COLS]
    out_shapes += [jax.ShapeDtypeStruct((b, 256, s), F32), jax.ShapeDtypeStruct((b, 16, s), F32)]
    out_specs = [row(nm, w) for nm, w in _U_COLS]
    out_specs += [pl.BlockSpec((None, 256, tm), lambda i, j: (i, 0, j)),
                  pl.BlockSpec((None, 16, tm), lambda i, j: (i, 0, j))]
    vec = lambda: pl.BlockSpec((None, 1, d), lambda i, j: (i, 0, 0))
    outs = pl.pallas_call(
        _inproj_kernel,
        grid=(b, s // tm),
        in_specs=[pl.BlockSpec((None, tm, d), lambda i, j: (i, j, 0)),
                  pl.BlockSpec((1, d), lambda i, j: (0, 0)),
                  vec(), vec(),
                  pl.BlockSpec((d, _U_TOTAL), lambda i, j: (0, 0)),
                  pl.BlockSpec((_UT_ROWS, d), lambda i, j: (0, 0))],
        out_specs=out_specs,
        out_shape=out_shapes,
        compiler_params=_params("parallel", "parallel"),
        name="in_proj",
    )(x, gain.reshape(1, d), scale.reshape(b, 1, d), shift.reshape(b, 1, d), w1, w2t)
    names = [nm for nm, _ in _U_COLS] + ["kT", "gT"]
    return dict(zip(names, outs))


def _mlstm_dir(q, v, kt, gt, gc, c_ref, m_ref, base, rev):
    L = q.shape[0]
    r = lax.broadcasted_iota(jnp.int32, (L, L), 0)
    c = lax.broadcasted_iota(jnp.int32, (L, L), 1)
    tri = (c >= r) if rev else (c <= r)
    io, fo = (8, 12) if rev else (0, 4)
    logf_rows = _log_sigmoid(gt[fo:fo + 4, :])
    logf_cols = _log_sigmoid(gc)
    a_cols = _dot_hi(jnp.where(tri, 1.0, 0.0), logf_cols)
    tri_t = (r >= c) if rev else (r <= c)
    a_rows = _dot_hi(logf_rows, jnp.where(tri_t, 1.0, 0.0))
    lane = lax.broadcasted_iota(jnp.int32, (L, LANES), 1)
    row128 = lax.broadcasted_iota(jnp.int32, (LANES, L), 0)
    scale = HEAD_DIM ** -0.5
    outs = []
    for pair in range(2):
        qp = q[:, pair * LANES:(pair + 1) * LANES]
        vp = v[:, pair * LANES:(pair + 1) * LANES]
        ktp = kt[pair * LANES:(pair + 1) * LANES, :] * scale
        pair_out = None
        for sub in range(2):
            h = pair * 2 + sub
            in_head = (row128 >= sub * HEAD_DIM) & (row128 < (sub + 1) * HEAD_DIM)
            kth = jnp.where(in_head, ktp, 0.0)
            vsh = vp if sub == 0 else pltpu.roll(vp, HEAD_DIM, 1)
            v_aug = jnp.where(lane < HEAD_DIM, vsh, jnp.where(lane == HEAD_DIM, 1.0, 0.0))
            v_aug_b = v_aug.astype(BF16)
            a_c = a_cols[:, fo + h:fo + h + 1]
            a_r = a_rows[h:h + 1, :]
            ig_r = gt[io + h:io + h + 1, :]
            a_end = jnp.sum(logf_rows[h:h + 1, :], axis=-1, keepdims=True)
            c_st = c_ref[base + h]
            m_st = m_ref[base + h][:, 0:1]
            d_log = jnp.where(tri, a_c - a_r + ig_r, -jnp.inf)
            inter = a_c + m_st
            m_t = jnp.maximum(inter, jnp.max(d_log, axis=-1, keepdims=True))
            qb = qp.astype(BF16)
            p = jnp.exp(d_log - m_t) * _dot(qb, kth.astype(BF16))
            sci = jnp.exp(inter - m_t)
            nd = _dot(p.astype(BF16), v_aug_b) + sci * _dot(qb, c_st.astype(BF16))
            den = nd[:, HEAD_DIM:HEAD_DIM + 1]
            out = nd / jnp.maximum(jnp.abs(den), jnp.exp(-m_t))
            w_st = a_end - a_r + ig_r
            m_loc = jnp.max(w_st, axis=-1, keepdims=True)
            ke = kth * jnp.exp(w_st - m_loc)
            c_loc = _dot(ke.astype(BF16), v_aug_b)
            m_new = jnp.maximum(a_end + m_st, m_loc)
            sp = jnp.exp(a_end + m_st - m_new)
            sl = jnp.exp(m_loc - m_new)
            c_ref[base + h] = sp * c_st + sl * c_loc
            m_ref[base + h] = jnp.broadcast_to(m_new, (1, LANES))
            if sub == 0:
                pair_out = out
            else:
                pair_out = jnp.where(lane < HEAD_DIM, pair_out, pltpu.roll(out, HEAD_DIM, 1))
        outs.append(pair_out)
    return jnp.concatenate(outs, axis=1)


def _mlstm_kernel(qf_ref, vf_ref, ktf_ref, gtf_ref, gcf_ref,
                  qb_ref, vb_ref, ktb_ref, gtb_ref, gcb_ref, brow_ref, bcol_ref,
                  hf_ref, hb_ref, c_ref, m_ref):
    @pl.when(pl.program_id(1) == 0)
    def _():
        c_ref[...] = jnp.zeros_like(c_ref)
        m_ref[...] = jnp.zeros_like(m_ref)

    bcol = bcol_ref[...]
    brow = brow_ref[...]
    hf_ref[...] = _mlstm_dir(qf_ref[...], vf_ref[...], ktf_ref[...], gtf_ref[...] + bcol,
                             gcf_ref[...] + brow, c_ref, m_ref, 0, False)
    hb_ref[...] = _mlstm_dir(qb_ref[...], vb_ref[...], ktb_ref[...], gtb_ref[...] + bcol,
                             gcb_ref[...] + brow, c_ref, m_ref, N_HEADS, True)


def mlstm(q, v, kt, gt, g, gate_b):
    b, s, w = q.shape
    L = ML_CHUNK
    nc = s // L
    bflat = gate_b.reshape(16)
    brow = jnp.zeros((1, LANES), F32).at[0, :16].set(bflat)
    bcol = bflat.reshape(16, 1)
    fw = lambda i, j: (i, j, 0)
    bw = lambda i, j: (i, nc - 1 - j, 0)
    fwt = lambda i, j: (i, 0, j)
    bwt = lambda i, j: (i, 0, nc - 1 - j)

    def specs(m, mt):
        return [pl.BlockSpec((None, L, w), m), pl.BlockSpec((None, L, w), m),
                pl.BlockSpec((None, w, L), mt), pl.BlockSpec((None, 16, L), mt),
                pl.BlockSpec((None, L, LANES), m)]

    return pl.pallas_call(
        _mlstm_kernel,
        grid=(b, nc),
        in_specs=specs(fw, fwt) + specs(bw, bwt) + [
            pl.BlockSpec((1, LANES), lambda i, j: (0, 0)),
            pl.BlockSpec((16, 1), lambda i, j: (0, 0))],
        out_specs=[pl.BlockSpec((None, L, w), fw), pl.BlockSpec((None, L, w), bw)],
        out_shape=[jax.ShapeDtypeStruct((b, s, w), F32)] * 2,
        scratch_shapes=[pltpu.VMEM((2 * N_HEADS, LANES, LANES), F32),
                        pltpu.VMEM((2 * N_HEADS, 1, LANES), F32)],
        compiler_params=_params("parallel", "arbitrary"),
        name="mlstm",
    )(q, v, kt, gt, g, q, v, kt, gt, g, brow, bcol)


def _rope_table_kernel(pos_ref, post_ref, inv_ref, invt_ref, cos_ref, sin_ref, cost_ref, sint_ref):
    ang = pos_ref[...].astype(F32) * inv_ref[...]
    cos_ref[...] = jnp.cos(ang)
    sin_ref[...] = jnp.sin(ang)
    ang_t = invt_ref[...] * post_ref[...].astype(F32)
    cost_ref[...] = jnp.cos(ang_t)
    sint_ref[...] = jnp.sin(ang_t)


def rope_tables(positions):
    b, s = positions.shape
    tm = min(512, s)
    inv = ROPE_THETA ** (-jnp.arange(ROPE_HALF, dtype=F32) / ROPE_HALF)
    inv_row = jnp.zeros((1, LANES), F32).at[0, MLA_NOPE:MLA_NOPE + ROPE_HALF].set(inv)
    inv_row = inv_row.at[0, MLA_NOPE + ROPE_HALF:MLA_NOPE + MLA_ROPE].set(inv)
    spec = pl.BlockSpec((None, tm, LANES), lambda i, j: (i, j, 0))
    spec_t = pl.BlockSpec((None, LANES, tm), lambda i, j: (i, 0, j))
    return pl.pallas_call(
        _rope_table_kernel,
        grid=(b, s // tm),
        in_specs=[pl.BlockSpec((None, tm, 1), lambda i, j: (i, j, 0)),
                  pl.BlockSpec((None, 1, tm), lambda i, j: (i, 0, j)),
                  pl.BlockSpec((1, LANES), lambda i, j: (0, 0)),
                  pl.BlockSpec((LANES, 1), lambda i, j: (0, 0))],
        out_specs=[spec, spec, spec_t, spec_t],
        out_shape=[jax.ShapeDtypeStruct((b, s, LANES), F32)] * 2
        + [jax.ShapeDtypeStruct((b, LANES, s), F32)] * 2,
        compiler_params=_params("parallel", "parallel"),
        name="rope_tables",
    )(positions.reshape(b, s, 1), positions.reshape(b, 1, s), inv_row, inv_row.reshape(LANES, 1))


def _mla_proj_kernel(cq_ref, ckv_ref, kr_ref, cos_ref, sin_ref, cost_ref, sint_ref, qg_ref, kvg_ref,
                     wqt_ref, wk_ref, wvt_ref, qt_ref, k_ref, vt_ref):
    cos = cos_ref[...]
    sin = sin_ref[...]
    lane = lax.broadcasted_iota(jnp.int32, cos.shape, 1)
    x1 = (lane >= MLA_NOPE) & (lane < MLA_NOPE + ROPE_HALF)
    x2 = (lane >= MLA_NOPE + ROPE_HALF) & (lane < MLA_NOPE + MLA_ROPE)
    kr = kr_ref[...]
    krr = (kr * cos + pltpu.roll(kr, LANES - ROPE_HALF, 1) * jnp.where(x1, -sin, 0.0)
           + pltpu.roll(kr, ROPE_HALF, 1) * jnp.where(x2, sin, 0.0))

    cqn = (_rms(cq_ref[...], MLA_Q_RANK) * qg_ref[...]).astype(BF16)
    ckvn = (_rms(ckv_ref[...], MLA_KV_RANK) * kvg_ref[...]).astype(BF16)
    kn = _dot(ckvn, wk_ref[...])
    for h in range(N_HEADS):
        sl = slice(h * LANES, (h + 1) * LANES)
        k_ref[:, sl] = (kn[:, sl] + krr).astype(BF16)
    vt_ref[...] = _dot_nt(wvt_ref[...], ckvn).astype(BF16)

    cos_t = cost_ref[...]
    sin_t = sint_ref[...]
    row = lax.broadcasted_iota(jnp.int32, cos_t.shape, 0)
    x1t = (row >= MLA_NOPE) & (row < MLA_NOPE + ROPE_HALF)
    x2t = (row >= MLA_NOPE + ROPE_HALF) & (row < MLA_NOPE + MLA_ROPE)
    sin_at = jnp.where(x1t, -sin_t, 0.0)
    sin_bt = jnp.where(x2t, sin_t, 0.0)
    qt = _dot_nt(wqt_ref[...], cqn)
    scale = (MLA_NOPE + MLA_ROPE) ** -0.5 * math.log2(math.e)
    for h in range(N_HEADS):
        sl = slice(h * LANES, (h + 1) * LANES)
        xs = qt[sl, :]
        roped = (xs * cos_t + pltpu.roll(xs, LANES - ROPE_HALF, 0) * sin_at
                 + pltpu.roll(xs, ROPE_HALF, 0) * sin_bt)
        qt_ref[sl, :] = (roped * scale).astype(BF16)


def mla_proj(cq, ckv, kr, tables, q_norm, kv_norm, w_uq, w_uk, w_uv):
    b, s, _ = cq.shape
    tm = min(512, s)
    dqk = MLA_NOPE + MLA_ROPE
    cos, sin, cos_t, sin_t = tables
    wq = w_uq.reshape(MLA_Q_RANK, N_HEADS, dqk)
    wq = jnp.pad(wq, ((0, 256 - MLA_Q_RANK), (0, 0), (0, LANES - dqk))).reshape(256, N_HEADS * LANES)
    wk = w_uk.reshape(MLA_KV_RANK, N_HEADS, MLA_NOPE)
    wk = jnp.pad(wk, ((0, 0), (0, 0), (0, LANES - MLA_NOPE))).reshape(MLA_KV_RANK, N_HEADS * LANES)
    qg = jnp.pad(q_norm, (0, 256 - MLA_Q_RANK)).reshape(1, 256)
    kvg = kv_norm.reshape(1, MLA_KV_RANK)
    row = lambda w: pl.BlockSpec((None, tm, w), lambda i, j: (i, j, 0))
    col = lambda w: pl.BlockSpec((None, w, tm), lambda i, j: (i, 0, j))
    full = lambda a: pl.BlockSpec(a.shape, lambda i, j: (0,) * a.ndim)
    wqt, wkb, wvt = wq.T.astype(BF16), wk.astype(BF16), w_uv.T.astype(BF16)
    return pl.pallas_call(
        _mla_proj_kernel,
        grid=(b, s // tm),
        in_specs=[row(256), row(128), row(128), row(128), row(128), col(128), col(128),
                  full(qg), full(kvg), full(wqt), full(wkb), full(wvt)],
        out_specs=[col(512), row(512), col(256)],
        out_shape=[jax.ShapeDtypeStruct((b, 512, s), BF16), jax.ShapeDtypeStruct((b, s, 512), BF16),
                   jax.ShapeDtypeStruct((b, 256, s), BF16)],
        compiler_params=_params("parallel", "parallel"),
        name="mla_proj",
    )(cq, ckv, kr, cos, sin, cos_t, sin_t, qg, kvg, wqt, wkb, wvt)


def _flash_kernel(qt_ref, k_ref, vt_ref, o_ref, *, tk):
    s = k_ref.shape[0]
    tq = qt_ref.shape[1]
    nk = s // tk
    qts = [qt_ref[sub * LANES:(sub + 1) * LANES, :] for sub in range(2)]

    def scores(t):
        off = pl.multiple_of(t * tk, tk)
        return [_dot(k_ref[pl.ds(off, tk), sub * LANES:(sub + 1) * LANES], qts[sub])
                for sub in range(2)]

    def update(t, sc, stats):
        off = pl.multiple_of(t * tk, tk)
        new = []
        for sub in range(2):
            m, l, acc = stats[sub]
            vt = vt_ref[sub * HEAD_DIM:(sub + 1) * HEAD_DIM, pl.ds(off, tk)]
            m_new = jnp.maximum(m, jnp.max(sc[sub], axis=0, keepdims=True))
            alpha = jnp.exp2(m - m_new)
            p = jnp.exp2(sc[sub] - m_new)
            l = alpha * l + jnp.sum(p, axis=0, keepdims=True)
            acc = alpha * acc + _dot(vt, p.astype(BF16))
            new.append((m_new, l, acc))
        return new

    def body(u, carry):
        sc_a, stats = carry
        ta = 2 * u
        sc_b = scores(ta + 1)
        stats = update(ta, sc_a, stats)
        sc_a = scores(jnp.minimum(ta + 2, nk - 1))
        stats = update(ta + 1, sc_b, stats)
        return sc_a, stats

    init = [(jnp.full((1, tq), -jnp.inf, F32), jnp.zeros((1, tq), F32),
             jnp.zeros((HEAD_DIM, tq), F32)) for _ in range(2)]
    _, fin = lax.fori_loop(0, nk // 2, body, (scores(0), init))
    out_t = jnp.concatenate([acc / l for (_, l, acc) in fin], axis=0)
    o_ref[...] = out_t.T


def flash_attention(qt, k, vt):
    b, _, s = qt.shape
    tq = min(256, s)
    tk = min(512, s)
    return pl.pallas_call(
        functools.partial(_flash_kernel, tk=tk),
        grid=(b, 2, s // tq),
        in_specs=[pl.BlockSpec((None, 256, tq), lambda i, p, j: (i, p, j)),
                  pl.BlockSpec((None, s, 256), lambda i, p, j: (i, 0, p)),
                  pl.BlockSpec((None, LANES, s), lambda i, p, j: (i, p, 0))],
        out_specs=pl.BlockSpec((None, tq, LANES), lambda i, p, j: (i, j, p)),
        out_shape=jax.ShapeDtypeStruct((b, s, 256), F32),
        compiler_params=_params("parallel", "parallel", "parallel"),
        name="flash_attention",
    )(qt, k, vt)


def _shifted(x, prev_row, next_row, first, last):
    tm = x.shape[0]
    row = lax.broadcasted_iota(jnp.int32, x.shape, 0)
    prev_row = jnp.where(first, 0.0, prev_row)
    next_row = jnp.where(last, 0.0, next_row)
    xm1 = jnp.where(row == 0, prev_row, pltpu.roll(x, 1, 0))
    xp1 = jnp.where(row == tm - 1, next_row, pltpu.roll(x, tm - 1, 0))
    return xm1, xp1


def _conv_kernel(hy_ref, hyp_ref, hyn_ref, sc_ref, scp_ref, scn_ref, hw_ref, hb_ref, sw_ref,
                 x1_ref, x2_ref, z_ref, yd_ref):
    j = pl.program_id(1)
    first = j == 0
    last = j == pl.num_programs(1) - 1
    x = hy_ref[...]
    xm1, xp1 = _shifted(x, hyp_ref[7:8, :], hyn_ref[0:1, :], first, last)
    hw = hw_ref[...]
    proj = xm1 * hw[0:1] + x * hw[1:2] + xp1 * hw[2:3] + hb_ref[...]
    x1_ref[...] = proj[:, 0:GROUP_W]
    x2_ref[...] = proj[:, GROUP_W:2 * GROUP_W]
    z_ref[...] = proj[:, 2 * GROUP_W:]
    su = sc_ref[...]
    g = GROUP_W
    prod = su[:, g:2 * g] * su[:, 2 * g:]
    pprev = scp_ref[7:8, g:2 * g] * scp_ref[7:8, 2 * g:]
    pnext = scn_ref[0:1, g:2 * g] * scn_ref[0:1, 2 * g:]
    pm1, pp1 = _shifted(prod, pprev, pnext, first, last)
    sw = sw_ref[...]
    yd_ref[...] = su[:, :g] * (pm1 * sw[0:1] + prod * sw[1:2] + pp1 * sw[2:3])


def conv_mixers(hy_u, sc_u, hy_conv_w, hy_conv_b, sc_conv_w):
    b, s, w3 = hy_u.shape
    tm = min(512, s)
    nb8 = s // 8
    r8 = tm // 8
    main = pl.BlockSpec((None, tm, w3), lambda i, j: (i, j, 0))
    prev = pl.BlockSpec((None, 8, w3), lambda i, j: (i, jnp.maximum(j * r8 - 1, 0), 0))
    nxt = pl.BlockSpec((None, 8, w3), lambda i, j: (i, jnp.minimum((j + 1) * r8, nb8 - 1), 0))
    full = lambda a: pl.BlockSpec(a.shape, lambda i, j: (0,) * a.ndim)
    hw = hy_conv_w.T
    hb = hy_conv_b.reshape(1, w3)
    sw = sc_conv_w.T
    out = pl.BlockSpec((None, tm, GROUP_W), lambda i, j: (i, j, 0))
    return pl.pallas_call(
        _conv_kernel,
        grid=(b, s // tm),
        in_specs=[main, prev, nxt, main, prev, nxt, full(hw), full(hb), full(sw)],
        out_specs=[out] * 4,
        out_shape=[jax.ShapeDtypeStruct((b, s, GROUP_W), F32)] * 4,
        compiler_params=_params("parallel", "parallel"),
        name="conv_mixers",
    )(hy_u, hy_u, hy_u, sc_u, sc_u, sc_u, hw, hb, sw)


def _fft_dims(s):
    n = 2 * s
    lg = int(round(math.log2(n)))
    assert 1 << lg == n
    n1 = 1 << ((lg + 1) // 2)
    return n1, n // n1


def _filter_kernel(frow_ref, w1_ref, b1_ref, fr1_ref, w2_ref, b2_ref, fr2_ref, w3_ref, b3_ref,
                   ld_ref, k_ref, norm_ref, *, length):
    i = pl.program_id(0)
    tm = k_ref.shape[0]
    n = i * tm + lax.broadcasted_iota(jnp.int32, (tm, 1), 0)
    tt = jnp.where(n < length, n, 2 * length - 1 - n)
    t = tt.astype(F32) / length
    lane = lax.broadcasted_iota(jnp.int32, (tm, LANES), 1)
    ang = t * frow_ref[...]
    z = jnp.where(lane == 0, t,
                  jnp.where(lane <= HY_BANDS, jnp.sin(ang),
                            jnp.where(lane <= 2 * HY_BANDS, jnp.cos(ang), 0.0)))
    hid = jnp.sin(fr1_ref[...] * (_dot_hi(z, w1_ref[...]) + b1_ref[...]))
    hid = jnp.sin(fr2_ref[...] * (_dot_hi(hid, w2_ref[...]) + b2_ref[...]))
    filt = (_dot_hi(hid, w3_ref[...]) + b3_ref[...]) * jnp.exp(-t * jnp.exp(ld_ref[...]))
    k_ref[...] = filt

    @pl.when(i == 0)
    def _():
        norm_ref[...] = jnp.zeros_like(norm_ref)

    norm_ref[...] += jnp.sum(jnp.abs(filt), axis=0, keepdims=True)


def hyena_filter_taps(length, w1, b1, fr1, w2, b2, fr2, w3, b3, log_decay):
    tm = min(512, length)
    n_half = length // tm
    oc = 2 * GROUP_W
    bands = jnp.arange(1, HY_BANDS + 1, dtype=F32) * (2.0 * math.pi)
    frow = jnp.zeros((1, LANES), F32).at[0, 1:1 + HY_BANDS].set(bands)
    frow = frow.at[0, 1 + HY_BANDS:1 + 2 * HY_BANDS].set(bands)
    w1p = jnp.zeros((LANES, HY_FFN), F32).at[:1 + 2 * HY_BANDS].set(w1)
    bydir = lambda a: jnp.moveaxis(a.reshape(a.shape[0], 2, 2, GROUP_W), 2, 0).reshape(2, a.shape[0], oc)
    w3d, b3d, ldd = bydir(w3), bydir(b3.reshape(1, -1)), bydir(log_decay.reshape(1, -1))
    full = lambda a: pl.BlockSpec(a.shape, lambda i: (0,) * a.ndim)
    dirspec = lambda a: pl.BlockSpec((None,) + a.shape[1:], lambda i: (i // n_half, 0, 0))
    r1 = lambda a: a.reshape(1, -1)
    return pl.pallas_call(
        functools.partial(_filter_kernel, length=length),
        grid=(2 * n_half,),
        in_specs=[full(frow), full(w1p), full(r1(b1)), full(r1(fr1)), full(w2), full(r1(b2)),
                  full(r1(fr2)), dirspec(w3d), dirspec(b3d), dirspec(ldd)],
        out_specs=[pl.BlockSpec((tm, oc), lambda i: (i, 0)), pl.BlockSpec((1, oc), lambda i: (0, 0))],
        out_shape=[jax.ShapeDtypeStruct((2 * length, oc), F32), jax.ShapeDtypeStruct((1, oc), F32)],
        compiler_params=_params("arbitrary"),
        name="hyena_filter",
    )(frow, w1p, r1(b1), r1(fr1), w2, r1(b2), r1(fr2), w3d, b3d, ldd)


def _dft_consts(s):
    n1, n2 = _fft_dims(s)
    n = n1 * n2
    a1 = 2.0 * np.pi * np.outer(np.arange(n1), np.arange(n1)) / n1
    f1 = np.concatenate([np.cos(a1), -np.sin(a1)], axis=0)
    a2 = 2.0 * np.pi * np.outer(np.arange(n2), np.arange(n2)) / n2
    c2, s2 = np.cos(a2), np.sin(a2)
    f2_fwd = np.block([[c2, s2], [-s2, c2]])
    f2_inv = np.block([[c2, -s2], [s2, c2]])
    at = 2.0 * np.pi * np.outer(np.arange(n1), np.arange(n2)) / n
    tw = np.stack([np.cos(at), np.sin(at)], axis=0)[..., None]
    m1 = np.arange(n1 // 2)
    a3 = 2.0 * np.pi * np.outer(m1, np.arange(n1)) / n1
    f3 = np.concatenate([np.cos(a3), -np.sin(a3)], axis=1) / n
    f = lambda a: jnp.asarray(a, F32)
    return dict(n1=n1, n2=n2, f1=f(f1), f2_fwd=f(f2_fwd), f2_inv=f(f2_inv), tw=f(tw), f3=f(f3))


_N2_TILE = 8


def _dft1_kernel(x_ref, f_ref, s_ref, o_ref):
    f = f_ref[...]
    scale = s_ref[...]
    for m in range(_N2_TILE):
        o_ref[:, m, :] = _dot_hi(f, x_ref[:, m, :] * scale)


def dft_stage1(x, f1, scale_row):
    g, r, n2, ch = x.shape
    m = f1.shape[0]
    return pl.pallas_call(
        _dft1_kernel,
        grid=(g, n2 // _N2_TILE),
        in_specs=[pl.BlockSpec((None, r, _N2_TILE, ch), lambda i, j: (i, 0, j, 0)),
                  pl.BlockSpec((m, r), lambda i, j: (0, 0)),
                  pl.BlockSpec((1, ch), lambda i, j: (0, 0))],
        out_specs=pl.BlockSpec((None, m, _N2_TILE, ch), lambda i, j: (i, 0, j, 0)),
        out_shape=jax.ShapeDtypeStruct((g, m, n2, ch), F32),
        compiler_params=_params("parallel", "parallel"),
        name="dft_stage1",
    )(x, f1, scale_row)


def _twiddle(re, im, tc, ts, conj):
    if conj:
        return re * tc - im * ts, im * tc + re * ts
    return re * tc + im * ts, im * tc - re * ts


def _dft2_filter_kernel(a_ref, tw_ref, ff_ref, o_ref):
    n2 = a_ref.shape[2]
    br, bi = _twiddle(a_ref[0, 0], a_ref[1, 0], tw_ref[0, 0], tw_ref[1, 0], False)
    zz = _dot_hi(ff_ref[...], jnp.concatenate([br, bi], axis=0))
    o_ref[0, 0] = zz[:n2]
    o_ref[1, 0] = zz[n2:]


def dft_stage2_filter(a, consts):
    _, n1, n2, ch = a.shape
    return pl.pallas_call(
        _dft2_filter_kernel,
        grid=(n1,),
        in_specs=[pl.BlockSpec((2, 1, n2, ch), lambda k: (0, k, 0, 0)),
                  pl.BlockSpec((2, 1, n2, 1), lambda k: (0, k, 0, 0)),
                  pl.BlockSpec((2 * n2, 2 * n2), lambda k: (0, 0))],
        out_specs=pl.BlockSpec((2, 1, n2, ch), lambda k: (0, k, 0, 0)),
        out_shape=jax.ShapeDtypeStruct(a.shape, F32),
        compiler_params=_params("parallel"),
        name="dft_stage2_filter",
    )(a, consts["tw"], consts["f2_fwd"])


def _dft2_conv_kernel(a_ref, tw_ref, kf_ref, ff_ref, fi_ref, o_ref):
    n2 = a_ref.shape[2]
    tc, ts = tw_ref[0, 0], tw_ref[1, 0]
    br, bi = _twiddle(a_ref[0, 0], a_ref[1, 0], tc, ts, False)
    zz = _dot_hi(ff_ref[...], jnp.concatenate([br, bi], axis=0))
    zr, zi = zz[:n2], zz[n2:]
    kr, ki = kf_ref[0, 0], kf_ref[1, 0]
    pr = zr * kr - zi * ki
    pi = zr * ki + zi * kr
    vv = _dot_hi(fi_ref[...], jnp.concatenate([pr, pi], axis=0))
    vr, vi = _twiddle(vv[:n2], vv[n2:], tc, ts, True)
    o_ref[0, 0] = vr
    o_ref[1, 0] = vi


def dft_stage2_conv(a, kf, order, consts):
    b, _, n1, n2, ch = a.shape
    blk = pl.BlockSpec((None, 2, 1, n2, ch), lambda i, k: (i, 0, k, 0, 0))
    mat = pl.BlockSpec((2 * n2, 2 * n2), lambda i, k: (0, 0))
    return pl.pallas_call(
        _dft2_conv_kernel,
        grid=(b, n1),
        in_specs=[blk,
                  pl.BlockSpec((2, 1, n2, 1), lambda i, k: (0, k, 0, 0)),
                  pl.BlockSpec((2, 1, n2, ch), lambda i, k: (0, k, 0, order)),
                  mat, mat],
        out_specs=blk,
        out_shape=jax.ShapeDtypeStruct(a.shape, F32),
        compiler_params=_params("parallel", "parallel"),
        name="dft_stage2_conv",
    )(a, consts["tw"], kf, consts["f2_fwd"], consts["f2_inv"])


def _dft3_kernel(v_ref, f_ref, gate_ref, z_ref, bias_ref, o_ref):
    f = f_ref[...]
    bias = bias_ref[...]
    for m in range(_N2_TILE):
        y = _dot_hi(f, v_ref[:, m, :])
        o_ref[:, m, :] = gate_ref[:, m, :] * (y + z_ref[:, m, :] * bias)


def dft_stage3_gate(v, f3, gate, z, bias_row):
    b, m, n2, ch = v.shape
    r = f3.shape[0]
    row = pl.BlockSpec((None, r, _N2_TILE, ch), lambda i, j: (i, 0, j, 0))
    return pl.pallas_call(
        _dft3_kernel,
        grid=(b, n2 // _N2_TILE),
        in_specs=[pl.BlockSpec((None, m, _N2_TILE, ch), lambda i, j: (i, 0, j, 0)),
                  pl.BlockSpec((r, m), lambda i, j: (0, 0)),
                  row, row, pl.BlockSpec((1, ch), lambda i, j: (0, 0))],
        out_specs=row,
        out_shape=jax.ShapeDtypeStruct((b, r, n2, ch), F32),
        compiler_params=_params("parallel", "parallel"),
        name="dft_stage3_gate",
    )(v, f3, gate, z, bias_row)


def hyena_mixer(x1, x2, z, hy_w1, hy_b1, hy_fr1, hy_w2, hy_b2, hy_fr2, hy_w3, hy_b3, hy_log_decay,
                hy_bias):
    b, s, ch = z.shape
    consts = _dft_consts(s)
    n1, n2 = consts["n1"], consts["n2"]
    oc = 2 * ch
    taps, norm = hyena_filter_taps(s, hy_w1, hy_b1, hy_fr1, hy_w2, hy_b2, hy_fr2, hy_w3, hy_b3,
                                   hy_log_decay)
    ka = dft_stage1(taps.reshape(1, n1, n2, oc), consts["f1"], 1.0 / norm)
    kf = dft_stage2_filter(ka.reshape(2, n1, n2, oc), consts)
    half = n1 // 2
    view = lambda t: t.reshape(b, half, n2, ch)
    ones = jnp.ones((1, ch), F32)
    f1_half = consts["f1"][:, :half]
    cur = view(z)
    for order, gate in enumerate((x1, x2)):
        a = dft_stage1(cur, f1_half, ones)
        v = dft_stage2_conv(a.reshape(b, 2, n1, n2, ch), kf, order, consts)
        cur = dft_stage3_gate(v.reshape(b, 2 * n1, n2, ch), consts["f3"], view(gate), cur,
                              hy_bias[order].reshape(1, ch))
    return cur.reshape(b, s, ch)


def _outproj_kernel(x_ref, hf_ref, hb_ref, o_ref, yb_ref, yc_ref, yd_ref, og_ref, hm_ref, wout_ref,
                    pg_ref, g1_ref, fg_ref, sc2_ref, sh2_ref, rw_ref, rwt_ref,
                    xn_ref, h2_ref, aff_ref, afft_ref):
    hm = hm_ref[...]
    y_a = _sigmoid(o_ref[...]) * (hf_ref[...] + hb_ref[...])
    acc = None
    for idx, y in enumerate((y_a, yb_ref[...], yc_ref[...], yd_ref[...])):
        ms = _dot_hi(y * y, hm)
        yn = y * lax.rsqrt(ms + NORM_EPS) * og_ref[:, idx * GROUP_W:(idx + 1) * GROUP_W]
        part = _dot(yn.astype(BF16), wout_ref[idx * GROUP_W:(idx + 1) * GROUP_W, :])
        acc = part if acc is None else acc + part
    d = acc.shape[-1]
    xn = x_ref[...] + g1_ref[...] * (_rms(acc, d) * pg_ref[...])
    xn_ref[...] = xn
    h2 = _rms(xn, d) * fg_ref[...] * (1.0 + sc2_ref[...]) + sh2_ref[...]
    h2_ref[...] = h2.astype(BF16)
    logits_t = _dot_nt_hi(rwt_ref[...], h2)
    mx = jnp.max(logits_t, axis=0, keepdims=True)
    ex = jnp.exp(logits_t - mx)
    afft_ref[...] = ex / jnp.sum(ex, axis=0, keepdims=True)
    logits = _dot_hi(h2, rw_ref[...])
    mx = jnp.max(logits, axis=-1, keepdims=True)
    ex = jnp.exp(logits - mx)
    aff_ref[...] = ex / jnp.sum(ex, axis=-1, keepdims=True)


def out_proj(x, hf, hb, o, yb, yc, yd, out_g, w_out, post_g, g1, ffn_g, sc2, sh2, router_w):
    b, s, d = x.shape
    tm = min(512, s)
    e = router_w.shape[1]
    hm = jnp.asarray(np.kron(np.eye(GROUP_W // HEAD_DIM), np.ones((HEAD_DIM, HEAD_DIM))) / HEAD_DIM, F32)
    row = lambda w: pl.BlockSpec((None, tm, w), lambda i, j: (i, j, 0))
    full = lambda a: pl.BlockSpec(a.shape, lambda i, j: (0,) * a.ndim)
    vec = lambda: pl.BlockSpec((None, 1, d), lambda i, j: (i, 0, 0))
    r1 = lambda a: a.reshape(1, -1)
    wob = w_out.astype(BF16)
    rwt = router_w.T
    return pl.pallas_call(
        _outproj_kernel,
        grid=(b, s // tm),
        in_specs=[row(d)] + [row(GROUP_W)] * 6 + [full(r1(out_g)), full(hm), full(wob),
                                                  full(r1(post_g)), vec(), full(r1(ffn_g)), vec(), vec(),
                                                  full(router_w), full(rwt)],
        out_specs=[row(d), row(d), row(e), pl.BlockSpec((None, e, tm), lambda i, j: (i, 0, j))],
        out_shape=[jax.ShapeDtypeStruct((b, s, d), F32), jax.ShapeDtypeStruct((b, s, d), BF16),
                   jax.ShapeDtypeStruct((b, s, e), F32), jax.ShapeDtypeStruct((b, e, s), F32)],
        compiler_params=_params("parallel", "parallel"),
        name="out_proj",
    )(x, hf, hb, o, yb, yc, yd, r1(out_g), hm, wob, r1(post_g), g1.reshape(b, 1, d), r1(ffn_g),
      sc2.reshape(b, 1, d), sh2.reshape(b, 1, d), router_w, rwt)


def _lane_cumsum(x):
    n = x.shape[-1]
    lane = lax.broadcasted_iota(jnp.int32, x.shape, x.ndim - 1)
    shift = 1
    while shift < n:
        x = x + jnp.where(lane >= shift, pltpu.roll(x, shift, x.ndim - 1), 0.0)
        shift *= 2
    return x


def _select_kernel(aff_ref, u_ref, ps_ref, cb_ref, *, cap):
    aff = aff_ref[...]
    capf = float(cap)

    def body(i, bits):
        cand = bits | (jnp.int32(1) << (30 - i))
        cnt = jnp.sum(jnp.where(aff >= pltpu.bitcast(cand, F32), 1.0, 0.0), axis=-1, keepdims=True)
        return jnp.where(cnt >= capf, cand, bits)

    bits = lax.fori_loop(0, 31, body, jnp.zeros((aff.shape[0], 1), jnp.int32))
    thr = pltpu.bitcast(bits, F32)
    gt = aff > thr
    eq = aff == thr
    n_gt = jnp.sum(jnp.where(gt, 1.0, 0.0), axis=-1, keepdims=True)
    eqf = jnp.where(eq, 1.0, 0.0)
    rank_eq = _lane_cumsum(eqf) - eqf
    sel = gt | (eq & (rank_eq < capf - n_gt))
    self_ = jnp.where(sel, 1.0, 0.0)
    pos = _lane_cumsum(self_) - self_
    ps_ref[...] = jnp.where(sel, pos, -1.0).astype(jnp.int32)
    cb_ref[...] = _dot(self_.astype(BF16), u_ref[...]).astype(jnp.int32)


def ec_select(aff_t, cap):
    b, e, s = aff_t.shape
    nt = s // TOKEN_TILE
    assert nt + 1 <= LANES
    tok = np.arange(s)[:, None]
    u = jnp.asarray(tok < (np.arange(LANES)[None, :] * TOKEN_TILE), BF16)
    return pl.pallas_call(
        functools.partial(_select_kernel, cap=cap),
        grid=(b,),
        in_specs=[pl.BlockSpec((None, e, s), lambda i: (i, 0, 0)),
                  pl.BlockSpec((s, LANES), lambda i: (0, 0))],
        out_specs=[pl.BlockSpec((None, e, s), lambda i: (i, 0, 0)),
                   pl.BlockSpec((None, e, LANES), lambda i: (i, 0, 0))],
        out_shape=[jax.ShapeDtypeStruct((b, e, s), jnp.int32),
                   jax.ShapeDtypeStruct((b, e, LANES), jnp.int32)],
        compiler_params=_params("parallel"),
        name="ec_select",
    )(aff_t, u)


def _gather_kernel(cb_ref, ps_ref, h_ref, o_ref, acc_ref, *, n_tiles, n_blocks):
    bi, ei = pl.program_id(0), pl.program_id(1)
    base = (bi * N_EXPERTS + ei) * LANES
    slot = lax.broadcasted_iota(jnp.int32, (SLOT_BLOCK, TOKEN_TILE), 0)

    def block(j, carry):
        first = j * SLOT_BLOCK

        def bounds(t, c):
            return (c[0] + (cb_ref[base + t + 1] <= first).astype(jnp.int32),
                    c[1] + (cb_ref[base + t] < first + SLOT_BLOCK).astype(jnp.int32))

        t_lo, t_hi = lax.fori_loop(0, n_tiles, bounds, (jnp.int32(0), jnp.int32(0)))
        acc_ref[...] = jnp.zeros_like(acc_ref)

        def tile(t, c):
            off = pl.multiple_of(t * TOKEN_TILE, TOKEN_TILE)
            ps = ps_ref[pl.ds(t, 1), :]
            onehot = jnp.where(ps == slot + first, 1.0, 0.0).astype(BF16)
            acc_ref[...] += _dot(onehot, h_ref[pl.ds(off, TOKEN_TILE), :])
            return c

        lax.fori_loop(t_lo, t_hi, tile, 0)
        o_ref[pl.ds(pl.multiple_of(first, SLOT_BLOCK), SLOT_BLOCK), :] = acc_ref[...].astype(o_ref.dtype)
        return carry

    lax.fori_loop(0, n_blocks, block, 0)


def ec_gather(h, ps, cb, cap):
    b, s, d = h.shape
    e = ps.shape[1]
    nt = s // TOKEN_TILE
    grid_spec = pltpu.PrefetchScalarGridSpec(
        num_scalar_prefetch=1,
        grid=(b, e),
        in_specs=[pl.BlockSpec((None, None, nt, TOKEN_TILE), lambda i, j, cb: (i, j, 0, 0)),
                  pl.BlockSpec((None, s, d), lambda i, j, cb: (i, 0, 0))],
        out_specs=pl.BlockSpec((None, None, cap, d), lambda i, j, cb: (i, j, 0, 0)),
        scratch_shapes=[pltpu.VMEM((SLOT_BLOCK, d), F32)],
    )
    return pl.pallas_call(
        functools.partial(_gather_kernel, n_tiles=nt, n_blocks=cap // SLOT_BLOCK),
        grid_spec=grid_spec,
        out_shape=jax.ShapeDtypeStruct((b, e, cap, d), BF16),
        compiler_params=_params("parallel", "arbitrary"),
        name="ec_gather",
    )(cb.reshape(-1), ps.reshape(b, e, nt, TOKEN_TILE), h)


def _ffn_kernel(x_ref, wg_ref, wu_ref, wd_ref, o_ref, acc_ref, *, n_f):
    f = pl.program_id(1)

    @pl.when(f == 0)
    def _():
        acc_ref[...] = jnp.zeros_like(acc_ref)

    bsz, cap, d = x_ref.shape
    x = x_ref[...].reshape(bsz * cap, d)
    a = _dot(x, wg_ref[...].astype(BF16))
    up = _dot(x, wu_ref[...].astype(BF16))
    act = (a * _sigmoid(a) * up).astype(BF16)
    acc_ref[...] += _dot(act, wd_ref[...].astype(BF16))

    @pl.when(f == n_f - 1)
    def _():
        o_ref[...] = acc_ref[...].reshape(bsz, cap, d).astype(o_ref.dtype)


def expert_ffn(xe, w_gate, w_up, w_down, layer):
    b, e, cap, d = xe.shape
    ff = w_gate.shape[-1]
    tf = min(512, ff)
    n_f = ff // tf
    return pl.pallas_call(
        functools.partial(_ffn_kernel, n_f=n_f),
        grid=(e, n_f),
        in_specs=[pl.BlockSpec((b, None, cap, d), lambda j, f: (0, j, 0, 0)),
                  pl.BlockSpec((None, None, d, tf), lambda j, f: (layer, j, 0, f)),
                  pl.BlockSpec((None, None, d, tf), lambda j, f: (layer, j, 0, f)),
                  pl.BlockSpec((None, None, tf, d), lambda j, f: (layer, j, f, 0))],
        out_specs=pl.BlockSpec((b, None, cap, d), lambda j, f: (0, j, 0, 0)),
        out_shape=jax.ShapeDtypeStruct((b, e, cap, d), BF16),
        scratch_shapes=[pltpu.VMEM((b * cap, d), F32)],
        compiler_params=_params("parallel", "arbitrary"),
        name="expert_ffn",
    )(xe, w_gate, w_up, w_down)


def _scatter_kernel(cb_ref, ps_ref, aff_ref, ye_ref, x_ref, pg_ref, g2_ref, o_ref, acc_ref,
                    *, n_sub):
    bi, ti, ei = pl.program_id(0), pl.program_id(1), pl.program_id(2)

    @pl.when(ei == 0)
    def _():
        acc_ref[...] = jnp.zeros_like(acc_ref)

    elane = lax.broadcasted_iota(jnp.int32, (TOKEN_TILE, N_EXPERTS), 1)
    slot = lax.broadcasted_iota(jnp.int32, (TOKEN_TILE, SLOT_BLOCK), 1)
    for sub in range(n_sub):
        base = (bi * N_EXPERTS + ei) * LANES + ti * n_sub + sub
        lo = cb_ref[base]
        hi = cb_ref[base + 1]

        @pl.when(hi > lo)
        def _(sub=sub, lo=lo, hi=hi):
            rows = pl.ds(sub * TOKEN_TILE, TOKEN_TILE)
            ps = jnp.sum(jnp.where(elane == ei, ps_ref[rows, :], 0), axis=-1, keepdims=True)
            gate = jnp.sum(jnp.where(elane == ei, aff_ref[rows, :], 0.0), axis=-1, keepdims=True)

            def body(j, carry):
                start = pl.multiple_of(j * SLOT_BLOCK, SLOT_BLOCK)
                onehot = jnp.where(ps == slot + start, 1.0, 0.0).astype(BF16)
                acc_ref[rows, :] += gate * _dot(onehot, ye_ref[pl.ds(start, SLOT_BLOCK), :])
                return carry

            lax.fori_loop(lo // SLOT_BLOCK, (hi - 1) // SLOT_BLOCK + 1, body, 0)

    @pl.when(ei == N_EXPERTS - 1)
    def _():
        y = acc_ref[...]
        o_ref[...] = x_ref[...] + g2_ref[...] * (_rms(y, y.shape[-1]) * pg_ref[...])


def ec_scatter(ye, ps_col, aff, cb, x, post_g, g2):
    b, e, cap, d = ye.shape
    s = x.shape[1]
    n_sub = min(4, s // TOKEN_TILE)
    tt = n_sub * TOKEN_TILE
    grid_spec = pltpu.PrefetchScalarGridSpec(
        num_scalar_prefetch=1,
        grid=(b, s // tt, e),
        in_specs=[pl.BlockSpec((None, tt, e), lambda i, t, j, cb: (i, t, 0)),
                  pl.BlockSpec((None, tt, e), lambda i, t, j, cb: (i, t, 0)),
                  pl.BlockSpec((None, None, cap, d), lambda i, t, j, cb: (i, j, 0, 0)),
                  pl.BlockSpec((None, tt, d), lambda i, t, j, cb: (i, t, 0)),
                  pl.BlockSpec((1, d), lambda i, t, j, cb: (0, 0)),
                  pl.BlockSpec((None, 1, d), lambda i, t, j, cb: (i, 0, 0))],
        out_specs=pl.BlockSpec((None, tt, d), lambda i, t, j, cb: (i, t, 0)),
        scratch_shapes=[pltpu.VMEM((tt, d), F32)],
    )
    return pl.pallas_call(
        functools.partial(_scatter_kernel, n_sub=n_sub),
        grid_spec=grid_spec,
        out_shape=jax.ShapeDtypeStruct((b, s, d), F32),
        compiler_params=_params("parallel", "parallel", "arbitrary"),
        name="ec_scatter",
    )(cb.reshape(-1), ps_col, aff, ye, x, post_g.reshape(1, d), g2.reshape(b, 1, d))


def kernel(x, c, positions, ada_w, ada_b, mix_pre_g, mix_post_g, ffn_pre_g, ffn_post_g, w_in, ml_gate_b, mla_q_norm, mla_kv_norm, mla_w_uq, mla_w_uk, mla_w_uv, hy_conv_w, hy_conv_b, hy_w1, hy_b1, hy_fr1, hy_w2, hy_b2, hy_fr2, hy_w3, hy_b3, hy_log_decay, hy_bias, sc_conv_w, mix_out_g, w_out, router_w, exp_w_gate, exp_w_up, exp_w_down):
    depth = ada_w.shape[0]
    b, s, d = x.shape
    cap = EC_CAPACITY * s // N_EXPERTS
    mod = ada_mod(c, ada_w, ada_b)
    tables = rope_tables(positions)
    for l in range(depth):
        sh1, sc1, g1, sh2, sc2, g2 = (mod[l, :, i * d:(i + 1) * d] for i in range(6))
        u = in_proj(x, mix_pre_g[l], sc1, sh1, w_in[l])
        hf, hb = mlstm(u["q"], u["v"], u["kT"], u["gT"], u["g"], ml_gate_b[l])
        qa, ka, va = mla_proj(u["cq"], u["ckv"], u["kr"], tables, mla_q_norm[l], mla_kv_norm[l],
                              mla_w_uq[l], mla_w_uk[l], mla_w_uv[l])
        y_b = flash_attention(qa, ka, va)
        x1, x2, z, y_d = conv_mixers(u["hy"], u["sc"], hy_conv_w[l], hy_conv_b[l], sc_conv_w[l])
        y_c = hyena_mixer(x1, x2, z, hy_w1[l], hy_b1[l], hy_fr1[l], hy_w2[l], hy_b2[l], hy_fr2[l],
                          hy_w3[l], hy_b3[l], hy_log_decay[l], hy_bias[l])
        xn, h2, aff, aff_t = out_proj(x, hf, hb, u["o"], y_b, y_c, y_d, mix_out_g[l], w_out[l],
                                      mix_post_g[l], g1, ffn_pre_g[l], sc2, sh2, router_w[l])
        ps, cb = ec_select(aff_t, cap)
        xe = ec_gather(h2, ps, cb, cap)
        ye = expert_ffn(xe, exp_w_gate, exp_w_up, exp_w_down, l)
        x = ec_scatter(ye, jnp.swapaxes(ps, 1, 2), aff, cb, xn, ffn_post_g[l], g2)
    return x
```

```python
import functools
import math

import numpy as np
import jax
import jax.numpy as jnp
from jax import lax
from jax.experimental import pallas as pl
from jax.experimental.pallas import tpu as pltpu

F32 = jnp.float32
BF16 = jnp.bfloat16
HIGHEST = lax.Precision.HIGHEST

GROUP_W = 256
HEAD_DIM = 64
N_HEADS = 4
ML_CHUNK = 128
MLA_Q_RANK = 224
MLA_KV_RANK = 128
MLA_NOPE = 64
MLA_ROPE = 32
ROPE_HALF = MLA_ROPE // 2
ROPE_THETA = 10000.0
HY_BANDS = 8
HY_FFN = 64
N_EXPERTS = 16
EC_CAPACITY = 2
NORM_EPS = 1e-6
LANES = 128
SLOT_BLOCK = 128
TOKEN_TILE = 256
VMEM_LIMIT = 56 * 1024 * 1024


def _params(*sem):
    return pltpu.CompilerParams(dimension_semantics=sem, vmem_limit_bytes=VMEM_LIMIT)


def _dot(a, b):
    return jnp.dot(a, b, preferred_element_type=F32)


def _dot_hi(a, b):
    return jnp.dot(a, b, precision=HIGHEST, preferred_element_type=F32)


def _dot_nt(a, b):
    return lax.dot_general(a, b, (((1,), (1,)), ((), ())), preferred_element_type=F32)


def _dot_nt_hi(a, b):
    return lax.dot_general(a, b, (((1,), (1,)), ((), ())), precision=HIGHEST,
                           preferred_element_type=F32)


def _split2(x):
    hi = x.astype(BF16)
    lo = (x - hi.astype(F32)).astype(BF16)
    return hi, lo


def _split3(x):
    hi = x.astype(BF16)
    r = x - hi.astype(F32)
    mid = r.astype(BF16)
    lo = (r - mid.astype(F32)).astype(BF16)
    return hi, mid, lo


def _lhs_x3(f):
    hi, lo = _split2(f)
    return jnp.concatenate([hi, lo, hi], axis=1)


def _dot_x3(f3, x):
    hi, lo = _split2(x)
    return _dot(f3, jnp.concatenate([hi, hi, lo], axis=0))


def _dot3(a, w_hi, w_lo):
    a_hi, a_lo = _split2(a)
    return _dot(a_hi, w_hi) + _dot(a_hi, w_lo) + _dot(a_lo, w_hi)


def _rms(x, n):
    ms = jnp.sum(x * x, axis=-1, keepdims=True) * (1.0 / n)
    return x * lax.rsqrt(ms + NORM_EPS)


def _log_sigmoid(x):
    return jnp.minimum(x, 0.0) - jnp.log(1.0 + jnp.exp(-jnp.abs(x)))


def _sigmoid(x):
    return 1.0 / (1.0 + jnp.exp(-x))


def _ada_kernel(c_ref, w_ref, b_ref, o_ref):
    c = c_ref[...]
    cs = c * _sigmoid(c)
    o_ref[...] = _dot_hi(cs, w_ref[...]) + b_ref[...]


def ada_mod(c, ada_w, ada_b):
    depth, d, n6 = ada_w.shape
    b = c.shape[0]
    bp = 8
    cp = jnp.zeros((bp, d), F32).at[:b].set(c)
    tn = 1536
    out = pl.pallas_call(
        _ada_kernel,
        grid=(depth, n6 // tn),
        in_specs=[pl.BlockSpec((bp, d), lambda l, j: (0, 0)),
                  pl.BlockSpec((None, d, tn), lambda l, j: (l, 0, j)),
                  pl.BlockSpec((None, 1, tn), lambda l, j: (l, 0, j))],
        out_specs=pl.BlockSpec((None, bp, tn), lambda l, j: (l, 0, j)),
        out_shape=jax.ShapeDtypeStruct((depth, bp, n6), F32),
        compiler_params=_params("parallel", "parallel"),
        name="ada_mod",
    )(cp, ada_w, ada_b.reshape(depth, 1, n6))
    return out[:, :b]


_U_COLS = (("q", 256), ("v", 256), ("o", 256), ("g", 128), ("cq", 256), ("ckv", 128),
           ("kr", 128), ("hy", 768), ("sc", 768))
_U_TOTAL = sum(w for _, w in _U_COLS)
_UT_ROWS = 256 + 16


def _inproj_kernel(x_ref, gain_ref, sc_ref, sh_ref, w_ref, wt_ref,
                   q_ref, v_ref, o_ref, g_ref, cq_ref, ckv_ref, kr_ref, hy_ref, scu_ref,
                   kt_ref, gt_ref):
    x = x_ref[...]
    d = x.shape[-1]
    h = _rms(x, d) * gain_ref[...] * (1.0 + sc_ref[...]) + sh_ref[...]
    hb = h.astype(BF16)
    u = _dot(hb, w_ref[...])
    off = 0
    for ref, (_, width) in zip((q_ref, v_ref, o_ref, g_ref, cq_ref, ckv_ref, kr_ref, hy_ref, scu_ref),
                               _U_COLS):
        ref[...] = u[:, off:off + width]
        off += width
    ut = _dot_nt(wt_ref[...], hb)
    kt_ref[...] = ut[:256]
    gt_ref[...] = ut[256:]


def in_proj(x, gain, scale, shift, w_in):
    b, s, d = x.shape
    tm = min(512, s)
    cuts = np.cumsum([0, 256, 256, 256, 256, 16, MLA_Q_RANK, MLA_KV_RANK, MLA_ROPE, 768, 768])
    wq, wk, wv, wo, wg, wcq, wckv, wkr, why, wsc = (w_in[:, cuts[i]:cuts[i + 1]] for i in range(10))
    pad = lambda w, n: jnp.pad(w, ((0, 0), (0, n - w.shape[1])))
    wkr_p = jnp.pad(wkr, ((0, 0), (MLA_NOPE, LANES - MLA_NOPE - MLA_ROPE)))
    w1 = jnp.concatenate([wq, wv, wo, pad(wg, 128), pad(wcq, 256), wckv, wkr_p, why, wsc],
                         axis=1).astype(BF16)
    w2t = jnp.concatenate([wk, wg], axis=1).T.astype(BF16)
    row = lambda nm, w: pl.BlockSpec((None, tm, w), lambda i, j: (i, j, 0))
    out_shapes = [jax.ShapeDtypeStruct((b, s, w), F32) for _, w in _U_COLS]
    out_shapes += [jax.ShapeDtypeStruct((b, 256, s), F32), jax.ShapeDtypeStruct((b, 16, s), F32)]
    out_specs = [row(nm, w) for nm, w in _U_COLS]
    out_specs += [pl.BlockSpec((None, 256, tm), lambda i, j: (i, 0, j)),
                  pl.BlockSpec((None, 16, tm), lambda i, j: (i, 0, j))]
    vec = lambda: pl.BlockSpec((None, 1, d), lambda i, j: (i, 0, 0))
    outs = pl.pallas_call(
        _inproj_kernel,
        grid=(b, s // tm),
        in_specs=[pl.BlockSpec((None, tm, d), lambda i, j: (i, j, 0)),
                  pl.BlockSpec((1, d), lambda i, j: (0, 0)),
                  vec(), vec(),
                  pl.BlockSpec((d, _U_TOTAL), lambda i, j: (0, 0)),
                  pl.BlockSpec((_UT_ROWS, d), lambda i, j: (0, 0))],
        out_specs=out_specs,
        out_shape=out_shapes,
        compiler_params=_params("parallel", "parallel"),
        name="in_proj",
    )(x, gain.reshape(1, d), scale.reshape(b, 1, d), shift.reshape(b, 1, d), w1, w2t)
    names = [nm for nm, _ in _U_COLS] + ["kT", "gT"]
    return dict(zip(names, outs))


def _mlstm_dir(q, v, kt, gt, gc, c_ref, m_ref, base, rev):
    L = q.shape[0]
    r = lax.broadcasted_iota(jnp.int32, (L, L), 0)
    c = lax.broadcasted_iota(jnp.int32, (L, L), 1)
    tri = (c >= r) if rev else (c <= r)
    io, fo = (8, 12) if rev else (0, 4)
    logf_rows = _log_sigmoid(gt[fo:fo + 4, :])
    logf_cols = _log_sigmoid(gc)
    tri_b = jnp.where(tri, 1.0, 0.0).astype(BF16)
    a_cols = _dot(jnp.concatenate([tri_b] * 3, axis=1),
                  jnp.concatenate(_split3(logf_cols), axis=0))
    tri_t = (r >= c) if rev else (r <= c)
    tri_tb = jnp.where(tri_t, 1.0, 0.0).astype(BF16)
    a_rows = _dot(jnp.concatenate(_split3(_log_sigmoid(gt)), axis=1),
                  jnp.concatenate([tri_tb] * 3, axis=0))[fo:fo + 4, :]
    lane = lax.broadcasted_iota(jnp.int32, (L, LANES), 1)
    row128 = lax.broadcasted_iota(jnp.int32, (LANES, L), 0)
    scale = HEAD_DIM ** -0.5
    outs = []
    for pair in range(2):
        qp = q[:, pair * LANES:(pair + 1) * LANES]
        vp = v[:, pair * LANES:(pair + 1) * LANES]
        ktp = kt[pair * LANES:(pair + 1) * LANES, :] * scale
        pair_out = None
        for sub in range(2):
            h = pair * 2 + sub
            in_head = (row128 >= sub * HEAD_DIM) & (row128 < (sub + 1) * HEAD_DIM)
            kth = jnp.where(in_head, ktp, 0.0)
            vsh = vp if sub == 0 else pltpu.roll(vp, HEAD_DIM, 1)
            v_aug = jnp.where(lane < HEAD_DIM, vsh, jnp.where(lane == HEAD_DIM, 1.0, 0.0))
            v_aug_b = v_aug.astype(BF16)
            a_c = a_cols[:, fo + h:fo + h + 1]
            a_r = a_rows[h:h + 1, :]
            ig_r = gt[io + h:io + h + 1, :]
            a_end = jnp.sum(logf_rows[h:h + 1, :], axis=-1, keepdims=True)
            c_st = c_ref[base + h]
            m_st = m_ref[base + h][:, 0:1]
            d_log = jnp.where(tri, a_c - a_r + ig_r, -jnp.inf)
            inter = a_c + m_st
            m_t = jnp.maximum(inter, jnp.max(d_log, axis=-1, keepdims=True))
            qb = qp.astype(BF16)
            p = jnp.exp(d_log - m_t) * _dot(qb, kth.astype(BF16))
            sci = jnp.exp(inter - m_t)
            nd = _dot(p.astype(BF16), v_aug_b) + sci * _dot(qb, c_st.astype(BF16))
            den = nd[:, HEAD_DIM:HEAD_DIM + 1]
            out = nd / jnp.maximum(jnp.abs(den), jnp.exp(-m_t))
            w_st = a_end - a_r + ig_r
            m_loc = jnp.max(w_st, axis=-1, keepdims=True)
            ke = kth * jnp.exp(w_st - m_loc)
            c_loc = _dot(ke.astype(BF16), v_aug_b)
            m_new = jnp.maximum(a_end + m_st, m_loc)
            sp = jnp.exp(a_end + m_st - m_new)
            sl = jnp.exp(m_loc - m_new)
            c_ref[base + h] = sp * c_st + sl * c_loc
            m_ref[base + h] = jnp.broadcast_to(m_new, (1, LANES))
            if sub == 0:
                pair_out = out
            else:
                pair_out = jnp.where(lane < HEAD_DIM, pair_out, pltpu.roll(out, HEAD_DIM, 1))
        outs.append(pair_out)
    return jnp.concatenate(outs, axis=1)


def _mlstm_kernel(qf_ref, vf_ref, ktf_ref, gtf_ref, gcf_ref,
                  qb_ref, vb_ref, ktb_ref, gtb_ref, gcb_ref, brow_ref, bcol_ref,
                  hf_ref, hb_ref, c_ref, m_ref):
    @pl.when(pl.program_id(1) == 0)
    def _():
        c_ref[...] = jnp.zeros_like(c_ref)
        m_ref[...] = jnp.zeros_like(m_ref)

    bcol = bcol_ref[...]
    brow = brow_ref[...]
    hf_ref[...] = _mlstm_dir(qf_ref[...], vf_ref[...], ktf_ref[...], gtf_ref[...] + bcol,
                             gcf_ref[...] + brow, c_ref, m_ref, 0, False)
    hb_ref[...] = _mlstm_dir(qb_ref[...], vb_ref[...], ktb_ref[...], gtb_ref[...] + bcol,
                             gcb_ref[...] + brow, c_ref, m_ref, N_HEADS, True)


def mlstm(q, v, kt, gt, g, gate_b):
    b, s, w = q.shape
    L = ML_CHUNK
    nc = s // L
    bflat = gate_b.reshape(16)
    brow = jnp.zeros((1, LANES), F32).at[0, :16].set(bflat)
    bcol = bflat.reshape(16, 1)
    fw = lambda i, j: (i, j, 0)
    bw = lambda i, j: (i, nc - 1 - j, 0)
    fwt = lambda i, j: (i, 0, j)
    bwt = lambda i, j: (i, 0, nc - 1 - j)

    def specs(m, mt):
        return [pl.BlockSpec((None, L, w), m), pl.BlockSpec((None, L, w), m),
                pl.BlockSpec((None, w, L), mt), pl.BlockSpec((None, 16, L), mt),
                pl.BlockSpec((None, L, LANES), m)]

    return pl.pallas_call(
        _mlstm_kernel,
        grid=(b, nc),
        in_specs=specs(fw, fwt) + specs(bw, bwt) + [
            pl.BlockSpec((1, LANES), lambda i, j: (0, 0)),
            pl.BlockSpec((16, 1), lambda i, j: (0, 0))],
        out_specs=[pl.BlockSpec((None, L, w), fw), pl.BlockSpec((None, L, w), bw)],
        out_shape=[jax.ShapeDtypeStruct((b, s, w), F32)] * 2,
        scratch_shapes=[pltpu.VMEM((2 * N_HEADS, LANES, LANES), F32),
                        pltpu.VMEM((2 * N_HEADS, 1, LANES), F32)],
        compiler_params=_params("parallel", "arbitrary"),
        name="mlstm",
    )(q, v, kt, gt, g, q, v, kt, gt, g, brow, bcol)


def _rope_table_kernel(pos_ref, post_ref, inv_ref, invt_ref, cos_ref, sin_ref, cost_ref, sint_ref):
    ang = pos_ref[...].astype(F32) * inv_ref[...]
    cos_ref[...] = jnp.cos(ang)
    sin_ref[...] = jnp.sin(ang)
    ang_t = invt_ref[...] * post_ref[...].astype(F32)
    cost_ref[...] = jnp.cos(ang_t)
    sint_ref[...] = jnp.sin(ang_t)


def rope_tables(positions):
    b, s = positions.shape
    tm = min(512, s)
    inv = ROPE_THETA ** (-jnp.arange(ROPE_HALF, dtype=F32) / ROPE_HALF)
    inv_row = jnp.zeros((1, LANES), F32).at[0, MLA_NOPE:MLA_NOPE + ROPE_HALF].set(inv)
    inv_row = inv_row.at[0, MLA_NOPE + ROPE_HALF:MLA_NOPE + MLA_ROPE].set(inv)
    spec = pl.BlockSpec((None, tm, LANES), lambda i, j: (i, j, 0))
    spec_t = pl.BlockSpec((None, LANES, tm), lambda i, j: (i, 0, j))
    return pl.pallas_call(
        _rope_table_kernel,
        grid=(b, s // tm),
        in_specs=[pl.BlockSpec((None, tm, 1), lambda i, j: (i, j, 0)),
                  pl.BlockSpec((None, 1, tm), lambda i, j: (i, 0, j)),
                  pl.BlockSpec((1, LANES), lambda i, j: (0, 0)),
                  pl.BlockSpec((LANES, 1), lambda i, j: (0, 0))],
        out_specs=[spec, spec, spec_t, spec_t],
        out_shape=[jax.ShapeDtypeStruct((b, s, LANES), F32)] * 2
        + [jax.ShapeDtypeStruct((b, LANES, s), F32)] * 2,
        compiler_params=_params("parallel", "parallel"),
        name="rope_tables",
    )(positions.reshape(b, s, 1), positions.reshape(b, 1, s), inv_row, inv_row.reshape(LANES, 1))


def _mla_proj_kernel(cq_ref, ckv_ref, kr_ref, cos_ref, sin_ref, cost_ref, sint_ref, qg_ref, kvg_ref,
                     wqt_ref, wk_ref, wvt_ref, qt_ref, k_ref, vt_ref):
    cos = cos_ref[...]
    sin = sin_ref[...]
    lane = lax.broadcasted_iota(jnp.int32, cos.shape, 1)
    x1 = (lane >= MLA_NOPE) & (lane < MLA_NOPE + ROPE_HALF)
    x2 = (lane >= MLA_NOPE + ROPE_HALF) & (lane < MLA_NOPE + MLA_ROPE)
    kr = kr_ref[...]
    krr = (kr * cos + pltpu.roll(kr, LANES - ROPE_HALF, 1) * jnp.where(x1, -sin, 0.0)
           + pltpu.roll(kr, ROPE_HALF, 1) * jnp.where(x2, sin, 0.0))

    cqn = (_rms(cq_ref[...], MLA_Q_RANK) * qg_ref[...]).astype(BF16)
    ckvn = (_rms(ckv_ref[...], MLA_KV_RANK) * kvg_ref[...]).astype(BF16)
    kn = _dot(ckvn, wk_ref[...])
    for h in range(N_HEADS):
        sl = slice(h * LANES, (h + 1) * LANES)
        k_ref[:, sl] = (kn[:, sl] + krr).astype(BF16)
    vt_ref[...] = _dot_nt(wvt_ref[...], ckvn).astype(BF16)

    cos_t = cost_ref[...]
    sin_t = sint_ref[...]
    row = lax.broadcasted_iota(jnp.int32, cos_t.shape, 0)
    x1t = (row >= MLA_NOPE) & (row < MLA_NOPE + ROPE_HALF)
    x2t = (row >= MLA_NOPE + ROPE_HALF) & (row < MLA_NOPE + MLA_ROPE)
    sin_at = jnp.where(x1t, -sin_t, 0.0)
    sin_bt = jnp.where(x2t, sin_t, 0.0)
    qt = _dot_nt(wqt_ref[...], cqn)
    scale = (MLA_NOPE + MLA_ROPE) ** -0.5 * math.log2(math.e)
    for h in range(N_HEADS):
        sl = slice(h * LANES, (h + 1) * LANES)
        xs = qt[sl, :]
        roped = (xs * cos_t + pltpu.roll(xs, LANES - ROPE_HALF, 0) * sin_at
                 + pltpu.roll(xs, ROPE_HALF, 0) * sin_bt)
        qt_ref[sl, :] = (roped * scale).astype(BF16)


def mla_proj(cq, ckv, kr, tables, q_norm, kv_norm, w_uq, w_uk, w_uv):
    b, s, _ = cq.shape
    tm = min(512, s)
    dqk = MLA_NOPE + MLA_ROPE
    cos, sin, cos_t, sin_t = tables
    wq = w_uq.reshape(MLA_Q_RANK, N_HEADS, dqk)
    wq = jnp.pad(wq, ((0, 256 - MLA_Q_RANK), (0, 0), (0, LANES - dqk))).reshape(256, N_HEADS * LANES)
    wk = w_uk.reshape(MLA_KV_RANK, N_HEADS, MLA_NOPE)
    wk = jnp.pad(wk, ((0, 0), (0, 0), (0, LANES - MLA_NOPE))).reshape(MLA_KV_RANK, N_HEADS * LANES)
    qg = jnp.pad(q_norm, (0, 256 - MLA_Q_RANK)).reshape(1, 256)
    kvg = kv_norm.reshape(1, MLA_KV_RANK)
    row = lambda w: pl.BlockSpec((None, tm, w), lambda i, j: (i, j, 0))
    col = lambda w: pl.BlockSpec((None, w, tm), lambda i, j: (i, 0, j))
    full = lambda a: pl.BlockSpec(a.shape, lambda i, j: (0,) * a.ndim)
    wqt, wkb, wvt = wq.T.astype(BF16), wk.astype(BF16), w_uv.T.astype(BF16)
    return pl.pallas_call(
        _mla_proj_kernel,
        grid=(b, s // tm),
        in_specs=[row(256), row(128), row(128), row(128), row(128), col(128), col(128),
                  full(qg), full(kvg), full(wqt), full(wkb), full(wvt)],
        out_specs=[col(512), row(512), col(256)],
        out_shape=[jax.ShapeDtypeStruct((b, 512, s), BF16), jax.ShapeDtypeStruct((b, s, 512), BF16),
                   jax.ShapeDtypeStruct((b, 256, s), BF16)],
        compiler_params=_params("parallel", "parallel"),
        name="mla_proj",
    )(cq, ckv, kr, cos, sin, cos_t, sin_t, qg, kvg, wqt, wkb, wvt)


def _flash_kernel(qt_ref, k_ref, vt_ref, o_ref, *, tk):
    s = k_ref.shape[0]
    tq = qt_ref.shape[1]
    nk = s // tk
    qts = [qt_ref[sub * LANES:(sub + 1) * LANES, :] for sub in range(2)]

    def scores(t):
        off = pl.multiple_of(t * tk, tk)
        return [_dot(k_ref[pl.ds(off, tk), sub * LANES:(sub + 1) * LANES], qts[sub])
                for sub in range(2)]

    def update(t, sc, stats):
        off = pl.multiple_of(t * tk, tk)
        new = []
        for sub in range(2):
            m, l, acc = stats[sub]
            vt = vt_ref[sub * HEAD_DIM:(sub + 1) * HEAD_DIM, pl.ds(off, tk)]
            m_new = jnp.maximum(m, jnp.max(sc[sub], axis=0, keepdims=True))
            alpha = jnp.exp2(m - m_new)
            p = jnp.exp2(sc[sub] - m_new)
            l = alpha * l + jnp.sum(p, axis=0, keepdims=True)
            acc = alpha * acc + _dot(vt, p.astype(BF16))
            new.append((m_new, l, acc))
        return new

    def body(u, carry):
        sc_a, stats = carry
        ta = 2 * u
        sc_b = scores(ta + 1)
        stats = update(ta, sc_a, stats)
        sc_a = scores(jnp.minimum(ta + 2, nk - 1))
        stats = update(ta + 1, sc_b, stats)
        return sc_a, stats

    init = [(jnp.full((1, tq), -jnp.inf, F32), jnp.zeros((1, tq), F32),
             jnp.zeros((HEAD_DIM, tq), F32)) for _ in range(2)]
    _, fin = lax.fori_loop(0, nk // 2, body, (scores(0), init))
    out_t = jnp.concatenate([acc / l for (_, l, acc) in fin], axis=0)
    o_ref[...] = out_t.T


def flash_attention(qt, k, vt):
    b, _, s = qt.shape
    tq = min(256, s)
    tk = min(512, s)
    return pl.pallas_call(
        functools.partial(_flash_kernel, tk=tk),
        grid=(b, 2, s // tq),
        in_specs=[pl.BlockSpec((None, 256, tq), lambda i, p, j: (i, p, j)),
                  pl.BlockSpec((None, s, 256), lambda i, p, j: (i, 0, p)),
                  pl.BlockSpec((None, LANES, s), lambda i, p, j: (i, p, 0))],
        out_specs=pl.BlockSpec((None, tq, LANES), lambda i, p, j: (i, j, p)),
        out_shape=jax.ShapeDtypeStruct((b, s, 256), F32),
        compiler_params=_params("parallel", "parallel", "parallel"),
        name="flash_attention",
    )(qt, k, vt)


def _shifted(x, prev_row, next_row, first, last):
    tm = x.shape[0]
    row = lax.broadcasted_iota(jnp.int32, x.shape, 0)
    prev_row = jnp.where(first, 0.0, prev_row)
    next_row = jnp.where(last, 0.0, next_row)
    xm1 = jnp.where(row == 0, prev_row, pltpu.roll(x, 1, 0))
    xp1 = jnp.where(row == tm - 1, next_row, pltpu.roll(x, tm - 1, 0))
    return xm1, xp1


def _conv_kernel(hy_ref, hyp_ref, hyn_ref, sc_ref, scp_ref, scn_ref, hw_ref, hb_ref, sw_ref,
                 x1_ref, x2_ref, z_ref, yd_ref):
    j = pl.program_id(1)
    first = j == 0
    last = j == pl.num_programs(1) - 1
    x = hy_ref[...]
    xm1, xp1 = _shifted(x, hyp_ref[7:8, :], hyn_ref[0:1, :], first, last)
    hw = hw_ref[...]
    proj = xm1 * hw[0:1] + x * hw[1:2] + xp1 * hw[2:3] + hb_ref[...]
    x1_ref[...] = proj[:, 0:GROUP_W]
    x2_ref[...] = proj[:, GROUP_W:2 * GROUP_W]
    z_ref[...] = proj[:, 2 * GROUP_W:]
    su = sc_ref[...]
    g = GROUP_W
    prod = su[:, g:2 * g] * su[:, 2 * g:]
    pprev = scp_ref[7:8, g:2 * g] * scp_ref[7:8, 2 * g:]
    pnext = scn_ref[0:1, g:2 * g] * scn_ref[0:1, 2 * g:]
    pm1, pp1 = _shifted(prod, pprev, pnext, first, last)
    sw = sw_ref[...]
    yd_ref[...] = su[:, :g] * (pm1 * sw[0:1] + prod * sw[1:2] + pp1 * sw[2:3])


def conv_mixers(hy_u, sc_u, hy_conv_w, hy_conv_b, sc_conv_w):
    b, s, w3 = hy_u.shape
    tm = min(512, s)
    nb8 = s // 8
    r8 = tm // 8
    main = pl.BlockSpec((None, tm, w3), lambda i, j: (i, j, 0))
    prev = pl.BlockSpec((None, 8, w3), lambda i, j: (i, jnp.maximum(j * r8 - 1, 0), 0))
    nxt = pl.BlockSpec((None, 8, w3), lambda i, j: (i, jnp.minimum((j + 1) * r8, nb8 - 1), 0))
    full = lambda a: pl.BlockSpec(a.shape, lambda i, j: (0,) * a.ndim)
    hw = hy_conv_w.T
    hb = hy_conv_b.reshape(1, w3)
    sw = sc_conv_w.T
    out = pl.BlockSpec((None, tm, GROUP_W), lambda i, j: (i, j, 0))
    return pl.pallas_call(
        _conv_kernel,
        grid=(b, s // tm),
        in_specs=[main, prev, nxt, main, prev, nxt, full(hw), full(hb), full(sw)],
        out_specs=[out] * 4,
        out_shape=[jax.ShapeDtypeStruct((b, s, GROUP_W), F32)] * 4,
        compiler_params=_params("parallel", "parallel"),
        name="conv_mixers",
    )(hy_u, hy_u, hy_u, sc_u, sc_u, sc_u, hw, hb, sw)


def _fft_dims(s):
    n = 2 * s
    lg = int(round(math.log2(n)))
    assert 1 << lg == n
    n1 = 1 << ((lg + 1) // 2)
    return n1, n // n1


def _filter_kernel(frow_ref, w1h_ref, w1l_ref, b1_ref, fr1_ref, w2h_ref, w2l_ref, b2_ref, fr2_ref,
                   w3h_ref, w3l_ref, b3_ref, ld_ref, k_ref, norm_ref, *, length):
    i = pl.program_id(0)
    tm = k_ref.shape[0]
    n = i * tm + lax.broadcasted_iota(jnp.int32, (tm, 1), 0)
    tt = jnp.where(n < length, n, 2 * length - 1 - n)
    t = tt.astype(F32) / length
    lane = lax.broadcasted_iota(jnp.int32, (tm, LANES), 1)
    ang = t * frow_ref[...]
    z = jnp.where(lane == 0, t,
                  jnp.where(lane <= HY_BANDS, jnp.sin(ang),
                            jnp.where(lane <= 2 * HY_BANDS, jnp.cos(ang), 0.0)))
    hid = jnp.sin(fr1_ref[...] * (_dot3(z, w1h_ref[...], w1l_ref[...]) + b1_ref[...]))
    hid = jnp.sin(fr2_ref[...] * (_dot3(hid, w2h_ref[...], w2l_ref[...]) + b2_ref[...]))
    filt = ((_dot3(hid, w3h_ref[...], w3l_ref[...]) + b3_ref[...])
            * jnp.exp(-t * jnp.exp(ld_ref[...])))
    k_ref[...] = filt

    @pl.when(i == 0)
    def _():
        norm_ref[...] = jnp.zeros_like(norm_ref)

    norm_ref[...] += jnp.sum(jnp.abs(filt), axis=0, keepdims=True)


def hyena_filter_taps(length, w1, b1, fr1, w2, b2, fr2, w3, b3, log_decay):
    tm = min(512, length)
    n_half = length // tm
    oc = 2 * GROUP_W
    bands = jnp.arange(1, HY_BANDS + 1, dtype=F32) * (2.0 * math.pi)
    frow = jnp.zeros((1, LANES), F32).at[0, 1:1 + HY_BANDS].set(bands)
    frow = frow.at[0, 1 + HY_BANDS:1 + 2 * HY_BANDS].set(bands)
    w1p = jnp.zeros((LANES, HY_FFN), F32).at[:1 + 2 * HY_BANDS].set(w1)
    bydir = lambda a: jnp.moveaxis(a.reshape(a.shape[0], 2, 2, GROUP_W), 2, 0).reshape(2, a.shape[0], oc)
    w3d, b3d, ldd = bydir(w3), bydir(b3.reshape(1, -1)), bydir(log_decay.reshape(1, -1))
    full = lambda a: pl.BlockSpec(a.shape, lambda i: (0,) * a.ndim)
    dirspec = lambda a: pl.BlockSpec((None,) + a.shape[1:], lambda i: (i // n_half, 0, 0))
    r1 = lambda a: a.reshape(1, -1)
    w1h, w1l = _split2(w1p)
    w2h, w2l = _split2(w2)
    w3h, w3l = _split2(w3d)
    return pl.pallas_call(
        functools.partial(_filter_kernel, length=length),
        grid=(2 * n_half,),
        in_specs=[full(frow), full(w1h), full(w1l), full(r1(b1)), full(r1(fr1)), full(w2h), full(w2l),
                  full(r1(b2)), full(r1(fr2)), dirspec(w3h), dirspec(w3l), dirspec(b3d), dirspec(ldd)],
        out_specs=[pl.BlockSpec((tm, oc), lambda i: (i, 0)), pl.BlockSpec((1, oc), lambda i: (0, 0))],
        out_shape=[jax.ShapeDtypeStruct((2 * length, oc), F32), jax.ShapeDtypeStruct((1, oc), F32)],
        compiler_params=_params("arbitrary"),
        name="hyena_filter",
    )(frow, w1h, w1l, r1(b1), r1(fr1), w2h, w2l, r1(b2), r1(fr2), w3h, w3l, b3d, ldd)


def _dft_consts(s):
    n1, n2 = _fft_dims(s)
    n = n1 * n2
    a1 = 2.0 * np.pi * np.outer(np.arange(n1), np.arange(n1)) / n1
    f1 = np.concatenate([np.cos(a1), -np.sin(a1)], axis=0)
    a2 = 2.0 * np.pi * np.outer(np.arange(n2), np.arange(n2)) / n2
    c2, s2 = np.cos(a2), np.sin(a2)
    f2_fwd = np.block([[c2, s2], [-s2, c2]])
    f2_inv = np.block([[c2, -s2], [s2, c2]])
    at = 2.0 * np.pi * np.outer(np.arange(n1), np.arange(n2)) / n
    tw = np.stack([np.cos(at), np.sin(at)], axis=0)[..., None]
    m1 = np.arange(n1 // 2)
    a3 = 2.0 * np.pi * np.outer(m1, np.arange(n1)) / n1
    f3 = np.concatenate([np.cos(a3), -np.sin(a3)], axis=1) / n
    f = lambda a: jnp.asarray(a, F32)
    x3 = lambda a: _lhs_x3(f(a))
    return dict(n1=n1, n2=n2, f1=x3(f1), f1_half=x3(f1[:, :n1 // 2]), f2_fwd=x3(f2_fwd),
                f2_inv=x3(f2_inv), tw=f(tw), f3=x3(f3))


_N2_TILE = 8


def _dft1_kernel(x_ref, f_ref, s_ref, o_ref, xs_ref):
    f = f_ref[...]
    scale = s_ref[...]
    for m in range(_N2_TILE):
        xs_ref[...] = x_ref[:, m, :]
        o_ref[:, m, :] = _dot_x3(f, xs_ref[...] * scale)


def dft_stage1(x, f1, scale_row):
    g, r, n2, ch = x.shape
    m = f1.shape[0]
    return pl.pallas_call(
        _dft1_kernel,
        grid=(g, n2 // _N2_TILE),
        in_specs=[pl.BlockSpec((None, r, _N2_TILE, ch), lambda i, j: (i, 0, j, 0)),
                  pl.BlockSpec(f1.shape, lambda i, j: (0, 0)),
                  pl.BlockSpec((1, ch), lambda i, j: (0, 0))],
        out_specs=pl.BlockSpec((None, m, _N2_TILE, ch), lambda i, j: (i, 0, j, 0)),
        out_shape=jax.ShapeDtypeStruct((g, m, n2, ch), F32),
        scratch_shapes=[pltpu.VMEM((r, ch), F32)],
        compiler_params=_params("parallel", "parallel"),
        name="dft_stage1",
    )(x, f1, scale_row)


def _twiddle(re, im, tc, ts, conj):
    if conj:
        return re * tc - im * ts, im * tc + re * ts
    return re * tc + im * ts, im * tc - re * ts


def _dft2_filter_kernel(a_ref, tw_ref, ff_ref, o_ref):
    n2 = a_ref.shape[2]
    br, bi = _twiddle(a_ref[0, 0], a_ref[1, 0], tw_ref[0, 0], tw_ref[1, 0], False)
    zz = _dot_x3(ff_ref[...], jnp.concatenate([br, bi], axis=0))
    o_ref[0, 0] = zz[:n2]
    o_ref[1, 0] = zz[n2:]


def dft_stage2_filter(a, consts):
    _, n1, n2, ch = a.shape
    return pl.pallas_call(
        _dft2_filter_kernel,
        grid=(n1,),
        in_specs=[pl.BlockSpec((2, 1, n2, ch), lambda k: (0, k, 0, 0)),
                  pl.BlockSpec((2, 1, n2, 1), lambda k: (0, k, 0, 0)),
                  pl.BlockSpec(consts["f2_fwd"].shape, lambda k: (0, 0))],
        out_specs=pl.BlockSpec((2, 1, n2, ch), lambda k: (0, k, 0, 0)),
        out_shape=jax.ShapeDtypeStruct(a.shape, F32),
        compiler_params=_params("parallel"),
        name="dft_stage2_filter",
    )(a, consts["tw"], consts["f2_fwd"])


def _dft2_conv_kernel(a_ref, tw_ref, kf_ref, ff_ref, fi_ref, o_ref):
    n2 = a_ref.shape[2]
    tc, ts = tw_ref[0, 0], tw_ref[1, 0]
    br, bi = _twiddle(a_ref[0, 0], a_ref[1, 0], tc, ts, False)
    zz = _dot_x3(ff_ref[...], jnp.concatenate([br, bi], axis=0))
    zr, zi = zz[:n2], zz[n2:]
    kr, ki = kf_ref[0, 0], kf_ref[1, 0]
    pr = zr * kr - zi * ki
    pi = zr * ki + zi * kr
    vv = _dot_x3(fi_ref[...], jnp.concatenate([pr, pi], axis=0))
    vr, vi = _twiddle(vv[:n2], vv[n2:], tc, ts, True)
    o_ref[0, 0] = vr
    o_ref[1, 0] = vi


def dft_stage2_conv(a, kf, order, consts):
    b, _, n1, n2, ch = a.shape
    blk = pl.BlockSpec((None, 2, 1, n2, ch), lambda i, k: (i, 0, k, 0, 0))
    mat = pl.BlockSpec(consts["f2_fwd"].shape, lambda i, k: (0, 0))
    return pl.pallas_call(
        _dft2_conv_kernel,
        grid=(b, n1),
        in_specs=[blk,
                  pl.BlockSpec((2, 1, n2, 1), lambda i, k: (0, k, 0, 0)),
                  pl.BlockSpec((2, 1, n2, ch), lambda i, k: (0, k, 0, order)),
                  mat, mat],
        out_specs=blk,
        out_shape=jax.ShapeDtypeStruct(a.shape, F32),
        compiler_params=_params("parallel", "parallel"),
        name="dft_stage2_conv",
    )(a, consts["tw"], kf, consts["f2_fwd"], consts["f2_inv"])


def _dft3_kernel(v_ref, f_ref, gate_ref, z_ref, bias_ref, o_ref, vs_ref, ys_ref):
    f = f_ref[...]
    for m in range(_N2_TILE):
        vs_ref[...] = v_ref[:, m, :]
        ys_ref[:, m, :] = _dot_x3(f, vs_ref[...])
    o_ref[...] = gate_ref[...] * (ys_ref[...] + z_ref[...] * bias_ref[...])


def dft_stage3_gate(v, f3, gate, z, bias_row):
    b, m, n2, ch = v.shape
    r = f3.shape[0]
    row = pl.BlockSpec((None, r, _N2_TILE, ch), lambda i, j: (i, 0, j, 0))
    return pl.pallas_call(
        _dft3_kernel,
        grid=(b, n2 // _N2_TILE),
        in_specs=[pl.BlockSpec((None, m, _N2_TILE, ch), lambda i, j: (i, 0, j, 0)),
                  pl.BlockSpec(f3.shape, lambda i, j: (0, 0)),
                  row, row, pl.BlockSpec((1, ch), lambda i, j: (0, 0))],
        out_specs=row,
        out_shape=jax.ShapeDtypeStruct((b, r, n2, ch), F32),
        scratch_shapes=[pltpu.VMEM((m, ch), F32), pltpu.VMEM((r, _N2_TILE, ch), F32)],
        compiler_params=_params("parallel", "parallel"),
        name="dft_stage3_gate",
    )(v, f3, gate, z, bias_row)


def hyena_mixer(x1, x2, z, hy_w1, hy_b1, hy_fr1, hy_w2, hy_b2, hy_fr2, hy_w3, hy_b3, hy_log_decay,
                hy_bias):
    b, s, ch = z.shape
    consts = _dft_consts(s)
    n1, n2 = consts["n1"], consts["n2"]
    oc = 2 * ch
    taps, norm = hyena_filter_taps(s, hy_w1, hy_b1, hy_fr1, hy_w2, hy_b2, hy_fr2, hy_w3, hy_b3,
                                   hy_log_decay)
    ka = dft_stage1(taps.reshape(1, n1, n2, oc), consts["f1"], 1.0 / norm)
    kf = dft_stage2_filter(ka.reshape(2, n1, n2, oc), consts)
    half = n1 // 2
    view = lambda t: t.reshape(b, half, n2, ch)
    ones = jnp.ones((1, ch), F32)
    f1_half = consts["f1_half"]
    cur = view(z)
    for order, gate in enumerate((x1, x2)):
        a = dft_stage1(cur, f1_half, ones)
        v = dft_stage2_conv(a.reshape(b, 2, n1, n2, ch), kf, order, consts)
        cur = dft_stage3_gate(v.reshape(b, 2 * n1, n2, ch), consts["f3"], view(gate), cur,
                              hy_bias[order].reshape(1, ch))
    return cur.reshape(b, s, ch)


def _outproj_kernel(x_ref, hf_ref, hb_ref, o_ref, yb_ref, yc_ref, yd_ref, og_ref, hm_ref, wout_ref,
                    pg_ref, g1_ref, fg_ref, sc2_ref, sh2_ref, rwt_ref,
                    xn_ref, h2_ref, afft_ref):
    hm = hm_ref[...]
    y_a = _sigmoid(o_ref[...]) * (hf_ref[...] + hb_ref[...])
    acc = None
    for idx, y in enumerate((y_a, yb_ref[...], yc_ref[...], yd_ref[...])):
        ms = _dot(jnp.concatenate(_split2(y * y), axis=1), hm)
        yn = y * lax.rsqrt(ms + NORM_EPS) * og_ref[:, idx * GROUP_W:(idx + 1) * GROUP_W]
        part = _dot(yn.astype(BF16), wout_ref[idx * GROUP_W:(idx + 1) * GROUP_W, :])
        acc = part if acc is None else acc + part
    d = acc.shape[-1]
    xn = x_ref[...] + g1_ref[...] * (_rms(acc, d) * pg_ref[...])
    xn_ref[...] = xn
    h2 = _rms(xn, d) * fg_ref[...] * (1.0 + sc2_ref[...]) + sh2_ref[...]
    h2_ref[...] = h2.astype(BF16)
    logits_t = _dot_nt_hi(rwt_ref[...], h2)
    mx = jnp.max(logits_t, axis=0, keepdims=True)
    ex = jnp.exp(logits_t - mx)
    afft_ref[...] = ex / jnp.sum(ex, axis=0, keepdims=True)


def out_proj(x, hf, hb, o, yb, yc, yd, out_g, w_out, post_g, g1, ffn_g, sc2, sh2, router_w):
    b, s, d = x.shape
    tm = min(512, s)
    e = router_w.shape[1]
    hm1 = np.kron(np.eye(GROUP_W // HEAD_DIM), np.ones((HEAD_DIM, HEAD_DIM))) / HEAD_DIM
    hm = jnp.asarray(np.concatenate([hm1, hm1], axis=0), BF16)
    row = lambda w: pl.BlockSpec((None, tm, w), lambda i, j: (i, j, 0))
    full = lambda a: pl.BlockSpec(a.shape, lambda i, j: (0,) * a.ndim)
    vec = lambda: pl.BlockSpec((None, 1, d), lambda i, j: (i, 0, 0))
    r1 = lambda a: a.reshape(1, -1)
    wob = w_out.astype(BF16)
    rwt = router_w.T
    return pl.pallas_call(
        _outproj_kernel,
        grid=(b, s // tm),
        in_specs=[row(d)] + [row(GROUP_W)] * 6 + [full(r1(out_g)), full(hm), full(wob),
                                                  full(r1(post_g)), vec(), full(r1(ffn_g)), vec(), vec(),
                                                  full(rwt)],
        out_specs=[row(d), row(d), pl.BlockSpec((None, e, tm), lambda i, j: (i, 0, j))],
        out_shape=[jax.ShapeDtypeStruct((b, s, d), F32), jax.ShapeDtypeStruct((b, s, d), BF16),
                   jax.ShapeDtypeStruct((b, e, s), F32)],
        compiler_params=_params("parallel", "parallel"),
        name="out_proj",
    )(x, hf, hb, o, yb, yc, yd, r1(out_g), hm, wob, r1(post_g), g1.reshape(b, 1, d), r1(ffn_g),
      sc2.reshape(b, 1, d), sh2.reshape(b, 1, d), rwt)


def _lane_cumsum(x):
    n = x.shape[-1]
    lane = lax.broadcasted_iota(jnp.int32, x.shape, x.ndim - 1)
    shift = 1
    while shift < n:
        x = x + jnp.where(lane >= shift, pltpu.roll(x, shift, x.ndim - 1), 0.0)
        shift *= 2
    return x


def _select_kernel(aff_ref, u_ref, ps_ref, cb_ref, *, cap):
    aff = aff_ref[...]
    capf = float(cap)

    def body(i, bits):
        cand = bits | (jnp.int32(1) << (30 - i))
        cnt = jnp.sum(jnp.where(aff >= pltpu.bitcast(cand, F32), 1.0, 0.0), axis=-1, keepdims=True)
        return jnp.where(cnt >= capf, cand, bits)

    bits = lax.fori_loop(0, 31, body, jnp.zeros((aff.shape[0], 1), jnp.int32))
    thr = pltpu.bitcast(bits, F32)
    gt = aff > thr
    eq = aff == thr
    n_gt = jnp.sum(jnp.where(gt, 1.0, 0.0), axis=-1, keepdims=True)
    eqf = jnp.where(eq, 1.0, 0.0)
    rank_eq = _lane_cumsum(eqf) - eqf
    sel = gt | (eq & (rank_eq < capf - n_gt))
    self_ = jnp.where(sel, 1.0, 0.0)
    pos = _lane_cumsum(self_) - self_
    ps_ref[...] = jnp.where(sel, pos, -1.0).astype(jnp.int32)
    cb_ref[...] = _dot(self_.astype(BF16), u_ref[...]).astype(jnp.int32)


def ec_select(aff_t, cap):
    b, e, s = aff_t.shape
    nt = s // TOKEN_TILE
    assert nt + 1 <= LANES
    tok = np.arange(s)[:, None]
    u = jnp.asarray(tok < (np.arange(LANES)[None, :] * TOKEN_TILE), BF16)
    return pl.pallas_call(
        functools.partial(_select_kernel, cap=cap),
        grid=(b,),
        in_specs=[pl.BlockSpec((None, e, s), lambda i: (i, 0, 0)),
                  pl.BlockSpec((s, LANES), lambda i: (0, 0))],
        out_specs=[pl.BlockSpec((None, e, s), lambda i: (i, 0, 0)),
                   pl.BlockSpec((None, e, LANES), lambda i: (i, 0, 0))],
        out_shape=[jax.ShapeDtypeStruct((b, e, s), jnp.int32),
                   jax.ShapeDtypeStruct((b, e, LANES), jnp.int32)],
        compiler_params=_params("parallel"),
        name="ec_select",
    )(aff_t, u)


def _gather_kernel(cb_ref, ps_ref, h_ref, o_ref, acc_ref, *, n_tiles, n_blocks):
    bi, ei = pl.program_id(0), pl.program_id(1)
    base = (bi * N_EXPERTS + ei) * LANES
    slot = lax.broadcasted_iota(jnp.int32, (SLOT_BLOCK, TOKEN_TILE), 0)

    def block(j, carry):
        first = j * SLOT_BLOCK

        def bounds(t, c):
            return (c[0] + (cb_ref[base + t + 1] <= first).astype(jnp.int32),
                    c[1] + (cb_ref[base + t] < first + SLOT_BLOCK).astype(jnp.int32))

        t_lo, t_hi = lax.fori_loop(0, n_tiles, bounds, (jnp.int32(0), jnp.int32(0)))
        acc_ref[...] = jnp.zeros_like(acc_ref)

        def tile(t, c):
            off = pl.multiple_of(t * TOKEN_TILE, TOKEN_TILE)
            ps = ps_ref[pl.ds(t, 1), :]
            onehot = jnp.where(ps == slot + first, 1.0, 0.0).astype(BF16)
            acc_ref[...] += _dot(onehot, h_ref[pl.ds(off, TOKEN_TILE), :])
            return c

        lax.fori_loop(t_lo, t_hi, tile, 0)
        o_ref[pl.ds(pl.multiple_of(first, SLOT_BLOCK), SLOT_BLOCK), :] = acc_ref[...].astype(o_ref.dtype)
        return carry

    lax.fori_loop(0, n_blocks, block, 0)


def ec_gather(h, ps, cb, cap):
    b, s, d = h.shape
    e = ps.shape[1]
    nt = s // TOKEN_TILE
    grid_spec = pltpu.PrefetchScalarGridSpec(
        num_scalar_prefetch=1,
        grid=(b, e),
        in_specs=[pl.BlockSpec((None, None, nt, TOKEN_TILE), lambda i, j, cb: (i, j, 0, 0)),
                  pl.BlockSpec((None, s, d), lambda i, j, cb: (i, 0, 0))],
        out_specs=pl.BlockSpec((None, None, cap, d), lambda i, j, cb: (i, j, 0, 0)),
        scratch_shapes=[pltpu.VMEM((SLOT_BLOCK, d), F32)],
    )
    return pl.pallas_call(
        functools.partial(_gather_kernel, n_tiles=nt, n_blocks=cap // SLOT_BLOCK),
        grid_spec=grid_spec,
        out_shape=jax.ShapeDtypeStruct((b, e, cap, d), BF16),
        compiler_params=_params("parallel", "arbitrary"),
        name="ec_gather",
    )(cb.reshape(-1), ps.reshape(b, e, nt, TOKEN_TILE), h)


def _ffn_kernel(x_ref, wg_ref, wu_ref, wd_ref, o_ref, acc_ref, *, n_f):
    f = pl.program_id(1)

    @pl.when(f == 0)
    def _():
        acc_ref[...] = jnp.zeros_like(acc_ref)

    bsz, cap, d = x_ref.shape
    x = x_ref[...].reshape(bsz * cap, d)
    a = _dot(x, wg_ref[...].astype(BF16))
    up = _dot(x, wu_ref[...].astype(BF16))
    act = (a * _sigmoid(a) * up).astype(BF16)
    acc_ref[...] += _dot(act, wd_ref[...].astype(BF16))

    @pl.when(f == n_f - 1)
    def _():
        o_ref[...] = acc_ref[...].reshape(bsz, cap, d).astype(o_ref.dtype)


def expert_ffn(xe, w_gate, w_up, w_down, layer):
    b, e, cap, d = xe.shape
    ff = w_gate.shape[-1]
    tf = min(512, ff)
    n_f = ff // tf
    return pl.pallas_call(
        functools.partial(_ffn_kernel, n_f=n_f),
        grid=(e, n_f),
        in_specs=[pl.BlockSpec((b, None, cap, d), lambda j, f: (0, j, 0, 0)),
                  pl.BlockSpec((None, None, d, tf), lambda j, f: (layer, j, 0, f)),
                  pl.BlockSpec((None, None, d, tf), lambda j, f: (layer, j, 0, f)),
                  pl.BlockSpec((None, None, tf, d), lambda j, f: (layer, j, f, 0))],
        out_specs=pl.BlockSpec((b, None, cap, d), lambda j, f: (0, j, 0, 0)),
        out_shape=jax.ShapeDtypeStruct((b, e, cap, d), BF16),
        scratch_shapes=[pltpu.VMEM((b * cap, d), F32)],
        compiler_params=_params("parallel", "arbitrary"),
        name="expert_ffn",
    )(xe, w_gate, w_up, w_down)


def _scatter_kernel(cb_ref, ps_ref, aff_ref, ye_ref, x_ref, pg_ref, g2_ref, o_ref, acc_ref,
                    *, n_sub):
    bi, ti, ei = pl.program_id(0), pl.program_id(1), pl.program_id(2)

    @pl.when(ei == 0)
    def _():
        acc_ref[...] = jnp.zeros_like(acc_ref)

    elane = lax.broadcasted_iota(jnp.int32, (TOKEN_TILE, N_EXPERTS), 1)
    slot = lax.broadcasted_iota(jnp.int32, (TOKEN_TILE, SLOT_BLOCK), 1)
    for sub in range(n_sub):
        base = (bi * N_EXPERTS + ei) * LANES + ti * n_sub + sub
        lo = cb_ref[base]
        hi = cb_ref[base + 1]

        @pl.when(hi > lo)
        def _(sub=sub, lo=lo, hi=hi):
            rows = pl.ds(sub * TOKEN_TILE, TOKEN_TILE)
            ps = jnp.sum(jnp.where(elane == ei, ps_ref[rows, :], 0), axis=-1, keepdims=True)
            gate = jnp.sum(jnp.where(elane == ei, aff_ref[rows, :], 0.0), axis=-1, keepdims=True)

            def body(j, carry):
                start = pl.multiple_of(j * SLOT_BLOCK, SLOT_BLOCK)
                onehot = jnp.where(ps == slot + start, 1.0, 0.0).astype(BF16)
                acc_ref[rows, :] += gate * _dot(onehot, ye_ref[pl.ds(start, SLOT_BLOCK), :])
                return carry

            lax.fori_loop(lo // SLOT_BLOCK, (hi - 1) // SLOT_BLOCK + 1, body, 0)

    @pl.when(ei == N_EXPERTS - 1)
    def _():
        y = acc_ref[...]
        o_ref[...] = x_ref[...] + g2_ref[...] * (_rms(y, y.shape[-1]) * pg_ref[...])


def ec_scatter(ye, ps_col, aff, cb, x, post_g, g2):
    b, e, cap, d = ye.shape
    s = x.shape[1]
    n_sub = min(4, s // TOKEN_TILE)
    tt = n_sub * TOKEN_TILE
    grid_spec = pltpu.PrefetchScalarGridSpec(
        num_scalar_prefetch=1,
        grid=(b, s // tt, e),
        in_specs=[pl.BlockSpec((None, tt, e), lambda i, t, j, cb: (i, t, 0)),
                  pl.BlockSpec((None, tt, e), lambda i, t, j, cb: (i, t, 0)),
                  pl.BlockSpec((None, None, cap, d), lambda i, t, j, cb: (i, j, 0, 0)),
                  pl.BlockSpec((None, tt, d), lambda i, t, j, cb: (i, t, 0)),
                  pl.BlockSpec((1, d), lambda i, t, j, cb: (0, 0)),
                  pl.BlockSpec((None, 1, d), lambda i, t, j, cb: (i, 0, 0))],
        out_specs=pl.BlockSpec((None, tt, d), lambda i, t, j, cb: (i, t, 0)),
        scratch_shapes=[pltpu.VMEM((tt, d), F32)],
    )
    return pl.pallas_call(
        functools.partial(_scatter_kernel, n_sub=n_sub),
        grid_spec=grid_spec,
        out_shape=jax.ShapeDtypeStruct((b, s, d), F32),
        compiler_params=_params("parallel", "parallel", "arbitrary"),
        name="ec_scatter",
    )(cb.reshape(-1), ps_col, aff, ye, x, post_g.reshape(1, d), g2.reshape(b, 1, d))


def kernel(x, c, positions, ada_w, ada_b, mix_pre_g, mix_post_g, ffn_pre_g, ffn_post_g, w_in, ml_gate_b, mla_q_norm, mla_kv_norm, mla_w_uq, mla_w_uk, mla_w_uv, hy_conv_w, hy_conv_b, hy_w1, hy_b1, hy_fr1, hy_w2, hy_b2, hy_fr2, hy_w3, hy_b3, hy_log_decay, hy_bias, sc_conv_w, mix_out_g, w_out, router_w, exp_w_gate, exp_w_up, exp_w_down):
    depth = ada_w.shape[0]
    b, s, d = x.shape
    cap = EC_CAPACITY * s // N_EXPERTS
    mod = ada_mod(c, ada_w, ada_b)
    tables = rope_tables(positions)
    for l in range(depth):
        sh1, sc1, g1, sh2, sc2, g2 = (mod[l, :, i * d:(i + 1) * d] for i in range(6))
        u = in_proj(x, mix_pre_g[l], sc1, sh1, w_in[l])
        hf, hb = mlstm(u["q"], u["v"], u["kT"], u["gT"], u["g"], ml_gate_b[l])
        qa, ka, va = mla_proj(u["cq"], u["ckv"], u["kr"], tables, mla_q_norm[l], mla_kv_norm[l],
                              mla_w_uq[l], mla_w_uk[l], mla_w_uv[l])
        y_b = flash_attention(qa, ka, va)
        x1, x2, z, y_d = conv_mixers(u["hy"], u["sc"], hy_conv_w[l], hy_conv_b[l], sc_conv_w[l])
        y_c = hyena_mixer(x1, x2, z, hy_w1[l], hy_b1[l], hy_fr1[l], hy_w2[l], hy_b2[l], hy_fr2[l],
                          hy_w3[l], hy_b3[l], hy_log_decay[l], hy_bias[l])
        xn, h2, aff_t = out_proj(x, hf, hb, u["o"], y_b, y_c, y_d, mix_out_g[l], w_out[l],
                                      mix_post_g[l], g1, ffn_pre_g[l], sc2, sh2, router_w[l])
        ps, cb = ec_select(aff_t, cap)
        xe = ec_gather(h2, ps, cb, cap)
        ye = expert_ffn(xe, exp_w_gate, exp_w_up, exp_w_down, l)
        x = ec_scatter(ye, jnp.swapaxes(ps, 1, 2), jnp.swapaxes(aff_t, 1, 2), cb, xn, ffn_post_g[l], g2)
    return x
```

```python
import functools
import math

import numpy as np
import jax
import jax.numpy as jnp
from jax import lax
from jax.experimental import pallas as pl
from jax.experimental.pallas import tpu as pltpu

F32 = jnp.float32
BF16 = jnp.bfloat16
HIGHEST = lax.Precision.HIGHEST

GROUP_W = 256
HEAD_DIM = 64
N_HEADS = 4
ML_CHUNK = 128
MLA_Q_RANK = 224
MLA_KV_RANK = 128
MLA_NOPE = 64
MLA_ROPE = 32
ROPE_HALF = MLA_ROPE // 2
ROPE_THETA = 10000.0
HY_BANDS = 8
HY_FFN = 64
N_EXPERTS = 16
EC_CAPACITY = 2
NORM_EPS = 1e-6
LANES = 128
BF16_ROWS = 16
SLOT_BLOCK = 128
TOKEN_TILE = 256
VMEM_LIMIT = 56 * 1024 * 1024


def _params(*sem):
    return pltpu.CompilerParams(dimension_semantics=sem, vmem_limit_bytes=VMEM_LIMIT)


def _dot(a, b):
    return jnp.dot(a, b, preferred_element_type=F32)


def _dot_hi(a, b):
    return jnp.dot(a, b, precision=HIGHEST, preferred_element_type=F32)


def _dot_nt(a, b):
    return lax.dot_general(a, b, (((1,), (1,)), ((), ())), preferred_element_type=F32)


def _dot_nt_hi(a, b):
    return lax.dot_general(a, b, (((1,), (1,)), ((), ())), precision=HIGHEST,
                           preferred_element_type=F32)


def _split2(x):
    hi = x.astype(BF16)
    lo = (x - hi.astype(F32)).astype(BF16)
    return hi, lo


def _split3(x):
    hi = x.astype(BF16)
    r = x - hi.astype(F32)
    mid = r.astype(BF16)
    lo = (r - mid.astype(F32)).astype(BF16)
    return hi, mid, lo


def _lhs_x3(f):
    hi, lo = _split2(f)
    return jnp.concatenate([hi, lo, hi], axis=1)


def _dot_x3(f3, x):
    hi, lo = _split2(x)
    return _dot(f3, jnp.concatenate([hi, hi, lo], axis=0))


def _dot3(a, w_hi, w_lo):
    a_hi, a_lo = _split2(a)
    return _dot(a_hi, w_hi) + _dot(a_hi, w_lo) + _dot(a_lo, w_hi)


def _rms(x, n):
    ms = jnp.sum(x * x, axis=-1, keepdims=True) * (1.0 / n)
    return x * lax.rsqrt(ms + NORM_EPS)


def _log_sigmoid(x):
    return jnp.minimum(x, 0.0) - jnp.log(1.0 + jnp.exp(-jnp.abs(x)))


def _sigmoid(x):
    return 1.0 / (1.0 + jnp.exp(-x))


def _ada_kernel(c_ref, w_ref, b_ref, o_ref):
    c = c_ref[...]
    cs = c * _sigmoid(c)
    o_ref[...] = _dot_hi(cs, w_ref[...]) + b_ref[...]


def ada_mod(c, ada_w, ada_b):
    depth, d, n6 = ada_w.shape
    b = c.shape[0]
    bp = 8
    cp = jnp.zeros((bp, d), F32).at[:b].set(c)
    tn = 1536
    out = pl.pallas_call(
        _ada_kernel,
        grid=(depth, n6 // tn),
        in_specs=[pl.BlockSpec((bp, d), lambda l, j: (0, 0)),
                  pl.BlockSpec((None, d, tn), lambda l, j: (l, 0, j)),
                  pl.BlockSpec((None, 1, tn), lambda l, j: (l, 0, j))],
        out_specs=pl.BlockSpec((None, bp, tn), lambda l, j: (l, 0, j)),
        out_shape=jax.ShapeDtypeStruct((depth, bp, n6), F32),
        compiler_params=_params("parallel", "parallel"),
        name="ada_mod",
    )(cp, ada_w, ada_b.reshape(depth, 1, n6))
    return out[:, :b]


_U_COLS = (("q", 256), ("v", 256), ("o", 256), ("g", 128), ("cq", 256), ("ckv", 128),
           ("kr", 128), ("hy", 768), ("sc", 768))
_U_TOTAL = sum(w for _, w in _U_COLS)
_UT_ROWS = 256 + 16


def _inproj_kernel(x_ref, gain_ref, sc_ref, sh_ref, w_ref, wt_ref,
                   q_ref, v_ref, o_ref, g_ref, cq_ref, ckv_ref, kr_ref, hy_ref, scu_ref,
                   kt_ref, gt_ref):
    x = x_ref[...]
    d = x.shape[-1]
    h = _rms(x, d) * gain_ref[...] * (1.0 + sc_ref[...]) + sh_ref[...]
    hb = h.astype(BF16)
    u = _dot(hb, w_ref[...])
    off = 0
    for ref, (_, width) in zip((q_ref, v_ref, o_ref, g_ref, cq_ref, ckv_ref, kr_ref, hy_ref, scu_ref),
                               _U_COLS):
        ref[...] = u[:, off:off + width]
        off += width
    ut = _dot_nt(wt_ref[...], hb)
    kt_ref[...] = ut[:256]
    gt_ref[...] = ut[256:]


def in_proj(x, gain, scale, shift, w_in):
    b, s, d = x.shape
    tm = min(512, s)
    cuts = np.cumsum([0, 256, 256, 256, 256, 16, MLA_Q_RANK, MLA_KV_RANK, MLA_ROPE, 768, 768])
    wq, wk, wv, wo, wg, wcq, wckv, wkr, why, wsc = (w_in[:, cuts[i]:cuts[i + 1]] for i in range(10))
    pad = lambda w, n: jnp.pad(w, ((0, 0), (0, n - w.shape[1])))
    wkr_p = jnp.pad(wkr, ((0, 0), (MLA_NOPE, LANES - MLA_NOPE - MLA_ROPE)))
    w1 = jnp.concatenate([wq, wv, wo, pad(wg, 128), pad(wcq, 256), wckv, wkr_p, why, wsc],
                         axis=1).astype(BF16)
    w2t = jnp.concatenate([wk, wg], axis=1).T.astype(BF16)
    row = lambda nm, w: pl.BlockSpec((None, tm, w), lambda i, j: (i, j, 0))
    out_shapes = [jax.ShapeDtypeStruct((b, s, w), F32) for _, w in _U_COLS]
    out_shapes += [jax.ShapeDtypeStruct((b, 256, s), F32), jax.ShapeDtypeStruct((b, 16, s), F32)]
    out_specs = [row(nm, w) for nm, w in _U_COLS]
    out_specs += [pl.BlockSpec((None, 256, tm), lambda i, j: (i, 0, j)),
                  pl.BlockSpec((None, 16, tm), lambda i, j: (i, 0, j))]
    vec = lambda: pl.BlockSpec((None, 1, d), lambda i, j: (i, 0, 0))
    outs = pl.pallas_call(
        _inproj_kernel,
        grid=(b, s // tm),
        in_specs=[pl.BlockSpec((None, tm, d), lambda i, j: (i, j, 0)),
                  pl.BlockSpec((1, d), lambda i, j: (0, 0)),
                  vec(), vec(),
                  pl.BlockSpec((d, _U_TOTAL), lambda i, j: (0, 0)),
                  pl.BlockSpec((_UT_ROWS, d), lambda i, j: (0, 0))],
        out_specs=out_specs,
        out_shape=out_shapes,
        compiler_params=_params("parallel", "parallel"),
        name="in_proj",
    )(x, gain.reshape(1, d), scale.reshape(b, 1, d), shift.reshape(b, 1, d), w1, w2t)
    names = [nm for nm, _ in _U_COLS] + ["kT", "gT"]
    return dict(zip(names, outs))


def _mlstm_dir(q, v, kt, gt, gc, c_ref, m_ref, base, rev):
    L = q.shape[0]
    r = lax.broadcasted_iota(jnp.int32, (L, L), 0)
    c = lax.broadcasted_iota(jnp.int32, (L, L), 1)
    tri = (c >= r) if rev else (c <= r)
    io, fo = (8, 12) if rev else (0, 4)
    logf_rows = _log_sigmoid(gt[fo:fo + 4, :])
    logf_cols = _log_sigmoid(gc)
    tri_b = jnp.where(tri, 1.0, 0.0).astype(BF16)
    a_cols = _dot(jnp.concatenate([tri_b] * 3, axis=1),
                  jnp.concatenate(_split3(logf_cols), axis=0))
    tri_t = (r >= c) if rev else (r <= c)
    tri_tb = jnp.where(tri_t, 1.0, 0.0).astype(BF16)
    a_rows = _dot(jnp.concatenate(_split3(_log_sigmoid(gt)), axis=1),
                  jnp.concatenate([tri_tb] * 3, axis=0))[fo:fo + 4, :]
    lane = lax.broadcasted_iota(jnp.int32, (L, LANES), 1)
    row128 = lax.broadcasted_iota(jnp.int32, (LANES, L), 0)
    scale = HEAD_DIM ** -0.5
    outs = []
    for pair in range(2):
        qp = q[:, pair * LANES:(pair + 1) * LANES]
        vp = v[:, pair * LANES:(pair + 1) * LANES]
        ktp = kt[pair * LANES:(pair + 1) * LANES, :] * scale
        pair_out = None
        for sub in range(2):
            h = pair * 2 + sub
            in_head = (row128 >= sub * HEAD_DIM) & (row128 < (sub + 1) * HEAD_DIM)
            kth = jnp.where(in_head, ktp, 0.0)
            vsh = vp if sub == 0 else pltpu.roll(vp, HEAD_DIM, 1)
            v_aug = jnp.where(lane < HEAD_DIM, vsh, jnp.where(lane == HEAD_DIM, 1.0, 0.0))
            v_aug_b = v_aug.astype(BF16)
            a_c = a_cols[:, fo + h:fo + h + 1]
            a_r = a_rows[h:h + 1, :]
            ig_r = gt[io + h:io + h + 1, :]
            a_end = jnp.sum(logf_rows[h:h + 1, :], axis=-1, keepdims=True)
            c_st = c_ref[base + h]
            m_st = m_ref[base + h][:, 0:1]
            d_log = jnp.where(tri, a_c - a_r + ig_r, -jnp.inf)
            inter = a_c + m_st
            m_t = jnp.maximum(inter, jnp.max(d_log, axis=-1, keepdims=True))
            qb = qp.astype(BF16)
            p = jnp.exp(d_log - m_t) * _dot(qb, kth.astype(BF16))
            sci = jnp.exp(inter - m_t)
            nd = _dot(p.astype(BF16), v_aug_b) + sci * _dot(qb, c_st.astype(BF16))
            den = nd[:, HEAD_DIM:HEAD_DIM + 1]
            out = nd / jnp.maximum(jnp.abs(den), jnp.exp(-m_t))
            w_st = a_end - a_r + ig_r
            m_loc = jnp.max(w_st, axis=-1, keepdims=True)
            ke = kth * jnp.exp(w_st - m_loc)
            c_loc = _dot(ke.astype(BF16), v_aug_b)
            m_new = jnp.maximum(a_end + m_st, m_loc)
            sp = jnp.exp(a_end + m_st - m_new)
            sl = jnp.exp(m_loc - m_new)
            c_ref[base + h] = sp * c_st + sl * c_loc
            m_ref[base + h] = jnp.broadcast_to(m_new, (1, LANES))
            if sub == 0:
                pair_out = out
            else:
                pair_out = jnp.where(lane < HEAD_DIM, pair_out, pltpu.roll(out, HEAD_DIM, 1))
        outs.append(pair_out)
    return jnp.concatenate(outs, axis=1)


def _mlstm_kernel(qf_ref, vf_ref, ktf_ref, gtf_ref, gcf_ref,
                  qb_ref, vb_ref, ktb_ref, gtb_ref, gcb_ref, brow_ref, bcol_ref,
                  hf_ref, hb_ref, c_ref, m_ref):
    @pl.when(pl.program_id(1) == 0)
    def _():
        c_ref[...] = jnp.zeros_like(c_ref)
        m_ref[...] = jnp.zeros_like(m_ref)

    bcol = bcol_ref[...]
    brow = brow_ref[...]
    hf_ref[...] = _mlstm_dir(qf_ref[...], vf_ref[...], ktf_ref[...], gtf_ref[...] + bcol,
                             gcf_ref[...] + brow, c_ref, m_ref, 0, False)
    hb_ref[...] = _mlstm_dir(qb_ref[...], vb_ref[...], ktb_ref[...], gtb_ref[...] + bcol,
                             gcb_ref[...] + brow, c_ref, m_ref, N_HEADS, True)


def mlstm(q, v, kt, gt, g, gate_b):
    b, s, w = q.shape
    L = ML_CHUNK
    nc = s // L
    bflat = gate_b.reshape(16)
    brow = jnp.zeros((1, LANES), F32).at[0, :16].set(bflat)
    bcol = bflat.reshape(16, 1)
    fw = lambda i, j: (i, j, 0)
    bw = lambda i, j: (i, nc - 1 - j, 0)
    fwt = lambda i, j: (i, 0, j)
    bwt = lambda i, j: (i, 0, nc - 1 - j)

    def specs(m, mt):
        return [pl.BlockSpec((None, L, w), m), pl.BlockSpec((None, L, w), m),
                pl.BlockSpec((None, w, L), mt), pl.BlockSpec((None, 16, L), mt),
                pl.BlockSpec((None, L, LANES), m)]

    return pl.pallas_call(
        _mlstm_kernel,
        grid=(b, nc),
        in_specs=specs(fw, fwt) + specs(bw, bwt) + [
            pl.BlockSpec((1, LANES), lambda i, j: (0, 0)),
            pl.BlockSpec((16, 1), lambda i, j: (0, 0))],
        out_specs=[pl.BlockSpec((None, L, w), fw), pl.BlockSpec((None, L, w), bw)],
        out_shape=[jax.ShapeDtypeStruct((b, s, w), F32)] * 2,
        scratch_shapes=[pltpu.VMEM((2 * N_HEADS, LANES, LANES), F32),
                        pltpu.VMEM((2 * N_HEADS, 1, LANES), F32)],
        compiler_params=_params("parallel", "arbitrary"),
        name="mlstm",
    )(q, v, kt, gt, g, q, v, kt, gt, g, brow, bcol)


def _rope_table_kernel(pos_ref, post_ref, inv_ref, invt_ref, cos_ref, sin_ref, cost_ref, sint_ref):
    ang = pos_ref[...].astype(F32) * inv_ref[...]
    cos_ref[...] = jnp.cos(ang)
    sin_ref[...] = jnp.sin(ang)
    ang_t = invt_ref[...] * post_ref[...].astype(F32)
    cost_ref[...] = jnp.cos(ang_t)
    sint_ref[...] = jnp.sin(ang_t)


def rope_tables(positions):
    b, s = positions.shape
    tm = min(512, s)
    inv = ROPE_THETA ** (-jnp.arange(ROPE_HALF, dtype=F32) / ROPE_HALF)
    inv_row = jnp.zeros((1, LANES), F32).at[0, MLA_NOPE:MLA_NOPE + ROPE_HALF].set(inv)
    inv_row = inv_row.at[0, MLA_NOPE + ROPE_HALF:MLA_NOPE + MLA_ROPE].set(inv)
    spec = pl.BlockSpec((None, tm, LANES), lambda i, j: (i, j, 0))
    spec_t = pl.BlockSpec((None, LANES, tm), lambda i, j: (i, 0, j))
    return pl.pallas_call(
        _rope_table_kernel,
        grid=(b, s // tm),
        in_specs=[pl.BlockSpec((None, tm, 1), lambda i, j: (i, j, 0)),
                  pl.BlockSpec((None, 1, tm), lambda i, j: (i, 0, j)),
                  pl.BlockSpec((1, LANES), lambda i, j: (0, 0)),
                  pl.BlockSpec((LANES, 1), lambda i, j: (0, 0))],
        out_specs=[spec, spec, spec_t, spec_t],
        out_shape=[jax.ShapeDtypeStruct((b, s, LANES), F32)] * 2
        + [jax.ShapeDtypeStruct((b, LANES, s), F32)] * 2,
        compiler_params=_params("parallel", "parallel"),
        name="rope_tables",
    )(positions.reshape(b, s, 1), positions.reshape(b, 1, s), inv_row, inv_row.reshape(LANES, 1))


def _mla_proj_kernel(cq_ref, ckv_ref, kr_ref, cos_ref, sin_ref, cost_ref, sint_ref, qg_ref, kvg_ref,
                     wqt_ref, wk_ref, wvt_ref, qt_ref, k_ref, vt_ref):
    cos = cos_ref[...]
    sin = sin_ref[...]
    lane = lax.broadcasted_iota(jnp.int32, cos.shape, 1)
    x1 = (lane >= MLA_NOPE) & (lane < MLA_NOPE + ROPE_HALF)
    x2 = (lane >= MLA_NOPE + ROPE_HALF) & (lane < MLA_NOPE + MLA_ROPE)
    kr = kr_ref[...]
    krr = (kr * cos + pltpu.roll(kr, LANES - ROPE_HALF, 1) * jnp.where(x1, -sin, 0.0)
           + pltpu.roll(kr, ROPE_HALF, 1) * jnp.where(x2, sin, 0.0))

    cqn = (_rms(cq_ref[...], MLA_Q_RANK) * qg_ref[...]).astype(BF16)
    ckvn = (_rms(ckv_ref[...], MLA_KV_RANK) * kvg_ref[...]).astype(BF16)
    kn = _dot(ckvn, wk_ref[...])
    for h in range(N_HEADS):
        sl = slice(h * LANES, (h + 1) * LANES)
        k_ref[:, sl] = (kn[:, sl] + krr).astype(BF16)
    vt_ref[...] = _dot_nt(wvt_ref[...], ckvn).astype(BF16)

    cos_t = cost_ref[...]
    sin_t = sint_ref[...]
    row = lax.broadcasted_iota(jnp.int32, cos_t.shape, 0)
    x1t = (row >= MLA_NOPE) & (row < MLA_NOPE + ROPE_HALF)
    x2t = (row >= MLA_NOPE + ROPE_HALF) & (row < MLA_NOPE + MLA_ROPE)
    sin_at = jnp.where(x1t, -sin_t, 0.0)
    sin_bt = jnp.where(x2t, sin_t, 0.0)
    qt = _dot_nt(wqt_ref[...], cqn)
    scale = (MLA_NOPE + MLA_ROPE) ** -0.5 * math.log2(math.e)
    for h in range(N_HEADS):
        sl = slice(h * LANES, (h + 1) * LANES)
        xs = qt[sl, :]
        roped = (xs * cos_t + pltpu.roll(xs, LANES - ROPE_HALF, 0) * sin_at
                 + pltpu.roll(xs, ROPE_HALF, 0) * sin_bt)
        qt_ref[sl, :] = (roped * scale).astype(BF16)


def mla_proj(cq, ckv, kr, tables, q_norm, kv_norm, w_uq, w_uk, w_uv):
    b, s, _ = cq.shape
    tm = min(512, s)
    dqk = MLA_NOPE + MLA_ROPE
    cos, sin, cos_t, sin_t = tables
    wq = w_uq.reshape(MLA_Q_RANK, N_HEADS, dqk)
    wq = jnp.pad(wq, ((0, 256 - MLA_Q_RANK), (0, 0), (0, LANES - dqk))).reshape(256, N_HEADS * LANES)
    wk = w_uk.reshape(MLA_KV_RANK, N_HEADS, MLA_NOPE)
    wk = jnp.pad(wk, ((0, 0), (0, 0), (0, LANES - MLA_NOPE))).reshape(MLA_KV_RANK, N_HEADS * LANES)
    qg = jnp.pad(q_norm, (0, 256 - MLA_Q_RANK)).reshape(1, 256)
    kvg = kv_norm.reshape(1, MLA_KV_RANK)
    row = lambda w: pl.BlockSpec((None, tm, w), lambda i, j: (i, j, 0))
    col = lambda w: pl.BlockSpec((None, w, tm), lambda i, j: (i, 0, j))
    full = lambda a: pl.BlockSpec(a.shape, lambda i, j: (0,) * a.ndim)
    wqt, wkb, wvt = wq.T.astype(BF16), wk.astype(BF16), w_uv.T.astype(BF16)
    return pl.pallas_call(
        _mla_proj_kernel,
        grid=(b, s // tm),
        in_specs=[row(256), row(128), row(128), row(128), row(128), col(128), col(128),
                  full(qg), full(kvg), full(wqt), full(wkb), full(wvt)],
        out_specs=[col(512), row(512), col(256)],
        out_shape=[jax.ShapeDtypeStruct((b, 512, s), BF16), jax.ShapeDtypeStruct((b, s, 512), BF16),
                   jax.ShapeDtypeStruct((b, 256, s), BF16)],
        compiler_params=_params("parallel", "parallel"),
        name="mla_proj",
    )(cq, ckv, kr, cos, sin, cos_t, sin_t, qg, kvg, wqt, wkb, wvt)


def _flash_kernel(qt_ref, k_ref, vt_ref, o_ref, *, tk):
    s = k_ref.shape[0]
    tq = qt_ref.shape[1]
    nk = s // tk
    qts = [qt_ref[sub * LANES:(sub + 1) * LANES, :] for sub in range(2)]

    def scores(t):
        off = pl.multiple_of(t * tk, tk)
        return [_dot(k_ref[pl.ds(off, tk), sub * LANES:(sub + 1) * LANES], qts[sub])
                for sub in range(2)]

    def update(t, sc, stats):
        off = pl.multiple_of(t * tk, tk)
        new = []
        for sub in range(2):
            m, l, acc = stats[sub]
            vt = vt_ref[sub * HEAD_DIM:(sub + 1) * HEAD_DIM, pl.ds(off, tk)]
            m_new = jnp.maximum(m, jnp.max(sc[sub], axis=0, keepdims=True))
            alpha = jnp.exp2(m - m_new)
            p = jnp.exp2(sc[sub] - m_new)
            l = alpha * l + jnp.sum(p, axis=0, keepdims=True)
            acc = alpha * acc + _dot(vt, p.astype(BF16))
            new.append((m_new, l, acc))
        return new

    def body(u, carry):
        sc_a, stats = carry
        ta = 2 * u
        sc_b = scores(ta + 1)
        stats = update(ta, sc_a, stats)
        sc_a = scores(jnp.minimum(ta + 2, nk - 1))
        stats = update(ta + 1, sc_b, stats)
        return sc_a, stats

    init = [(jnp.full((1, tq), -jnp.inf, F32), jnp.zeros((1, tq), F32),
             jnp.zeros((HEAD_DIM, tq), F32)) for _ in range(2)]
    _, fin = lax.fori_loop(0, nk // 2, body, (scores(0), init))
    out_t = jnp.concatenate([acc / l for (_, l, acc) in fin], axis=0)
    o_ref[...] = out_t.T


def flash_attention(qt, k, vt):
    b, _, s = qt.shape
    tq = min(256, s)
    tk = min(512, s)
    return pl.pallas_call(
        functools.partial(_flash_kernel, tk=tk),
        grid=(b, 2, s // tq),
        in_specs=[pl.BlockSpec((None, 256, tq), lambda i, p, j: (i, p, j)),
                  pl.BlockSpec((None, s, 256), lambda i, p, j: (i, 0, p)),
                  pl.BlockSpec((None, LANES, s), lambda i, p, j: (i, p, 0))],
        out_specs=pl.BlockSpec((None, tq, LANES), lambda i, p, j: (i, j, p)),
        out_shape=jax.ShapeDtypeStruct((b, s, 256), F32),
        compiler_params=_params("parallel", "parallel", "parallel"),
        name="flash_attention",
    )(qt, k, vt)


def _shifted(x, prev_row, next_row, first, last):
    tm = x.shape[0]
    row = lax.broadcasted_iota(jnp.int32, x.shape, 0)
    prev_row = jnp.where(first, 0.0, prev_row)
    next_row = jnp.where(last, 0.0, next_row)
    xm1 = jnp.where(row == 0, prev_row, pltpu.roll(x, 1, 0))
    xp1 = jnp.where(row == tm - 1, next_row, pltpu.roll(x, tm - 1, 0))
    return xm1, xp1


def _conv_kernel(hy_ref, hyp_ref, hyn_ref, sc_ref, scp_ref, scn_ref, hw_ref, hb_ref, sw_ref,
                 x1_ref, x2_ref, z_ref, yd_ref):
    j = pl.program_id(1)
    first = j == 0
    last = j == pl.num_programs(1) - 1
    x = hy_ref[...]
    xm1, xp1 = _shifted(x, hyp_ref[7:8, :], hyn_ref[0:1, :], first, last)
    hw = hw_ref[...]
    proj = xm1 * hw[0:1] + x * hw[1:2] + xp1 * hw[2:3] + hb_ref[...]
    x1_ref[...] = proj[:, 0:GROUP_W]
    x2_ref[...] = proj[:, GROUP_W:2 * GROUP_W]
    z_ref[...] = proj[:, 2 * GROUP_W:]
    su = sc_ref[...]
    g = GROUP_W
    prod = su[:, g:2 * g] * su[:, 2 * g:]
    pprev = scp_ref[7:8, g:2 * g] * scp_ref[7:8, 2 * g:]
    pnext = scn_ref[0:1, g:2 * g] * scn_ref[0:1, 2 * g:]
    pm1, pp1 = _shifted(prod, pprev, pnext, first, last)
    sw = sw_ref[...]
    yd_ref[...] = su[:, :g] * (pm1 * sw[0:1] + prod * sw[1:2] + pp1 * sw[2:3])


def conv_mixers(hy_u, sc_u, hy_conv_w, hy_conv_b, sc_conv_w):
    b, s, w3 = hy_u.shape
    tm = min(512, s)
    nb8 = s // 8
    r8 = tm // 8
    main = pl.BlockSpec((None, tm, w3), lambda i, j: (i, j, 0))
    prev = pl.BlockSpec((None, 8, w3), lambda i, j: (i, jnp.maximum(j * r8 - 1, 0), 0))
    nxt = pl.BlockSpec((None, 8, w3), lambda i, j: (i, jnp.minimum((j + 1) * r8, nb8 - 1), 0))
    full = lambda a: pl.BlockSpec(a.shape, lambda i, j: (0,) * a.ndim)
    hw = hy_conv_w.T
    hb = hy_conv_b.reshape(1, w3)
    sw = sc_conv_w.T
    out = pl.BlockSpec((None, tm, GROUP_W), lambda i, j: (i, j, 0))
    return pl.pallas_call(
        _conv_kernel,
        grid=(b, s // tm),
        in_specs=[main, prev, nxt, main, prev, nxt, full(hw), full(hb), full(sw)],
        out_specs=[out] * 4,
        out_shape=[jax.ShapeDtypeStruct((b, s, GROUP_W), F32)] * 4,
        compiler_params=_params("parallel", "parallel"),
        name="conv_mixers",
    )(hy_u, hy_u, hy_u, sc_u, sc_u, sc_u, hw, hb, sw)


def _fft_dims(s):
    n = 2 * s
    lg = int(round(math.log2(n)))
    assert 1 << lg == n
    n1 = 1 << ((lg + 1) // 2)
    return n1, n // n1


def _filter_kernel(frow_ref, w1h_ref, w1l_ref, b1_ref, fr1_ref, w2h_ref, w2l_ref, b2_ref, fr2_ref,
                   w3h_ref, w3l_ref, b3_ref, ld_ref, k_ref, norm_ref, *, length):
    i = pl.program_id(0)
    tm = k_ref.shape[0]
    n = i * tm + lax.broadcasted_iota(jnp.int32, (tm, 1), 0)
    tt = jnp.where(n < length, n, 2 * length - 1 - n)
    t = tt.astype(F32) / length
    lane = lax.broadcasted_iota(jnp.int32, (tm, LANES), 1)
    ang = t * frow_ref[...]
    z = jnp.where(lane == 0, t,
                  jnp.where(lane <= HY_BANDS, jnp.sin(ang),
                            jnp.where(lane <= 2 * HY_BANDS, jnp.cos(ang), 0.0)))
    hid = jnp.sin(fr1_ref[...] * (_dot3(z, w1h_ref[...], w1l_ref[...]) + b1_ref[...]))
    hid = jnp.sin(fr2_ref[...] * (_dot3(hid, w2h_ref[...], w2l_ref[...]) + b2_ref[...]))
    filt = ((_dot3(hid, w3h_ref[...], w3l_ref[...]) + b3_ref[...])
            * jnp.exp(-t * jnp.exp(ld_ref[...])))
    k_ref[...] = filt

    @pl.when(i == 0)
    def _():
        norm_ref[...] = jnp.zeros_like(norm_ref)

    norm_ref[...] += jnp.sum(jnp.abs(filt), axis=0, keepdims=True)


def hyena_filter_taps(length, w1, b1, fr1, w2, b2, fr2, w3, b3, log_decay):
    tm = min(512, length)
    n_half = length // tm
    oc = 2 * GROUP_W
    bands = jnp.arange(1, HY_BANDS + 1, dtype=F32) * (2.0 * math.pi)
    frow = jnp.zeros((1, LANES), F32).at[0, 1:1 + HY_BANDS].set(bands)
    frow = frow.at[0, 1 + HY_BANDS:1 + 2 * HY_BANDS].set(bands)
    w1p = jnp.zeros((LANES, HY_FFN), F32).at[:1 + 2 * HY_BANDS].set(w1)
    bydir = lambda a: jnp.moveaxis(a.reshape(a.shape[0], 2, 2, GROUP_W), 2, 0).reshape(2, a.shape[0], oc)
    w3d, b3d, ldd = bydir(w3), bydir(b3.reshape(1, -1)), bydir(log_decay.reshape(1, -1))
    full = lambda a: pl.BlockSpec(a.shape, lambda i: (0,) * a.ndim)
    dirspec = lambda a: pl.BlockSpec((None,) + a.shape[1:], lambda i: (i // n_half, 0, 0))
    r1 = lambda a: a.reshape(1, -1)
    w1h, w1l = _split2(w1p)
    w2h, w2l = _split2(w2)
    w3h, w3l = _split2(w3d)
    return pl.pallas_call(
        functools.partial(_filter_kernel, length=length),
        grid=(2 * n_half,),
        in_specs=[full(frow), full(w1h), full(w1l), full(r1(b1)), full(r1(fr1)), full(w2h), full(w2l),
                  full(r1(b2)), full(r1(fr2)), dirspec(w3h), dirspec(w3l), dirspec(b3d), dirspec(ldd)],
        out_specs=[pl.BlockSpec((tm, oc), lambda i: (i, 0)), pl.BlockSpec((1, oc), lambda i: (0, 0))],
        out_shape=[jax.ShapeDtypeStruct((2 * length, oc), F32), jax.ShapeDtypeStruct((1, oc), F32)],
        compiler_params=_params("arbitrary"),
        name="hyena_filter",
    )(frow, w1h, w1l, r1(b1), r1(fr1), w2h, w2l, r1(b2), r1(fr2), w3h, w3l, b3d, ldd)


def _dft_consts(s):
    n1, n2 = _fft_dims(s)
    n = n1 * n2
    kh = n1 // 2 + 1
    kp = -(-kh // 4) * 4
    pad_rows = lambda a: np.concatenate([a, np.zeros((kp - kh, a.shape[1]))], axis=0)
    a1 = 2.0 * np.pi * np.outer(np.arange(kh), np.arange(n1)) / n1
    f1 = np.concatenate([pad_rows(np.cos(a1)), pad_rows(-np.sin(a1))], axis=0)
    a2 = 2.0 * np.pi * np.outer(np.arange(n2), np.arange(n2)) / n2
    c2, s2 = np.cos(a2), np.sin(a2)
    f2_fwd = np.block([[c2, s2], [-s2, c2]])
    f2_inv = np.block([[c2, -s2], [s2, c2]])
    at = 2.0 * np.pi * np.outer(np.arange(kh), np.arange(n2)) / n
    tw = np.stack([np.cos(at), np.sin(at)], axis=0)[..., None]
    m1 = np.arange(n1 // 2)
    a3 = 2.0 * np.pi * np.outer(m1, np.arange(kh)) / n1
    wgt = np.where((np.arange(kh) == 0) | (np.arange(kh) == n1 // 2), 1.0, 2.0) / n
    f3 = np.concatenate([pad_rows((np.cos(a3) * wgt).T).T, pad_rows((-np.sin(a3) * wgt).T).T],
                        axis=1)
    f = lambda a: jnp.asarray(a, F32)
    x3 = lambda a: _lhs_x3(f(a))
    return dict(n1=n1, n2=n2, kh=kh, kp=kp, f1=x3(f1), f1_half=x3(f1[:, :n1 // 2]), f2_fwd=x3(f2_fwd),
                f2_inv=x3(f2_inv), tw=f(tw), f3=x3(f3))


_N2_TILE = 8


def _dft1_kernel(x_ref, f_ref, s_ref, o_ref, xs_ref):
    f = f_ref[...]
    scale = s_ref[...]
    for m in range(_N2_TILE):
        xs_ref[...] = x_ref[:, m, :]
        o_ref[:, m, :] = _dot_x3(f, xs_ref[...] * scale)


def dft_stage1(x, f1, scale_row):
    g, r, n2, ch = x.shape
    m = f1.shape[0]
    return pl.pallas_call(
        _dft1_kernel,
        grid=(g, n2 // _N2_TILE),
        in_specs=[pl.BlockSpec((None, r, _N2_TILE, ch), lambda i, j: (i, 0, j, 0)),
                  pl.BlockSpec(f1.shape, lambda i, j: (0, 0)),
                  pl.BlockSpec((1, ch), lambda i, j: (0, 0))],
        out_specs=pl.BlockSpec((None, m, _N2_TILE, ch), lambda i, j: (i, 0, j, 0)),
        out_shape=jax.ShapeDtypeStruct((g, m, n2, ch), F32),
        scratch_shapes=[pltpu.VMEM((r, ch), F32)],
        compiler_params=_params("parallel", "parallel"),
        name="dft_stage1",
    )(x, f1, scale_row)


def _twiddle(re, im, tc, ts, conj):
    if conj:
        return re * tc - im * ts, im * tc + re * ts
    return re * tc + im * ts, im * tc - re * ts


def _dft2_filter_kernel(a_ref, tw_ref, ff_ref, o_ref):
    n2 = a_ref.shape[2]
    br, bi = _twiddle(a_ref[0, 0], a_ref[1, 0], tw_ref[0, 0], tw_ref[1, 0], False)
    zz = _dot_x3(ff_ref[...], jnp.concatenate([br, bi], axis=0))
    o_ref[0, 0] = zz[:n2]
    o_ref[1, 0] = zz[n2:]


def dft_stage2_filter(a, consts):
    _, _, n2, ch = a.shape
    return pl.pallas_call(
        _dft2_filter_kernel,
        grid=(consts["kh"],),
        input_output_aliases={0: 0},
        in_specs=[pl.BlockSpec((2, 1, n2, ch), lambda k: (0, k, 0, 0)),
                  pl.BlockSpec((2, 1, n2, 1), lambda k: (0, k, 0, 0)),
                  pl.BlockSpec(consts["f2_fwd"].shape, lambda k: (0, 0))],
        out_specs=pl.BlockSpec((2, 1, n2, ch), lambda k: (0, k, 0, 0)),
        out_shape=jax.ShapeDtypeStruct(a.shape, F32),
        compiler_params=_params("parallel"),
        name="dft_stage2_filter",
    )(a, consts["tw"], consts["f2_fwd"])


def _dft2_conv_kernel(a_ref, tw_ref, kf_ref, ff_ref, fi_ref, o_ref):
    n2 = a_ref.shape[2]
    tc, ts = tw_ref[0, 0], tw_ref[1, 0]
    br, bi = _twiddle(a_ref[0, 0], a_ref[1, 0], tc, ts, False)
    zz = _dot_x3(ff_ref[...], jnp.concatenate([br, bi], axis=0))
    zr, zi = zz[:n2], zz[n2:]
    kr, ki = kf_ref[0, 0], kf_ref[1, 0]
    pr = zr * kr - zi * ki
    pi = zr * ki + zi * kr
    vv = _dot_x3(fi_ref[...], jnp.concatenate([pr, pi], axis=0))
    vr, vi = _twiddle(vv[:n2], vv[n2:], tc, ts, True)
    o_ref[0, 0] = vr
    o_ref[1, 0] = vi


def dft_stage2_conv(a, kf, order, consts):
    b, _, _, n2, ch = a.shape
    blk = pl.BlockSpec((None, 2, 1, n2, ch), lambda i, k: (i, 0, k, 0, 0))
    mat = pl.BlockSpec(consts["f2_fwd"].shape, lambda i, k: (0, 0))
    return pl.pallas_call(
        _dft2_conv_kernel,
        grid=(b, consts["kh"]),
        input_output_aliases={0: 0},
        in_specs=[blk,
                  pl.BlockSpec((2, 1, n2, 1), lambda i, k: (0, k, 0, 0)),
                  pl.BlockSpec((2, 1, n2, ch), lambda i, k: (0, k, 0, order)),
                  mat, mat],
        out_specs=blk,
        out_shape=jax.ShapeDtypeStruct(a.shape, F32),
        compiler_params=_params("parallel", "parallel"),
        name="dft_stage2_conv",
    )(a, consts["tw"], kf, consts["f2_fwd"], consts["f2_inv"])


def _dft3_kernel(v_ref, f_ref, gate_ref, z_ref, bias_ref, o_ref, vs_ref, ys_ref):
    f = f_ref[...]
    for m in range(_N2_TILE):
        vs_ref[...] = v_ref[:, m, :]
        ys_ref[:, m, :] = _dot_x3(f, vs_ref[...])
    o_ref[...] = gate_ref[...] * (ys_ref[...] + z_ref[...] * bias_ref[...])


def dft_stage3_gate(v, f3, gate, z, bias_row):
    b, m, n2, ch = v.shape
    r = f3.shape[0]
    row = pl.BlockSpec((None, r, _N2_TILE, ch), lambda i, j: (i, 0, j, 0))
    return pl.pallas_call(
        _dft3_kernel,
        grid=(b, n2 // _N2_TILE),
        in_specs=[pl.BlockSpec((None, m, _N2_TILE, ch), lambda i, j: (i, 0, j, 0)),
                  pl.BlockSpec(f3.shape, lambda i, j: (0, 0)),
                  row, row, pl.BlockSpec((1, ch), lambda i, j: (0, 0))],
        out_specs=row,
        out_shape=jax.ShapeDtypeStruct((b, r, n2, ch), F32),
        scratch_shapes=[pltpu.VMEM((m, ch), F32), pltpu.VMEM((r, _N2_TILE, ch), F32)],
        compiler_params=_params("parallel", "parallel"),
        name="dft_stage3_gate",
    )(v, f3, gate, z, bias_row)


def hyena_mixer(x1, x2, z, hy_w1, hy_b1, hy_fr1, hy_w2, hy_b2, hy_fr2, hy_w3, hy_b3, hy_log_decay,
                hy_bias):
    b, s, ch = z.shape
    consts = _dft_consts(s)
    n1, n2 = consts["n1"], consts["n2"]
    oc = 2 * ch
    taps, norm = hyena_filter_taps(s, hy_w1, hy_b1, hy_fr1, hy_w2, hy_b2, hy_fr2, hy_w3, hy_b3,
                                   hy_log_decay)
    kp = consts["kp"]
    ka = dft_stage1(taps.reshape(1, n1, n2, oc), consts["f1"], 1.0 / norm)
    kf = dft_stage2_filter(ka.reshape(2, kp, n2, oc), consts)
    half = n1 // 2
    view = lambda t: t.reshape(b, half, n2, ch)
    ones = jnp.ones((1, ch), F32)
    f1_half = consts["f1_half"]
    cur = view(z)
    for order, gate in enumerate((x1, x2)):
        a = dft_stage1(cur, f1_half, ones)
        v = dft_stage2_conv(a.reshape(b, 2, kp, n2, ch), kf, order, consts)
        cur = dft_stage3_gate(v.reshape(b, 2 * kp, n2, ch), consts["f3"], view(gate), cur,
                              hy_bias[order].reshape(1, ch))
    return cur.reshape(b, s, ch)


def _outproj_kernel(x_ref, hf_ref, hb_ref, o_ref, yb_ref, yc_ref, yd_ref, og_ref, hm_ref, wout_ref,
                    pg_ref, g1_ref, fg_ref, sc2_ref, sh2_ref, rwt_ref,
                    xn_ref, h2_ref, afft_ref):
    hm = hm_ref[...]
    y_a = _sigmoid(o_ref[...]) * (hf_ref[...] + hb_ref[...])
    acc = None
    for idx, y in enumerate((y_a, yb_ref[...], yc_ref[...], yd_ref[...])):
        ms = _dot(jnp.concatenate(_split2(y * y), axis=1), hm)
        yn = y * lax.rsqrt(ms + NORM_EPS) * og_ref[:, idx * GROUP_W:(idx + 1) * GROUP_W]
        part = _dot(yn.astype(BF16), wout_ref[idx * GROUP_W:(idx + 1) * GROUP_W, :])
        acc = part if acc is None else acc + part
    d = acc.shape[-1]
    xn = x_ref[...] + g1_ref[...] * (_rms(acc, d) * pg_ref[...])
    xn_ref[...] = xn
    h2 = _rms(xn, d) * fg_ref[...] * (1.0 + sc2_ref[...]) + sh2_ref[...]
    h2_ref[...] = h2.astype(BF16)
    logits_t = _dot_nt_hi(rwt_ref[...], h2)
    mx = jnp.max(logits_t, axis=0, keepdims=True)
    ex = jnp.exp(logits_t - mx)
    afft_ref[...] = ex / jnp.sum(ex, axis=0, keepdims=True)


def out_proj(x, hf, hb, o, yb, yc, yd, out_g, w_out, post_g, g1, ffn_g, sc2, sh2, router_w):
    b, s, d = x.shape
    tm = min(512, s)
    e = router_w.shape[1]
    hm1 = np.kron(np.eye(GROUP_W // HEAD_DIM), np.ones((HEAD_DIM, HEAD_DIM))) / HEAD_DIM
    hm = jnp.asarray(np.concatenate([hm1, hm1], axis=0), BF16)
    row = lambda w: pl.BlockSpec((None, tm, w), lambda i, j: (i, j, 0))
    full = lambda a: pl.BlockSpec(a.shape, lambda i, j: (0,) * a.ndim)
    vec = lambda: pl.BlockSpec((None, 1, d), lambda i, j: (i, 0, 0))
    r1 = lambda a: a.reshape(1, -1)
    wob = w_out.astype(BF16)
    rwt = router_w.T
    return pl.pallas_call(
        _outproj_kernel,
        grid=(b, s // tm),
        in_specs=[row(d)] + [row(GROUP_W)] * 6 + [full(r1(out_g)), full(hm), full(wob),
                                                  full(r1(post_g)), vec(), full(r1(ffn_g)), vec(), vec(),
                                                  full(rwt)],
        out_specs=[row(d), row(d), pl.BlockSpec((None, e, tm), lambda i, j: (i, 0, j))],
        out_shape=[jax.ShapeDtypeStruct((b, s, d), F32), jax.ShapeDtypeStruct((b, s, d), BF16),
                   jax.ShapeDtypeStruct((b, e, s), F32)],
        compiler_params=_params("parallel", "parallel"),
        name="out_proj",
    )(x, hf, hb, o, yb, yc, yd, r1(out_g), hm, wob, r1(post_g), g1.reshape(b, 1, d), r1(ffn_g),
      sc2.reshape(b, 1, d), sh2.reshape(b, 1, d), rwt)


def _lane_cumsum(x):
    n = x.shape[-1]
    lane = lax.broadcasted_iota(jnp.int32, x.shape, x.ndim - 1)
    shift = 1
    while shift < n:
        x = x + jnp.where(lane >= shift, pltpu.roll(x, shift, x.ndim - 1), 0.0)
        shift *= 2
    return x


def _select_kernel(aff_ref, u_ref, ps_ref, cb_ref, *, cap):
    aff = aff_ref[...]
    capf = float(cap)

    def body(i, bits):
        cand = bits | (jnp.int32(1) << (30 - i))
        cnt = jnp.sum(jnp.where(aff >= pltpu.bitcast(cand, F32), 1.0, 0.0), axis=-1, keepdims=True)
        return jnp.where(cnt >= capf, cand, bits)

    bits = lax.fori_loop(0, 31, body, jnp.zeros((aff.shape[0], 1), jnp.int32))
    thr = pltpu.bitcast(bits, F32)
    gt = aff > thr
    eq = aff == thr
    n_gt = jnp.sum(jnp.where(gt, 1.0, 0.0), axis=-1, keepdims=True)
    eqf = jnp.where(eq, 1.0, 0.0)
    rank_eq = _lane_cumsum(eqf) - eqf
    sel = gt | (eq & (rank_eq < capf - n_gt))
    self_ = jnp.where(sel, 1.0, 0.0)
    pos = _lane_cumsum(self_) - self_
    ps_ref[...] = jnp.where(sel, pos, -1.0).astype(jnp.int32)
    cb_ref[...] = _dot(self_.astype(BF16), u_ref[...]).astype(jnp.int32)


def ec_select(aff_t, cap):
    b, e, s = aff_t.shape
    nt = s // TOKEN_TILE
    assert nt + 1 <= LANES
    tok = np.arange(s)[:, None]
    u = jnp.asarray(tok < (np.arange(LANES)[None, :] * TOKEN_TILE), BF16)
    return pl.pallas_call(
        functools.partial(_select_kernel, cap=cap),
        grid=(b,),
        in_specs=[pl.BlockSpec((None, e, s), lambda i: (i, 0, 0)),
                  pl.BlockSpec((s, LANES), lambda i: (0, 0))],
        out_specs=[pl.BlockSpec((None, e, s), lambda i: (i, 0, 0)),
                   pl.BlockSpec((None, e, LANES), lambda i: (i, 0, 0))],
        out_shape=[jax.ShapeDtypeStruct((b, e, s), jnp.int32),
                   jax.ShapeDtypeStruct((b, e, LANES), jnp.int32)],
        compiler_params=_params("parallel"),
        name="ec_select",
    )(aff_t, u)


def _gather_kernel(cb_ref, ps_ref, h_ref, o_ref, acc_ref, *, n_tiles, n_blocks):
    bi, ei = pl.program_id(0), pl.program_id(1)
    base = (bi * N_EXPERTS + ei) * LANES
    slot = lax.broadcasted_iota(jnp.int32, (SLOT_BLOCK, TOKEN_TILE), 0)

    def block(j, carry):
        first = j * SLOT_BLOCK

        def bounds(t, c):
            return (c[0] + (cb_ref[base + t + 1] <= first).astype(jnp.int32),
                    c[1] + (cb_ref[base + t] < first + SLOT_BLOCK).astype(jnp.int32))

        t_lo, t_hi = lax.fori_loop(0, n_tiles, bounds, (jnp.int32(0), jnp.int32(0)))
        acc_ref[...] = jnp.zeros_like(acc_ref)

        def tile(t, c):
            off = pl.multiple_of(t * TOKEN_TILE, TOKEN_TILE)
            ps = ps_ref[pl.ds(t, 1), :]
            onehot = jnp.where(ps == slot + first, 1.0, 0.0).astype(BF16)
            acc_ref[...] += _dot(onehot, h_ref[pl.ds(off, TOKEN_TILE), :])
            return c

        lax.fori_loop(t_lo, t_hi, tile, 0)
        o_ref[pl.ds(pl.multiple_of(first, SLOT_BLOCK), SLOT_BLOCK), :] = acc_ref[...].astype(o_ref.dtype)
        return carry

    lax.fori_loop(0, n_blocks, block, 0)


def ec_gather(h, ps, cb, cap):
    b, s, d = h.shape
    e = ps.shape[1]
    nt = s // TOKEN_TILE
    grid_spec = pltpu.PrefetchScalarGridSpec(
        num_scalar_prefetch=1,
        grid=(b, e),
        in_specs=[pl.BlockSpec((None, None, nt, TOKEN_TILE), lambda i, j, cb: (i, j, 0, 0)),
                  pl.BlockSpec((None, s, d), lambda i, j, cb: (i, 0, 0))],
        out_specs=pl.BlockSpec((None, None, cap, d), lambda i, j, cb: (i, j, 0, 0)),
        scratch_shapes=[pltpu.VMEM((SLOT_BLOCK, d), F32)],
    )
    return pl.pallas_call(
        functools.partial(_gather_kernel, n_tiles=nt, n_blocks=cap // SLOT_BLOCK),
        grid_spec=grid_spec,
        out_shape=jax.ShapeDtypeStruct((b, e, cap, d), BF16),
        compiler_params=_params("parallel", "arbitrary"),
        name="ec_gather",
    )(cb.reshape(-1), ps.reshape(b, e, nt, TOKEN_TILE), h)


def _ffn_kernel(x_ref, wg_ref, wu_ref, wd_ref, o_ref, acc_ref, *, n_f):
    f = pl.program_id(1)

    @pl.when(f == 0)
    def _():
        acc_ref[...] = jnp.zeros_like(acc_ref)

    bsz, cap, d = x_ref.shape
    x = x_ref[...].reshape(bsz * cap, d)
    a = _dot(x, wg_ref[...].astype(BF16))
    up = _dot(x, wu_ref[...].astype(BF16))
    act = (a * _sigmoid(a) * up).astype(BF16)
    acc_ref[...] += _dot(act, wd_ref[...].astype(BF16))

    @pl.when(f == n_f - 1)
    def _():
        o_ref[...] = acc_ref[...].reshape(bsz, cap, d).astype(o_ref.dtype)


def expert_ffn(xe, w_gate, w_up, w_down, layer):
    b, e, cap, d = xe.shape
    ff = w_gate.shape[-1]
    tf = min(512, ff)
    n_f = ff // tf
    return pl.pallas_call(
        functools.partial(_ffn_kernel, n_f=n_f),
        grid=(e, n_f),
        in_specs=[pl.BlockSpec((b, None, cap, d), lambda j, f: (0, j, 0, 0)),
                  pl.BlockSpec((None, None, d, tf), lambda j, f: (layer, j, 0, f)),
                  pl.BlockSpec((None, None, d, tf), lambda j, f: (layer, j, 0, f)),
                  pl.BlockSpec((None, None, tf, d), lambda j, f: (layer, j, f, 0))],
        out_specs=pl.BlockSpec((b, None, cap, d), lambda j, f: (0, j, 0, 0)),
        out_shape=jax.ShapeDtypeStruct((b, e, cap, d), BF16),
        scratch_shapes=[pltpu.VMEM((b * cap, d), F32)],
        compiler_params=_params("parallel", "arbitrary"),
        name="expert_ffn",
    )(xe, w_gate, w_up, w_down)


def _scatter_kernel(cb_ref, ps_ref, aff_ref, ye_ref, x_ref, pg_ref, g2_ref, o_ref, acc_ref,
                    *, n_sub, window):
    bi, ti, ei = pl.program_id(0), pl.program_id(1), pl.program_id(2)

    @pl.when(ei == 0)
    def _():
        acc_ref[...] = jnp.zeros_like(acc_ref)

    tt = acc_ref.shape[0]
    base = (bi * N_EXPERTS + ei) * LANES + ti * n_sub
    lo = cb_ref[base]
    hi = cb_ref[base + n_sub]

    @pl.when(hi > lo)
    def _():
        elane = lax.broadcasted_iota(jnp.int32, (tt, N_EXPERTS), 1)
        ps = jnp.sum(jnp.where(elane == ei, ps_ref[...], 0), axis=-1, keepdims=True)
        gate = jnp.sum(jnp.where(elane == ei, aff_ref[...], 0.0), axis=-1, keepdims=True)
        slot = lax.broadcasted_iota(jnp.int32, (tt, window), 1)
        cap = ye_ref.shape[0]
        first = (lo // BF16_ROWS) * BF16_ROWS

        def body(j, carry):
            lower = first + j * window
            start = pl.multiple_of(jnp.minimum(lower, cap - window), BF16_ROWS)
            onehot = jnp.where(jnp.where(ps >= lower, ps, -1) == slot + start, 1.0, 0.0).astype(BF16)
            acc_ref[...] += gate * _dot(onehot, ye_ref[pl.ds(start, window), :])
            return carry

        lax.fori_loop(0, (hi - first + window - 1) // window, body, 0)

    @pl.when(ei == N_EXPERTS - 1)
    def _():
        y = acc_ref[...]
        o_ref[...] = x_ref[...] + g2_ref[...] * (_rms(y, y.shape[-1]) * pg_ref[...])


def ec_scatter(ye, ps_col, aff, cb, x, post_g, g2):
    b, e, cap, d = ye.shape
    s = x.shape[1]
    n_sub = min(4, s // TOKEN_TILE)
    tt = n_sub * TOKEN_TILE
    grid_spec = pltpu.PrefetchScalarGridSpec(
        num_scalar_prefetch=1,
        grid=(b, s // tt, e),
        in_specs=[pl.BlockSpec((None, tt, e), lambda i, t, j, cb: (i, t, 0)),
                  pl.BlockSpec((None, tt, e), lambda i, t, j, cb: (i, t, 0)),
                  pl.BlockSpec((None, None, cap, d), lambda i, t, j, cb: (i, j, 0, 0)),
                  pl.BlockSpec((None, tt, d), lambda i, t, j, cb: (i, t, 0)),
                  pl.BlockSpec((1, d), lambda i, t, j, cb: (0, 0)),
                  pl.BlockSpec((None, 1, d), lambda i, t, j, cb: (i, 0, 0))],
        out_specs=pl.BlockSpec((None, tt, d), lambda i, t, j, cb: (i, t, 0)),
        scratch_shapes=[pltpu.VMEM((tt, d), F32)],
    )
    return pl.pallas_call(
        functools.partial(_scatter_kernel, n_sub=n_sub, window=min(2 * SLOT_BLOCK, cap)),
        grid_spec=grid_spec,
        out_shape=jax.ShapeDtypeStruct((b, s, d), F32),
        compiler_params=_params("parallel", "parallel", "arbitrary"),
        name="ec_scatter",
    )(cb.reshape(-1), ps_col, aff, ye, x, post_g.reshape(1, d), g2.reshape(b, 1, d))


def kernel(x, c, positions, ada_w, ada_b, mix_pre_g, mix_post_g, ffn_pre_g, ffn_post_g, w_in, ml_gate_b, mla_q_norm, mla_kv_norm, mla_w_uq, mla_w_uk, mla_w_uv, hy_conv_w, hy_conv_b, hy_w1, hy_b1, hy_fr1, hy_w2, hy_b2, hy_fr2, hy_w3, hy_b3, hy_log_decay, hy_bias, sc_conv_w, mix_out_g, w_out, router_w, exp_w_gate, exp_w_up, exp_w_down):
    depth = ada_w.shape[0]
    b, s, d = x.shape
    cap = EC_CAPACITY * s // N_EXPERTS
    mod = ada_mod(c, ada_w, ada_b)
    tables = rope_tables(positions)
    for l in range(depth):
        sh1, sc1, g1, sh2, sc2, g2 = (mod[l, :, i * d:(i + 1) * d] for i in range(6))
        u = in_proj(x, mix_pre_g[l], sc1, sh1, w_in[l])
        hf, hb = mlstm(u["q"], u["v"], u["kT"], u["gT"], u["g"], ml_gate_b[l])
        qa, ka, va = mla_proj(u["cq"], u["ckv"], u["kr"], tables, mla_q_norm[l], mla_kv_norm[l],
                              mla_w_uq[l], mla_w_uk[l], mla_w_uv[l])
        y_b = flash_attention(qa, ka, va)
        x1, x2, z, y_d = conv_mixers(u["hy"], u["sc"], hy_conv_w[l], hy_conv_b[l], sc_conv_w[l])
        y_c = hyena_mixer(x1, x2, z, hy_w1[l], hy_b1[l], hy_fr1[l], hy_w2[l], hy_b2[l], hy_fr2[l],
                          hy_w3[l], hy_b3[l], hy_log_decay[l], hy_bias[l])
        xn, h2, aff_t = out_proj(x, hf, hb, u["o"], y_b, y_c, y_d, mix_out_g[l], w_out[l],
                                      mix_post_g[l], g1, ffn_pre_g[l], sc2, sh2, router_w[l])
        ps, cb = ec_select(aff_t, cap)
        xe = ec_gather(h2, ps, cb, cap)
        ye = expert_ffn(xe, exp_w_gate, exp_w_up, exp_w_down, l)
        x = ec_scatter(ye, jnp.swapaxes(ps, 1, 2), jnp.swapaxes(aff_t, 1, 2), cb, xn, ffn_post_g[l], g2)
    return x
```

```python
import functools
import math

import numpy as np
import jax
import jax.numpy as jnp
from jax import lax
from jax.experimental import pallas as pl
from jax.experimental.pallas import tpu as pltpu

F32 = jnp.float32
BF16 = jnp.bfloat16
HIGHEST = lax.Precision.HIGHEST

GROUP_W = 256
HEAD_DIM = 64
N_HEADS = 4
ML_CHUNK = 128
MLA_Q_RANK = 224
MLA_KV_RANK = 128
MLA_NOPE = 64
MLA_ROPE = 32
ROPE_HALF = MLA_ROPE // 2
ROPE_THETA = 10000.0
HY_BANDS = 8
HY_FFN = 64
N_EXPERTS = 16
EC_CAPACITY = 2
NORM_EPS = 1e-6
LANES = 128
BF16_ROWS = 16
V_ROWS = HEAD_DIM + BF16_ROWS
SLOT_BLOCK = 128
TOKEN_TILE = 256
VMEM_LIMIT = 56 * 1024 * 1024


def _params(*sem):
    return pltpu.CompilerParams(dimension_semantics=sem, vmem_limit_bytes=VMEM_LIMIT)


def _dot(a, b):
    return jnp.dot(a, b, preferred_element_type=F32)


def _dot_hi(a, b):
    return jnp.dot(a, b, precision=HIGHEST, preferred_element_type=F32)


def _dot_nt(a, b):
    return lax.dot_general(a, b, (((1,), (1,)), ((), ())), preferred_element_type=F32)


def _dot_nt_hi(a, b):
    return lax.dot_general(a, b, (((1,), (1,)), ((), ())), precision=HIGHEST,
                           preferred_element_type=F32)


def _split2(x):
    hi = x.astype(BF16)
    lo = (x - hi.astype(F32)).astype(BF16)
    return hi, lo


def _split3(x):
    hi = x.astype(BF16)
    r = x - hi.astype(F32)
    mid = r.astype(BF16)
    lo = (r - mid.astype(F32)).astype(BF16)
    return hi, mid, lo


def _lhs_x3(f):
    hi, lo = _split2(f)
    return jnp.concatenate([hi, lo, hi], axis=1)


def _dot_x3(f3, x):
    hi, lo = _split2(x)
    return _dot(f3, jnp.concatenate([hi, hi, lo], axis=0))


def _dot3(a, w_hi, w_lo):
    a_hi, a_lo = _split2(a)
    return _dot(a_hi, w_hi) + _dot(a_hi, w_lo) + _dot(a_lo, w_hi)


def _rms(x, n):
    ms = jnp.sum(x * x, axis=-1, keepdims=True) * (1.0 / n)
    return x * lax.rsqrt(ms + NORM_EPS)


def _log_sigmoid(x):
    return jnp.minimum(x, 0.0) - jnp.log(1.0 + jnp.exp(-jnp.abs(x)))


def _sigmoid(x):
    return 1.0 / (1.0 + jnp.exp(-x))


def _ada_kernel(c_ref, w_ref, b_ref, o_ref):
    c = c_ref[...]
    cs = c * _sigmoid(c)
    o_ref[...] = _dot_hi(cs, w_ref[...]) + b_ref[...]


def ada_mod(c, ada_w, ada_b):
    depth, d, n6 = ada_w.shape
    b = c.shape[0]
    bp = 8
    cp = jnp.zeros((bp, d), F32).at[:b].set(c)
    tn = 1536
    out = pl.pallas_call(
        _ada_kernel,
        grid=(depth, n6 // tn),
        in_specs=[pl.BlockSpec((bp, d), lambda l, j: (0, 0)),
                  pl.BlockSpec((None, d, tn), lambda l, j: (l, 0, j)),
                  pl.BlockSpec((None, 1, tn), lambda l, j: (l, 0, j))],
        out_specs=pl.BlockSpec((None, bp, tn), lambda l, j: (l, 0, j)),
        out_shape=jax.ShapeDtypeStruct((depth, bp, n6), F32),
        compiler_params=_params("parallel", "parallel"),
        name="ada_mod",
    )(cp, ada_w, ada_b.reshape(depth, 1, n6))
    return out[:, :b]


_U_COLS = (("q", 256), ("v", 256), ("o", 256), ("g", 128), ("cq", 256), ("ckv", 128),
           ("kr", 128), ("hy", 768), ("sc", 768))
_U_TOTAL = sum(w for _, w in _U_COLS)
_UT_ROWS = 256 + 16


def _inproj_kernel(x_ref, gain_ref, sc_ref, sh_ref, w_ref, wt_ref,
                   q_ref, v_ref, o_ref, g_ref, cq_ref, ckv_ref, kr_ref, hy_ref, scu_ref,
                   kt_ref, gt_ref):
    x = x_ref[...]
    d = x.shape[-1]
    h = _rms(x, d) * gain_ref[...] * (1.0 + sc_ref[...]) + sh_ref[...]
    hb = h.astype(BF16)
    u = _dot(hb, w_ref[...])
    off = 0
    for ref, (_, width) in zip((q_ref, v_ref, o_ref, g_ref, cq_ref, ckv_ref, kr_ref, hy_ref, scu_ref),
                               _U_COLS):
        ref[...] = u[:, off:off + width]
        off += width
    ut = _dot_nt(wt_ref[...], hb)
    kt_ref[...] = ut[:256]
    gt_ref[...] = ut[256:]


def in_proj(x, gain, scale, shift, w_in):
    b, s, d = x.shape
    tm = min(512, s)
    cuts = np.cumsum([0, 256, 256, 256, 256, 16, MLA_Q_RANK, MLA_KV_RANK, MLA_ROPE, 768, 768])
    wq, wk, wv, wo, wg, wcq, wckv, wkr, why, wsc = (w_in[:, cuts[i]:cuts[i + 1]] for i in range(10))
    pad = lambda w, n: jnp.pad(w, ((0, 0), (0, n - w.shape[1])))
    wkr_p = jnp.pad(wkr, ((0, 0), (MLA_NOPE, LANES - MLA_NOPE - MLA_ROPE)))
    w1 = jnp.concatenate([wq, wv, wo, pad(wg, 128), pad(wcq, 256), wckv, wkr_p, why, wsc],
                         axis=1).astype(BF16)
    w2t = jnp.concatenate([wk, wg], axis=1).T.astype(BF16)
    row = lambda nm, w: pl.BlockSpec((None, tm, w), lambda i, j: (i, j, 0))
    out_shapes = [jax.ShapeDtypeStruct((b, s, w), F32) for _, w in _U_COLS]
    out_shapes += [jax.ShapeDtypeStruct((b, 256, s), F32), jax.ShapeDtypeStruct((b, 16, s), F32)]
    out_specs = [row(nm, w) for nm, w in _U_COLS]
    out_specs += [pl.BlockSpec((None, 256, tm), lambda i, j: (i, 0, j)),
                  pl.BlockSpec((None, 16, tm), lambda i, j: (i, 0, j))]
    vec = lambda: pl.BlockSpec((None, 1, d), lambda i, j: (i, 0, 0))
    outs = pl.pallas_call(
        _inproj_kernel,
        grid=(b, s // tm),
        in_specs=[pl.BlockSpec((None, tm, d), lambda i, j: (i, j, 0)),
                  pl.BlockSpec((1, d), lambda i, j: (0, 0)),
                  vec(), vec(),
                  pl.BlockSpec((d, _U_TOTAL), lambda i, j: (0, 0)),
                  pl.BlockSpec((_UT_ROWS, d), lambda i, j: (0, 0))],
        out_specs=out_specs,
        out_shape=out_shapes,
        compiler_params=_params("parallel", "parallel"),
        name="in_proj",
    )(x, gain.reshape(1, d), scale.reshape(b, 1, d), shift.reshape(b, 1, d), w1, w2t)
    names = [nm for nm, _ in _U_COLS] + ["kT", "gT"]
    return dict(zip(names, outs))


def _mlstm_dir(q, v, kt, gt, gc, c_ref, m_ref, base, rev):
    L = q.shape[0]
    r = lax.broadcasted_iota(jnp.int32, (L, L), 0)
    c = lax.broadcasted_iota(jnp.int32, (L, L), 1)
    tri = (c >= r) if rev else (c <= r)
    io, fo = (8, 12) if rev else (0, 4)
    logf_rows = _log_sigmoid(gt[fo:fo + 4, :])
    logf_cols = _log_sigmoid(gc)
    tri_b = jnp.where(tri, 1.0, 0.0).astype(BF16)
    a_cols = _dot(jnp.concatenate([tri_b] * 3, axis=1),
                  jnp.concatenate(_split3(logf_cols), axis=0))
    tri_t = (r >= c) if rev else (r <= c)
    tri_tb = jnp.where(tri_t, 1.0, 0.0).astype(BF16)
    a_rows = _dot(jnp.concatenate(_split3(_log_sigmoid(gt)), axis=1),
                  jnp.concatenate([tri_tb] * 3, axis=0))[fo:fo + 4, :]
    lane = lax.broadcasted_iota(jnp.int32, (L, LANES), 1)
    row128 = lax.broadcasted_iota(jnp.int32, (LANES, L), 0)
    scale = HEAD_DIM ** -0.5
    outs = []
    for pair in range(2):
        qp = q[:, pair * LANES:(pair + 1) * LANES]
        vp = v[:, pair * LANES:(pair + 1) * LANES]
        ktp = kt[pair * LANES:(pair + 1) * LANES, :] * scale
        pair_out = None
        for sub in range(2):
            h = pair * 2 + sub
            in_head = (row128 >= sub * HEAD_DIM) & (row128 < (sub + 1) * HEAD_DIM)
            kth = jnp.where(in_head, ktp, 0.0)
            vsh = vp if sub == 0 else pltpu.roll(vp, HEAD_DIM, 1)
            v_aug = jnp.where(lane < HEAD_DIM, vsh, jnp.where(lane == HEAD_DIM, 1.0, 0.0))
            v_aug_b = v_aug.astype(BF16)
            a_c = a_cols[:, fo + h:fo + h + 1]
            a_r = a_rows[h:h + 1, :]
            ig_r = gt[io + h:io + h + 1, :]
            a_end = jnp.sum(logf_rows[h:h + 1, :], axis=-1, keepdims=True)
            c_st = c_ref[base + h]
            m_st = m_ref[base + h][:, 0:1]
            d_log = jnp.where(tri, a_c - a_r + ig_r, -jnp.inf)
            inter = a_c + m_st
            m_t = jnp.maximum(inter, jnp.max(d_log, axis=-1, keepdims=True))
            qb = qp.astype(BF16)
            p = jnp.exp(d_log - m_t) * _dot(qb, kth.astype(BF16))
            sci = jnp.exp(inter - m_t)
            nd = _dot(p.astype(BF16), v_aug_b) + sci * _dot(qb, c_st.astype(BF16))
            den = nd[:, HEAD_DIM:HEAD_DIM + 1]
            out = nd / jnp.maximum(jnp.abs(den), jnp.exp(-m_t))
            w_st = a_end - a_r + ig_r
            m_loc = jnp.max(w_st, axis=-1, keepdims=True)
            ke = kth * jnp.exp(w_st - m_loc)
            c_loc = _dot(ke.astype(BF16), v_aug_b)
            m_new = jnp.maximum(a_end + m_st, m_loc)
            sp = jnp.exp(a_end + m_st - m_new)
            sl = jnp.exp(m_loc - m_new)
            c_ref[base + h] = sp * c_st + sl * c_loc
            m_ref[base + h] = jnp.broadcast_to(m_new, (1, LANES))
            if sub == 0:
                pair_out = out
            else:
                pair_out = jnp.where(lane < HEAD_DIM, pair_out, pltpu.roll(out, HEAD_DIM, 1))
        outs.append(pair_out)
    return jnp.concatenate(outs, axis=1)


def _mlstm_kernel(qf_ref, vf_ref, ktf_ref, gtf_ref, gcf_ref,
                  qb_ref, vb_ref, ktb_ref, gtb_ref, gcb_ref, brow_ref, bcol_ref,
                  hf_ref, hb_ref, c_ref, m_ref):
    @pl.when(pl.program_id(1) == 0)
    def _():
        c_ref[...] = jnp.zeros_like(c_ref)
        m_ref[...] = jnp.zeros_like(m_ref)

    bcol = bcol_ref[...]
    brow = brow_ref[...]
    hf_ref[...] = _mlstm_dir(qf_ref[...], vf_ref[...], ktf_ref[...], gtf_ref[...] + bcol,
                             gcf_ref[...] + brow, c_ref, m_ref, 0, False)
    hb_ref[...] = _mlstm_dir(qb_ref[...], vb_ref[...], ktb_ref[...], gtb_ref[...] + bcol,
                             gcb_ref[...] + brow, c_ref, m_ref, N_HEADS, True)


def mlstm(q, v, kt, gt, g, gate_b):
    b, s, w = q.shape
    L = ML_CHUNK
    nc = s // L
    bflat = gate_b.reshape(16)
    brow = jnp.zeros((1, LANES), F32).at[0, :16].set(bflat)
    bcol = bflat.reshape(16, 1)
    fw = lambda i, j: (i, j, 0)
    bw = lambda i, j: (i, nc - 1 - j, 0)
    fwt = lambda i, j: (i, 0, j)
    bwt = lambda i, j: (i, 0, nc - 1 - j)

    def specs(m, mt):
        return [pl.BlockSpec((None, L, w), m), pl.BlockSpec((None, L, w), m),
                pl.BlockSpec((None, w, L), mt), pl.BlockSpec((None, 16, L), mt),
                pl.BlockSpec((None, L, LANES), m)]

    return pl.pallas_call(
        _mlstm_kernel,
        grid=(b, nc),
        in_specs=specs(fw, fwt) + specs(bw, bwt) + [
            pl.BlockSpec((1, LANES), lambda i, j: (0, 0)),
            pl.BlockSpec((16, 1), lambda i, j: (0, 0))],
        out_specs=[pl.BlockSpec((None, L, w), fw), pl.BlockSpec((None, L, w), bw)],
        out_shape=[jax.ShapeDtypeStruct((b, s, w), F32)] * 2,
        scratch_shapes=[pltpu.VMEM((2 * N_HEADS, LANES, LANES), F32),
                        pltpu.VMEM((2 * N_HEADS, 1, LANES), F32)],
        compiler_params=_params("parallel", "arbitrary"),
        name="mlstm",
    )(q, v, kt, gt, g, q, v, kt, gt, g, brow, bcol)


def _rope_table_kernel(pos_ref, post_ref, inv_ref, invt_ref, cos_ref, sin_ref, cost_ref, sint_ref):
    ang = pos_ref[...].astype(F32) * inv_ref[...]
    cos_ref[...] = jnp.cos(ang)
    sin_ref[...] = jnp.sin(ang)
    ang_t = invt_ref[...] * post_ref[...].astype(F32)
    cost_ref[...] = jnp.cos(ang_t)
    sint_ref[...] = jnp.sin(ang_t)


def rope_tables(positions):
    b, s = positions.shape
    tm = min(512, s)
    inv = ROPE_THETA ** (-jnp.arange(ROPE_HALF, dtype=F32) / ROPE_HALF)
    inv_row = jnp.zeros((1, LANES), F32).at[0, MLA_NOPE:MLA_NOPE + ROPE_HALF].set(inv)
    inv_row = inv_row.at[0, MLA_NOPE + ROPE_HALF:MLA_NOPE + MLA_ROPE].set(inv)
    spec = pl.BlockSpec((None, tm, LANES), lambda i, j: (i, j, 0))
    spec_t = pl.BlockSpec((None, LANES, tm), lambda i, j: (i, 0, j))
    return pl.pallas_call(
        _rope_table_kernel,
        grid=(b, s // tm),
        in_specs=[pl.BlockSpec((None, tm, 1), lambda i, j: (i, j, 0)),
                  pl.BlockSpec((None, 1, tm), lambda i, j: (i, 0, j)),
                  pl.BlockSpec((1, LANES), lambda i, j: (0, 0)),
                  pl.BlockSpec((LANES, 1), lambda i, j: (0, 0))],
        out_specs=[spec, spec, spec_t, spec_t],
        out_shape=[jax.ShapeDtypeStruct((b, s, LANES), F32)] * 2
        + [jax.ShapeDtypeStruct((b, LANES, s), F32)] * 2,
        compiler_params=_params("parallel", "parallel"),
        name="rope_tables",
    )(positions.reshape(b, s, 1), positions.reshape(b, 1, s), inv_row, inv_row.reshape(LANES, 1))


def _mla_proj_kernel(cq_ref, ckv_ref, kr_ref, cos_ref, sin_ref, cost_ref, sint_ref, qg_ref, kvg_ref,
                     wqt_ref, wk_ref, wvt_ref, qt_ref, k_ref, vt_ref):
    cos = cos_ref[...]
    sin = sin_ref[...]
    lane = lax.broadcasted_iota(jnp.int32, cos.shape, 1)
    x1 = (lane >= MLA_NOPE) & (lane < MLA_NOPE + ROPE_HALF)
    x2 = (lane >= MLA_NOPE + ROPE_HALF) & (lane < MLA_NOPE + MLA_ROPE)
    kr = kr_ref[...]
    krr = (kr * cos + pltpu.roll(kr, LANES - ROPE_HALF, 1) * jnp.where(x1, -sin, 0.0)
           + pltpu.roll(kr, ROPE_HALF, 1) * jnp.where(x2, sin, 0.0))

    cqn = (_rms(cq_ref[...], MLA_Q_RANK) * qg_ref[...]).astype(BF16)
    ckvn = (_rms(ckv_ref[...], MLA_KV_RANK) * kvg_ref[...]).astype(BF16)
    kn = _dot(ckvn, wk_ref[...])
    for h in range(N_HEADS):
        sl = slice(h * LANES, (h + 1) * LANES)
        k_ref[:, sl] = (kn[:, sl] + krr).astype(BF16)
    vrow = lax.broadcasted_iota(jnp.int32, vt_ref.shape, 0)
    ones_row = jnp.where(vrow % V_ROWS == HEAD_DIM, 1.0, 0.0)
    vt_ref[...] = (_dot_nt(wvt_ref[...], ckvn) + ones_row).astype(BF16)

    cos_t = cost_ref[...]
    sin_t = sint_ref[...]
    row = lax.broadcasted_iota(jnp.int32, cos_t.shape, 0)
    x1t = (row >= MLA_NOPE) & (row < MLA_NOPE + ROPE_HALF)
    x2t = (row >= MLA_NOPE + ROPE_HALF) & (row < MLA_NOPE + MLA_ROPE)
    sin_at = jnp.where(x1t, -sin_t, 0.0)
    sin_bt = jnp.where(x2t, sin_t, 0.0)
    qt = _dot_nt(wqt_ref[...], cqn)
    scale = (MLA_NOPE + MLA_ROPE) ** -0.5 * math.log2(math.e)
    for h in range(N_HEADS):
        sl = slice(h * LANES, (h + 1) * LANES)
        xs = qt[sl, :]
        roped = (xs * cos_t + pltpu.roll(xs, LANES - ROPE_HALF, 0) * sin_at
                 + pltpu.roll(xs, ROPE_HALF, 0) * sin_bt)
        qt_ref[sl, :] = (roped * scale).astype(BF16)


def mla_proj(cq, ckv, kr, tables, q_norm, kv_norm, w_uq, w_uk, w_uv):
    b, s, _ = cq.shape
    tm = min(512, s)
    dqk = MLA_NOPE + MLA_ROPE
    cos, sin, cos_t, sin_t = tables
    wq = w_uq.reshape(MLA_Q_RANK, N_HEADS, dqk)
    wq = jnp.pad(wq, ((0, 256 - MLA_Q_RANK), (0, 0), (0, LANES - dqk))).reshape(256, N_HEADS * LANES)
    wk = w_uk.reshape(MLA_KV_RANK, N_HEADS, MLA_NOPE)
    wk = jnp.pad(wk, ((0, 0), (0, 0), (0, LANES - MLA_NOPE))).reshape(MLA_KV_RANK, N_HEADS * LANES)
    qg = jnp.pad(q_norm, (0, 256 - MLA_Q_RANK)).reshape(1, 256)
    kvg = kv_norm.reshape(1, MLA_KV_RANK)
    row = lambda w: pl.BlockSpec((None, tm, w), lambda i, j: (i, j, 0))
    col = lambda w: pl.BlockSpec((None, w, tm), lambda i, j: (i, 0, j))
    full = lambda a: pl.BlockSpec(a.shape, lambda i, j: (0,) * a.ndim)
    wv = jnp.pad(w_uv.reshape(MLA_KV_RANK, N_HEADS, HEAD_DIM), ((0, 0), (0, 0), (0, V_ROWS - HEAD_DIM)))
    wqt, wkb = wq.T.astype(BF16), wk.astype(BF16)
    wvt = wv.reshape(MLA_KV_RANK, N_HEADS * V_ROWS).T.astype(BF16)
    return pl.pallas_call(
        _mla_proj_kernel,
        grid=(b, s // tm),
        in_specs=[row(256), row(128), row(128), row(128), row(128), col(128), col(128),
                  full(qg), full(kvg), full(wqt), full(wkb), full(wvt)],
        out_specs=[col(512), row(512), col(N_HEADS * V_ROWS)],
        out_shape=[jax.ShapeDtypeStruct((b, 512, s), BF16), jax.ShapeDtypeStruct((b, s, 512), BF16),
                   jax.ShapeDtypeStruct((b, N_HEADS * V_ROWS, s), BF16)],
        compiler_params=_params("parallel", "parallel"),
        name="mla_proj",
    )(cq, ckv, kr, cos, sin, cos_t, sin_t, qg, kvg, wqt, wkb, wvt)


def _flash_kernel(qt_ref, k_ref, vt_ref, o_ref, *, tk):
    s = k_ref.shape[0]
    tq = qt_ref.shape[1]
    nk = s // tk
    qts = [qt_ref[sub * LANES:(sub + 1) * LANES, :] for sub in range(2)]

    def scores(t):
        off = pl.multiple_of(t * tk, tk)
        return [_dot(k_ref[pl.ds(off, tk), sub * LANES:(sub + 1) * LANES], qts[sub])
                for sub in range(2)]

    def update(t, sc, stats):
        off = pl.multiple_of(t * tk, tk)
        new = []
        for sub in range(2):
            m, acc = stats[sub]
            vt = vt_ref[sub * V_ROWS:(sub + 1) * V_ROWS, pl.ds(off, tk)]
            m_new = jnp.maximum(m, jnp.max(sc[sub], axis=0, keepdims=True))
            alpha = jnp.exp2(m - m_new)
            p = jnp.exp2((sc[sub] - m_new).astype(BF16))
            acc = alpha * acc + _dot(vt, p)
            new.append((m_new, acc))
        return new

    def body(u, carry):
        sc_a, stats = carry
        ta = 2 * u
        sc_b = scores(ta + 1)
        stats = update(ta, sc_a, stats)
        sc_a = scores(jnp.minimum(ta + 2, nk - 1))
        stats = update(ta + 1, sc_b, stats)
        return sc_a, stats

    init = [(jnp.full((1, tq), -jnp.inf, F32), jnp.zeros((V_ROWS, tq), F32)) for _ in range(2)]
    _, fin = lax.fori_loop(0, nk // 2, body, (scores(0), init))
    out_t = jnp.concatenate([acc[:HEAD_DIM] / acc[HEAD_DIM:HEAD_DIM + 1] for (_, acc) in fin],
                            axis=0)
    o_ref[...] = out_t.T


def flash_attention(qt, k, vt):
    b, _, s = qt.shape
    tq = min(256, s)
    tk = min(512, s)
    return pl.pallas_call(
        functools.partial(_flash_kernel, tk=tk),
        grid=(b, 2, s // tq),
        in_specs=[pl.BlockSpec((None, 256, tq), lambda i, p, j: (i, p, j)),
                  pl.BlockSpec((None, s, 256), lambda i, p, j: (i, 0, p)),
                  pl.BlockSpec((None, 2 * V_ROWS, s), lambda i, p, j: (i, p, 0))],
        out_specs=pl.BlockSpec((None, tq, LANES), lambda i, p, j: (i, j, p)),
        out_shape=jax.ShapeDtypeStruct((b, s, 256), F32),
        compiler_params=_params("parallel", "parallel", "parallel"),
        name="flash_attention",
    )(qt, k, vt)


def _shifted(x, prev_row, next_row, first, last):
    tm = x.shape[0]
    row = lax.broadcasted_iota(jnp.int32, x.shape, 0)
    prev_row = jnp.where(first, 0.0, prev_row)
    next_row = jnp.where(last, 0.0, next_row)
    xm1 = jnp.where(row == 0, prev_row, pltpu.roll(x, 1, 0))
    xp1 = jnp.where(row == tm - 1, next_row, pltpu.roll(x, tm - 1, 0))
    return xm1, xp1


def _conv_kernel(hy_ref, hyp_ref, hyn_ref, sc_ref, scp_ref, scn_ref, hw_ref, hb_ref, sw_ref,
                 x1_ref, x2_ref, z_ref, yd_ref):
    j = pl.program_id(1)
    first = j == 0
    last = j == pl.num_programs(1) - 1
    x = hy_ref[...]
    xm1, xp1 = _shifted(x, hyp_ref[7:8, :], hyn_ref[0:1, :], first, last)
    hw = hw_ref[...]
    proj = xm1 * hw[0:1] + x * hw[1:2] + xp1 * hw[2:3] + hb_ref[...]
    x1_ref[...] = proj[:, 0:GROUP_W]
    x2_ref[...] = proj[:, GROUP_W:2 * GROUP_W]
    z_ref[...] = proj[:, 2 * GROUP_W:]
    su = sc_ref[...]
    g = GROUP_W
    prod = su[:, g:2 * g] * su[:, 2 * g:]
    pprev = scp_ref[7:8, g:2 * g] * scp_ref[7:8, 2 * g:]
    pnext = scn_ref[0:1, g:2 * g] * scn_ref[0:1, 2 * g:]
    pm1, pp1 = _shifted(prod, pprev, pnext, first, last)
    sw = sw_ref[...]
    yd_ref[...] = su[:, :g] * (pm1 * sw[0:1] + prod * sw[1:2] + pp1 * sw[2:3])


def conv_mixers(hy_u, sc_u, hy_conv_w, hy_conv_b, sc_conv_w):
    b, s, w3 = hy_u.shape
    tm = min(512, s)
    nb8 = s // 8
    r8 = tm // 8
    main = pl.BlockSpec((None, tm, w3), lambda i, j: (i, j, 0))
    prev = pl.BlockSpec((None, 8, w3), lambda i, j: (i, jnp.maximum(j * r8 - 1, 0), 0))
    nxt = pl.BlockSpec((None, 8, w3), lambda i, j: (i, jnp.minimum((j + 1) * r8, nb8 - 1), 0))
    full = lambda a: pl.BlockSpec(a.shape, lambda i, j: (0,) * a.ndim)
    hw = hy_conv_w.T
    hb = hy_conv_b.reshape(1, w3)
    sw = sc_conv_w.T
    out = pl.BlockSpec((None, tm, GROUP_W), lambda i, j: (i, j, 0))
    return pl.pallas_call(
        _conv_kernel,
        grid=(b, s // tm),
        in_specs=[main, prev, nxt, main, prev, nxt, full(hw), full(hb), full(sw)],
        out_specs=[out] * 4,
        out_shape=[jax.ShapeDtypeStruct((b, s, GROUP_W), F32)] * 4,
        compiler_params=_params("parallel", "parallel"),
        name="conv_mixers",
    )(hy_u, hy_u, hy_u, sc_u, sc_u, sc_u, hw, hb, sw)


def _fft_dims(s):
    n = 2 * s
    lg = int(round(math.log2(n)))
    assert 1 << lg == n
    n1 = 1 << ((lg + 1) // 2)
    return n1, n // n1


def _filter_kernel(frow_ref, w1h_ref, w1l_ref, b1_ref, fr1_ref, w2h_ref, w2l_ref, b2_ref, fr2_ref,
                   w3h_ref, w3l_ref, b3_ref, ld_ref, k_ref, norm_ref, *, length):
    i = pl.program_id(0)
    tm = k_ref.shape[0]
    n = i * tm + lax.broadcasted_iota(jnp.int32, (tm, 1), 0)
    tt = jnp.where(n < length, n, 2 * length - 1 - n)
    t = tt.astype(F32) / length
    lane = lax.broadcasted_iota(jnp.int32, (tm, LANES), 1)
    ang = t * frow_ref[...]
    z = jnp.where(lane == 0, t,
                  jnp.where(lane <= HY_BANDS, jnp.sin(ang),
                            jnp.where(lane <= 2 * HY_BANDS, jnp.cos(ang), 0.0)))
    hid = jnp.sin(fr1_ref[...] * (_dot3(z, w1h_ref[...], w1l_ref[...]) + b1_ref[...]))
    hid = jnp.sin(fr2_ref[...] * (_dot3(hid, w2h_ref[...], w2l_ref[...]) + b2_ref[...]))
    filt = ((_dot3(hid, w3h_ref[...], w3l_ref[...]) + b3_ref[...])
            * jnp.exp(-t * jnp.exp(ld_ref[...])))
    k_ref[...] = filt

    @pl.when(i == 0)
    def _():
        norm_ref[...] = jnp.zeros_like(norm_ref)

    norm_ref[...] += jnp.sum(jnp.abs(filt), axis=0, keepdims=True)


def hyena_filter_taps(length, w1, b1, fr1, w2, b2, fr2, w3, b3, log_decay):
    tm = min(512, length)
    n_half = length // tm
    oc = 2 * GROUP_W
    bands = jnp.arange(1, HY_BANDS + 1, dtype=F32) * (2.0 * math.pi)
    frow = jnp.zeros((1, LANES), F32).at[0, 1:1 + HY_BANDS].set(bands)
    frow = frow.at[0, 1 + HY_BANDS:1 + 2 * HY_BANDS].set(bands)
    w1p = jnp.zeros((LANES, HY_FFN), F32).at[:1 + 2 * HY_BANDS].set(w1)
    bydir = lambda a: jnp.moveaxis(a.reshape(a.shape[0], 2, 2, GROUP_W), 2, 0).reshape(2, a.shape[0], oc)
    w3d, b3d, ldd = bydir(w3), bydir(b3.reshape(1, -1)), bydir(log_decay.reshape(1, -1))
    full = lambda a: pl.BlockSpec(a.shape, lambda i: (0,) * a.ndim)
    dirspec = lambda a: pl.BlockSpec((None,) + a.shape[1:], lambda i: (i // n_half, 0, 0))
    r1 = lambda a: a.reshape(1, -1)
    w1h, w1l = _split2(w1p)
    w2h, w2l = _split2(w2)
    w3h, w3l = _split2(w3d)
    return pl.pallas_call(
        functools.partial(_filter_kernel, length=length),
        grid=(2 * n_half,),
        in_specs=[full(frow), full(w1h), full(w1l), full(r1(b1)), full(r1(fr1)), full(w2h), full(w2l),
                  full(r1(b2)), full(r1(fr2)), dirspec(w3h), dirspec(w3l), dirspec(b3d), dirspec(ldd)],
        out_specs=[pl.BlockSpec((tm, oc), lambda i: (i, 0)), pl.BlockSpec((1, oc), lambda i: (0, 0))],
        out_shape=[jax.ShapeDtypeStruct((2 * length, oc), F32), jax.ShapeDtypeStruct((1, oc), F32)],
        compiler_params=_params("arbitrary"),
        name="hyena_filter",
    )(frow, w1h, w1l, r1(b1), r1(fr1), w2h, w2l, r1(b2), r1(fr2), w3h, w3l, b3d, ldd)


def _dft_consts(s):
    n1, n2 = _fft_dims(s)
    n = n1 * n2
    kh = n1 // 2 + 1
    kp = -(-kh // 4) * 4
    pad_rows = lambda a: np.concatenate([a, np.zeros((kp - kh, a.shape[1]))], axis=0)
    a1 = 2.0 * np.pi * np.outer(np.arange(kh), np.arange(n1)) / n1
    f1 = np.concatenate([pad_rows(np.cos(a1)), pad_rows(-np.sin(a1))], axis=0)
    a2 = 2.0 * np.pi * np.outer(np.arange(n2), np.arange(n2)) / n2
    c2, s2 = np.cos(a2), np.sin(a2)
    f2_fwd = np.block([[c2, s2], [-s2, c2]])
    f2_inv = np.block([[c2, -s2], [s2, c2]])
    at = 2.0 * np.pi * np.outer(np.arange(kh), np.arange(n2)) / n
    tw = np.stack([np.cos(at), np.sin(at)], axis=0)[..., None]
    m1 = np.arange(n1 // 2)
    a3 = 2.0 * np.pi * np.outer(m1, np.arange(kh)) / n1
    wgt = np.where((np.arange(kh) == 0) | (np.arange(kh) == n1 // 2), 1.0, 2.0) / n
    f3 = np.concatenate([pad_rows((np.cos(a3) * wgt).T).T, pad_rows((-np.sin(a3) * wgt).T).T],
                        axis=1)
    f = lambda a: jnp.asarray(a, F32)
    x3 = lambda a: _lhs_x3(f(a))
    return dict(n1=n1, n2=n2, kh=kh, kp=kp, f1=x3(f1), f1_half=x3(f1[:, :n1 // 2]), f2_fwd=x3(f2_fwd),
                f2_inv=x3(f2_inv), tw=f(tw), f3=x3(f3))


_N2_TILE = 8


def _dft1_kernel(x_ref, f_ref, s_ref, o_ref, xs_ref):
    f = f_ref[...]
    scale = s_ref[...]
    for m in range(_N2_TILE):
        xs_ref[...] = x_ref[:, m, :]
        o_ref[:, m, :] = _dot_x3(f, xs_ref[...] * scale)


def dft_stage1(x, f1, scale_row):
    g, r, n2, ch = x.shape
    m = f1.shape[0]
    return pl.pallas_call(
        _dft1_kernel,
        grid=(g, n2 // _N2_TILE),
        in_specs=[pl.BlockSpec((None, r, _N2_TILE, ch), lambda i, j: (i, 0, j, 0)),
                  pl.BlockSpec(f1.shape, lambda i, j: (0, 0)),
                  pl.BlockSpec((1, ch), lambda i, j: (0, 0))],
        out_specs=pl.BlockSpec((None, m, _N2_TILE, ch), lambda i, j: (i, 0, j, 0)),
        out_shape=jax.ShapeDtypeStruct((g, m, n2, ch), F32),
        scratch_shapes=[pltpu.VMEM((r, ch), F32)],
        compiler_params=_params("parallel", "parallel"),
        name="dft_stage1",
    )(x, f1, scale_row)


def _twiddle(re, im, tc, ts, conj):
    if conj:
        return re * tc - im * ts, im * tc + re * ts
    return re * tc + im * ts, im * tc - re * ts


def _dft2_filter_kernel(a_ref, tw_ref, ff_ref, o_ref):
    n2 = a_ref.shape[2]
    br, bi = _twiddle(a_ref[0, 0], a_ref[1, 0], tw_ref[0, 0], tw_ref[1, 0], False)
    zz = _dot_x3(ff_ref[...], jnp.concatenate([br, bi], axis=0))
    o_ref[0, 0] = zz[:n2]
    o_ref[1, 0] = zz[n2:]


def dft_stage2_filter(a, consts):
    _, _, n2, ch = a.shape
    return pl.pallas_call(
        _dft2_filter_kernel,
        grid=(consts["kh"],),
        input_output_aliases={0: 0},
        in_specs=[pl.BlockSpec((2, 1, n2, ch), lambda k: (0, k, 0, 0)),
                  pl.BlockSpec((2, 1, n2, 1), lambda k: (0, k, 0, 0)),
                  pl.BlockSpec(consts["f2_fwd"].shape, lambda k: (0, 0))],
        out_specs=pl.BlockSpec((2, 1, n2, ch), lambda k: (0, k, 0, 0)),
        out_shape=jax.ShapeDtypeStruct(a.shape, F32),
        compiler_params=_params("parallel"),
        name="dft_stage2_filter",
    )(a, consts["tw"], consts["f2_fwd"])


def _dft2_conv_kernel(a_ref, tw_ref, kf_ref, ff_ref, fi_ref, o_ref):
    n2 = a_ref.shape[2]
    tc, ts = tw_ref[0, 0], tw_ref[1, 0]
    br, bi = _twiddle(a_ref[0, 0], a_ref[1, 0], tc, ts, False)
    zz = _dot_x3(ff_ref[...], jnp.concatenate([br, bi], axis=0))
    zr, zi = zz[:n2], zz[n2:]
    kr, ki = kf_ref[0, 0], kf_ref[1, 0]
    pr = zr * kr - zi * ki
    pi = zr * ki + zi * kr
    vv = _dot_x3(fi_ref[...], jnp.concatenate([pr, pi], axis=0))
    vr, vi = _twiddle(vv[:n2], vv[n2:], tc, ts, True)
    o_ref[0, 0] = vr
    o_ref[1, 0] = vi


def dft_stage2_conv(a, kf, order, consts):
    b, _, _, n2, ch = a.shape
    blk = pl.BlockSpec((None, 2, 1, n2, ch), lambda i, k: (i, 0, k, 0, 0))
    mat = pl.BlockSpec(consts["f2_fwd"].shape, lambda i, k: (0, 0))
    return pl.pallas_call(
        _dft2_conv_kernel,
        grid=(b, consts["kh"]),
        input_output_aliases={0: 0},
        in_specs=[blk,
                  pl.BlockSpec((2, 1, n2, 1), lambda i, k: (0, k, 0, 0)),
                  pl.BlockSpec((2, 1, n2, ch), lambda i, k: (0, k, 0, order)),
                  mat, mat],
        out_specs=blk,
        out_shape=jax.ShapeDtypeStruct(a.shape, F32),
        compiler_params=_params("parallel", "parallel"),
        name="dft_stage2_conv",
    )(a, consts["tw"], kf, consts["f2_fwd"], consts["f2_inv"])


def _dft3_kernel(v_ref, f_ref, gate_ref, z_ref, bias_ref, o_ref, vs_ref, ys_ref):
    f = f_ref[...]
    for m in range(_N2_TILE):
        vs_ref[...] = v_ref[:, m, :]
        ys_ref[:, m, :] = _dot_x3(f, vs_ref[...])
    o_ref[...] = gate_ref[...] * (ys_ref[...] + z_ref[...] * bias_ref[...])


def dft_stage3_gate(v, f3, gate, z, bias_row):
    b, m, n2, ch = v.shape
    r = f3.shape[0]
    row = pl.BlockSpec((None, r, _N2_TILE, ch), lambda i, j: (i, 0, j, 0))
    return pl.pallas_call(
        _dft3_kernel,
        grid=(b, n2 // _N2_TILE),
        in_specs=[pl.BlockSpec((None, m, _N2_TILE, ch), lambda i, j: (i, 0, j, 0)),
                  pl.BlockSpec(f3.shape, lambda i, j: (0, 0)),
                  row, row, pl.BlockSpec((1, ch), lambda i, j: (0, 0))],
        out_specs=row,
        out_shape=jax.ShapeDtypeStruct((b, r, n2, ch), F32),
        scratch_shapes=[pltpu.VMEM((m, ch), F32), pltpu.VMEM((r, _N2_TILE, ch), F32)],
        compiler_params=_params("parallel", "parallel"),
        name="dft_stage3_gate",
    )(v, f3, gate, z, bias_row)


def hyena_mixer(x1, x2, z, hy_w1, hy_b1, hy_fr1, hy_w2, hy_b2, hy_fr2, hy_w3, hy_b3, hy_log_decay,
                hy_bias):
    b, s, ch = z.shape
    consts = _dft_consts(s)
    n1, n2 = consts["n1"], consts["n2"]
    oc = 2 * ch
    taps, norm = hyena_filter_taps(s, hy_w1, hy_b1, hy_fr1, hy_w2, hy_b2, hy_fr2, hy_w3, hy_b3,
                                   hy_log_decay)
    kp = consts["kp"]
    ka = dft_stage1(taps.reshape(1, n1, n2, oc), consts["f1"], 1.0 / norm)
    kf = dft_stage2_filter(ka.reshape(2, kp, n2, oc), consts)
    half = n1 // 2
    view = lambda t: t.reshape(b, half, n2, ch)
    ones = jnp.ones((1, ch), F32)
    f1_half = consts["f1_half"]
    cur = view(z)
    for order, gate in enumerate((x1, x2)):
        a = dft_stage1(cur, f1_half, ones)
        v = dft_stage2_conv(a.reshape(b, 2, kp, n2, ch), kf, order, consts)
        cur = dft_stage3_gate(v.reshape(b, 2 * kp, n2, ch), consts["f3"], view(gate), cur,
                              hy_bias[order].reshape(1, ch))
    return cur.reshape(b, s, ch)


def _outproj_kernel(x_ref, hf_ref, hb_ref, o_ref, yb_ref, yc_ref, yd_ref, og_ref, hm_ref, wout_ref,
                    pg_ref, g1_ref, fg_ref, sc2_ref, sh2_ref, rwt_ref,
                    xn_ref, h2_ref, afft_ref):
    hm = hm_ref[...]
    y_a = _sigmoid(o_ref[...]) * (hf_ref[...] + hb_ref[...])
    acc = None
    for idx, y in enumerate((y_a, yb_ref[...], yc_ref[...], yd_ref[...])):
        ms = _dot(jnp.concatenate(_split2(y * y), axis=1), hm)
        yn = y * lax.rsqrt(ms + NORM_EPS) * og_ref[:, idx * GROUP_W:(idx + 1) * GROUP_W]
        part = _dot(yn.astype(BF16), wout_ref[idx * GROUP_W:(idx + 1) * GROUP_W, :])
        acc = part if acc is None else acc + part
    d = acc.shape[-1]
    xn = x_ref[...] + g1_ref[...] * (_rms(acc, d) * pg_ref[...])
    xn_ref[...] = xn
    h2 = _rms(xn, d) * fg_ref[...] * (1.0 + sc2_ref[...]) + sh2_ref[...]
    h2_ref[...] = h2.T.astype(BF16)
    logits_t = _dot_nt_hi(rwt_ref[...], h2)
    mx = jnp.max(logits_t, axis=0, keepdims=True)
    ex = jnp.exp(logits_t - mx)
    afft_ref[...] = ex / jnp.sum(ex, axis=0, keepdims=True)


def out_proj(x, hf, hb, o, yb, yc, yd, out_g, w_out, post_g, g1, ffn_g, sc2, sh2, router_w):
    b, s, d = x.shape
    tm = min(512, s)
    e = router_w.shape[1]
    hm1 = np.kron(np.eye(GROUP_W // HEAD_DIM), np.ones((HEAD_DIM, HEAD_DIM))) / HEAD_DIM
    hm = jnp.asarray(np.concatenate([hm1, hm1], axis=0), BF16)
    row = lambda w: pl.BlockSpec((None, tm, w), lambda i, j: (i, j, 0))
    full = lambda a: pl.BlockSpec(a.shape, lambda i, j: (0,) * a.ndim)
    vec = lambda: pl.BlockSpec((None, 1, d), lambda i, j: (i, 0, 0))
    r1 = lambda a: a.reshape(1, -1)
    wob = w_out.astype(BF16)
    rwt = router_w.T
    return pl.pallas_call(
        _outproj_kernel,
        grid=(b, s // tm),
        in_specs=[row(d)] + [row(GROUP_W)] * 6 + [full(r1(out_g)), full(hm), full(wob),
                                                  full(r1(post_g)), vec(), full(r1(ffn_g)), vec(), vec(),
                                                  full(rwt)],
        out_specs=[row(d), pl.BlockSpec((None, d, tm), lambda i, j: (i, 0, j)),
                   pl.BlockSpec((None, e, tm), lambda i, j: (i, 0, j))],
        out_shape=[jax.ShapeDtypeStruct((b, s, d), F32), jax.ShapeDtypeStruct((b, d, s), BF16),
                   jax.ShapeDtypeStruct((b, e, s), F32)],
        compiler_params=_params("parallel", "parallel"),
        name="out_proj",
    )(x, hf, hb, o, yb, yc, yd, r1(out_g), hm, wob, r1(post_g), g1.reshape(b, 1, d), r1(ffn_g),
      sc2.reshape(b, 1, d), sh2.reshape(b, 1, d), rwt)


def _lane_cumsum(x):
    n = x.shape[-1]
    lane = lax.broadcasted_iota(jnp.int32, x.shape, x.ndim - 1)
    shift = 1
    while shift < n:
        x = x + jnp.where(lane >= shift, pltpu.roll(x, shift, x.ndim - 1), 0.0)
        shift *= 2
    return x


def _select_kernel(aff_ref, u_ref, ps_ref, cb_ref, *, cap):
    aff = aff_ref[...]
    capf = float(cap)

    def body(i, bits):
        cand = bits | (jnp.int32(1) << (30 - i))
        cnt = jnp.sum(jnp.where(aff >= pltpu.bitcast(cand, F32), 1.0, 0.0), axis=-1, keepdims=True)
        return jnp.where(cnt >= capf, cand, bits)

    bits = lax.fori_loop(0, 31, body, jnp.zeros((aff.shape[0], 1), jnp.int32))
    thr = pltpu.bitcast(bits, F32)
    gt = aff > thr
    eq = aff == thr
    n_gt = jnp.sum(jnp.where(gt, 1.0, 0.0), axis=-1, keepdims=True)
    eqf = jnp.where(eq, 1.0, 0.0)
    rank_eq = _lane_cumsum(eqf) - eqf
    sel = gt | (eq & (rank_eq < capf - n_gt))
    self_ = jnp.where(sel, 1.0, 0.0)
    pos = _lane_cumsum(self_) - self_
    ps_ref[...] = jnp.where(sel, pos, -1.0).astype(jnp.int32)
    cb_ref[...] = _dot(self_.astype(BF16), u_ref[...]).astype(jnp.int32)


def ec_select(aff_t, cap):
    b, e, s = aff_t.shape
    nt = s // TOKEN_TILE
    assert nt + 1 <= LANES
    tok = np.arange(s)[:, None]
    u = jnp.asarray(tok < (np.arange(LANES)[None, :] * TOKEN_TILE), BF16)
    return pl.pallas_call(
        functools.partial(_select_kernel, cap=cap),
        grid=(b,),
        in_specs=[pl.BlockSpec((None, e, s), lambda i: (i, 0, 0)),
                  pl.BlockSpec((s, LANES), lambda i: (0, 0))],
        out_specs=[pl.BlockSpec((None, e, s), lambda i: (i, 0, 0)),
                   pl.BlockSpec((None, e, LANES), lambda i: (i, 0, 0))],
        out_shape=[jax.ShapeDtypeStruct((b, e, s), jnp.int32),
                   jax.ShapeDtypeStruct((b, e, LANES), jnp.int32)],
        compiler_params=_params("parallel"),
        name="ec_select",
    )(aff_t, u)


def _gather_kernel(cb_ref, ps_ref, ht_ref, o_ref, *, n_tiles, n_blocks, block, chunk):
    bi, ei = pl.program_id(0), pl.program_id(1)
    base = (bi * N_EXPERTS + ei) * LANES
    d, s = ht_ref.shape
    slot = lax.broadcasted_iota(jnp.int32, (block, chunk), 0)
    tok = lax.broadcasted_iota(jnp.int32, (1, chunk), 1)

    def one_block(j, carry):
        first = j * block

        def bounds(t, c):
            return (c[0] + (cb_ref[base + t + 1] <= first).astype(jnp.int32),
                    c[1] + (cb_ref[base + t] < first + block).astype(jnp.int32))

        t_lo, t_hi = lax.fori_loop(0, n_tiles, bounds, (jnp.int32(0), jnp.int32(0)))
        lo_tok = t_lo * TOKEN_TILE

        def part(i, acc):
            lower = lo_tok + i * chunk
            start = pl.multiple_of(jnp.minimum(lower, s - chunk), TOKEN_TILE)
            ps = jnp.where(tok + start >= lower, ps_ref[:, pl.ds(start, chunk)], -1)
            onehot = jnp.where(ps == slot + first, 1.0, 0.0).astype(BF16)
            return acc + _dot_nt(ht_ref[:, pl.ds(start, chunk)], onehot)

        n_parts = ((t_hi - t_lo) * TOKEN_TILE + chunk - 1) // chunk
        acc = lax.fori_loop(0, n_parts, part, jnp.zeros((d, block), F32))
        o_ref[pl.ds(pl.multiple_of(first, block), block), :] = acc.T.astype(o_ref.dtype)
        return carry

    lax.fori_loop(0, n_blocks, one_block, 0)


def ec_gather(ht, ps, cb, cap):
    b, d, s = ht.shape
    e = ps.shape[1]
    nt = s // TOKEN_TILE
    block = min(2 * SLOT_BLOCK, cap)
    chunk = min(block * s // cap + 2 * TOKEN_TILE, s)
    grid_spec = pltpu.PrefetchScalarGridSpec(
        num_scalar_prefetch=1,
        grid=(b, e),
        in_specs=[pl.BlockSpec((None, None, 1, s), lambda i, j, cb: (i, j, 0, 0)),
                  pl.BlockSpec((None, d, s), lambda i, j, cb: (i, 0, 0))],
        out_specs=pl.BlockSpec((None, None, cap, d), lambda i, j, cb: (i, j, 0, 0)),
    )
    return pl.pallas_call(
        functools.partial(_gather_kernel, n_tiles=nt, n_blocks=cap // block, block=block, chunk=chunk),
        grid_spec=grid_spec,
        out_shape=jax.ShapeDtypeStruct((b, e, cap, d), BF16),
        compiler_params=_params("parallel", "arbitrary"),
        name="ec_gather",
    )(cb.reshape(-1), ps.reshape(b, e, 1, s), ht)


def _ffn_kernel(x_ref, wg_ref, wu_ref, wd_ref, o_ref, acc_ref, *, n_f):
    f = pl.program_id(1)

    @pl.when(f == 0)
    def _():
        acc_ref[...] = jnp.zeros_like(acc_ref)

    bsz, cap, d = x_ref.shape
    x = x_ref[...].reshape(bsz * cap, d)
    a = _dot(x, wg_ref[...].astype(BF16))
    up = _dot(x, wu_ref[...].astype(BF16))
    act = (a * _sigmoid(a) * up).astype(BF16)
    acc_ref[...] += _dot(act, wd_ref[...].astype(BF16))

    @pl.when(f == n_f - 1)
    def _():
        o_ref[...] = acc_ref[...].reshape(bsz, cap, d).astype(o_ref.dtype)


def expert_ffn(xe, w_gate, w_up, w_down, layer):
    b, e, cap, d = xe.shape
    ff = w_gate.shape[-1]
    tf = min(512, ff)
    n_f = ff // tf
    return pl.pallas_call(
        functools.partial(_ffn_kernel, n_f=n_f),
        grid=(e, n_f),
        in_specs=[pl.BlockSpec((b, None, cap, d), lambda j, f: (0, j, 0, 0)),
                  pl.BlockSpec((None, None, d, tf), lambda j, f: (layer, j, 0, f)),
                  pl.BlockSpec((None, None, d, tf), lambda j, f: (layer, j, 0, f)),
                  pl.BlockSpec((None, None, tf, d), lambda j, f: (layer, j, f, 0))],
        out_specs=pl.BlockSpec((b, None, cap, d), lambda j, f: (0, j, 0, 0)),
        out_shape=jax.ShapeDtypeStruct((b, e, cap, d), BF16),
        scratch_shapes=[pltpu.VMEM((b * cap, d), F32)],
        compiler_params=_params("parallel", "arbitrary"),
        name="expert_ffn",
    )(xe, w_gate, w_up, w_down)


def _scatter_kernel(cb_ref, ps_ref, aff_ref, ye_ref, x_ref, pg_ref, g2_ref, o_ref, acc_ref,
                    *, n_sub, window):
    bi, ti, ei = pl.program_id(0), pl.program_id(1), pl.program_id(2)

    @pl.when(ei == 0)
    def _():
        acc_ref[...] = jnp.zeros_like(acc_ref)

    tt = acc_ref.shape[0]
    base = (bi * N_EXPERTS + ei) * LANES + ti * n_sub
    lo = cb_ref[base]
    hi = cb_ref[base + n_sub]

    @pl.when(hi > lo)
    def _():
        elane = lax.broadcasted_iota(jnp.int32, (tt, N_EXPERTS), 1)
        ps = jnp.sum(jnp.where(elane == ei, ps_ref[...], 0), axis=-1, keepdims=True)
        gate = jnp.sum(jnp.where(elane == ei, aff_ref[...], 0.0), axis=-1, keepdims=True)
        slot = lax.broadcasted_iota(jnp.int32, (tt, window), 1)
        cap = ye_ref.shape[0]
        first = (lo // BF16_ROWS) * BF16_ROWS

        def body(j, carry):
            lower = first + j * window
            start = pl.multiple_of(jnp.minimum(lower, cap - window), BF16_ROWS)
            onehot = jnp.where(jnp.where(ps >= lower, ps, -1) == slot + start, 1.0, 0.0).astype(BF16)
            acc_ref[...] += gate * _dot(onehot, ye_ref[pl.ds(start, window), :])
            return carry

        lax.fori_loop(0, (hi - first + window - 1) // window, body, 0)

    @pl.when(ei == N_EXPERTS - 1)
    def _():
        y = acc_ref[...]
        o_ref[...] = x_ref[...] + g2_ref[...] * (_rms(y, y.shape[-1]) * pg_ref[...])


def ec_scatter(ye, ps_col, aff, cb, x, post_g, g2):
    b, e, cap, d = ye.shape
    s = x.shape[1]
    n_sub = min(4, s // TOKEN_TILE)
    tt = n_sub * TOKEN_TILE
    grid_spec = pltpu.PrefetchScalarGridSpec(
        num_scalar_prefetch=1,
        grid=(b, s // tt, e),
        in_specs=[pl.BlockSpec((None, tt, e), lambda i, t, j, cb: (i, t, 0)),
                  pl.BlockSpec((None, tt, e), lambda i, t, j, cb: (i, t, 0)),
                  pl.BlockSpec((None, None, cap, d), lambda i, t, j, cb: (i, j, 0, 0)),
                  pl.BlockSpec((None, tt, d), lambda i, t, j, cb: (i, t, 0)),
                  pl.BlockSpec((1, d), lambda i, t, j, cb: (0, 0)),
                  pl.BlockSpec((None, 1, d), lambda i, t, j, cb: (i, 0, 0))],
        out_specs=pl.BlockSpec((None, tt, d), lambda i, t, j, cb: (i, t, 0)),
        scratch_shapes=[pltpu.VMEM((tt, d), F32)],
    )
    return pl.pallas_call(
        functools.partial(_scatter_kernel, n_sub=n_sub, window=min(2 * SLOT_BLOCK, cap)),
        grid_spec=grid_spec,
        out_shape=jax.ShapeDtypeStruct((b, s, d), F32),
        compiler_params=_params("parallel", "parallel", "arbitrary"),
        name="ec_scatter",
    )(cb.reshape(-1), ps_col, aff, ye, x, post_g.reshape(1, d), g2.reshape(b, 1, d))


def kernel(x, c, positions, ada_w, ada_b, mix_pre_g, mix_post_g, ffn_pre_g, ffn_post_g, w_in, ml_gate_b, mla_q_norm, mla_kv_norm, mla_w_uq, mla_w_uk, mla_w_uv, hy_conv_w, hy_conv_b, hy_w1, hy_b1, hy_fr1, hy_w2, hy_b2, hy_fr2, hy_w3, hy_b3, hy_log_decay, hy_bias, sc_conv_w, mix_out_g, w_out, router_w, exp_w_gate, exp_w_up, exp_w_down):
    depth = ada_w.shape[0]
    b, s, d = x.shape
    cap = EC_CAPACITY * s // N_EXPERTS
    mod = ada_mod(c, ada_w, ada_b)
    tables = rope_tables(positions)
    for l in range(depth):
        sh1, sc1, g1, sh2, sc2, g2 = (mod[l, :, i * d:(i + 1) * d] for i in range(6))
        u = in_proj(x, mix_pre_g[l], sc1, sh1, w_in[l])
        hf, hb = mlstm(u["q"], u["v"], u["kT"], u["gT"], u["g"], ml_gate_b[l])
        qa, ka, va = mla_proj(u["cq"], u["ckv"], u["kr"], tables, mla_q_norm[l], mla_kv_norm[l],
                              mla_w_uq[l], mla_w_uk[l], mla_w_uv[l])
        y_b = flash_attention(qa, ka, va)
        x1, x2, z, y_d = conv_mixers(u["hy"], u["sc"], hy_conv_w[l], hy_conv_b[l], sc_conv_w[l])
        y_c = hyena_mixer(x1, x2, z, hy_w1[l], hy_b1[l], hy_fr1[l], hy_w2[l], hy_b2[l], hy_fr2[l],
                          hy_w3[l], hy_b3[l], hy_log_decay[l], hy_bias[l])
        xn, h2, aff_t = out_proj(x, hf, hb, u["o"], y_b, y_c, y_d, mix_out_g[l], w_out[l],
                                      mix_post_g[l], g1, ffn_pre_g[l], sc2, sh2, router_w[l])
        ps, cb = ec_select(aff_t, cap)
        xe = ec_gather(h2, ps, cb, cap)
        ye = expert_ffn(xe, exp_w_gate, exp_w_up, exp_w_down, l)
        x = ec_scatter(ye, jnp.swapaxes(ps, 1, 2), jnp.swapaxes(aff_t, 1, 2), cb, xn, ffn_post_g[l], g2)
    return x
```

```python
import functools
import math

import numpy as np
import jax
import jax.numpy as jnp
from jax import lax
from jax.experimental import pallas as pl
from jax.experimental.pallas import tpu as pltpu

F32 = jnp.float32
BF16 = jnp.bfloat16
HIGHEST = lax.Precision.HIGHEST

GROUP_W = 256
HEAD_DIM = 64
N_HEADS = 4
ML_CHUNK = 128
MLA_Q_RANK = 224
MLA_KV_RANK = 128
MLA_NOPE = 64
MLA_ROPE = 32
ROPE_HALF = MLA_ROPE // 2
ROPE_THETA = 10000.0
HY_BANDS = 8
HY_FFN = 64
N_EXPERTS = 16
EC_CAPACITY = 2
NORM_EPS = 1e-6
LANES = 128
BF16_ROWS = 16
V_ROWS = HEAD_DIM + BF16_ROWS
SLOT_BLOCK = 128
TOKEN_TILE = 256
VMEM_LIMIT = 56 * 1024 * 1024


def _params(*sem):
    return pltpu.CompilerParams(dimension_semantics=sem, vmem_limit_bytes=VMEM_LIMIT)


def _dot(a, b):
    return jnp.dot(a, b, preferred_element_type=F32)


def _dot_hi(a, b):
    return jnp.dot(a, b, precision=HIGHEST, preferred_element_type=F32)


def _dot_nt(a, b):
    return lax.dot_general(a, b, (((1,), (1,)), ((), ())), preferred_element_type=F32)


def _dot_nt_hi(a, b):
    return lax.dot_general(a, b, (((1,), (1,)), ((), ())), precision=HIGHEST,
                           preferred_element_type=F32)


def _split2(x):
    hi = x.astype(BF16)
    lo = (x - hi.astype(F32)).astype(BF16)
    return hi, lo


def _split3(x):
    hi = x.astype(BF16)
    r = x - hi.astype(F32)
    mid = r.astype(BF16)
    lo = (r - mid.astype(F32)).astype(BF16)
    return hi, mid, lo


def _lhs_x3(f):
    hi, lo = _split2(f)
    return jnp.concatenate([hi, lo, hi], axis=1)


def _dot_x3(f3, x):
    hi, lo = _split2(x)
    return _dot(f3, jnp.concatenate([hi, hi, lo], axis=0))


def _dot3(a, w_hi, w_lo):
    a_hi, a_lo = _split2(a)
    return _dot(a_hi, w_hi) + _dot(a_hi, w_lo) + _dot(a_lo, w_hi)


def _rms(x, n):
    ms = jnp.sum(x * x, axis=-1, keepdims=True) * (1.0 / n)
    return x * lax.rsqrt(ms + NORM_EPS)


def _log_sigmoid(x):
    return jnp.minimum(x, 0.0) - jnp.log(1.0 + jnp.exp(-jnp.abs(x)))


def _sigmoid(x):
    return 1.0 / (1.0 + jnp.exp(-x))


def _ada_kernel(c_ref, w_ref, b_ref, o_ref):
    c = c_ref[...]
    cs = c * _sigmoid(c)
    o_ref[...] = _dot_hi(cs, w_ref[...]) + b_ref[...]


def ada_mod(c, ada_w, ada_b):
    depth, d, n6 = ada_w.shape
    b = c.shape[0]
    bp = 8
    cp = jnp.zeros((bp, d), F32).at[:b].set(c)
    tn = 1536
    out = pl.pallas_call(
        _ada_kernel,
        grid=(depth, n6 // tn),
        in_specs=[pl.BlockSpec((bp, d), lambda l, j: (0, 0)),
                  pl.BlockSpec((None, d, tn), lambda l, j: (l, 0, j)),
                  pl.BlockSpec((None, 1, tn), lambda l, j: (l, 0, j))],
        out_specs=pl.BlockSpec((None, bp, tn), lambda l, j: (l, 0, j)),
        out_shape=jax.ShapeDtypeStruct((depth, bp, n6), F32),
        compiler_params=_params("parallel", "parallel"),
        name="ada_mod",
    )(cp, ada_w, ada_b.reshape(depth, 1, n6))
    return out[:, :b]


_U_COLS = (("q", 256), ("v", 256), ("o", 256), ("g", 128), ("cq", 256), ("ckv", 128),
           ("kr", 128), ("hy", 768), ("sc", 768))
_U_TOTAL = sum(w for _, w in _U_COLS)
_UT_ROWS = 256 + 16


def _inproj_kernel(x_ref, gain_ref, sc_ref, sh_ref, w_ref, wt_ref,
                   q_ref, v_ref, o_ref, g_ref, cq_ref, ckv_ref, kr_ref, hy_ref, scu_ref,
                   kt_ref, gt_ref):
    x = x_ref[...]
    d = x.shape[-1]
    h = _rms(x, d) * gain_ref[...] * (1.0 + sc_ref[...]) + sh_ref[...]
    hb = h.astype(BF16)
    u = _dot(hb, w_ref[...])
    off = 0
    for ref, (_, width) in zip((q_ref, v_ref, o_ref, g_ref, cq_ref, ckv_ref, kr_ref, hy_ref, scu_ref),
                               _U_COLS):
        ref[...] = u[:, off:off + width]
        off += width
    ut = _dot_nt(wt_ref[...], hb)
    kt_ref[...] = ut[:256]
    gt_ref[...] = ut[256:]


def in_proj(x, gain, scale, shift, w_in):
    b, s, d = x.shape
    tm = min(512, s)
    cuts = np.cumsum([0, 256, 256, 256, 256, 16, MLA_Q_RANK, MLA_KV_RANK, MLA_ROPE, 768, 768])
    wq, wk, wv, wo, wg, wcq, wckv, wkr, why, wsc = (w_in[:, cuts[i]:cuts[i + 1]] for i in range(10))
    pad = lambda w, n: jnp.pad(w, ((0, 0), (0, n - w.shape[1])))
    wkr_p = jnp.pad(wkr, ((0, 0), (MLA_NOPE, LANES - MLA_NOPE - MLA_ROPE)))
    w1 = jnp.concatenate([wq, wv, wo, pad(wg, 128), pad(wcq, 256), wckv, wkr_p, why, wsc],
                         axis=1).astype(BF16)
    w2t = jnp.concatenate([wk, wg], axis=1).T.astype(BF16)
    row = lambda nm, w: pl.BlockSpec((None, tm, w), lambda i, j: (i, j, 0))
    out_shapes = [jax.ShapeDtypeStruct((b, s, w), F32) for _, w in _U_COLS]
    out_shapes += [jax.ShapeDtypeStruct((b, 256, s), F32), jax.ShapeDtypeStruct((b, 16, s), F32)]
    out_specs = [row(nm, w) for nm, w in _U_COLS]
    out_specs += [pl.BlockSpec((None, 256, tm), lambda i, j: (i, 0, j)),
                  pl.BlockSpec((None, 16, tm), lambda i, j: (i, 0, j))]
    vec = lambda: pl.BlockSpec((None, 1, d), lambda i, j: (i, 0, 0))
    outs = pl.pallas_call(
        _inproj_kernel,
        grid=(b, s // tm),
        in_specs=[pl.BlockSpec((None, tm, d), lambda i, j: (i, j, 0)),
                  pl.BlockSpec((1, d), lambda i, j: (0, 0)),
                  vec(), vec(),
                  pl.BlockSpec((d, _U_TOTAL), lambda i, j: (0, 0)),
                  pl.BlockSpec((_UT_ROWS, d), lambda i, j: (0, 0))],
        out_specs=out_specs,
        out_shape=out_shapes,
        compiler_params=_params("parallel", "parallel"),
        name="in_proj",
    )(x, gain.reshape(1, d), scale.reshape(b, 1, d), shift.reshape(b, 1, d), w1, w2t)
    names = [nm for nm, _ in _U_COLS] + ["kT", "gT"]
    return dict(zip(names, outs))


def _mlstm_dir(q, v, kt, gt, gc, c_ref, m_ref, base, rev):
    L = q.shape[0]
    r = lax.broadcasted_iota(jnp.int32, (L, L), 0)
    c = lax.broadcasted_iota(jnp.int32, (L, L), 1)
    tri = (c >= r) if rev else (c <= r)
    io, fo = (8, 12) if rev else (0, 4)
    logf_rows = _log_sigmoid(gt[fo:fo + 4, :])
    logf_cols = _log_sigmoid(gc)
    tri_b = jnp.where(tri, 1.0, 0.0).astype(BF16)
    a_cols = _dot(jnp.concatenate([tri_b] * 3, axis=1),
                  jnp.concatenate(_split3(logf_cols), axis=0))
    tri_t = (r >= c) if rev else (r <= c)
    tri_tb = jnp.where(tri_t, 1.0, 0.0).astype(BF16)
    a_rows = _dot(jnp.concatenate(_split3(_log_sigmoid(gt)), axis=1),
                  jnp.concatenate([tri_tb] * 3, axis=0))[fo:fo + 4, :]
    lane = lax.broadcasted_iota(jnp.int32, (L, LANES), 1)
    row128 = lax.broadcasted_iota(jnp.int32, (LANES, L), 0)
    scale = HEAD_DIM ** -0.5
    outs = []
    for pair in range(2):
        qp = q[:, pair * LANES:(pair + 1) * LANES]
        vp = v[:, pair * LANES:(pair + 1) * LANES]
        ktp = kt[pair * LANES:(pair + 1) * LANES, :] * scale
        pair_out = None
        for sub in range(2):
            h = pair * 2 + sub
            in_head = (row128 >= sub * HEAD_DIM) & (row128 < (sub + 1) * HEAD_DIM)
            kth = jnp.where(in_head, ktp, 0.0)
            vsh = vp if sub == 0 else pltpu.roll(vp, HEAD_DIM, 1)
            v_aug = jnp.where(lane < HEAD_DIM, vsh, jnp.where(lane == HEAD_DIM, 1.0, 0.0))
            v_aug_b = v_aug.astype(BF16)
            a_c = a_cols[:, fo + h:fo + h + 1]
            a_r = a_rows[h:h + 1, :]
            ig_r = gt[io + h:io + h + 1, :]
            a_end = jnp.sum(logf_rows[h:h + 1, :], axis=-1, keepdims=True)
            c_st = c_ref[base + h]
            m_st = m_ref[base + h][:, 0:1]
            d_log = jnp.where(tri, a_c - a_r + ig_r, -jnp.inf)
            inter = a_c + m_st
            m_t = jnp.maximum(inter, jnp.max(d_log, axis=-1, keepdims=True))
            qb = qp.astype(BF16)
            p = jnp.exp(d_log - m_t) * _dot(qb, kth.astype(BF16))
            sci = jnp.exp(inter - m_t)
            nd = _dot(p.astype(BF16), v_aug_b) + sci * _dot(qb, c_st.astype(BF16))
            den = nd[:, HEAD_DIM:HEAD_DIM + 1]
            out = nd / jnp.maximum(jnp.abs(den), jnp.exp(-m_t))
            w_st = a_end - a_r + ig_r
            m_loc = jnp.max(w_st, axis=-1, keepdims=True)
            ke = kth * jnp.exp(w_st - m_loc)
            c_loc = _dot(ke.astype(BF16), v_aug_b)
            m_new = jnp.maximum(a_end + m_st, m_loc)
            sp = jnp.exp(a_end + m_st - m_new)
            sl = jnp.exp(m_loc - m_new)
            c_ref[base + h] = sp * c_st + sl * c_loc
            m_ref[base + h] = jnp.broadcast_to(m_new, (1, LANES))
            if sub == 0:
                pair_out = out
            else:
                pair_out = jnp.where(lane < HEAD_DIM, pair_out, pltpu.roll(out, HEAD_DIM, 1))
        outs.append(pair_out)
    return jnp.concatenate(outs, axis=1)


def _mlstm_kernel(qf_ref, vf_ref, ktf_ref, gtf_ref, gcf_ref,
                  qb_ref, vb_ref, ktb_ref, gtb_ref, gcb_ref, brow_ref, bcol_ref,
                  hf_ref, hb_ref, c_ref, m_ref, *, n_sub):
    @pl.when(pl.program_id(1) == 0)
    def _():
        c_ref[...] = jnp.zeros_like(c_ref)
        m_ref[...] = jnp.zeros_like(m_ref)

    bcol = bcol_ref[...]
    brow = brow_ref[...]
    L = ML_CHUNK
    for ci in range(n_sub):
        rf = slice(ci * L, (ci + 1) * L)
        hf_ref[rf, :] = _mlstm_dir(qf_ref[rf, :], vf_ref[rf, :], ktf_ref[:, rf], gtf_ref[:, rf] + bcol,
                                   gcf_ref[rf, :] + brow, c_ref, m_ref, 0, False)
        rb = slice((n_sub - 1 - ci) * L, (n_sub - ci) * L)
        hb_ref[rb, :] = _mlstm_dir(qb_ref[rb, :], vb_ref[rb, :], ktb_ref[:, rb], gtb_ref[:, rb] + bcol,
                                   gcb_ref[rb, :] + brow, c_ref, m_ref, N_HEADS, True)


def mlstm(q, v, kt, gt, g, gate_b):
    b, s, w = q.shape
    n_sub = 2
    L = n_sub * ML_CHUNK
    nc = s // L
    bflat = gate_b.reshape(16)
    brow = jnp.zeros((1, LANES), F32).at[0, :16].set(bflat)
    bcol = bflat.reshape(16, 1)
    fw = lambda i, j: (i, j, 0)
    bw = lambda i, j: (i, nc - 1 - j, 0)
    fwt = lambda i, j: (i, 0, j)
    bwt = lambda i, j: (i, 0, nc - 1 - j)

    def specs(m, mt):
        return [pl.BlockSpec((None, L, w), m), pl.BlockSpec((None, L, w), m),
                pl.BlockSpec((None, w, L), mt), pl.BlockSpec((None, 16, L), mt),
                pl.BlockSpec((None, L, LANES), m)]

    return pl.pallas_call(
        functools.partial(_mlstm_kernel, n_sub=n_sub),
        grid=(b, nc),
        in_specs=specs(fw, fwt) + specs(bw, bwt) + [
            pl.BlockSpec((1, LANES), lambda i, j: (0, 0)),
            pl.BlockSpec((16, 1), lambda i, j: (0, 0))],
        out_specs=[pl.BlockSpec((None, L, w), fw), pl.BlockSpec((None, L, w), bw)],
        out_shape=[jax.ShapeDtypeStruct((b, s, w), F32)] * 2,
        scratch_shapes=[pltpu.VMEM((2 * N_HEADS, LANES, LANES), F32),
                        pltpu.VMEM((2 * N_HEADS, 1, LANES), F32)],
        compiler_params=_params("parallel", "arbitrary"),
        name="mlstm",
    )(q, v, kt, gt, g, q, v, kt, gt, g, brow, bcol)


def _rope_table_kernel(pos_ref, post_ref, inv_ref, invt_ref, cos_ref, sin_ref, cost_ref, sint_ref):
    ang = pos_ref[...].astype(F32) * inv_ref[...]
    cos_ref[...] = jnp.cos(ang)
    sin_ref[...] = jnp.sin(ang)
    ang_t = invt_ref[...] * post_ref[...].astype(F32)
    cost_ref[...] = jnp.cos(ang_t)
    sint_ref[...] = jnp.sin(ang_t)


def rope_tables(positions):
    b, s = positions.shape
    tm = min(512, s)
    inv = ROPE_THETA ** (-jnp.arange(ROPE_HALF, dtype=F32) / ROPE_HALF)
    inv_row = jnp.zeros((1, LANES), F32).at[0, MLA_NOPE:MLA_NOPE + ROPE_HALF].set(inv)
    inv_row = inv_row.at[0, MLA_NOPE + ROPE_HALF:MLA_NOPE + MLA_ROPE].set(inv)
    spec = pl.BlockSpec((None, tm, LANES), lambda i, j: (i, j, 0))
    spec_t = pl.BlockSpec((None, LANES, tm), lambda i, j: (i, 0, j))
    return pl.pallas_call(
        _rope_table_kernel,
        grid=(b, s // tm),
        in_specs=[pl.BlockSpec((None, tm, 1), lambda i, j: (i, j, 0)),
                  pl.BlockSpec((None, 1, tm), lambda i, j: (i, 0, j)),
                  pl.BlockSpec((1, LANES), lambda i, j: (0, 0)),
                  pl.BlockSpec((LANES, 1), lambda i, j: (0, 0))],
        out_specs=[spec, spec, spec_t, spec_t],
        out_shape=[jax.ShapeDtypeStruct((b, s, LANES), F32)] * 2
        + [jax.ShapeDtypeStruct((b, LANES, s), F32)] * 2,
        compiler_params=_params("parallel", "parallel"),
        name="rope_tables",
    )(positions.reshape(b, s, 1), positions.reshape(b, 1, s), inv_row, inv_row.reshape(LANES, 1))


def _mla_proj_kernel(cq_ref, ckv_ref, kr_ref, cos_ref, sin_ref, cost_ref, sint_ref, qg_ref, kvg_ref,
                     wqt_ref, wk_ref, wvt_ref, qt_ref, k_ref, vt_ref):
    cos = cos_ref[...]
    sin = sin_ref[...]
    lane = lax.broadcasted_iota(jnp.int32, cos.shape, 1)
    x1 = (lane >= MLA_NOPE) & (lane < MLA_NOPE + ROPE_HALF)
    x2 = (lane >= MLA_NOPE + ROPE_HALF) & (lane < MLA_NOPE + MLA_ROPE)
    kr = kr_ref[...]
    krr = (kr * cos + pltpu.roll(kr, LANES - ROPE_HALF, 1) * jnp.where(x1, -sin, 0.0)
           + pltpu.roll(kr, ROPE_HALF, 1) * jnp.where(x2, sin, 0.0))

    cqn = (_rms(cq_ref[...], MLA_Q_RANK) * qg_ref[...]).astype(BF16)
    ckvn = (_rms(ckv_ref[...], MLA_KV_RANK) * kvg_ref[...]).astype(BF16)
    kn = _dot(ckvn, wk_ref[...])
    for h in range(N_HEADS):
        sl = slice(h * LANES, (h + 1) * LANES)
        k_ref[:, sl] = (kn[:, sl] + krr).astype(BF16)
    vrow = lax.broadcasted_iota(jnp.int32, vt_ref.shape, 0)
    ones_row = jnp.where(vrow % V_ROWS == HEAD_DIM, 1.0, 0.0)
    vt_ref[...] = (_dot_nt(wvt_ref[...], ckvn) + ones_row).astype(BF16)

    cos_t = cost_ref[...]
    sin_t = sint_ref[...]
    row = lax.broadcasted_iota(jnp.int32, cos_t.shape, 0)
    x1t = (row >= MLA_NOPE) & (row < MLA_NOPE + ROPE_HALF)
    x2t = (row >= MLA_NOPE + ROPE_HALF) & (row < MLA_NOPE + MLA_ROPE)
    sin_at = jnp.where(x1t, -sin_t, 0.0)
    sin_bt = jnp.where(x2t, sin_t, 0.0)
    qt = _dot_nt(wqt_ref[...], cqn)
    scale = (MLA_NOPE + MLA_ROPE) ** -0.5 * math.log2(math.e)
    for h in range(N_HEADS):
        sl = slice(h * LANES, (h + 1) * LANES)
        xs = qt[sl, :]
        roped = (xs * cos_t + pltpu.roll(xs, LANES - ROPE_HALF, 0) * sin_at
                 + pltpu.roll(xs, ROPE_HALF, 0) * sin_bt)
        qt_ref[sl, :] = (roped * scale).astype(BF16)


def mla_proj(cq, ckv, kr, tables, q_norm, kv_norm, w_uq, w_uk, w_uv):
    b, s, _ = cq.shape
    tm = min(512, s)
    dqk = MLA_NOPE + MLA_ROPE
    cos, sin, cos_t, sin_t = tables
    wq = w_uq.reshape(MLA_Q_RANK, N_HEADS, dqk)
    wq = jnp.pad(wq, ((0, 256 - MLA_Q_RANK), (0, 0), (0, LANES - dqk))).reshape(256, N_HEADS * LANES)
    wk = w_uk.reshape(MLA_KV_RANK, N_HEADS, MLA_NOPE)
    wk = jnp.pad(wk, ((0, 0), (0, 0), (0, LANES - MLA_NOPE))).reshape(MLA_KV_RANK, N_HEADS * LANES)
    qg = jnp.pad(q_norm, (0, 256 - MLA_Q_RANK)).reshape(1, 256)
    kvg = kv_norm.reshape(1, MLA_KV_RANK)
    row = lambda w: pl.BlockSpec((None, tm, w), lambda i, j: (i, j, 0))
    col = lambda w: pl.BlockSpec((None, w, tm), lambda i, j: (i, 0, j))
    full = lambda a: pl.BlockSpec(a.shape, lambda i, j: (0,) * a.ndim)
    wv = jnp.pad(w_uv.reshape(MLA_KV_RANK, N_HEADS, HEAD_DIM), ((0, 0), (0, 0), (0, V_ROWS - HEAD_DIM)))
    wqt, wkb = wq.T.astype(BF16), wk.astype(BF16)
    wvt = wv.reshape(MLA_KV_RANK, N_HEADS * V_ROWS).T.astype(BF16)
    return pl.pallas_call(
        _mla_proj_kernel,
        grid=(b, s // tm),
        in_specs=[row(256), row(128), row(128), row(128), row(128), col(128), col(128),
                  full(qg), full(kvg), full(wqt), full(wkb), full(wvt)],
        out_specs=[col(512), row(512), col(N_HEADS * V_ROWS)],
        out_shape=[jax.ShapeDtypeStruct((b, 512, s), BF16), jax.ShapeDtypeStruct((b, s, 512), BF16),
                   jax.ShapeDtypeStruct((b, N_HEADS * V_ROWS, s), BF16)],
        compiler_params=_params("parallel", "parallel"),
        name="mla_proj",
    )(cq, ckv, kr, cos, sin, cos_t, sin_t, qg, kvg, wqt, wkb, wvt)


def _flash_kernel(qt_ref, k_ref, vt_ref, o_ref, *, tk):
    s = k_ref.shape[0]
    tq = qt_ref.shape[1]
    nk = s // tk
    qts = [qt_ref[sub * LANES:(sub + 1) * LANES, :] for sub in range(2)]

    def scores(t):
        off = pl.multiple_of(t * tk, tk)
        return [_dot(k_ref[pl.ds(off, tk), sub * LANES:(sub + 1) * LANES], qts[sub])
                for sub in range(2)]

    def update(t, sc, stats):
        off = pl.multiple_of(t * tk, tk)
        new = []
        for sub in range(2):
            m, acc = stats[sub]
            vt = vt_ref[sub * V_ROWS:(sub + 1) * V_ROWS, pl.ds(off, tk)]
            m_new = jnp.maximum(m, jnp.max(sc[sub], axis=0, keepdims=True))
            alpha = jnp.exp2(m - m_new)
            p = jnp.exp2((sc[sub] - m_new).astype(BF16))
            acc = alpha * acc + _dot(vt, p)
            new.append((m_new, acc))
        return new

    def body(u, carry):
        sc_a, stats = carry
        ta = 2 * u
        sc_b = scores(ta + 1)
        stats = update(ta, sc_a, stats)
        sc_a = scores(jnp.minimum(ta + 2, nk - 1))
        stats = update(ta + 1, sc_b, stats)
        return sc_a, stats

    init = [(jnp.full((1, tq), -jnp.inf, F32), jnp.zeros((V_ROWS, tq), F32)) for _ in range(2)]
    _, fin = lax.fori_loop(0, nk // 2, body, (scores(0), init))
    out_t = jnp.concatenate([acc[:HEAD_DIM] / acc[HEAD_DIM:HEAD_DIM + 1] for (_, acc) in fin],
                            axis=0)
    o_ref[...] = out_t.T


def flash_attention(qt, k, vt):
    b, _, s = qt.shape
    tq = min(256, s)
    tk = min(512, s)
    return pl.pallas_call(
        functools.partial(_flash_kernel, tk=tk),
        grid=(b, 2, s // tq),
        in_specs=[pl.BlockSpec((None, 256, tq), lambda i, p, j: (i, p, j)),
                  pl.BlockSpec((None, s, 256), lambda i, p, j: (i, 0, p)),
                  pl.BlockSpec((None, 2 * V_ROWS, s), lambda i, p, j: (i, p, 0))],
        out_specs=pl.BlockSpec((None, tq, LANES), lambda i, p, j: (i, j, p)),
        out_shape=jax.ShapeDtypeStruct((b, s, 256), F32),
        compiler_params=_params("parallel", "parallel", "parallel"),
        name="flash_attention",
    )(qt, k, vt)


def _shifted(x, prev_row, next_row, first, last):
    tm = x.shape[0]
    row = lax.broadcasted_iota(jnp.int32, x.shape, 0)
    prev_row = jnp.where(first, 0.0, prev_row)
    next_row = jnp.where(last, 0.0, next_row)
    xm1 = jnp.where(row == 0, prev_row, pltpu.roll(x, 1, 0))
    xp1 = jnp.where(row == tm - 1, next_row, pltpu.roll(x, tm - 1, 0))
    return xm1, xp1


def _conv_kernel(hy_ref, hyp_ref, hyn_ref, sc_ref, scp_ref, scn_ref, hw_ref, hb_ref, sw_ref,
                 x1_ref, x2_ref, z_ref, yd_ref):
    j = pl.program_id(1)
    first = j == 0
    last = j == pl.num_programs(1) - 1
    x = hy_ref[...]
    xm1, xp1 = _shifted(x, hyp_ref[7:8, :], hyn_ref[0:1, :], first, last)
    hw = hw_ref[...]
    proj = xm1 * hw[0:1] + x * hw[1:2] + xp1 * hw[2:3] + hb_ref[...]
    x1_ref[...] = proj[:, 0:GROUP_W]
    x2_ref[...] = proj[:, GROUP_W:2 * GROUP_W]
    z_ref[...] = proj[:, 2 * GROUP_W:]
    su = sc_ref[...]
    g = GROUP_W
    prod = su[:, g:2 * g] * su[:, 2 * g:]
    pprev = scp_ref[7:8, g:2 * g] * scp_ref[7:8, 2 * g:]
    pnext = scn_ref[0:1, g:2 * g] * scn_ref[0:1, 2 * g:]
    pm1, pp1 = _shifted(prod, pprev, pnext, first, last)
    sw = sw_ref[...]
    yd_ref[...] = su[:, :g] * (pm1 * sw[0:1] + prod * sw[1:2] + pp1 * sw[2:3])


def conv_mixers(hy_u, sc_u, hy_conv_w, hy_conv_b, sc_conv_w):
    b, s, w3 = hy_u.shape
    tm = min(512, s)
    nb8 = s // 8
    r8 = tm // 8
    main = pl.BlockSpec((None, tm, w3), lambda i, j: (i, j, 0))
    prev = pl.BlockSpec((None, 8, w3), lambda i, j: (i, jnp.maximum(j * r8 - 1, 0), 0))
    nxt = pl.BlockSpec((None, 8, w3), lambda i, j: (i, jnp.minimum((j + 1) * r8, nb8 - 1), 0))
    full = lambda a: pl.BlockSpec(a.shape, lambda i, j: (0,) * a.ndim)
    hw = hy_conv_w.T
    hb = hy_conv_b.reshape(1, w3)
    sw = sc_conv_w.T
    out = pl.BlockSpec((None, tm, GROUP_W), lambda i, j: (i, j, 0))
    return pl.pallas_call(
        _conv_kernel,
        grid=(b, s // tm),
        in_specs=[main, prev, nxt, main, prev, nxt, full(hw), full(hb), full(sw)],
        out_specs=[out] * 4,
        out_shape=[jax.ShapeDtypeStruct((b, s, GROUP_W), F32)] * 4,
        compiler_params=_params("parallel", "parallel"),
        name="conv_mixers",
    )(hy_u, hy_u, hy_u, sc_u, sc_u, sc_u, hw, hb, sw)


def _fft_dims(s):
    n = 2 * s
    lg = int(round(math.log2(n)))
    assert 1 << lg == n
    n1 = 1 << ((lg + 1) // 2)
    return n1, n // n1


def _filter_kernel(frow_ref, w1h_ref, w1l_ref, b1_ref, fr1_ref, w2h_ref, w2l_ref, b2_ref, fr2_ref,
                   w3h_ref, w3l_ref, b3_ref, ld_ref, k_ref, norm_ref, *, length):
    i = pl.program_id(0)
    tm = k_ref.shape[0]
    n = i * tm + lax.broadcasted_iota(jnp.int32, (tm, 1), 0)
    tt = jnp.where(n < length, n, 2 * length - 1 - n)
    t = tt.astype(F32) / length
    lane = lax.broadcasted_iota(jnp.int32, (tm, LANES), 1)
    ang = t * frow_ref[...]
    z = jnp.where(lane == 0, t,
                  jnp.where(lane <= HY_BANDS, jnp.sin(ang),
                            jnp.where(lane <= 2 * HY_BANDS, jnp.cos(ang), 0.0)))
    hid = jnp.sin(fr1_ref[...] * (_dot3(z, w1h_ref[...], w1l_ref[...]) + b1_ref[...]))
    hid = jnp.sin(fr2_ref[...] * (_dot3(hid, w2h_ref[...], w2l_ref[...]) + b2_ref[...]))
    filt = ((_dot3(hid, w3h_ref[...], w3l_ref[...]) + b3_ref[...])
            * jnp.exp(-t * jnp.exp(ld_ref[...])))
    k_ref[...] = filt

    @pl.when(i == 0)
    def _():
        norm_ref[...] = jnp.zeros_like(norm_ref)

    norm_ref[...] += jnp.sum(jnp.abs(filt), axis=0, keepdims=True)


def hyena_filter_taps(length, w1, b1, fr1, w2, b2, fr2, w3, b3, log_decay):
    tm = min(512, length)
    n_half = length // tm
    oc = 2 * GROUP_W
    bands = jnp.arange(1, HY_BANDS + 1, dtype=F32) * (2.0 * math.pi)
    frow = jnp.zeros((1, LANES), F32).at[0, 1:1 + HY_BANDS].set(bands)
    frow = frow.at[0, 1 + HY_BANDS:1 + 2 * HY_BANDS].set(bands)
    w1p = jnp.zeros((LANES, HY_FFN), F32).at[:1 + 2 * HY_BANDS].set(w1)
    bydir = lambda a: jnp.moveaxis(a.reshape(a.shape[0], 2, 2, GROUP_W), 2, 0).reshape(2, a.shape[0], oc)
    w3d, b3d, ldd = bydir(w3), bydir(b3.reshape(1, -1)), bydir(log_decay.reshape(1, -1))
    full = lambda a: pl.BlockSpec(a.shape, lambda i: (0,) * a.ndim)
    dirspec = lambda a: pl.BlockSpec((None,) + a.shape[1:], lambda i: (i // n_half, 0, 0))
    r1 = lambda a: a.reshape(1, -1)
    w1h, w1l = _split2(w1p)
    w2h, w2l = _split2(w2)
    w3h, w3l = _split2(w3d)
    return pl.pallas_call(
        functools.partial(_filter_kernel, length=length),
        grid=(2 * n_half,),
        in_specs=[full(frow), full(w1h), full(w1l), full(r1(b1)), full(r1(fr1)), full(w2h), full(w2l),
                  full(r1(b2)), full(r1(fr2)), dirspec(w3h), dirspec(w3l), dirspec(b3d), dirspec(ldd)],
        out_specs=[pl.BlockSpec((tm, oc), lambda i: (i, 0)), pl.BlockSpec((1, oc), lambda i: (0, 0))],
        out_shape=[jax.ShapeDtypeStruct((2 * length, oc), F32), jax.ShapeDtypeStruct((1, oc), F32)],
        compiler_params=_params("arbitrary"),
        name="hyena_filter",
    )(frow, w1h, w1l, r1(b1), r1(fr1), w2h, w2l, r1(b2), r1(fr2), w3h, w3l, b3d, ldd)


def _dft_consts(s):
    n1, n2 = _fft_dims(s)
    n = n1 * n2
    kh = n1 // 2 + 1
    kp = -(-kh // 4) * 4
    pad_rows = lambda a: np.concatenate([a, np.zeros((kp - kh, a.shape[1]))], axis=0)
    a1 = 2.0 * np.pi * np.outer(np.arange(kh), np.arange(n1)) / n1
    f1 = np.concatenate([pad_rows(np.cos(a1)), pad_rows(-np.sin(a1))], axis=0)
    a2 = 2.0 * np.pi * np.outer(np.arange(n2), np.arange(n2)) / n2
    c2, s2 = np.cos(a2), np.sin(a2)
    f2_fwd = np.block([[c2, s2], [-s2, c2]])
    f2_inv = np.block([[c2, -s2], [s2, c2]])
    at = 2.0 * np.pi * np.outer(np.arange(kp), np.arange(n2)) / n
    tw = np.stack([np.cos(at), np.sin(at)], axis=0)[..., None]
    m1 = np.arange(n1 // 2)
    a3 = 2.0 * np.pi * np.outer(m1, np.arange(kh)) / n1
    wgt = np.where((np.arange(kh) == 0) | (np.arange(kh) == n1 // 2), 1.0, 2.0) / n
    f3 = np.concatenate([pad_rows((np.cos(a3) * wgt).T).T, pad_rows((-np.sin(a3) * wgt).T).T],
                        axis=1)
    f = lambda a: jnp.asarray(a, F32)
    x3 = lambda a: _lhs_x3(f(a))
    return dict(n1=n1, n2=n2, kh=kh, kp=kp, f1=x3(f1), f1_half=x3(f1[:, :n1 // 2]), f2_fwd=x3(f2_fwd),
                f2_inv=x3(f2_inv), tw=f(tw), f3=x3(f3))


_N2_TILE = 8


def _dft1_kernel(x_ref, f_ref, s_ref, o_ref, xs_ref):
    f = f_ref[...]
    scale = s_ref[...]
    for m in range(_N2_TILE):
        xs_ref[...] = x_ref[:, m, :]
        o_ref[:, m, :] = _dot_x3(f, xs_ref[...] * scale)


def dft_stage1(x, f1, scale_row):
    g, r, n2, ch = x.shape
    m = f1.shape[0]
    return pl.pallas_call(
        _dft1_kernel,
        grid=(g, n2 // _N2_TILE),
        in_specs=[pl.BlockSpec((None, r, _N2_TILE, ch), lambda i, j: (i, 0, j, 0)),
                  pl.BlockSpec(f1.shape, lambda i, j: (0, 0)),
                  pl.BlockSpec((1, ch), lambda i, j: (0, 0))],
        out_specs=pl.BlockSpec((None, m, _N2_TILE, ch), lambda i, j: (i, 0, j, 0)),
        out_shape=jax.ShapeDtypeStruct((g, m, n2, ch), F32),
        scratch_shapes=[pltpu.VMEM((r, ch), F32)],
        compiler_params=_params("parallel", "parallel"),
        name="dft_stage1",
    )(x, f1, scale_row)


def _twiddle(re, im, tc, ts, conj):
    if conj:
        return re * tc - im * ts, im * tc + re * ts
    return re * tc + im * ts, im * tc - re * ts


_K1_STEP = 4


def _dft2_filter_kernel(a_ref, tw_ref, ff_ref, o_ref):
    n2 = a_ref.shape[2]
    for i in range(_K1_STEP):
        br, bi = _twiddle(a_ref[0, i], a_ref[1, i], tw_ref[0, i], tw_ref[1, i], False)
        zz = _dot_x3(ff_ref[...], jnp.concatenate([br, bi], axis=0))
        o_ref[0, i] = zz[:n2]
        o_ref[1, i] = zz[n2:]


def dft_stage2_filter(a, consts):
    _, kp, n2, ch = a.shape
    return pl.pallas_call(
        _dft2_filter_kernel,
        grid=(kp // _K1_STEP,),
        input_output_aliases={0: 0},
        in_specs=[pl.BlockSpec((2, _K1_STEP, n2, ch), lambda k: (0, k, 0, 0)),
                  pl.BlockSpec((2, _K1_STEP, n2, 1), lambda k: (0, k, 0, 0)),
                  pl.BlockSpec(consts["f2_fwd"].shape, lambda k: (0, 0))],
        out_specs=pl.BlockSpec((2, _K1_STEP, n2, ch), lambda k: (0, k, 0, 0)),
        out_shape=jax.ShapeDtypeStruct(a.shape, F32),
        compiler_params=_params("parallel"),
        name="dft_stage2_filter",
    )(a, consts["tw"], consts["f2_fwd"])


def _dft2_conv_kernel(a_ref, tw_ref, kf_ref, ff_ref, fi_ref, o_ref):
    n2 = a_ref.shape[2]
    for i in range(_K1_STEP):
        tc, ts = tw_ref[0, i], tw_ref[1, i]
        br, bi = _twiddle(a_ref[0, i], a_ref[1, i], tc, ts, False)
        zz = _dot_x3(ff_ref[...], jnp.concatenate([br, bi], axis=0))
        zr, zi = zz[:n2], zz[n2:]
        kr, ki = kf_ref[0, i], kf_ref[1, i]
        pr = zr * kr - zi * ki
        pi = zr * ki + zi * kr
        vv = _dot_x3(fi_ref[...], jnp.concatenate([pr, pi], axis=0))
        vr, vi = _twiddle(vv[:n2], vv[n2:], tc, ts, True)
        o_ref[0, i] = vr
        o_ref[1, i] = vi


def dft_stage2_conv(a, kf, order, consts):
    b, _, kp, n2, ch = a.shape
    blk = pl.BlockSpec((None, 2, _K1_STEP, n2, ch), lambda i, k: (i, 0, k, 0, 0))
    mat = pl.BlockSpec(consts["f2_fwd"].shape, lambda i, k: (0, 0))
    return pl.pallas_call(
        _dft2_conv_kernel,
        grid=(b, kp // _K1_STEP),
        input_output_aliases={0: 0},
        in_specs=[blk,
                  pl.BlockSpec((2, _K1_STEP, n2, 1), lambda i, k: (0, k, 0, 0)),
                  pl.BlockSpec((2, _K1_STEP, n2, ch), lambda i, k: (0, k, 0, order)),
                  mat, mat],
        out_specs=blk,
        out_shape=jax.ShapeDtypeStruct(a.shape, F32),
        compiler_params=_params("parallel", "parallel"),
        name="dft_stage2_conv",
    )(a, consts["tw"], kf, consts["f2_fwd"], consts["f2_inv"])


def _dft3_kernel(v_ref, f_ref, gate_ref, z_ref, bias_ref, o_ref, vs_ref, ys_ref):
    f = f_ref[...]
    for m in range(_N2_TILE):
        vs_ref[...] = v_ref[:, m, :]
        ys_ref[:, m, :] = _dot_x3(f, vs_ref[...])
    o_ref[...] = gate_ref[...] * (ys_ref[...] + z_ref[...] * bias_ref[...])


def dft_stage3_gate(v, f3, gate, z, bias_row):
    b, m, n2, ch = v.shape
    r = f3.shape[0]
    row = pl.BlockSpec((None, r, _N2_TILE, ch), lambda i, j: (i, 0, j, 0))
    return pl.pallas_call(
        _dft3_kernel,
        grid=(b, n2 // _N2_TILE),
        in_specs=[pl.BlockSpec((None, m, _N2_TILE, ch), lambda i, j: (i, 0, j, 0)),
                  pl.BlockSpec(f3.shape, lambda i, j: (0, 0)),
                  row, row, pl.BlockSpec((1, ch), lambda i, j: (0, 0))],
        out_specs=row,
        out_shape=jax.ShapeDtypeStruct((b, r, n2, ch), F32),
        scratch_shapes=[pltpu.VMEM((m, ch), F32), pltpu.VMEM((r, _N2_TILE, ch), F32)],
        compiler_params=_params("parallel", "parallel"),
        name="dft_stage3_gate",
    )(v, f3, gate, z, bias_row)


def hyena_mixer(x1, x2, z, hy_w1, hy_b1, hy_fr1, hy_w2, hy_b2, hy_fr2, hy_w3, hy_b3, hy_log_decay,
                hy_bias):
    b, s, ch = z.shape
    consts = _dft_consts(s)
    n1, n2 = consts["n1"], consts["n2"]
    oc = 2 * ch
    taps, norm = hyena_filter_taps(s, hy_w1, hy_b1, hy_fr1, hy_w2, hy_b2, hy_fr2, hy_w3, hy_b3,
                                   hy_log_decay)
    kp = consts["kp"]
    ka = dft_stage1(taps.reshape(1, n1, n2, oc), consts["f1"], 1.0 / norm)
    kf = dft_stage2_filter(ka.reshape(2, kp, n2, oc), consts)
    half = n1 // 2
    view = lambda t: t.reshape(b, half, n2, ch)
    ones = jnp.ones((1, ch), F32)
    f1_half = consts["f1_half"]
    cur = view(z)
    for order, gate in enumerate((x1, x2)):
        a = dft_stage1(cur, f1_half, ones)
        v = dft_stage2_conv(a.reshape(b, 2, kp, n2, ch), kf, order, consts)
        cur = dft_stage3_gate(v.reshape(b, 2 * kp, n2, ch), consts["f3"], view(gate), cur,
                              hy_bias[order].reshape(1, ch))
    return cur.reshape(b, s, ch)


def _outproj_kernel(x_ref, hf_ref, hb_ref, o_ref, yb_ref, yc_ref, yd_ref, og_ref, hm_ref, wout_ref,
                    pg_ref, g1_ref, fg_ref, sc2_ref, sh2_ref, rwt_ref,
                    xn_ref, h2_ref, afft_ref):
    hm = hm_ref[...]
    y_a = _sigmoid(o_ref[...]) * (hf_ref[...] + hb_ref[...])
    acc = None
    for idx, y in enumerate((y_a, yb_ref[...], yc_ref[...], yd_ref[...])):
        ms = _dot(jnp.concatenate(_split2(y * y), axis=1), hm)
        yn = y * lax.rsqrt(ms + NORM_EPS) * og_ref[:, idx * GROUP_W:(idx + 1) * GROUP_W]
        part = _dot(yn.astype(BF16), wout_ref[idx * GROUP_W:(idx + 1) * GROUP_W, :])
        acc = part if acc is None else acc + part
    d = acc.shape[-1]
    xn = x_ref[...] + g1_ref[...] * (_rms(acc, d) * pg_ref[...])
    xn_ref[...] = xn
    h2 = _rms(xn, d) * fg_ref[...] * (1.0 + sc2_ref[...]) + sh2_ref[...]
    h2_ref[...] = h2.T.astype(BF16)
    logits_t = _dot_nt_hi(rwt_ref[...], h2)
    mx = jnp.max(logits_t, axis=0, keepdims=True)
    ex = jnp.exp(logits_t - mx)
    afft_ref[...] = ex / jnp.sum(ex, axis=0, keepdims=True)


def out_proj(x, hf, hb, o, yb, yc, yd, out_g, w_out, post_g, g1, ffn_g, sc2, sh2, router_w):
    b, s, d = x.shape
    tm = min(512, s)
    e = router_w.shape[1]
    hm1 = np.kron(np.eye(GROUP_W // HEAD_DIM), np.ones((HEAD_DIM, HEAD_DIM))) / HEAD_DIM
    hm = jnp.asarray(np.concatenate([hm1, hm1], axis=0), BF16)
    row = lambda w: pl.BlockSpec((None, tm, w), lambda i, j: (i, j, 0))
    full = lambda a: pl.BlockSpec(a.shape, lambda i, j: (0,) * a.ndim)
    vec = lambda: pl.BlockSpec((None, 1, d), lambda i, j: (i, 0, 0))
    r1 = lambda a: a.reshape(1, -1)
    wob = w_out.astype(BF16)
    rwt = router_w.T
    return pl.pallas_call(
        _outproj_kernel,
        grid=(b, s // tm),
        in_specs=[row(d)] + [row(GROUP_W)] * 6 + [full(r1(out_g)), full(hm), full(wob),
                                                  full(r1(post_g)), vec(), full(r1(ffn_g)), vec(), vec(),
                                                  full(rwt)],
        out_specs=[row(d), pl.BlockSpec((None, d, tm), lambda i, j: (i, 0, j)),
                   pl.BlockSpec((None, e, tm), lambda i, j: (i, 0, j))],
        out_shape=[jax.ShapeDtypeStruct((b, s, d), F32), jax.ShapeDtypeStruct((b, d, s), BF16),
                   jax.ShapeDtypeStruct((b, e, s), F32)],
        compiler_params=_params("parallel", "parallel"),
        name="out_proj",
    )(x, hf, hb, o, yb, yc, yd, r1(out_g), hm, wob, r1(post_g), g1.reshape(b, 1, d), r1(ffn_g),
      sc2.reshape(b, 1, d), sh2.reshape(b, 1, d), rwt)


def _lane_cumsum(x):
    n = x.shape[-1]
    lane = lax.broadcasted_iota(jnp.int32, x.shape, x.ndim - 1)
    shift = 1
    while shift < n:
        x = x + jnp.where(lane >= shift, pltpu.roll(x, shift, x.ndim - 1), 0.0)
        shift *= 2
    return x


def _select_kernel(aff_ref, u_ref, ps_ref, cb_ref, *, cap):
    aff = aff_ref[...]
    capf = float(cap)

    def body(i, bits):
        cand = bits | (jnp.int32(1) << (30 - i))
        cnt = jnp.sum(jnp.where(aff >= pltpu.bitcast(cand, F32), 1.0, 0.0), axis=-1, keepdims=True)
        return jnp.where(cnt >= capf, cand, bits)

    bits = lax.fori_loop(0, 31, body, jnp.zeros((aff.shape[0], 1), jnp.int32))
    thr = pltpu.bitcast(bits, F32)
    gt = aff > thr
    eq = aff == thr
    n_gt = jnp.sum(jnp.where(gt, 1.0, 0.0), axis=-1, keepdims=True)
    eqf = jnp.where(eq, 1.0, 0.0)
    rank_eq = _lane_cumsum(eqf) - eqf
    sel = gt | (eq & (rank_eq < capf - n_gt))
    self_ = jnp.where(sel, 1.0, 0.0)
    pos = _lane_cumsum(self_) - self_
    ps_ref[...] = jnp.where(sel, pos, -1.0).astype(jnp.int32)
    cb_ref[...] = _dot(self_.astype(BF16), u_ref[...]).astype(jnp.int32)


def ec_select(aff_t, cap):
    b, e, s = aff_t.shape
    nt = s // TOKEN_TILE
    assert nt + 1 <= LANES
    tok = np.arange(s)[:, None]
    u = jnp.asarray(tok < (np.arange(LANES)[None, :] * TOKEN_TILE), BF16)
    return pl.pallas_call(
        functools.partial(_select_kernel, cap=cap),
        grid=(b,),
        in_specs=[pl.BlockSpec((None, e, s), lambda i: (i, 0, 0)),
                  pl.BlockSpec((s, LANES), lambda i: (0, 0))],
        out_specs=[pl.BlockSpec((None, e, s), lambda i: (i, 0, 0)),
                   pl.BlockSpec((None, e, LANES), lambda i: (i, 0, 0))],
        out_shape=[jax.ShapeDtypeStruct((b, e, s), jnp.int32),
                   jax.ShapeDtypeStruct((b, e, LANES), jnp.int32)],
        compiler_params=_params("parallel"),
        name="ec_select",
    )(aff_t, u)


def _gather_kernel(cb_ref, ps_ref, ht_ref, o_ref, *, n_tiles, n_blocks, block, chunk):
    bi, ei = pl.program_id(0), pl.program_id(1)
    base = (bi * N_EXPERTS + ei) * LANES
    d, s = ht_ref.shape
    slot = lax.broadcasted_iota(jnp.int32, (block, chunk), 0)
    tok = lax.broadcasted_iota(jnp.int32, (1, chunk), 1)

    def one_block(j, carry):
        first = j * block

        def bounds(t, c):
            return (c[0] + (cb_ref[base + t + 1] <= first).astype(jnp.int32),
                    c[1] + (cb_ref[base + t] < first + block).astype(jnp.int32))

        t_lo, t_hi = lax.fori_loop(0, n_tiles, bounds, (jnp.int32(0), jnp.int32(0)))
        lo_tok = t_lo * TOKEN_TILE

        def part(i, acc):
            lower = lo_tok + i * chunk
            start = pl.multiple_of(jnp.minimum(lower, s - chunk), TOKEN_TILE)
            ps = jnp.where(tok + start >= lower, ps_ref[:, pl.ds(start, chunk)], -1)
            onehot = jnp.where(ps == slot + first, 1.0, 0.0).astype(BF16)
            return acc + _dot_nt(ht_ref[:, pl.ds(start, chunk)], onehot)

        n_parts = ((t_hi - t_lo) * TOKEN_TILE + chunk - 1) // chunk
        acc = lax.fori_loop(0, n_parts, part, jnp.zeros((d, block), F32))
        o_ref[pl.ds(pl.multiple_of(first, block), block), :] = acc.T.astype(o_ref.dtype)
        return carry

    lax.fori_loop(0, n_blocks, one_block, 0)


def ec_gather(ht, ps, cb, cap):
    b, d, s = ht.shape
    e = ps.shape[1]
    nt = s // TOKEN_TILE
    block = min(2 * SLOT_BLOCK, cap)
    chunk = min(block * s // cap + 2 * TOKEN_TILE, s)
    grid_spec = pltpu.PrefetchScalarGridSpec(
        num_scalar_prefetch=1,
        grid=(b, e),
        in_specs=[pl.BlockSpec((None, None, 1, s), lambda i, j, cb: (i, j, 0, 0)),
                  pl.BlockSpec((None, d, s), lambda i, j, cb: (i, 0, 0))],
        out_specs=pl.BlockSpec((None, None, cap, d), lambda i, j, cb: (i, j, 0, 0)),
    )
    return pl.pallas_call(
        functools.partial(_gather_kernel, n_tiles=nt, n_blocks=cap // block, block=block, chunk=chunk),
        grid_spec=grid_spec,
        out_shape=jax.ShapeDtypeStruct((b, e, cap, d), BF16),
        compiler_params=_params("parallel", "arbitrary"),
        name="ec_gather",
    )(cb.reshape(-1), ps.reshape(b, e, 1, s), ht)


def _ffn_kernel(x_ref, wg_ref, wu_ref, wd_ref, o_ref, acc_ref, *, n_f):
    f = pl.program_id(1)

    @pl.when(f == 0)
    def _():
        acc_ref[...] = jnp.zeros_like(acc_ref)

    bsz, cap, d = x_ref.shape
    x = x_ref[...].reshape(bsz * cap, d)
    a = _dot(x, wg_ref[...].astype(BF16))
    up = _dot(x, wu_ref[...].astype(BF16))
    act = (a * _sigmoid(a) * up).astype(BF16)
    acc_ref[...] += _dot(act, wd_ref[...].astype(BF16))

    @pl.when(f == n_f - 1)
    def _():
        o_ref[...] = acc_ref[...].reshape(bsz, cap, d).astype(o_ref.dtype)


def expert_ffn(xe, w_gate, w_up, w_down, layer):
    b, e, cap, d = xe.shape
    ff = w_gate.shape[-1]
    tf = min(512, ff)
    n_f = ff // tf
    return pl.pallas_call(
        functools.partial(_ffn_kernel, n_f=n_f),
        grid=(e, n_f),
        in_specs=[pl.BlockSpec((b, None, cap, d), lambda j, f: (0, j, 0, 0)),
                  pl.BlockSpec((None, None, d, tf), lambda j, f: (layer, j, 0, f)),
                  pl.BlockSpec((None, None, d, tf), lambda j, f: (layer, j, 0, f)),
                  pl.BlockSpec((None, None, tf, d), lambda j, f: (layer, j, f, 0))],
        out_specs=pl.BlockSpec((b, None, cap, d), lambda j, f: (0, j, 0, 0)),
        out_shape=jax.ShapeDtypeStruct((b, e, cap, d), BF16),
        scratch_shapes=[pltpu.VMEM((b * cap, d), F32)],
        compiler_params=_params("parallel", "arbitrary"),
        name="expert_ffn",
    )(xe, w_gate, w_up, w_down)


def _scatter_kernel(cb_ref, ps_ref, aff_ref, ye_ref, x_ref, pg_ref, g2_ref, o_ref, acc_ref,
                    *, n_sub, window):
    bi, ti, ei = pl.program_id(0), pl.program_id(1), pl.program_id(2)

    @pl.when(ei == 0)
    def _():
        acc_ref[...] = jnp.zeros_like(acc_ref)

    tt = acc_ref.shape[0]
    base = (bi * N_EXPERTS + ei) * LANES + ti * n_sub
    lo = cb_ref[base]
    hi = cb_ref[base + n_sub]

    @pl.when(hi > lo)
    def _():
        elane = lax.broadcasted_iota(jnp.int32, (tt, N_EXPERTS), 1)
        ps = jnp.sum(jnp.where(elane == ei, ps_ref[...], 0), axis=-1, keepdims=True)
        gate = jnp.sum(jnp.where(elane == ei, aff_ref[...], 0.0), axis=-1, keepdims=True)
        slot = lax.broadcasted_iota(jnp.int32, (tt, window), 1)
        cap = ye_ref.shape[0]
        first = (lo // BF16_ROWS) * BF16_ROWS

        def body(j, carry):
            lower = first + j * window
            start = pl.multiple_of(jnp.minimum(lower, cap - window), BF16_ROWS)
            onehot = jnp.where(jnp.where(ps >= lower, ps, -1) == slot + start, 1.0, 0.0).astype(BF16)
            acc_ref[...] += gate * _dot(onehot, ye_ref[pl.ds(start, window), :])
            return carry

        lax.fori_loop(0, (hi - first + window - 1) // window, body, 0)

    @pl.when(ei == N_EXPERTS - 1)
    def _():
        y = acc_ref[...]
        o_ref[...] = x_ref[...] + g2_ref[...] * (_rms(y, y.shape[-1]) * pg_ref[...])


def ec_scatter(ye, ps_col, aff, cb, x, post_g, g2):
    b, e, cap, d = ye.shape
    s = x.shape[1]
    n_sub = min(4, s // TOKEN_TILE)
    tt = n_sub * TOKEN_TILE
    grid_spec = pltpu.PrefetchScalarGridSpec(
        num_scalar_prefetch=1,
        grid=(b, s // tt, e),
        in_specs=[pl.BlockSpec((None, tt, e), lambda i, t, j, cb: (i, t, 0)),
                  pl.BlockSpec((None, tt, e), lambda i, t, j, cb: (i, t, 0)),
                  pl.BlockSpec((None, None, cap, d), lambda i, t, j, cb: (i, j, 0, 0)),
                  pl.BlockSpec((None, tt, d), lambda i, t, j, cb: (i, t, 0)),
                  pl.BlockSpec((1, d), lambda i, t, j, cb: (0, 0)),
                  pl.BlockSpec((None, 1, d), lambda i, t, j, cb: (i, 0, 0))],
        out_specs=pl.BlockSpec((None, tt, d), lambda i, t, j, cb: (i, t, 0)),
        scratch_shapes=[pltpu.VMEM((tt, d), F32)],
    )
    return pl.pallas_call(
        functools.partial(_scatter_kernel, n_sub=n_sub, window=min(2 * SLOT_BLOCK, cap)),
        grid_spec=grid_spec,
        out_shape=jax.ShapeDtypeStruct((b, s, d), F32),
        compiler_params=_params("parallel", "parallel", "arbitrary"),
        name="ec_scatter",
    )(cb.reshape(-1), ps_col, aff, ye, x, post_g.reshape(1, d), g2.reshape(b, 1, d))


def kernel(x, c, positions, ada_w, ada_b, mix_pre_g, mix_post_g, ffn_pre_g, ffn_post_g, w_in, ml_gate_b, mla_q_norm, mla_kv_norm, mla_w_uq, mla_w_uk, mla_w_uv, hy_conv_w, hy_conv_b, hy_w1, hy_b1, hy_fr1, hy_w2, hy_b2, hy_fr2, hy_w3, hy_b3, hy_log_decay, hy_bias, sc_conv_w, mix_out_g, w_out, router_w, exp_w_gate, exp_w_up, exp_w_down):
    depth = ada_w.shape[0]
    b, s, d = x.shape
    cap = EC_CAPACITY * s // N_EXPERTS
    mod = ada_mod(c, ada_w, ada_b)
    tables = rope_tables(positions)
    for l in range(depth):
        sh1, sc1, g1, sh2, sc2, g2 = (mod[l, :, i * d:(i + 1) * d] for i in range(6))
        u = in_proj(x, mix_pre_g[l], sc1, sh1, w_in[l])
        hf, hb = mlstm(u["q"], u["v"], u["kT"], u["gT"], u["g"], ml_gate_b[l])
        qa, ka, va = mla_proj(u["cq"], u["ckv"], u["kr"], tables, mla_q_norm[l], mla_kv_norm[l],
                              mla_w_uq[l], mla_w_uk[l], mla_w_uv[l])
        y_b = flash_attention(qa, ka, va)
        x1, x2, z, y_d = conv_mixers(u["hy"], u["sc"], hy_conv_w[l], hy_conv_b[l], sc_conv_w[l])
        y_c = hyena_mixer(x1, x2, z, hy_w1[l], hy_b1[l], hy_fr1[l], hy_w2[l], hy_b2[l], hy_fr2[l],
                          hy_w3[l], hy_b3[l], hy_log_decay[l], hy_bias[l])
        xn, h2, aff_t = out_proj(x, hf, hb, u["o"], y_b, y_c, y_d, mix_out_g[l], w_out[l],
                                      mix_post_g[l], g1, ffn_pre_g[l], sc2, sh2, router_w[l])
        ps, cb = ec_select(aff_t, cap)
        xe = ec_gather(h2, ps, cb, cap)
        ye = expert_ffn(xe, exp_w_gate, exp_w_up, exp_w_down, l)
        x = ec_scatter(ye, jnp.swapaxes(ps, 1, 2), jnp.swapaxes(aff_t, 1, 2), cb, xn, ffn_post_g[l], g2)
    return x
```

```python
import functools
import math

import numpy as np
import jax
import jax.numpy as jnp
from jax import lax
from jax.experimental import pallas as pl
from jax.experimental.pallas import tpu as pltpu

F32 = jnp.float32
BF16 = jnp.bfloat16
HIGHEST = lax.Precision.HIGHEST

GROUP_W = 256
HEAD_DIM = 64
N_HEADS = 4
ML_CHUNK = 128
MLA_Q_RANK = 224
MLA_KV_RANK = 128
MLA_NOPE = 64
MLA_ROPE = 32
ROPE_HALF = MLA_ROPE // 2
ROPE_THETA = 10000.0
HY_BANDS = 8
HY_FFN = 64
N_EXPERTS = 16
EC_CAPACITY = 2
NORM_EPS = 1e-6
LANES = 128
BF16_ROWS = 16
V_ROWS = HEAD_DIM + BF16_ROWS
FLASH_HEADS = 2
SLOT_BLOCK = 128
TOKEN_TILE = 256
VMEM_LIMIT = 56 * 1024 * 1024


def _params(*sem):
    return pltpu.CompilerParams(dimension_semantics=sem, vmem_limit_bytes=VMEM_LIMIT)


def _dot(a, b):
    return jnp.dot(a, b, preferred_element_type=F32)


def _dot_hi(a, b):
    return jnp.dot(a, b, precision=HIGHEST, preferred_element_type=F32)


def _dot_nt(a, b):
    return lax.dot_general(a, b, (((1,), (1,)), ((), ())), preferred_element_type=F32)


def _dot_nt_hi(a, b):
    return lax.dot_general(a, b, (((1,), (1,)), ((), ())), precision=HIGHEST,
                           preferred_element_type=F32)


def _split2(x):
    hi = x.astype(BF16)
    lo = (x - hi.astype(F32)).astype(BF16)
    return hi, lo


def _split3(x):
    hi = x.astype(BF16)
    r = x - hi.astype(F32)
    mid = r.astype(BF16)
    lo = (r - mid.astype(F32)).astype(BF16)
    return hi, mid, lo


def _lhs_x3(f):
    hi, lo = _split2(f)
    return jnp.concatenate([hi, lo, hi], axis=1)


def _dot_x3(f3, x):
    hi, lo = _split2(x)
    return _dot(f3, jnp.concatenate([hi, hi, lo], axis=0))


def _dot3(a, w_hi, w_lo):
    a_hi, a_lo = _split2(a)
    return _dot(a_hi, w_hi) + _dot(a_hi, w_lo) + _dot(a_lo, w_hi)


def _rms(x, n):
    ms = jnp.sum(x * x, axis=-1, keepdims=True) * (1.0 / n)
    return x * lax.rsqrt(ms + NORM_EPS)


def _log_sigmoid(x):
    return jnp.minimum(x, 0.0) - jnp.log(1.0 + jnp.exp(-jnp.abs(x)))


def _sigmoid(x):
    return 1.0 / (1.0 + jnp.exp(-x))


def _ada_kernel(c_ref, w_ref, b_ref, o_ref):
    c = c_ref[...]
    cs = c * _sigmoid(c)
    o_ref[...] = _dot_hi(cs, w_ref[...]) + b_ref[...]


def ada_mod(c, ada_w, ada_b):
    depth, d, n6 = ada_w.shape
    b = c.shape[0]
    bp = 8
    cp = jnp.zeros((bp, d), F32).at[:b].set(c)
    tn = 1536
    out = pl.pallas_call(
        _ada_kernel,
        grid=(depth, n6 // tn),
        in_specs=[pl.BlockSpec((bp, d), lambda l, j: (0, 0)),
                  pl.BlockSpec((None, d, tn), lambda l, j: (l, 0, j)),
                  pl.BlockSpec((None, 1, tn), lambda l, j: (l, 0, j))],
        out_specs=pl.BlockSpec((None, bp, tn), lambda l, j: (l, 0, j)),
        out_shape=jax.ShapeDtypeStruct((depth, bp, n6), F32),
        compiler_params=_params("parallel", "parallel"),
        name="ada_mod",
    )(cp, ada_w, ada_b.reshape(depth, 1, n6))
    return out[:, :b]


_U_COLS = (("q", 256), ("v", 256), ("o", 256), ("g", 128), ("cq", 256), ("ckv", 128),
           ("kr", 128), ("hy", 768), ("sc", 768))
_U_TOTAL = sum(w for _, w in _U_COLS)
_UT_ROWS = 256 + 16


def _inproj_kernel(x_ref, gain_ref, sc_ref, sh_ref, w_ref, wt_ref,
                   q_ref, v_ref, o_ref, g_ref, cq_ref, ckv_ref, kr_ref, hy_ref, scu_ref,
                   kt_ref, gt_ref):
    x = x_ref[...]
    d = x.shape[-1]
    h = _rms(x, d) * gain_ref[...] * (1.0 + sc_ref[...]) + sh_ref[...]
    hb = h.astype(BF16)
    u = _dot(hb, w_ref[...])
    off = 0
    for ref, (_, width) in zip((q_ref, v_ref, o_ref, g_ref, cq_ref, ckv_ref, kr_ref, hy_ref, scu_ref),
                               _U_COLS):
        ref[...] = u[:, off:off + width]
        off += width
    ut = _dot_nt(wt_ref[...], hb)
    kt_ref[...] = ut[:256]
    gt_ref[...] = ut[256:]


def in_proj(x, gain, scale, shift, w_in):
    b, s, d = x.shape
    tm = min(512, s)
    cuts = np.cumsum([0, 256, 256, 256, 256, 16, MLA_Q_RANK, MLA_KV_RANK, MLA_ROPE, 768, 768])
    wq, wk, wv, wo, wg, wcq, wckv, wkr, why, wsc = (w_in[:, cuts[i]:cuts[i + 1]] for i in range(10))
    pad = lambda w, n: jnp.pad(w, ((0, 0), (0, n - w.shape[1])))
    wkr_p = jnp.pad(wkr, ((0, 0), (MLA_NOPE, LANES - MLA_NOPE - MLA_ROPE)))
    w1 = jnp.concatenate([wq, wv, wo, pad(wg, 128), pad(wcq, 256), wckv, wkr_p, why, wsc],
                         axis=1).astype(BF16)
    w2t = jnp.concatenate([wk, wg], axis=1).T.astype(BF16)
    row = lambda nm, w: pl.BlockSpec((None, tm, w), lambda i, j: (i, j, 0))
    out_shapes = [jax.ShapeDtypeStruct((b, s, w), F32) for _, w in _U_COLS]
    out_shapes += [jax.ShapeDtypeStruct((b, 256, s), F32), jax.ShapeDtypeStruct((b, 16, s), F32)]
    out_specs = [row(nm, w) for nm, w in _U_COLS]
    out_specs += [pl.BlockSpec((None, 256, tm), lambda i, j: (i, 0, j)),
                  pl.BlockSpec((None, 16, tm), lambda i, j: (i, 0, j))]
    vec = lambda: pl.BlockSpec((None, 1, d), lambda i, j: (i, 0, 0))
    outs = pl.pallas_call(
        _inproj_kernel,
        grid=(b, s // tm),
        in_specs=[pl.BlockSpec((None, tm, d), lambda i, j: (i, j, 0)),
                  pl.BlockSpec((1, d), lambda i, j: (0, 0)),
                  vec(), vec(),
                  pl.BlockSpec((d, _U_TOTAL), lambda i, j: (0, 0)),
                  pl.BlockSpec((_UT_ROWS, d), lambda i, j: (0, 0))],
        out_specs=out_specs,
        out_shape=out_shapes,
        compiler_params=_params("parallel", "parallel"),
        name="in_proj",
    )(x, gain.reshape(1, d), scale.reshape(b, 1, d), shift.reshape(b, 1, d), w1, w2t)
    names = [nm for nm, _ in _U_COLS] + ["kT", "gT"]
    return dict(zip(names, outs))


def _mlstm_dir(q, v, kt, gt, gc, c_ref, m_ref, base, rev):
    L = q.shape[0]
    r = lax.broadcasted_iota(jnp.int32, (L, L), 0)
    c = lax.broadcasted_iota(jnp.int32, (L, L), 1)
    tri = (c >= r) if rev else (c <= r)
    io, fo = (8, 12) if rev else (0, 4)
    logf_rows = _log_sigmoid(gt[fo:fo + 4, :])
    logf_cols = _log_sigmoid(gc)
    tri_b = jnp.where(tri, 1.0, 0.0).astype(BF16)
    a_cols = _dot(jnp.concatenate([tri_b] * 3, axis=1),
                  jnp.concatenate(_split3(logf_cols), axis=0))
    tri_t = (r >= c) if rev else (r <= c)
    tri_tb = jnp.where(tri_t, 1.0, 0.0).astype(BF16)
    a_rows = _dot(jnp.concatenate(_split3(_log_sigmoid(gt)), axis=1),
                  jnp.concatenate([tri_tb] * 3, axis=0))[fo:fo + 4, :]
    lane = lax.broadcasted_iota(jnp.int32, (L, LANES), 1)
    row128 = lax.broadcasted_iota(jnp.int32, (LANES, L), 0)
    scale = HEAD_DIM ** -0.5
    outs = []
    for pair in range(2):
        qp = q[:, pair * LANES:(pair + 1) * LANES]
        vp = v[:, pair * LANES:(pair + 1) * LANES]
        ktp = kt[pair * LANES:(pair + 1) * LANES, :] * scale
        pair_out = None
        for sub in range(2):
            h = pair * 2 + sub
            in_head = (row128 >= sub * HEAD_DIM) & (row128 < (sub + 1) * HEAD_DIM)
            kth = jnp.where(in_head, ktp, 0.0)
            vsh = vp if sub == 0 else pltpu.roll(vp, HEAD_DIM, 1)
            v_aug = jnp.where(lane < HEAD_DIM, vsh, jnp.where(lane == HEAD_DIM, 1.0, 0.0))
            v_aug_b = v_aug.astype(BF16)
            a_c = a_cols[:, fo + h:fo + h + 1]
            a_r = a_rows[h:h + 1, :]
            ig_r = gt[io + h:io + h + 1, :]
            a_end = jnp.sum(logf_rows[h:h + 1, :], axis=-1, keepdims=True)
            c_st = c_ref[base + h]
            m_st = m_ref[base + h][:, 0:1]
            d_log = jnp.where(tri, a_c - a_r + ig_r, -jnp.inf)
            inter = a_c + m_st
            m_t = jnp.maximum(inter, jnp.max(d_log, axis=-1, keepdims=True))
            qb = qp.astype(BF16)
            p = jnp.exp(d_log - m_t) * _dot(qb, kth.astype(BF16))
            sci = jnp.exp(inter - m_t)
            nd = _dot(p.astype(BF16), v_aug_b) + sci * _dot(qb, c_st.astype(BF16))
            den = nd[:, HEAD_DIM:HEAD_DIM + 1]
            out = nd / jnp.maximum(jnp.abs(den), jnp.exp(-m_t))
            w_st = a_end - a_r + ig_r
            m_loc = jnp.max(w_st, axis=-1, keepdims=True)
            ke = kth * jnp.exp(w_st - m_loc)
            c_loc = _dot(ke.astype(BF16), v_aug_b)
            m_new = jnp.maximum(a_end + m_st, m_loc)
            sp = jnp.exp(a_end + m_st - m_new)
            sl = jnp.exp(m_loc - m_new)
            c_ref[base + h] = sp * c_st + sl * c_loc
            m_ref[base + h] = jnp.broadcast_to(m_new, (1, LANES))
            if sub == 0:
                pair_out = out
            else:
                pair_out = jnp.where(lane < HEAD_DIM, pair_out, pltpu.roll(out, HEAD_DIM, 1))
        outs.append(pair_out)
    return jnp.concatenate(outs, axis=1)


def _mlstm_kernel(qf_ref, vf_ref, ktf_ref, gtf_ref, gcf_ref,
                  qb_ref, vb_ref, ktb_ref, gtb_ref, gcb_ref, brow_ref, bcol_ref,
                  hf_ref, hb_ref, c_ref, m_ref, *, n_sub):
    @pl.when(pl.program_id(1) == 0)
    def _():
        c_ref[...] = jnp.zeros_like(c_ref)
        m_ref[...] = jnp.zeros_like(m_ref)

    bcol = bcol_ref[...]
    brow = brow_ref[...]
    L = ML_CHUNK
    for ci in range(n_sub):
        rf = slice(ci * L, (ci + 1) * L)
        hf_ref[rf, :] = _mlstm_dir(qf_ref[rf, :], vf_ref[rf, :], ktf_ref[:, rf], gtf_ref[:, rf] + bcol,
                                   gcf_ref[rf, :] + brow, c_ref, m_ref, 0, False)
        rb = slice((n_sub - 1 - ci) * L, (n_sub - ci) * L)
        hb_ref[rb, :] = _mlstm_dir(qb_ref[rb, :], vb_ref[rb, :], ktb_ref[:, rb], gtb_ref[:, rb] + bcol,
                                   gcb_ref[rb, :] + brow, c_ref, m_ref, N_HEADS, True)


def mlstm(q, v, kt, gt, g, gate_b):
    b, s, w = q.shape
    n_sub = 2
    L = n_sub * ML_CHUNK
    nc = s // L
    bflat = gate_b.reshape(16)
    brow = jnp.zeros((1, LANES), F32).at[0, :16].set(bflat)
    bcol = bflat.reshape(16, 1)
    fw = lambda i, j: (i, j, 0)
    bw = lambda i, j: (i, nc - 1 - j, 0)
    fwt = lambda i, j: (i, 0, j)
    bwt = lambda i, j: (i, 0, nc - 1 - j)

    def specs(m, mt):
        return [pl.BlockSpec((None, L, w), m), pl.BlockSpec((None, L, w), m),
                pl.BlockSpec((None, w, L), mt), pl.BlockSpec((None, 16, L), mt),
                pl.BlockSpec((None, L, LANES), m)]

    return pl.pallas_call(
        functools.partial(_mlstm_kernel, n_sub=n_sub),
        grid=(b, nc),
        in_specs=specs(fw, fwt) + specs(bw, bwt) + [
            pl.BlockSpec((1, LANES), lambda i, j: (0, 0)),
            pl.BlockSpec((16, 1), lambda i, j: (0, 0))],
        out_specs=[pl.BlockSpec((None, L, w), fw), pl.BlockSpec((None, L, w), bw)],
        out_shape=[jax.ShapeDtypeStruct((b, s, w), F32)] * 2,
        scratch_shapes=[pltpu.VMEM((2 * N_HEADS, LANES, LANES), F32),
                        pltpu.VMEM((2 * N_HEADS, 1, LANES), F32)],
        compiler_params=_params("parallel", "arbitrary"),
        name="mlstm",
    )(q, v, kt, gt, g, q, v, kt, gt, g, brow, bcol)


def _rope_table_kernel(pos_ref, post_ref, inv_ref, invt_ref, cos_ref, sin_ref, cost_ref, sint_ref):
    ang = pos_ref[...].astype(F32) * inv_ref[...]
    cos = jnp.cos(ang)
    sin = jnp.sin(ang)
    cos_ref[...] = cos
    sin_ref[...] = sin
    cost_ref[...] = cos.T
    sint_ref[...] = sin.T


def rope_tables(positions):
    b, s = positions.shape
    tm = min(512, s)
    inv = ROPE_THETA ** (-jnp.arange(ROPE_HALF, dtype=F32) / ROPE_HALF)
    inv_row = jnp.zeros((1, LANES), F32).at[0, MLA_NOPE:MLA_NOPE + ROPE_HALF].set(inv)
    inv_row = inv_row.at[0, MLA_NOPE + ROPE_HALF:MLA_NOPE + MLA_ROPE].set(inv)
    spec = pl.BlockSpec((None, tm, LANES), lambda i, j: (i, j, 0))
    spec_t = pl.BlockSpec((None, LANES, tm), lambda i, j: (i, 0, j))
    return pl.pallas_call(
        _rope_table_kernel,
        grid=(b, s // tm),
        in_specs=[pl.BlockSpec((None, tm, 1), lambda i, j: (i, j, 0)),
                  pl.BlockSpec((None, 1, tm), lambda i, j: (i, 0, j)),
                  pl.BlockSpec((1, LANES), lambda i, j: (0, 0)),
                  pl.BlockSpec((LANES, 1), lambda i, j: (0, 0))],
        out_specs=[spec, spec, spec_t, spec_t],
        out_shape=[jax.ShapeDtypeStruct((b, s, LANES), F32)] * 2
        + [jax.ShapeDtypeStruct((b, LANES, s), F32)] * 2,
        compiler_params=_params("parallel", "parallel"),
        name="rope_tables",
    )(positions.reshape(b, s, 1), positions.reshape(b, 1, s), inv_row, inv_row.reshape(LANES, 1))


def _mla_proj_kernel(cq_ref, ckv_ref, kr_ref, cos_ref, sin_ref, cost_ref, sint_ref, qg_ref, kvg_ref,
                     wqt_ref, wk_ref, wvt_ref, qt_ref, k_ref, vt_ref):
    cos = cos_ref[...]
    sin = sin_ref[...]
    lane = lax.broadcasted_iota(jnp.int32, cos.shape, 1)
    x1 = (lane >= MLA_NOPE) & (lane < MLA_NOPE + ROPE_HALF)
    x2 = (lane >= MLA_NOPE + ROPE_HALF) & (lane < MLA_NOPE + MLA_ROPE)
    kr = kr_ref[...]
    krr = (kr * cos + pltpu.roll(kr, LANES - ROPE_HALF, 1) * jnp.where(x1, -sin, 0.0)
           + pltpu.roll(kr, ROPE_HALF, 1) * jnp.where(x2, sin, 0.0))

    cqn = (_rms(cq_ref[...], MLA_Q_RANK) * qg_ref[...]).astype(BF16)
    ckvn = (_rms(ckv_ref[...], MLA_KV_RANK) * kvg_ref[...]).astype(BF16)
    kn = _dot(ckvn, wk_ref[...])
    for h in range(N_HEADS):
        sl = slice(h * LANES, (h + 1) * LANES)
        k_ref[:, sl] = (kn[:, sl] + krr).astype(BF16)
    vrow = lax.broadcasted_iota(jnp.int32, vt_ref.shape, 0)
    ones_row = jnp.where(vrow % V_ROWS == HEAD_DIM, 1.0, 0.0)
    vt_ref[...] = (_dot_nt(wvt_ref[...], ckvn) + ones_row).astype(BF16)

    cos_t = cost_ref[...]
    sin_t = sint_ref[...]
    row = lax.broadcasted_iota(jnp.int32, cos_t.shape, 0)
    x1t = (row >= MLA_NOPE) & (row < MLA_NOPE + ROPE_HALF)
    x2t = (row >= MLA_NOPE + ROPE_HALF) & (row < MLA_NOPE + MLA_ROPE)
    sin_at = jnp.where(x1t, -sin_t, 0.0)
    sin_bt = jnp.where(x2t, sin_t, 0.0)
    qt = _dot_nt(wqt_ref[...], cqn)
    scale = (MLA_NOPE + MLA_ROPE) ** -0.5 * math.log2(math.e)
    for h in range(N_HEADS):
        sl = slice(h * LANES, (h + 1) * LANES)
        xs = qt[sl, :]
        roped = (xs * cos_t + pltpu.roll(xs, LANES - ROPE_HALF, 0) * sin_at
                 + pltpu.roll(xs, ROPE_HALF, 0) * sin_bt)
        qt_ref[sl, :] = (roped * scale).astype(BF16)


def mla_proj(cq, ckv, kr, tables, q_norm, kv_norm, w_uq, w_uk, w_uv):
    b, s, _ = cq.shape
    tm = min(512, s)
    dqk = MLA_NOPE + MLA_ROPE
    cos, sin, cos_t, sin_t = tables
    wq = w_uq.reshape(MLA_Q_RANK, N_HEADS, dqk)
    wq = jnp.pad(wq, ((0, 256 - MLA_Q_RANK), (0, 0), (0, LANES - dqk))).reshape(256, N_HEADS * LANES)
    wk = w_uk.reshape(MLA_KV_RANK, N_HEADS, MLA_NOPE)
    wk = jnp.pad(wk, ((0, 0), (0, 0), (0, LANES - MLA_NOPE))).reshape(MLA_KV_RANK, N_HEADS * LANES)
    qg = jnp.pad(q_norm, (0, 256 - MLA_Q_RANK)).reshape(1, 256)
    kvg = kv_norm.reshape(1, MLA_KV_RANK)
    row = lambda w: pl.BlockSpec((None, tm, w), lambda i, j: (i, j, 0))
    col = lambda w: pl.BlockSpec((None, w, tm), lambda i, j: (i, 0, j))
    full = lambda a: pl.BlockSpec(a.shape, lambda i, j: (0,) * a.ndim)
    wv = jnp.pad(w_uv.reshape(MLA_KV_RANK, N_HEADS, HEAD_DIM), ((0, 0), (0, 0), (0, V_ROWS - HEAD_DIM)))
    wqt, wkb = wq.T.astype(BF16), wk.astype(BF16)
    wvt = wv.reshape(MLA_KV_RANK, N_HEADS * V_ROWS).T.astype(BF16)
    return pl.pallas_call(
        _mla_proj_kernel,
        grid=(b, s // tm),
        in_specs=[row(256), row(128), row(128), row(128), row(128), col(128), col(128),
                  full(qg), full(kvg), full(wqt), full(wkb), full(wvt)],
        out_specs=[col(512), row(512), col(N_HEADS * V_ROWS)],
        out_shape=[jax.ShapeDtypeStruct((b, 512, s), BF16), jax.ShapeDtypeStruct((b, s, 512), BF16),
                   jax.ShapeDtypeStruct((b, N_HEADS * V_ROWS, s), BF16)],
        compiler_params=_params("parallel", "parallel"),
        name="mla_proj",
    )(cq, ckv, kr, cos, sin, cos_t, sin_t, qg, kvg, wqt, wkb, wvt)


def _flash_kernel(qt_ref, k_ref, vt_ref, o_ref, *, tk, nh):
    s = k_ref.shape[0]
    tq = qt_ref.shape[1]
    nk = s // tk
    qts = [qt_ref[sub * LANES:(sub + 1) * LANES, :] for sub in range(nh)]

    def scores(t):
        off = pl.multiple_of(t * tk, tk)
        return [_dot(k_ref[pl.ds(off, tk), sub * LANES:(sub + 1) * LANES], qts[sub])
                for sub in range(nh)]

    def update(t, sc, stats):
        off = pl.multiple_of(t * tk, tk)
        new = []
        for sub in range(nh):
            m, acc = stats[sub]
            vt = vt_ref[sub * V_ROWS:(sub + 1) * V_ROWS, pl.ds(off, tk)]
            m_new = jnp.maximum(m, jnp.max(sc[sub], axis=0, keepdims=True))
            alpha = jnp.exp2(m - m_new)
            p = jnp.exp2((sc[sub] - m_new).astype(BF16))
            acc = alpha * acc + _dot(vt, p)
            new.append((m_new, acc))
        return new

    def body(u, carry):
        sc_a, stats = carry
        ta = 2 * u
        sc_b = scores(ta + 1)
        stats = update(ta, sc_a, stats)
        sc_a = scores(jnp.minimum(ta + 2, nk - 1))
        stats = update(ta + 1, sc_b, stats)
        return sc_a, stats

    init = [(jnp.full((1, tq), -jnp.inf, F32), jnp.zeros((V_ROWS, tq), F32)) for _ in range(nh)]
    _, fin = lax.fori_loop(0, nk // 2, body, (scores(0), init))
    out_t = jnp.concatenate([acc[:HEAD_DIM] / acc[HEAD_DIM:HEAD_DIM + 1] for (_, acc) in fin],
                            axis=0)
    o_ref[...] = out_t.T


def flash_attention(qt, k, vt):
    b, _, s = qt.shape
    tq = min(256, s)
    tk = min(512, s)
    nh = FLASH_HEADS
    ng = N_HEADS // nh
    return pl.pallas_call(
        functools.partial(_flash_kernel, tk=tk, nh=nh),
        grid=(b, ng, s // tq),
        in_specs=[pl.BlockSpec((None, nh * LANES, tq), lambda i, p, j: (i, p, j)),
                  pl.BlockSpec((None, s, nh * LANES), lambda i, p, j: (i, 0, p)),
                  pl.BlockSpec((None, nh * V_ROWS, s), lambda i, p, j: (i, p, 0))],
        out_specs=pl.BlockSpec((None, tq, nh * HEAD_DIM), lambda i, p, j: (i, j, p)),
        out_shape=jax.ShapeDtypeStruct((b, s, N_HEADS * HEAD_DIM), F32),
        compiler_params=_params("parallel", "parallel", "parallel"),
        name="flash_attention",
    )(qt, k, vt)


def _shifted(x, prev_row, next_row, first, last):
    tm = x.shape[0]
    row = lax.broadcasted_iota(jnp.int32, x.shape, 0)
    prev_row = jnp.where(first, 0.0, prev_row)
    next_row = jnp.where(last, 0.0, next_row)
    xm1 = jnp.where(row == 0, prev_row, pltpu.roll(x, 1, 0))
    xp1 = jnp.where(row == tm - 1, next_row, pltpu.roll(x, tm - 1, 0))
    return xm1, xp1


def _conv_kernel(hy_ref, hyp_ref, hyn_ref, sc_ref, scp_ref, scn_ref, hw_ref, hb_ref, sw_ref,
                 x1_ref, x2_ref, z_ref, yd_ref):
    j = pl.program_id(1)
    first = j == 0
    last = j == pl.num_programs(1) - 1
    x = hy_ref[...]
    xm1, xp1 = _shifted(x, hyp_ref[7:8, :], hyn_ref[0:1, :], first, last)
    hw = hw_ref[...]
    proj = xm1 * hw[0:1] + x * hw[1:2] + xp1 * hw[2:3] + hb_ref[...]
    x1_ref[...] = proj[:, 0:GROUP_W]
    x2_ref[...] = proj[:, GROUP_W:2 * GROUP_W]
    z_ref[...] = proj[:, 2 * GROUP_W:]
    su = sc_ref[...]
    g = GROUP_W
    prod = su[:, g:2 * g] * su[:, 2 * g:]
    pprev = scp_ref[7:8, g:2 * g] * scp_ref[7:8, 2 * g:]
    pnext = scn_ref[0:1, g:2 * g] * scn_ref[0:1, 2 * g:]
    pm1, pp1 = _shifted(prod, pprev, pnext, first, last)
    sw = sw_ref[...]
    yd_ref[...] = su[:, :g] * (pm1 * sw[0:1] + prod * sw[1:2] + pp1 * sw[2:3])


def conv_mixers(hy_u, sc_u, hy_conv_w, hy_conv_b, sc_conv_w):
    b, s, w3 = hy_u.shape
    tm = min(512, s)
    nb8 = s // 8
    r8 = tm // 8
    main = pl.BlockSpec((None, tm, w3), lambda i, j: (i, j, 0))
    prev = pl.BlockSpec((None, 8, w3), lambda i, j: (i, jnp.maximum(j * r8 - 1, 0), 0))
    nxt = pl.BlockSpec((None, 8, w3), lambda i, j: (i, jnp.minimum((j + 1) * r8, nb8 - 1), 0))
    full = lambda a: pl.BlockSpec(a.shape, lambda i, j: (0,) * a.ndim)
    hw = hy_conv_w.T
    hb = hy_conv_b.reshape(1, w3)
    sw = sc_conv_w.T
    out = pl.BlockSpec((None, tm, GROUP_W), lambda i, j: (i, j, 0))
    return pl.pallas_call(
        _conv_kernel,
        grid=(b, s // tm),
        in_specs=[main, prev, nxt, main, prev, nxt, full(hw), full(hb), full(sw)],
        out_specs=[out] * 4,
        out_shape=[jax.ShapeDtypeStruct((b, s, GROUP_W), F32)] * 4,
        compiler_params=_params("parallel", "parallel"),
        name="conv_mixers",
    )(hy_u, hy_u, hy_u, sc_u, sc_u, sc_u, hw, hb, sw)


def _fft_dims(s):
    n = 2 * s
    lg = int(round(math.log2(n)))
    assert 1 << lg == n
    n1 = 1 << ((lg + 1) // 2)
    return n1, n // n1


def _filter_kernel(frow_ref, w1h_ref, w1l_ref, b1_ref, fr1_ref, w2h_ref, w2l_ref, b2_ref, fr2_ref,
                   w3h_ref, w3l_ref, b3_ref, ld_ref, k_ref, norm_ref, *, length):
    i = pl.program_id(0)
    tm = k_ref.shape[0]
    hm = tm // 2
    half = LANES // 2

    def times(first):
        n = first + lax.broadcasted_iota(jnp.int32, (hm, 1), 0)
        tt = jnp.where(n < length, n, 2 * length - 1 - n)
        return tt.astype(F32) / length

    ta, tb = times(i * tm), times(i * tm + hm)
    lane = lax.broadcasted_iota(jnp.int32, (hm, LANES), 1)
    sub = lane % half
    t2 = jnp.where(lane < half, ta, tb)
    ang = t2 * frow_ref[...] + jnp.where(sub > HY_BANDS, 0.5 * math.pi, 0.0)
    z = jnp.where(sub == 0, t2, jnp.where(sub <= 2 * HY_BANDS, jnp.sin(ang), 0.0))
    hid = jnp.sin(fr1_ref[...] * (_dot3(z, w1h_ref[...], w1l_ref[...]) + b1_ref[...]))
    hid = jnp.sin(fr2_ref[...] * (_dot3(hid, w2h_ref[...], w2l_ref[...]) + b2_ref[...]))
    decay = jnp.exp(ld_ref[...])
    total = None
    for part, t in enumerate((ta, tb)):
        filt = ((_dot3(hid, w3h_ref[part], w3l_ref[part]) + b3_ref[...]) * jnp.exp(-t * decay))
        k_ref[part * hm:(part + 1) * hm, :] = filt
        psum = jnp.sum(jnp.abs(filt), axis=0, keepdims=True)
        total = psum if total is None else total + psum

    @pl.when(i == 0)
    def _():
        norm_ref[...] = jnp.zeros_like(norm_ref)

    norm_ref[...] += total


def hyena_filter_taps(length, w1, b1, fr1, w2, b2, fr2, w3, b3, log_decay):
    tm = min(512, length)
    n_half = length // tm
    oc = 2 * GROUP_W
    half = LANES // 2
    assert HY_FFN == half
    bands = jnp.arange(1, HY_BANDS + 1, dtype=F32) * (2.0 * math.pi)
    fhalf = jnp.zeros((half,), F32).at[1:1 + HY_BANDS].set(bands).at[1 + HY_BANDS:1 + 2 * HY_BANDS].set(bands)
    frow = jnp.concatenate([fhalf, fhalf]).reshape(1, LANES)
    w1half = jnp.zeros((half, HY_FFN), F32).at[:1 + 2 * HY_BANDS].set(w1)
    blockdiag = lambda a: jnp.kron(jnp.eye(2, dtype=F32), a)
    w1p, w2p = blockdiag(w1half), blockdiag(w2)
    dup = lambda a: jnp.concatenate([a, a]).reshape(1, LANES)
    bydir = lambda a: jnp.moveaxis(a.reshape(a.shape[0], 2, 2, GROUP_W), 2, 0).reshape(2, a.shape[0], oc)
    w3d, b3d, ldd = bydir(w3), bydir(b3.reshape(1, -1)), bydir(log_decay.reshape(1, -1))
    zeros = jnp.zeros_like(w3d)
    w3p = jnp.stack([jnp.concatenate([w3d, zeros], axis=1), jnp.concatenate([zeros, w3d], axis=1)],
                    axis=1)
    full = lambda a: pl.BlockSpec(a.shape, lambda i: (0,) * a.ndim)
    dirspec = lambda a: pl.BlockSpec((None,) + a.shape[1:], lambda i: (i // n_half,) + (0,) * (a.ndim - 1))
    w1h, w1l = _split2(w1p)
    w2h, w2l = _split2(w2p)
    w3h, w3l = _split2(w3p)
    return pl.pallas_call(
        functools.partial(_filter_kernel, length=length),
        grid=(2 * n_half,),
        in_specs=[full(frow), full(w1h), full(w1l), full(dup(b1)), full(dup(fr1)), full(w2h), full(w2l),
                  full(dup(b2)), full(dup(fr2)), dirspec(w3h), dirspec(w3l), dirspec(b3d), dirspec(ldd)],
        out_specs=[pl.BlockSpec((tm, oc), lambda i: (i, 0)), pl.BlockSpec((1, oc), lambda i: (0, 0))],
        out_shape=[jax.ShapeDtypeStruct((2 * length, oc), F32), jax.ShapeDtypeStruct((1, oc), F32)],
        compiler_params=_params("arbitrary"),
        name="hyena_filter",
    )(frow, w1h, w1l, dup(b1), dup(fr1), w2h, w2l, dup(b2), dup(fr2), w3h, w3l, b3d, ldd)


def _dft_consts(s):
    n1, n2 = _fft_dims(s)
    n = n1 * n2
    kh = n1 // 2 + 1
    kp = -(-kh // 4) * 4
    pad_rows = lambda a: np.concatenate([a, np.zeros((kp - kh, a.shape[1]))], axis=0)
    a1 = 2.0 * np.pi * np.outer(np.arange(kh), np.arange(n1)) / n1
    f1 = np.concatenate([pad_rows(np.cos(a1)), pad_rows(-np.sin(a1))], axis=0)
    a2 = 2.0 * np.pi * np.outer(np.arange(n2), np.arange(n2)) / n2
    c2, s2 = np.cos(a2), np.sin(a2)
    f2_fwd = np.block([[c2, s2], [-s2, c2]])
    f2_inv = np.block([[c2, -s2], [s2, c2]])
    at = 2.0 * np.pi * np.outer(np.arange(kp), np.arange(n2)) / n
    tw = np.stack([np.cos(at), np.sin(at)], axis=0)[..., None]
    m1 = np.arange(n1 // 2)
    a3 = 2.0 * np.pi * np.outer(m1, np.arange(kh)) / n1
    wgt = np.where((np.arange(kh) == 0) | (np.arange(kh) == n1 // 2), 1.0, 2.0) / n
    f3 = np.concatenate([pad_rows((np.cos(a3) * wgt).T).T, pad_rows((-np.sin(a3) * wgt).T).T],
                        axis=1)
    f = lambda a: jnp.asarray(a, F32)
    x3 = lambda a: _lhs_x3(f(a))
    return dict(n1=n1, n2=n2, kh=kh, kp=kp, f1=x3(f1), f1_half=x3(f1[:, :n1 // 2]), f2_fwd=x3(f2_fwd),
                f2_inv=x3(f2_inv), tw=f(tw), f3=x3(f3))


_N2_TILE = 8


def _dft1_kernel(x_ref, f_ref, s_ref, o_ref, xs_ref):
    f = f_ref[...]
    scale = s_ref[...]
    for m in range(_N2_TILE):
        xs_ref[...] = x_ref[:, m, :]
        o_ref[:, m, :] = _dot_x3(f, xs_ref[...] * scale)


def dft_stage1(x, f1, scale_row):
    g, r, n2, ch = x.shape
    m = f1.shape[0]
    return pl.pallas_call(
        _dft1_kernel,
        grid=(g, n2 // _N2_TILE),
        in_specs=[pl.BlockSpec((None, r, _N2_TILE, ch), lambda i, j: (i, 0, j, 0)),
                  pl.BlockSpec(f1.shape, lambda i, j: (0, 0)),
                  pl.BlockSpec((1, ch), lambda i, j: (0, 0))],
        out_specs=pl.BlockSpec((None, m, _N2_TILE, ch), lambda i, j: (i, 0, j, 0)),
        out_shape=jax.ShapeDtypeStruct((g, m, n2, ch), F32),
        scratch_shapes=[pltpu.VMEM((r, ch), F32)],
        compiler_params=_params("parallel", "parallel"),
        name="dft_stage1",
    )(x, f1, scale_row)


def _twiddle(re, im, tc, ts, conj):
    if conj:
        return re * tc - im * ts, im * tc + re * ts
    return re * tc + im * ts, im * tc - re * ts


_K1_STEP = 4


def _dft2_filter_kernel(a_ref, tw_ref, ff_ref, o_ref):
    n2 = a_ref.shape[2]
    for i in range(_K1_STEP):
        br, bi = _twiddle(a_ref[0, i], a_ref[1, i], tw_ref[0, i], tw_ref[1, i], False)
        zz = _dot_x3(ff_ref[...], jnp.concatenate([br, bi], axis=0))
        o_ref[0, i] = zz[:n2]
        o_ref[1, i] = zz[n2:]


def dft_stage2_filter(a, consts):
    _, kp, n2, ch = a.shape
    return pl.pallas_call(
        _dft2_filter_kernel,
        grid=(kp // _K1_STEP,),
        input_output_aliases={0: 0},
        in_specs=[pl.BlockSpec((2, _K1_STEP, n2, ch), lambda k: (0, k, 0, 0)),
                  pl.BlockSpec((2, _K1_STEP, n2, 1), lambda k: (0, k, 0, 0)),
                  pl.BlockSpec(consts["f2_fwd"].shape, lambda k: (0, 0))],
        out_specs=pl.BlockSpec((2, _K1_STEP, n2, ch), lambda k: (0, k, 0, 0)),
        out_shape=jax.ShapeDtypeStruct(a.shape, F32),
        compiler_params=_params("parallel"),
        name="dft_stage2_filter",
    )(a, consts["tw"], consts["f2_fwd"])


def _dft2_conv_kernel(a_ref, tw_ref, kf_ref, ff_ref, fi_ref, o_ref):
    n2 = a_ref.shape[2]
    for i in range(_K1_STEP):
        tc, ts = tw_ref[0, i], tw_ref[1, i]
        br, bi = _twiddle(a_ref[0, i], a_ref[1, i], tc, ts, False)
        zz = _dot_x3(ff_ref[...], jnp.concatenate([br, bi], axis=0))
        zr, zi = zz[:n2], zz[n2:]
        kr, ki = kf_ref[0, i], kf_ref[1, i]
        pr = zr * kr - zi * ki
        pi = zr * ki + zi * kr
        vv = _dot_x3(fi_ref[...], jnp.concatenate([pr, pi], axis=0))
        vr, vi = _twiddle(vv[:n2], vv[n2:], tc, ts, True)
        o_ref[0, i] = vr
        o_ref[1, i] = vi


def dft_stage2_conv(a, kf, order, consts):
    b, _, kp, n2, ch = a.shape
    blk = pl.BlockSpec((None, 2, _K1_STEP, n2, ch), lambda i, k: (i, 0, k, 0, 0))
    mat = pl.BlockSpec(consts["f2_fwd"].shape, lambda i, k: (0, 0))
    return pl.pallas_call(
        _dft2_conv_kernel,
        grid=(b, kp // _K1_STEP),
        input_output_aliases={0: 0},
        in_specs=[blk,
                  pl.BlockSpec((2, _K1_STEP, n2, 1), lambda i, k: (0, k, 0, 0)),
                  pl.BlockSpec((2, _K1_STEP, n2, ch), lambda i, k: (0, k, 0, order)),
                  mat, mat],
        out_specs=blk,
        out_shape=jax.ShapeDtypeStruct(a.shape, F32),
        compiler_params=_params("parallel", "parallel"),
        name="dft_stage2_conv",
    )(a, consts["tw"], kf, consts["f2_fwd"], consts["f2_inv"])


def _dft3_kernel(v_ref, f_ref, gate_ref, z_ref, bias_ref, o_ref, vs_ref, ys_ref):
    f = f_ref[...]
    for m in range(_N2_TILE):
        vs_ref[...] = v_ref[:, m, :]
        ys_ref[:, m, :] = _dot_x3(f, vs_ref[...])
    o_ref[...] = gate_ref[...] * (ys_ref[...] + z_ref[...] * bias_ref[...])


def dft_stage3_gate(v, f3, gate, z, bias_row):
    b, m, n2, ch = v.shape
    r = f3.shape[0]
    row = pl.BlockSpec((None, r, _N2_TILE, ch), lambda i, j: (i, 0, j, 0))
    return pl.pallas_call(
        _dft3_kernel,
        grid=(b, n2 // _N2_TILE),
        in_specs=[pl.BlockSpec((None, m, _N2_TILE, ch), lambda i, j: (i, 0, j, 0)),
                  pl.BlockSpec(f3.shape, lambda i, j: (0, 0)),
                  row, row, pl.BlockSpec((1, ch), lambda i, j: (0, 0))],
        out_specs=row,
        out_shape=jax.ShapeDtypeStruct((b, r, n2, ch), F32),
        scratch_shapes=[pltpu.VMEM((m, ch), F32), pltpu.VMEM((r, _N2_TILE, ch), F32)],
        compiler_params=_params("parallel", "parallel"),
        name="dft_stage3_gate",
    )(v, f3, gate, z, bias_row)


def hyena_mixer(x1, x2, z, hy_w1, hy_b1, hy_fr1, hy_w2, hy_b2, hy_fr2, hy_w3, hy_b3, hy_log_decay,
                hy_bias):
    b, s, ch = z.shape
    consts = _dft_consts(s)
    n1, n2 = consts["n1"], consts["n2"]
    oc = 2 * ch
    taps, norm = hyena_filter_taps(s, hy_w1, hy_b1, hy_fr1, hy_w2, hy_b2, hy_fr2, hy_w3, hy_b3,
                                   hy_log_decay)
    kp = consts["kp"]
    ka = dft_stage1(taps.reshape(1, n1, n2, oc), consts["f1"], 1.0 / norm)
    kf = dft_stage2_filter(ka.reshape(2, kp, n2, oc), consts)
    half = n1 // 2
    view = lambda t: t.reshape(b, half, n2, ch)
    ones = jnp.ones((1, ch), F32)
    f1_half = consts["f1_half"]
    cur = view(z)
    for order, gate in enumerate((x1, x2)):
        a = dft_stage1(cur, f1_half, ones)
        v = dft_stage2_conv(a.reshape(b, 2, kp, n2, ch), kf, order, consts)
        cur = dft_stage3_gate(v.reshape(b, 2 * kp, n2, ch), consts["f3"], view(gate), cur,
                              hy_bias[order].reshape(1, ch))
    return cur.reshape(b, s, ch)


def _outproj_kernel(x_ref, hf_ref, hb_ref, o_ref, yb_ref, yc_ref, yd_ref, og_ref, hm_ref, wout_ref,
                    pg_ref, g1_ref, fg_ref, sc2_ref, sh2_ref, rwt_ref,
                    xn_ref, h2_ref, afft_ref):
    hm = hm_ref[...]
    y_a = _sigmoid(o_ref[...]) * (hf_ref[...] + hb_ref[...])
    acc = None
    for idx, y in enumerate((y_a, yb_ref[...], yc_ref[...], yd_ref[...])):
        ms = _dot(jnp.concatenate(_split2(y * y), axis=1), hm)
        yn = y * lax.rsqrt(ms + NORM_EPS) * og_ref[:, idx * GROUP_W:(idx + 1) * GROUP_W]
        part = _dot(yn.astype(BF16), wout_ref[idx * GROUP_W:(idx + 1) * GROUP_W, :])
        acc = part if acc is None else acc + part
    d = acc.shape[-1]
    xn = x_ref[...] + g1_ref[...] * (_rms(acc, d) * pg_ref[...])
    xn_ref[...] = xn
    h2 = _rms(xn, d) * fg_ref[...] * (1.0 + sc2_ref[...]) + sh2_ref[...]
    h2_ref[...] = h2.T.astype(BF16)
    logits_t = _dot_nt_hi(rwt_ref[...], h2)
    mx = jnp.max(logits_t, axis=0, keepdims=True)
    ex = jnp.exp(logits_t - mx)
    afft_ref[...] = ex / jnp.sum(ex, axis=0, keepdims=True)


def out_proj(x, hf, hb, o, yb, yc, yd, out_g, w_out, post_g, g1, ffn_g, sc2, sh2, router_w):
    b, s, d = x.shape
    tm = min(512, s)
    e = router_w.shape[1]
    hm1 = np.kron(np.eye(GROUP_W // HEAD_DIM), np.ones((HEAD_DIM, HEAD_DIM))) / HEAD_DIM
    hm = jnp.asarray(np.concatenate([hm1, hm1], axis=0), BF16)
    row = lambda w: pl.BlockSpec((None, tm, w), lambda i, j: (i, j, 0))
    full = lambda a: pl.BlockSpec(a.shape, lambda i, j: (0,) * a.ndim)
    vec = lambda: pl.BlockSpec((None, 1, d), lambda i, j: (i, 0, 0))
    r1 = lambda a: a.reshape(1, -1)
    wob = w_out.astype(BF16)
    rwt = router_w.T
    return pl.pallas_call(
        _outproj_kernel,
        grid=(b, s // tm),
        in_specs=[row(d)] + [row(GROUP_W)] * 6 + [full(r1(out_g)), full(hm), full(wob),
                                                  full(r1(post_g)), vec(), full(r1(ffn_g)), vec(), vec(),
                                                  full(rwt)],
        out_specs=[row(d), pl.BlockSpec((None, d, tm), lambda i, j: (i, 0, j)),
                   pl.BlockSpec((None, e, tm), lambda i, j: (i, 0, j))],
        out_shape=[jax.ShapeDtypeStruct((b, s, d), F32), jax.ShapeDtypeStruct((b, d, s), BF16),
                   jax.ShapeDtypeStruct((b, e, s), F32)],
        compiler_params=_params("parallel", "parallel"),
        name="out_proj",
    )(x, hf, hb, o, yb, yc, yd, r1(out_g), hm, wob, r1(post_g), g1.reshape(b, 1, d), r1(ffn_g),
      sc2.reshape(b, 1, d), sh2.reshape(b, 1, d), rwt)


def _lane_cumsum(x):
    n = x.shape[-1]
    lane = lax.broadcasted_iota(jnp.int32, x.shape, x.ndim - 1)
    shift = 1
    while shift < n:
        x = x + jnp.where(lane >= shift, pltpu.roll(x, shift, x.ndim - 1), 0.0)
        shift *= 2
    return x


def _select_kernel(aff_ref, u_ref, ps_ref, cb_ref, *, cap):
    aff = aff_ref[...]
    capf = float(cap)

    def body(i, bits):
        cand = bits | (jnp.int32(1) << (30 - i))
        cnt = jnp.sum(jnp.where(aff >= pltpu.bitcast(cand, F32), 1.0, 0.0), axis=-1, keepdims=True)
        return jnp.where(cnt >= capf, cand, bits)

    bits = lax.fori_loop(0, 31, body, jnp.zeros((aff.shape[0], 1), jnp.int32))
    thr = pltpu.bitcast(bits, F32)
    gt = aff > thr
    eq = aff == thr
    n_gt = jnp.sum(jnp.where(gt, 1.0, 0.0), axis=-1, keepdims=True)
    eqf = jnp.where(eq, 1.0, 0.0)
    rank_eq = _lane_cumsum(eqf) - eqf
    sel = gt | (eq & (rank_eq < capf - n_gt))
    self_ = jnp.where(sel, 1.0, 0.0)
    pos = _lane_cumsum(self_) - self_
    ps_ref[...] = jnp.where(sel, pos, -1.0).astype(jnp.int32)
    cb_ref[...] = _dot(self_.astype(BF16), u_ref[...]).astype(jnp.int32)


def ec_select(aff_t, cap):
    b, e, s = aff_t.shape
    nt = s // TOKEN_TILE
    assert nt + 1 <= LANES
    tok = np.arange(s)[:, None]
    u = jnp.asarray(tok < (np.arange(LANES)[None, :] * TOKEN_TILE), BF16)
    return pl.pallas_call(
        functools.partial(_select_kernel, cap=cap),
        grid=(b,),
        in_specs=[pl.BlockSpec((None, e, s), lambda i: (i, 0, 0)),
                  pl.BlockSpec((s, LANES), lambda i: (0, 0))],
        out_specs=[pl.BlockSpec((None, e, s), lambda i: (i, 0, 0)),
                   pl.BlockSpec((None, e, LANES), lambda i: (i, 0, 0))],
        out_shape=[jax.ShapeDtypeStruct((b, e, s), jnp.int32),
                   jax.ShapeDtypeStruct((b, e, LANES), jnp.int32)],
        compiler_params=_params("parallel"),
        name="ec_select",
    )(aff_t, u)


def _gather_kernel(cb_ref, ps_ref, ht_ref, o_ref, *, n_tiles, n_blocks, block, chunk):
    bi, ei = pl.program_id(0), pl.program_id(1)
    base = (bi * N_EXPERTS + ei) * LANES
    d, s = ht_ref.shape
    slot = lax.broadcasted_iota(jnp.int32, (block, chunk), 0)
    tok = lax.broadcasted_iota(jnp.int32, (1, chunk), 1)

    def one_block(j, carry):
        first = j * block

        def bounds(t, c):
            return (c[0] + (cb_ref[base + t + 1] <= first).astype(jnp.int32),
                    c[1] + (cb_ref[base + t] < first + block).astype(jnp.int32))

        t_lo, t_hi = lax.fori_loop(0, n_tiles, bounds, (jnp.int32(0), jnp.int32(0)))
        lo_tok = t_lo * TOKEN_TILE

        def part(i, acc):
            lower = lo_tok + i * chunk
            start = pl.multiple_of(jnp.minimum(lower, s - chunk), TOKEN_TILE)
            ps = jnp.where(tok + start >= lower, ps_ref[:, pl.ds(start, chunk)], -1)
            onehot = jnp.where(ps == slot + first, 1.0, 0.0).astype(BF16)
            return acc + _dot_nt(ht_ref[:, pl.ds(start, chunk)], onehot)

        n_parts = ((t_hi - t_lo) * TOKEN_TILE + chunk - 1) // chunk
        acc = lax.fori_loop(0, n_parts, part, jnp.zeros((d, block), F32))
        o_ref[pl.ds(pl.multiple_of(first, block), block), :] = acc.T.astype(o_ref.dtype)
        return carry

    lax.fori_loop(0, n_blocks, one_block, 0)


def ec_gather(ht, ps, cb, cap):
    b, d, s = ht.shape
    e = ps.shape[1]
    nt = s // TOKEN_TILE
    block = min(2 * SLOT_BLOCK, cap)
    chunk = min(block * s // cap + 2 * TOKEN_TILE, s)
    grid_spec = pltpu.PrefetchScalarGridSpec(
        num_scalar_prefetch=1,
        grid=(b, e),
        in_specs=[pl.BlockSpec((None, None, 1, s), lambda i, j, cb: (i, j, 0, 0)),
                  pl.BlockSpec((None, d, s), lambda i, j, cb: (i, 0, 0))],
        out_specs=pl.BlockSpec((None, None, cap, d), lambda i, j, cb: (i, j, 0, 0)),
    )
    return pl.pallas_call(
        functools.partial(_gather_kernel, n_tiles=nt, n_blocks=cap // block, block=block, chunk=chunk),
        grid_spec=grid_spec,
        out_shape=jax.ShapeDtypeStruct((b, e, cap, d), BF16),
        compiler_params=_params("parallel", "arbitrary"),
        name="ec_gather",
    )(cb.reshape(-1), ps.reshape(b, e, 1, s), ht)


def _ffn_kernel(x_ref, wg_ref, wu_ref, wd_ref, o_ref, acc_ref, *, n_f):
    f = pl.program_id(1)

    @pl.when(f == 0)
    def _():
        acc_ref[...] = jnp.zeros_like(acc_ref)

    bsz, cap, d = x_ref.shape
    x = x_ref[...].reshape(bsz * cap, d)
    a = _dot(x, wg_ref[...].astype(BF16))
    up = _dot(x, wu_ref[...].astype(BF16))
    act = (a * _sigmoid(a) * up).astype(BF16)
    acc_ref[...] += _dot(act, wd_ref[...].astype(BF16))

    @pl.when(f == n_f - 1)
    def _():
        o_ref[...] = acc_ref[...].reshape(bsz, cap, d).astype(o_ref.dtype)


def expert_ffn(xe, w_gate, w_up, w_down, layer):
    b, e, cap, d = xe.shape
    ff = w_gate.shape[-1]
    tf = min(512, ff)
    n_f = ff // tf
    return pl.pallas_call(
        functools.partial(_ffn_kernel, n_f=n_f),
        grid=(e, n_f),
        in_specs=[pl.BlockSpec((b, None, cap, d), lambda j, f: (0, j, 0, 0)),
                  pl.BlockSpec((None, None, d, tf), lambda j, f: (layer, j, 0, f)),
                  pl.BlockSpec((None, None, d, tf), lambda j, f: (layer, j, 0, f)),
                  pl.BlockSpec((None, None, tf, d), lambda j, f: (layer, j, f, 0))],
        out_specs=pl.BlockSpec((b, None, cap, d), lambda j, f: (0, j, 0, 0)),
        out_shape=jax.ShapeDtypeStruct((b, e, cap, d), BF16),
        scratch_shapes=[pltpu.VMEM((b * cap, d), F32)],
        compiler_params=_params("parallel", "arbitrary"),
        name="expert_ffn",
    )(xe, w_gate, w_up, w_down)


def _scatter_kernel(cb_ref, ps_ref, aff_ref, ye_ref, x_ref, pg_ref, g2_ref, o_ref, acc_ref,
                    *, n_sub, window):
    bi, ti, ei = pl.program_id(0), pl.program_id(1), pl.program_id(2)

    @pl.when(ei == 0)
    def _():
        acc_ref[...] = jnp.zeros_like(acc_ref)

    tt = acc_ref.shape[0]
    base = (bi * N_EXPERTS + ei) * LANES + ti * n_sub
    lo = cb_ref[base]
    hi = cb_ref[base + n_sub]

    @pl.when(hi > lo)
    def _():
        ps = ps_ref[...]
        gate = aff_ref[...]
        slot = lax.broadcasted_iota(jnp.int32, (window, tt), 0)
        cap = ye_ref.shape[0]
        first = (lo // BF16_ROWS) * BF16_ROWS

        def body(j, carry):
            lower = first + j * window
            start = pl.multiple_of(jnp.minimum(lower, cap - window), BF16_ROWS)
            hit = jnp.where(ps >= lower, ps, -1) == slot + start
            onehot = jnp.where(hit, 1.0, 0.0).astype(BF16)
            gate_slot = jnp.sum(jnp.where(hit, gate, 0.0), axis=1, keepdims=True)
            ye = (ye_ref[pl.ds(start, window), :].astype(F32) * gate_slot).astype(BF16)
            acc_ref[...] += lax.dot_general(onehot, ye, (((0,), (0,)), ((), ())),
                                            preferred_element_type=F32)
            return carry

        lax.fori_loop(0, (hi - first + window - 1) // window, body, 0)

    @pl.when(ei == N_EXPERTS - 1)
    def _():
        y = acc_ref[...]
        o_ref[...] = x_ref[...] + g2_ref[...] * (_rms(y, y.shape[-1]) * pg_ref[...])


def ec_scatter(ye, ps, aff_t, cb, x, post_g, g2):
    b, e, cap, d = ye.shape
    s = x.shape[1]
    n_sub = min(4, s // TOKEN_TILE)
    tt = n_sub * TOKEN_TILE
    grid_spec = pltpu.PrefetchScalarGridSpec(
        num_scalar_prefetch=1,
        grid=(b, s // tt, e),
        in_specs=[pl.BlockSpec((None, None, 1, tt), lambda i, t, j, cb: (i, j, 0, t)),
                  pl.BlockSpec((None, None, 1, tt), lambda i, t, j, cb: (i, j, 0, t)),
                  pl.BlockSpec((None, None, cap, d), lambda i, t, j, cb: (i, j, 0, 0)),
                  pl.BlockSpec((None, tt, d), lambda i, t, j, cb: (i, t, 0)),
                  pl.BlockSpec((1, d), lambda i, t, j, cb: (0, 0)),
                  pl.BlockSpec((None, 1, d), lambda i, t, j, cb: (i, 0, 0))],
        out_specs=pl.BlockSpec((None, tt, d), lambda i, t, j, cb: (i, t, 0)),
        scratch_shapes=[pltpu.VMEM((tt, d), F32)],
    )
    return pl.pallas_call(
        functools.partial(_scatter_kernel, n_sub=n_sub, window=min(2 * SLOT_BLOCK, cap)),
        grid_spec=grid_spec,
        out_shape=jax.ShapeDtypeStruct((b, s, d), F32),
        compiler_params=_params("parallel", "parallel", "arbitrary"),
        name="ec_scatter",
    )(cb.reshape(-1), ps.reshape(b, e, 1, s), aff_t.reshape(b, e, 1, s), ye, x, post_g.reshape(1, d),
      g2.reshape(b, 1, d))


def kernel(x, c, positions, ada_w, ada_b, mix_pre_g, mix_post_g, ffn_pre_g, ffn_post_g, w_in, ml_gate_b, mla_q_norm, mla_kv_norm, mla_w_uq, mla_w_uk, mla_w_uv, hy_conv_w, hy_conv_b, hy_w1, hy_b1, hy_fr1, hy_w2, hy_b2, hy_fr2, hy_w3, hy_b3, hy_log_decay, hy_bias, sc_conv_w, mix_out_g, w_out, router_w, exp_w_gate, exp_w_up, exp_w_down):
    depth = ada_w.shape[0]
    b, s, d = x.shape
    cap = EC_CAPACITY * s // N_EXPERTS
    mod = ada_mod(c, ada_w, ada_b)
    tables = rope_tables(positions)
    for l in range(depth):
        sh1, sc1, g1, sh2, sc2, g2 = (mod[l, :, i * d:(i + 1) * d] for i in range(6))
        u = in_proj(x, mix_pre_g[l], sc1, sh1, w_in[l])
        hf, hb = mlstm(u["q"], u["v"], u["kT"], u["gT"], u["g"], ml_gate_b[l])
        qa, ka, va = mla_proj(u["cq"], u["ckv"], u["kr"], tables, mla_q_norm[l], mla_kv_norm[l],
                              mla_w_uq[l], mla_w_uk[l], mla_w_uv[l])
        y_b = flash_attention(qa, ka, va)
        x1, x2, z, y_d = conv_mixers(u["hy"], u["sc"], hy_conv_w[l], hy_conv_b[l], sc_conv_w[l])
        y_c = hyena_mixer(x1, x2, z, hy_w1[l], hy_b1[l], hy_fr1[l], hy_w2[l], hy_b2[l], hy_fr2[l],
                          hy_w3[l], hy_b3[l], hy_log_decay[l], hy_bias[l])
        xn, h2, aff_t = out_proj(x, hf, hb, u["o"], y_b, y_c, y_d, mix_out_g[l], w_out[l],
                                      mix_post_g[l], g1, ffn_pre_g[l], sc2, sh2, router_w[l])
        ps, cb = ec_select(aff_t, cap)
        xe = ec_gather(h2, ps, cb, cap)
        ye = expert_ffn(xe, exp_w_gate, exp_w_up, exp_w_down, l)
        x = ec_scatter(ye, ps, aff_t, cb, xn, ffn_post_g[l], g2)
    return x
```

```python
import functools
import math

import numpy as np
import jax
import jax.numpy as jnp
from jax import lax
from jax.experimental import pallas as pl
from jax.experimental.pallas import tpu as pltpu

F32 = jnp.float32
BF16 = jnp.bfloat16
HIGHEST = lax.Precision.HIGHEST

GROUP_W = 256
HEAD_DIM = 64
N_HEADS = 4
ML_CHUNK = 128
MLA_Q_RANK = 224
MLA_KV_RANK = 128
MLA_NOPE = 64
MLA_ROPE = 32
ROPE_HALF = MLA_ROPE // 2
ROPE_THETA = 10000.0
HY_BANDS = 8
HY_FFN = 64
N_EXPERTS = 16
EC_CAPACITY = 2
NORM_EPS = 1e-6
LANES = 128
BF16_ROWS = 16
V_ROWS = HEAD_DIM + BF16_ROWS
FLASH_HEADS = 2
SLOT_BLOCK = 128
TOKEN_TILE = 256
VMEM_LIMIT = 56 * 1024 * 1024


def _params(*sem):
    return pltpu.CompilerParams(dimension_semantics=sem, vmem_limit_bytes=VMEM_LIMIT)


def _dot(a, b):
    return jnp.dot(a, b, preferred_element_type=F32)


def _dot_hi(a, b):
    return jnp.dot(a, b, precision=HIGHEST, preferred_element_type=F32)


def _dot_nt(a, b):
    return lax.dot_general(a, b, (((1,), (1,)), ((), ())), preferred_element_type=F32)


def _dot_nt_hi(a, b):
    return lax.dot_general(a, b, (((1,), (1,)), ((), ())), precision=HIGHEST,
                           preferred_element_type=F32)


def _split2(x):
    hi = x.astype(BF16)
    lo = (x - hi.astype(F32)).astype(BF16)
    return hi, lo


def _split3(x):
    hi = x.astype(BF16)
    r = x - hi.astype(F32)
    mid = r.astype(BF16)
    lo = (r - mid.astype(F32)).astype(BF16)
    return hi, mid, lo


def _lhs_x3(f):
    hi, lo = _split2(f)
    return jnp.concatenate([hi, lo, hi], axis=1)


def _dot_x3(f3, x):
    hi, lo = _split2(x)
    return _dot(f3, jnp.concatenate([hi, hi, lo], axis=0))


def _dot3(a, w_hi, w_lo):
    a_hi, a_lo = _split2(a)
    return _dot(a_hi, w_hi) + _dot(a_hi, w_lo) + _dot(a_lo, w_hi)


def _rms(x, n):
    ms = jnp.sum(x * x, axis=-1, keepdims=True) * (1.0 / n)
    return x * lax.rsqrt(ms + NORM_EPS)


def _log_sigmoid(x):
    return jnp.minimum(x, 0.0) - jnp.log(1.0 + jnp.exp(-jnp.abs(x)))


def _sigmoid(x):
    return 1.0 / (1.0 + jnp.exp(-x))


def _ada_kernel(c_ref, w_ref, b_ref, o_ref):
    c = c_ref[...]
    cs = c * _sigmoid(c)
    o_ref[...] = _dot_hi(cs, w_ref[...]) + b_ref[...]


def ada_mod(c, ada_w, ada_b):
    depth, d, n6 = ada_w.shape
    b = c.shape[0]
    bp = 8
    cp = jnp.zeros((bp, d), F32).at[:b].set(c)
    tn = 1536
    out = pl.pallas_call(
        _ada_kernel,
        grid=(depth, n6 // tn),
        in_specs=[pl.BlockSpec((bp, d), lambda l, j: (0, 0)),
                  pl.BlockSpec((None, d, tn), lambda l, j: (l, 0, j)),
                  pl.BlockSpec((None, 1, tn), lambda l, j: (l, 0, j))],
        out_specs=pl.BlockSpec((None, bp, tn), lambda l, j: (l, 0, j)),
        out_shape=jax.ShapeDtypeStruct((depth, bp, n6), F32),
        compiler_params=_params("parallel", "parallel"),
        name="ada_mod",
    )(cp, ada_w, ada_b.reshape(depth, 1, n6))
    return out[:, :b]


_U_COLS = (("q", 256), ("v", 256), ("o", 256), ("g", 128), ("cq", 256), ("ckv", 128),
           ("kr", 128), ("hy", 768), ("sc", 768))
_U_TOTAL = sum(w for _, w in _U_COLS)
_UT_ROWS = 256 + 16


def _inproj_kernel(x_ref, gain_ref, sc_ref, sh_ref, w_ref, wt_ref,
                   q_ref, v_ref, o_ref, g_ref, cq_ref, ckv_ref, kr_ref, hy_ref, scu_ref,
                   kt_ref, gt_ref):
    x = x_ref[...]
    d = x.shape[-1]
    h = _rms(x, d) * gain_ref[...] * (1.0 + sc_ref[...]) + sh_ref[...]
    hb = h.astype(BF16)
    u = _dot(hb, w_ref[...])
    off = 0
    for ref, (_, width) in zip((q_ref, v_ref, o_ref, g_ref, cq_ref, ckv_ref, kr_ref, hy_ref, scu_ref),
                               _U_COLS):
        ref[...] = u[:, off:off + width]
        off += width
    ut = _dot_nt(wt_ref[...], hb)
    kt_ref[...] = ut[:256]
    gt_ref[...] = ut[256:]


def in_proj(x, gain, scale, shift, w_in):
    b, s, d = x.shape
    tm = min(512, s)
    cuts = np.cumsum([0, 256, 256, 256, 256, 16, MLA_Q_RANK, MLA_KV_RANK, MLA_ROPE, 768, 768])
    wq, wk, wv, wo, wg, wcq, wckv, wkr, why, wsc = (w_in[:, cuts[i]:cuts[i + 1]] for i in range(10))
    pad = lambda w, n: jnp.pad(w, ((0, 0), (0, n - w.shape[1])))
    wkr_p = jnp.pad(wkr, ((0, 0), (MLA_NOPE, LANES - MLA_NOPE - MLA_ROPE)))
    w1 = jnp.concatenate([wq, wv, wo, pad(wg, 128), pad(wcq, 256), wckv, wkr_p, why, wsc],
                         axis=1).astype(BF16)
    w2t = jnp.concatenate([wk, wg], axis=1).T.astype(BF16)
    row = lambda nm, w: pl.BlockSpec((None, tm, w), lambda i, j: (i, j, 0))
    out_shapes = [jax.ShapeDtypeStruct((b, s, w), F32) for _, w in _U_COLS]
    out_shapes += [jax.ShapeDtypeStruct((b, 256, s), F32), jax.ShapeDtypeStruct((b, 16, s), F32)]
    out_specs = [row(nm, w) for nm, w in _U_COLS]
    out_specs += [pl.BlockSpec((None, 256, tm), lambda i, j: (i, 0, j)),
                  pl.BlockSpec((None, 16, tm), lambda i, j: (i, 0, j))]
    vec = lambda: pl.BlockSpec((None, 1, d), lambda i, j: (i, 0, 0))
    outs = pl.pallas_call(
        _inproj_kernel,
        grid=(b, s // tm),
        in_specs=[pl.BlockSpec((None, tm, d), lambda i, j: (i, j, 0)),
                  pl.BlockSpec((1, d), lambda i, j: (0, 0)),
                  vec(), vec(),
                  pl.BlockSpec((d, _U_TOTAL), lambda i, j: (0, 0)),
                  pl.BlockSpec((_UT_ROWS, d), lambda i, j: (0, 0))],
        out_specs=out_specs,
        out_shape=out_shapes,
        compiler_params=_params("parallel", "parallel"),
        name="in_proj",
    )(x, gain.reshape(1, d), scale.reshape(b, 1, d), shift.reshape(b, 1, d), w1, w2t)
    names = [nm for nm, _ in _U_COLS] + ["kT", "gT"]
    return dict(zip(names, outs))


def _mlstm_dir(q, v, kt, gt, gc, c_ref, m_ref, base, rev):
    L = q.shape[0]
    r = lax.broadcasted_iota(jnp.int32, (L, L), 0)
    c = lax.broadcasted_iota(jnp.int32, (L, L), 1)
    tri = (c >= r) if rev else (c <= r)
    io, fo = (8, 12) if rev else (0, 4)
    logf_rows = _log_sigmoid(gt[fo:fo + 4, :])
    logf_cols = _log_sigmoid(gc)
    tri_b = jnp.where(tri, 1.0, 0.0).astype(BF16)
    a_cols = _dot(jnp.concatenate([tri_b] * 3, axis=1),
                  jnp.concatenate(_split3(logf_cols), axis=0))
    tri_t = (r >= c) if rev else (r <= c)
    tri_tb = jnp.where(tri_t, 1.0, 0.0).astype(BF16)
    a_rows = _dot(jnp.concatenate(_split3(_log_sigmoid(gt)), axis=1),
                  jnp.concatenate([tri_tb] * 3, axis=0))[fo:fo + 4, :]
    lane = lax.broadcasted_iota(jnp.int32, (L, LANES), 1)
    row128 = lax.broadcasted_iota(jnp.int32, (LANES, L), 0)
    scale = HEAD_DIM ** -0.5
    outs = []
    for pair in range(2):
        qp = q[:, pair * LANES:(pair + 1) * LANES]
        vp = v[:, pair * LANES:(pair + 1) * LANES]
        ktp = kt[pair * LANES:(pair + 1) * LANES, :] * scale
        pair_out = None
        for sub in range(2):
            h = pair * 2 + sub
            in_head = (row128 >= sub * HEAD_DIM) & (row128 < (sub + 1) * HEAD_DIM)
            kth = jnp.where(in_head, ktp, 0.0)
            vsh = vp if sub == 0 else pltpu.roll(vp, HEAD_DIM, 1)
            v_aug = jnp.where(lane < HEAD_DIM, vsh, jnp.where(lane == HEAD_DIM, 1.0, 0.0))
            v_aug_b = v_aug.astype(BF16)
            a_c = a_cols[:, fo + h:fo + h + 1]
            a_r = a_rows[h:h + 1, :]
            ig_r = gt[io + h:io + h + 1, :]
            a_end = jnp.sum(logf_rows[h:h + 1, :], axis=-1, keepdims=True)
            c_st = c_ref[base + h]
            m_st = m_ref[base + h][:, 0:1]
            d_log = jnp.where(tri, a_c - a_r + ig_r, -jnp.inf)
            inter = a_c + m_st
            m_t = jnp.maximum(inter, jnp.max(d_log, axis=-1, keepdims=True))
            qb = qp.astype(BF16)
            p = jnp.exp(d_log - m_t) * _dot(qb, kth.astype(BF16))
            sci = jnp.exp(inter - m_t)
            nd = _dot(p.astype(BF16), v_aug_b) + sci * _dot(qb, c_st.astype(BF16))
            den = nd[:, HEAD_DIM:HEAD_DIM + 1]
            out = nd / jnp.maximum(jnp.abs(den), jnp.exp(-m_t))
            w_st = a_end - a_r + ig_r
            m_loc = jnp.max(w_st, axis=-1, keepdims=True)
            ke = kth * jnp.exp(w_st - m_loc)
            c_loc = _dot(ke.astype(BF16), v_aug_b)
            m_new = jnp.maximum(a_end + m_st, m_loc)
            sp = jnp.exp(a_end + m_st - m_new)
            sl = jnp.exp(m_loc - m_new)
            c_ref[base + h] = sp * c_st + sl * c_loc
            m_ref[base + h] = jnp.broadcast_to(m_new, (1, LANES))
            if sub == 0:
                pair_out = out
            else:
                pair_out = jnp.where(lane < HEAD_DIM, pair_out, pltpu.roll(out, HEAD_DIM, 1))
        outs.append(pair_out)
    return jnp.concatenate(outs, axis=1)


def _mlstm_kernel(qf_ref, vf_ref, ktf_ref, gtf_ref, gcf_ref,
                  qb_ref, vb_ref, ktb_ref, gtb_ref, gcb_ref, brow_ref, bcol_ref,
                  hf_ref, hb_ref, c_ref, m_ref, *, n_sub):
    @pl.when(pl.program_id(1) == 0)
    def _():
        c_ref[...] = jnp.zeros_like(c_ref)
        m_ref[...] = jnp.zeros_like(m_ref)

    bcol = bcol_ref[...]
    brow = brow_ref[...]
    L = ML_CHUNK
    for ci in range(n_sub):
        rf = slice(ci * L, (ci + 1) * L)
        hf_ref[rf, :] = _mlstm_dir(qf_ref[rf, :], vf_ref[rf, :], ktf_ref[:, rf], gtf_ref[:, rf] + bcol,
                                   gcf_ref[rf, :] + brow, c_ref, m_ref, 0, False)
        rb = slice((n_sub - 1 - ci) * L, (n_sub - ci) * L)
        hb_ref[rb, :] = _mlstm_dir(qb_ref[rb, :], vb_ref[rb, :], ktb_ref[:, rb], gtb_ref[:, rb] + bcol,
                                   gcb_ref[rb, :] + brow, c_ref, m_ref, N_HEADS, True)


def mlstm(q, v, kt, gt, g, gate_b):
    b, s, w = q.shape
    n_sub = 2
    L = n_sub * ML_CHUNK
    nc = s // L
    bflat = gate_b.reshape(16)
    brow = jnp.zeros((1, LANES), F32).at[0, :16].set(bflat)
    bcol = bflat.reshape(16, 1)
    fw = lambda i, j: (i, j, 0)
    bw = lambda i, j: (i, nc - 1 - j, 0)
    fwt = lambda i, j: (i, 0, j)
    bwt = lambda i, j: (i, 0, nc - 1 - j)

    def specs(m, mt):
        return [pl.BlockSpec((None, L, w), m), pl.BlockSpec((None, L, w), m),
                pl.BlockSpec((None, w, L), mt), pl.BlockSpec((None, 16, L), mt),
                pl.BlockSpec((None, L, LANES), m)]

    return pl.pallas_call(
        functools.partial(_mlstm_kernel, n_sub=n_sub),
        grid=(b, nc),
        in_specs=specs(fw, fwt) + specs(bw, bwt) + [
            pl.BlockSpec((1, LANES), lambda i, j: (0, 0)),
            pl.BlockSpec((16, 1), lambda i, j: (0, 0))],
        out_specs=[pl.BlockSpec((None, L, w), fw), pl.BlockSpec((None, L, w), bw)],
        out_shape=[jax.ShapeDtypeStruct((b, s, w), F32)] * 2,
        scratch_shapes=[pltpu.VMEM((2 * N_HEADS, LANES, LANES), F32),
                        pltpu.VMEM((2 * N_HEADS, 1, LANES), F32)],
        compiler_params=_params("parallel", "arbitrary"),
        name="mlstm",
    )(q, v, kt, gt, g, q, v, kt, gt, g, brow, bcol)


def _rope_table_kernel(pos_ref, post_ref, inv_ref, invt_ref, cos_ref, sin_ref, cost_ref, sint_ref):
    ang = pos_ref[...].astype(F32) * inv_ref[...]
    cos = jnp.cos(ang)
    sin = jnp.sin(ang)
    cos_ref[...] = cos
    sin_ref[...] = sin
    cost_ref[...] = cos.T
    sint_ref[...] = sin.T


def rope_tables(positions):
    b, s = positions.shape
    tm = min(512, s)
    inv = ROPE_THETA ** (-jnp.arange(ROPE_HALF, dtype=F32) / ROPE_HALF)
    inv_row = jnp.zeros((1, LANES), F32).at[0, MLA_NOPE:MLA_NOPE + ROPE_HALF].set(inv)
    inv_row = inv_row.at[0, MLA_NOPE + ROPE_HALF:MLA_NOPE + MLA_ROPE].set(inv)
    spec = pl.BlockSpec((None, tm, LANES), lambda i, j: (i, j, 0))
    spec_t = pl.BlockSpec((None, LANES, tm), lambda i, j: (i, 0, j))
    return pl.pallas_call(
        _rope_table_kernel,
        grid=(b, s // tm),
        in_specs=[pl.BlockSpec((None, tm, 1), lambda i, j: (i, j, 0)),
                  pl.BlockSpec((None, 1, tm), lambda i, j: (i, 0, j)),
                  pl.BlockSpec((1, LANES), lambda i, j: (0, 0)),
                  pl.BlockSpec((LANES, 1), lambda i, j: (0, 0))],
        out_specs=[spec, spec, spec_t, spec_t],
        out_shape=[jax.ShapeDtypeStruct((b, s, LANES), F32)] * 2
        + [jax.ShapeDtypeStruct((b, LANES, s), F32)] * 2,
        compiler_params=_params("parallel", "parallel"),
        name="rope_tables",
    )(positions.reshape(b, s, 1), positions.reshape(b, 1, s), inv_row, inv_row.reshape(LANES, 1))


def _mla_proj_kernel(cq_ref, ckv_ref, kr_ref, cos_ref, sin_ref, cost_ref, sint_ref, qg_ref, kvg_ref,
                     wqt_ref, wk_ref, wvt_ref, qt_ref, k_ref, vt_ref):
    cos = cos_ref[...]
    sin = sin_ref[...]
    lane = lax.broadcasted_iota(jnp.int32, cos.shape, 1)
    x1 = (lane >= MLA_NOPE) & (lane < MLA_NOPE + ROPE_HALF)
    x2 = (lane >= MLA_NOPE + ROPE_HALF) & (lane < MLA_NOPE + MLA_ROPE)
    kr = kr_ref[...]
    krr = (kr * cos + pltpu.roll(kr, LANES - ROPE_HALF, 1) * jnp.where(x1, -sin, 0.0)
           + pltpu.roll(kr, ROPE_HALF, 1) * jnp.where(x2, sin, 0.0))

    cqn = (_rms(cq_ref[...], MLA_Q_RANK) * qg_ref[...]).astype(BF16)
    ckvn = (_rms(ckv_ref[...], MLA_KV_RANK) * kvg_ref[...]).astype(BF16)
    kn = _dot(ckvn, wk_ref[...])
    for h in range(N_HEADS):
        sl = slice(h * LANES, (h + 1) * LANES)
        k_ref[:, sl] = (kn[:, sl] + krr).astype(BF16)
    vrow = lax.broadcasted_iota(jnp.int32, vt_ref.shape, 0)
    ones_row = jnp.where(vrow % V_ROWS == HEAD_DIM, 1.0, 0.0)
    vt_ref[...] = (_dot_nt(wvt_ref[...], ckvn) + ones_row).astype(BF16)

    cos_t = cost_ref[...]
    sin_t = sint_ref[...]
    row = lax.broadcasted_iota(jnp.int32, cos_t.shape, 0)
    x1t = (row >= MLA_NOPE) & (row < MLA_NOPE + ROPE_HALF)
    x2t = (row >= MLA_NOPE + ROPE_HALF) & (row < MLA_NOPE + MLA_ROPE)
    sin_at = jnp.where(x1t, -sin_t, 0.0)
    sin_bt = jnp.where(x2t, sin_t, 0.0)
    qt = _dot_nt(wqt_ref[...], cqn)
    scale = (MLA_NOPE + MLA_ROPE) ** -0.5 * math.log2(math.e)
    for h in range(N_HEADS):
        sl = slice(h * LANES, (h + 1) * LANES)
        xs = qt[sl, :]
        roped = (xs * cos_t + pltpu.roll(xs, LANES - ROPE_HALF, 0) * sin_at
                 + pltpu.roll(xs, ROPE_HALF, 0) * sin_bt)
        qt_ref[sl, :] = (roped * scale).astype(BF16)


def mla_proj(cq, ckv, kr, tables, q_norm, kv_norm, w_uq, w_uk, w_uv):
    b, s, _ = cq.shape
    tm = min(512, s)
    dqk = MLA_NOPE + MLA_ROPE
    cos, sin, cos_t, sin_t = tables
    wq = w_uq.reshape(MLA_Q_RANK, N_HEADS, dqk)
    wq = jnp.pad(wq, ((0, 256 - MLA_Q_RANK), (0, 0), (0, LANES - dqk))).reshape(256, N_HEADS * LANES)
    wk = w_uk.reshape(MLA_KV_RANK, N_HEADS, MLA_NOPE)
    wk = jnp.pad(wk, ((0, 0), (0, 0), (0, LANES - MLA_NOPE))).reshape(MLA_KV_RANK, N_HEADS * LANES)
    qg = jnp.pad(q_norm, (0, 256 - MLA_Q_RANK)).reshape(1, 256)
    kvg = kv_norm.reshape(1, MLA_KV_RANK)
    row = lambda w: pl.BlockSpec((None, tm, w), lambda i, j: (i, j, 0))
    col = lambda w: pl.BlockSpec((None, w, tm), lambda i, j: (i, 0, j))
    full = lambda a: pl.BlockSpec(a.shape, lambda i, j: (0,) * a.ndim)
    wv = jnp.pad(w_uv.reshape(MLA_KV_RANK, N_HEADS, HEAD_DIM), ((0, 0), (0, 0), (0, V_ROWS - HEAD_DIM)))
    wqt, wkb = wq.T.astype(BF16), wk.astype(BF16)
    wvt = wv.reshape(MLA_KV_RANK, N_HEADS * V_ROWS).T.astype(BF16)
    return pl.pallas_call(
        _mla_proj_kernel,
        grid=(b, s // tm),
        in_specs=[row(256), row(128), row(128), row(128), row(128), col(128), col(128),
                  full(qg), full(kvg), full(wqt), full(wkb), full(wvt)],
        out_specs=[col(512), row(512), col(N_HEADS * V_ROWS)],
        out_shape=[jax.ShapeDtypeStruct((b, 512, s), BF16), jax.ShapeDtypeStruct((b, s, 512), BF16),
                   jax.ShapeDtypeStruct((b, N_HEADS * V_ROWS, s), BF16)],
        compiler_params=_params("parallel", "parallel"),
        name="mla_proj",
    )(cq, ckv, kr, cos, sin, cos_t, sin_t, qg, kvg, wqt, wkb, wvt)


def _flash_kernel(qt_ref, k_ref, vt_ref, o_ref, sc0_ref, sc1_ref, p0_ref, p1_ref, *, tk, nh):
    s = k_ref.shape[0]
    tq = qt_ref.shape[1]
    nk = s // tk
    sc_bufs = (sc0_ref, sc1_ref)
    p_bufs = (p0_ref, p1_ref)

    def scores(t, par):
        off = pl.multiple_of(t * tk, tk)
        for sub in range(nh):
            sc_bufs[par][sub] = _dot(k_ref[pl.ds(off, tk), sub * LANES:(sub + 1) * LANES],
                                     qt_ref[sub * LANES:(sub + 1) * LANES, :])

    def soften(par, ms):
        m_out, alphas = [], []
        for sub in range(nh):
            sc = sc_bufs[par][sub]
            m_new = jnp.maximum(ms[sub], jnp.max(sc, axis=0, keepdims=True))
            alphas.append(jnp.exp2(ms[sub] - m_new))
            p_bufs[par][sub] = jnp.exp2((sc - m_new).astype(BF16))
            m_out.append(m_new)
        return m_out, alphas

    def accumulate(t, par, alphas, accs):
        off = pl.multiple_of(t * tk, tk)
        new = []
        for sub in range(nh):
            vt = vt_ref[sub * V_ROWS:(sub + 1) * V_ROWS, pl.ds(off, tk)]
            new.append(alphas[sub] * accs[sub] + _dot(vt, p_bufs[par][sub]))
        return new

    def step(t, par, ms, alphas, accs, with_scores=True):
        if with_scores:
            scores(t + 2, par)
        ms, alphas_next = soften(1 - par, ms)
        accs = accumulate(t, par, alphas, accs)
        return ms, alphas_next, accs

    def body(u, carry):
        ms, alphas, accs = carry
        ms, alphas, accs = step(2 * u, 0, ms, alphas, accs)
        ms, alphas, accs = step(2 * u + 1, 1, ms, alphas, accs)
        return ms, alphas, accs

    scores(0, 0)
    scores(1, 1)
    ms, alphas = soften(0, [jnp.full((1, tq), -jnp.inf, F32) for _ in range(nh)])
    accs = [jnp.zeros((V_ROWS, tq), F32) for _ in range(nh)]
    ms, alphas, accs = lax.fori_loop(0, nk // 2 - 1, body, (ms, alphas, accs))
    ms, alphas, accs = step(nk - 2, 0, ms, alphas, accs, with_scores=False)
    accs = accumulate(nk - 1, 1, alphas, accs)
    out_t = jnp.concatenate([acc[:HEAD_DIM] / acc[HEAD_DIM:HEAD_DIM + 1] for acc in accs],
                            axis=0)
    o_ref[...] = out_t.T


def flash_attention(qt, k, vt):
    b, _, s = qt.shape
    tq = min(256, s)
    tk = min(512, s // 2)
    nh = FLASH_HEADS
    ng = N_HEADS // nh
    return pl.pallas_call(
        functools.partial(_flash_kernel, tk=tk, nh=nh),
        grid=(b, ng, s // tq),
        in_specs=[pl.BlockSpec((None, nh * LANES, tq), lambda i, p, j: (i, p, j)),
                  pl.BlockSpec((None, s, nh * LANES), lambda i, p, j: (i, 0, p)),
                  pl.BlockSpec((None, nh * V_ROWS, s), lambda i, p, j: (i, p, 0))],
        out_specs=pl.BlockSpec((None, tq, nh * HEAD_DIM), lambda i, p, j: (i, j, p)),
        out_shape=jax.ShapeDtypeStruct((b, s, N_HEADS * HEAD_DIM), F32),
        scratch_shapes=[pltpu.VMEM((nh, tk, tq), F32)] * 2 + [pltpu.VMEM((nh, tk, tq), BF16)] * 2,
        compiler_params=_params("parallel", "parallel", "parallel"),
        name="flash_attention",
    )(qt, k, vt)


def _shifted(x, prev_row, next_row, first, last):
    tm = x.shape[0]
    row = lax.broadcasted_iota(jnp.int32, x.shape, 0)
    prev_row = jnp.where(first, 0.0, prev_row)
    next_row = jnp.where(last, 0.0, next_row)
    xm1 = jnp.where(row == 0, prev_row, pltpu.roll(x, 1, 0))
    xp1 = jnp.where(row == tm - 1, next_row, pltpu.roll(x, tm - 1, 0))
    return xm1, xp1


def _conv_kernel(hy_ref, hyp_ref, hyn_ref, sc_ref, scp_ref, scn_ref, hw_ref, hb_ref, sw_ref,
                 x1_ref, x2_ref, z_ref, yd_ref):
    j = pl.program_id(1)
    first = j == 0
    last = j == pl.num_programs(1) - 1
    x = hy_ref[...]
    xm1, xp1 = _shifted(x, hyp_ref[7:8, :], hyn_ref[0:1, :], first, last)
    hw = hw_ref[...]
    proj = xm1 * hw[0:1] + x * hw[1:2] + xp1 * hw[2:3] + hb_ref[...]
    x1_ref[...] = proj[:, 0:GROUP_W]
    x2_ref[...] = proj[:, GROUP_W:2 * GROUP_W]
    z_ref[...] = proj[:, 2 * GROUP_W:]
    su = sc_ref[...]
    g = GROUP_W
    prod = su[:, g:2 * g] * su[:, 2 * g:]
    pprev = scp_ref[7:8, g:2 * g] * scp_ref[7:8, 2 * g:]
    pnext = scn_ref[0:1, g:2 * g] * scn_ref[0:1, 2 * g:]
    pm1, pp1 = _shifted(prod, pprev, pnext, first, last)
    sw = sw_ref[...]
    yd_ref[...] = su[:, :g] * (pm1 * sw[0:1] + prod * sw[1:2] + pp1 * sw[2:3])


def conv_mixers(hy_u, sc_u, hy_conv_w, hy_conv_b, sc_conv_w):
    b, s, w3 = hy_u.shape
    tm = min(512, s)
    nb8 = s // 8
    r8 = tm // 8
    main = pl.BlockSpec((None, tm, w3), lambda i, j: (i, j, 0))
    prev = pl.BlockSpec((None, 8, w3), lambda i, j: (i, jnp.maximum(j * r8 - 1, 0), 0))
    nxt = pl.BlockSpec((None, 8, w3), lambda i, j: (i, jnp.minimum((j + 1) * r8, nb8 - 1), 0))
    full = lambda a: pl.BlockSpec(a.shape, lambda i, j: (0,) * a.ndim)
    hw = hy_conv_w.T
    hb = hy_conv_b.reshape(1, w3)
    sw = sc_conv_w.T
    out = pl.BlockSpec((None, tm, GROUP_W), lambda i, j: (i, j, 0))
    return pl.pallas_call(
        _conv_kernel,
        grid=(b, s // tm),
        in_specs=[main, prev, nxt, main, prev, nxt, full(hw), full(hb), full(sw)],
        out_specs=[out] * 4,
        out_shape=[jax.ShapeDtypeStruct((b, s, GROUP_W), F32)] * 4,
        compiler_params=_params("parallel", "parallel"),
        name="conv_mixers",
    )(hy_u, hy_u, hy_u, sc_u, sc_u, sc_u, hw, hb, sw)


def _fft_dims(s):
    n = 2 * s
    lg = int(round(math.log2(n)))
    assert 1 << lg == n
    n1 = 1 << ((lg + 1) // 2)
    return n1, n // n1


def _filter_kernel(frow_ref, w1h_ref, w1l_ref, b1_ref, fr1_ref, w2h_ref, w2l_ref, b2_ref, fr2_ref,
                   w3h_ref, w3l_ref, b3_ref, ld_ref, k_ref, norm_ref, *, length):
    i = pl.program_id(0)
    tm = k_ref.shape[0]
    hm = tm // 2
    half = LANES // 2

    def times(first):
        n = first + lax.broadcasted_iota(jnp.int32, (hm, 1), 0)
        tt = jnp.where(n < length, n, 2 * length - 1 - n)
        return tt.astype(F32) / length

    ta, tb = times(i * tm), times(i * tm + hm)
    lane = lax.broadcasted_iota(jnp.int32, (hm, LANES), 1)
    sub = lane % half
    t2 = jnp.where(lane < half, ta, tb)
    ang = t2 * frow_ref[...] + jnp.where(sub > HY_BANDS, 0.5 * math.pi, 0.0)
    z = jnp.where(sub == 0, t2, jnp.where(sub <= 2 * HY_BANDS, jnp.sin(ang), 0.0))
    hid = jnp.sin(fr1_ref[...] * (_dot3(z, w1h_ref[...], w1l_ref[...]) + b1_ref[...]))
    hid = jnp.sin(fr2_ref[...] * (_dot3(hid, w2h_ref[...], w2l_ref[...]) + b2_ref[...]))
    decay = jnp.exp(ld_ref[...])
    total = None
    for part, t in enumerate((ta, tb)):
        filt = ((_dot3(hid, w3h_ref[part], w3l_ref[part]) + b3_ref[...]) * jnp.exp(-t * decay))
        k_ref[part * hm:(part + 1) * hm, :] = filt
        psum = jnp.sum(jnp.abs(filt), axis=0, keepdims=True)
        total = psum if total is None else total + psum

    @pl.when(i == 0)
    def _():
        norm_ref[...] = jnp.zeros_like(norm_ref)

    norm_ref[...] += total


def hyena_filter_taps(length, w1, b1, fr1, w2, b2, fr2, w3, b3, log_decay):
    tm = min(512, length)
    n_half = length // tm
    oc = 2 * GROUP_W
    half = LANES // 2
    assert HY_FFN == half
    bands = jnp.arange(1, HY_BANDS + 1, dtype=F32) * (2.0 * math.pi)
    fhalf = jnp.zeros((half,), F32).at[1:1 + HY_BANDS].set(bands).at[1 + HY_BANDS:1 + 2 * HY_BANDS].set(bands)
    frow = jnp.concatenate([fhalf, fhalf]).reshape(1, LANES)
    w1half = jnp.zeros((half, HY_FFN), F32).at[:1 + 2 * HY_BANDS].set(w1)
    blockdiag = lambda a: jnp.kron(jnp.eye(2, dtype=F32), a)
    w1p, w2p = blockdiag(w1half), blockdiag(w2)
    dup = lambda a: jnp.concatenate([a, a]).reshape(1, LANES)
    bydir = lambda a: jnp.moveaxis(a.reshape(a.shape[0], 2, 2, GROUP_W), 2, 0).reshape(2, a.shape[0], oc)
    w3d, b3d, ldd = bydir(w3), bydir(b3.reshape(1, -1)), bydir(log_decay.reshape(1, -1))
    zeros = jnp.zeros_like(w3d)
    w3p = jnp.stack([jnp.concatenate([w3d, zeros], axis=1), jnp.concatenate([zeros, w3d], axis=1)],
                    axis=1)
    full = lambda a: pl.BlockSpec(a.shape, lambda i: (0,) * a.ndim)
    dirspec = lambda a: pl.BlockSpec((None,) + a.shape[1:], lambda i: (i // n_half,) + (0,) * (a.ndim - 1))
    w1h, w1l = _split2(w1p)
    w2h, w2l = _split2(w2p)
    w3h, w3l = _split2(w3p)
    return pl.pallas_call(
        functools.partial(_filter_kernel, length=length),
        grid=(2 * n_half,),
        in_specs=[full(frow), full(w1h), full(w1l), full(dup(b1)), full(dup(fr1)), full(w2h), full(w2l),
                  full(dup(b2)), full(dup(fr2)), dirspec(w3h), dirspec(w3l), dirspec(b3d), dirspec(ldd)],
        out_specs=[pl.BlockSpec((tm, oc), lambda i: (i, 0)), pl.BlockSpec((1, oc), lambda i: (0, 0))],
        out_shape=[jax.ShapeDtypeStruct((2 * length, oc), F32), jax.ShapeDtypeStruct((1, oc), F32)],
        compiler_params=_params("arbitrary"),
        name="hyena_filter",
    )(frow, w1h, w1l, dup(b1), dup(fr1), w2h, w2l, dup(b2), dup(fr2), w3h, w3l, b3d, ldd)


def _dft_consts(s):
    n1, n2 = _fft_dims(s)
    n = n1 * n2
    kh = n1 // 2 + 1
    kp = -(-kh // 4) * 4
    pad_rows = lambda a: np.concatenate([a, np.zeros((kp - kh, a.shape[1]))], axis=0)
    a1 = 2.0 * np.pi * np.outer(np.arange(kh), np.arange(n1)) / n1
    f1 = np.concatenate([pad_rows(np.cos(a1)), pad_rows(-np.sin(a1))], axis=0)
    a2 = 2.0 * np.pi * np.outer(np.arange(n2), np.arange(n2)) / n2
    c2, s2 = np.cos(a2), np.sin(a2)
    f2_fwd = np.block([[c2, s2], [-s2, c2]])
    f2_inv = np.block([[c2, -s2], [s2, c2]])
    at = 2.0 * np.pi * np.outer(np.arange(kp), np.arange(n2)) / n
    tw = np.stack([np.cos(at), np.sin(at)], axis=0)[..., None]
    m1 = np.arange(n1 // 2)
    a3 = 2.0 * np.pi * np.outer(m1, np.arange(kh)) / n1
    wgt = np.where((np.arange(kh) == 0) | (np.arange(kh) == n1 // 2), 1.0, 2.0) / n
    f3 = np.concatenate([pad_rows((np.cos(a3) * wgt).T).T, pad_rows((-np.sin(a3) * wgt).T).T],
                        axis=1)
    f = lambda a: jnp.asarray(a, F32)
    x3 = lambda a: _lhs_x3(f(a))
    return dict(n1=n1, n2=n2, kh=kh, kp=kp, f1=x3(f1), f1_half=x3(f1[:, :n1 // 2]), f2_fwd=x3(f2_fwd),
                f2_inv=x3(f2_inv), tw=f(tw), f3=x3(f3))


_N2_TILE = 8


def _dft1_kernel(x_ref, f_ref, s_ref, o_ref, xs_ref):
    f = f_ref[...]
    scale = s_ref[...]
    for m in range(_N2_TILE):
        xs_ref[...] = x_ref[:, m, :]
        o_ref[:, m, :] = _dot_x3(f, xs_ref[...] * scale)


def dft_stage1(x, f1, scale_row):
    g, r, n2, ch = x.shape
    m = f1.shape[0]
    return pl.pallas_call(
        _dft1_kernel,
        grid=(g, n2 // _N2_TILE),
        in_specs=[pl.BlockSpec((None, r, _N2_TILE, ch), lambda i, j: (i, 0, j, 0)),
                  pl.BlockSpec(f1.shape, lambda i, j: (0, 0)),
                  pl.BlockSpec((1, ch), lambda i, j: (0, 0))],
        out_specs=pl.BlockSpec((None, m, _N2_TILE, ch), lambda i, j: (i, 0, j, 0)),
        out_shape=jax.ShapeDtypeStruct((g, m, n2, ch), F32),
        scratch_shapes=[pltpu.VMEM((r, ch), F32)],
        compiler_params=_params("parallel", "parallel"),
        name="dft_stage1",
    )(x, f1, scale_row)


def _twiddle(re, im, tc, ts, conj):
    if conj:
        return re * tc - im * ts, im * tc + re * ts
    return re * tc + im * ts, im * tc - re * ts


_K1_STEP = 4


def _dft2_filter_kernel(a_ref, tw_ref, ff_ref, o_ref):
    n2 = a_ref.shape[2]
    for i in range(_K1_STEP):
        br, bi = _twiddle(a_ref[0, i], a_ref[1, i], tw_ref[0, i], tw_ref[1, i], False)
        zz = _dot_x3(ff_ref[...], jnp.concatenate([br, bi], axis=0))
        o_ref[0, i] = zz[:n2]
        o_ref[1, i] = zz[n2:]


def dft_stage2_filter(a, consts):
    _, kp, n2, ch = a.shape
    return pl.pallas_call(
        _dft2_filter_kernel,
        grid=(kp // _K1_STEP,),
        input_output_aliases={0: 0},
        in_specs=[pl.BlockSpec((2, _K1_STEP, n2, ch), lambda k: (0, k, 0, 0)),
                  pl.BlockSpec((2, _K1_STEP, n2, 1), lambda k: (0, k, 0, 0)),
                  pl.BlockSpec(consts["f2_fwd"].shape, lambda k: (0, 0))],
        out_specs=pl.BlockSpec((2, _K1_STEP, n2, ch), lambda k: (0, k, 0, 0)),
        out_shape=jax.ShapeDtypeStruct(a.shape, F32),
        compiler_params=_params("parallel"),
        name="dft_stage2_filter",
    )(a, consts["tw"], consts["f2_fwd"])


def _dft2_conv_kernel(a_ref, tw_ref, kf_ref, ff_ref, fi_ref, o_ref):
    n2 = a_ref.shape[2]
    for i in range(_K1_STEP):
        tc, ts = tw_ref[0, i], tw_ref[1, i]
        br, bi = _twiddle(a_ref[0, i], a_ref[1, i], tc, ts, False)
        zz = _dot_x3(ff_ref[...], jnp.concatenate([br, bi], axis=0))
        zr, zi = zz[:n2], zz[n2:]
        kr, ki = kf_ref[0, i], kf_ref[1, i]
        pr = zr * kr - zi * ki
        pi = zr * ki + zi * kr
        vv = _dot_x3(fi_ref[...], jnp.concatenate([pr, pi], axis=0))
        vr, vi = _twiddle(vv[:n2], vv[n2:], tc, ts, True)
        o_ref[0, i] = vr
        o_ref[1, i] = vi


def dft_stage2_conv(a, kf, order, consts):
    b, _, kp, n2, ch = a.shape
    blk = pl.BlockSpec((None, 2, _K1_STEP, n2, ch), lambda i, k: (i, 0, k, 0, 0))
    mat = pl.BlockSpec(consts["f2_fwd"].shape, lambda i, k: (0, 0))
    return pl.pallas_call(
        _dft2_conv_kernel,
        grid=(b, kp // _K1_STEP),
        input_output_aliases={0: 0},
        in_specs=[blk,
                  pl.BlockSpec((2, _K1_STEP, n2, 1), lambda i, k: (0, k, 0, 0)),
                  pl.BlockSpec((2, _K1_STEP, n2, ch), lambda i, k: (0, k, 0, order)),
                  mat, mat],
        out_specs=blk,
        out_shape=jax.ShapeDtypeStruct(a.shape, F32),
        compiler_params=_params("parallel", "parallel"),
        name="dft_stage2_conv",
    )(a, consts["tw"], kf, consts["f2_fwd"], consts["f2_inv"])


def _dft3_kernel(v_ref, f_ref, gate_ref, z_ref, bias_ref, o_ref, vs_ref, ys_ref):
    f = f_ref[...]
    for m in range(_N2_TILE):
        vs_ref[...] = v_ref[:, m, :]
        ys_ref[:, m, :] = _dot_x3(f, vs_ref[...])
    o_ref[...] = gate_ref[...] * (ys_ref[...] + z_ref[...] * bias_ref[...])


def dft_stage3_gate(v, f3, gate, z, bias_row):
    b, m, n2, ch = v.shape
    r = f3.shape[0]
    row = pl.BlockSpec((None, r, _N2_TILE, ch), lambda i, j: (i, 0, j, 0))
    return pl.pallas_call(
        _dft3_kernel,
        grid=(b, n2 // _N2_TILE),
        in_specs=[pl.BlockSpec((None, m, _N2_TILE, ch), lambda i, j: (i, 0, j, 0)),
                  pl.BlockSpec(f3.shape, lambda i, j: (0, 0)),
                  row, row, pl.BlockSpec((1, ch), lambda i, j: (0, 0))],
        out_specs=row,
        out_shape=jax.ShapeDtypeStruct((b, r, n2, ch), F32),
        scratch_shapes=[pltpu.VMEM((m, ch), F32), pltpu.VMEM((r, _N2_TILE, ch), F32)],
        compiler_params=_params("parallel", "parallel"),
        name="dft_stage3_gate",
    )(v, f3, gate, z, bias_row)


def hyena_mixer(x1, x2, z, hy_w1, hy_b1, hy_fr1, hy_w2, hy_b2, hy_fr2, hy_w3, hy_b3, hy_log_decay,
                hy_bias):
    b, s, ch = z.shape
    consts = _dft_consts(s)
    n1, n2 = consts["n1"], consts["n2"]
    oc = 2 * ch
    taps, norm = hyena_filter_taps(s, hy_w1, hy_b1, hy_fr1, hy_w2, hy_b2, hy_fr2, hy_w3, hy_b3,
                                   hy_log_decay)
    kp = consts["kp"]
    ka = dft_stage1(taps.reshape(1, n1, n2, oc), consts["f1"], 1.0 / norm)
    kf = dft_stage2_filter(ka.reshape(2, kp, n2, oc), consts)
    half = n1 // 2
    view = lambda t: t.reshape(b, half, n2, ch)
    ones = jnp.ones((1, ch), F32)
    f1_half = consts["f1_half"]
    cur = view(z)
    for order, gate in enumerate((x1, x2)):
        a = dft_stage1(cur, f1_half, ones)
        v = dft_stage2_conv(a.reshape(b, 2, kp, n2, ch), kf, order, consts)
        cur = dft_stage3_gate(v.reshape(b, 2 * kp, n2, ch), consts["f3"], view(gate), cur,
                              hy_bias[order].reshape(1, ch))
    return cur.reshape(b, s, ch)


def _outproj_kernel(x_ref, hf_ref, hb_ref, o_ref, yb_ref, yc_ref, yd_ref, og_ref, hm_ref, wout_ref,
                    pg_ref, g1_ref, fg_ref, sc2_ref, sh2_ref, rwt_ref,
                    xn_ref, h2_ref, afft_ref):
    hm = hm_ref[...]
    y_a = _sigmoid(o_ref[...]) * (hf_ref[...] + hb_ref[...])
    acc = None
    for idx, y in enumerate((y_a, yb_ref[...], yc_ref[...], yd_ref[...])):
        ms = _dot(jnp.concatenate(_split2(y * y), axis=1), hm)
        yn = y * lax.rsqrt(ms + NORM_EPS) * og_ref[:, idx * GROUP_W:(idx + 1) * GROUP_W]
        part = _dot(yn.astype(BF16), wout_ref[idx * GROUP_W:(idx + 1) * GROUP_W, :])
        acc = part if acc is None else acc + part
    d = acc.shape[-1]
    xn = x_ref[...] + g1_ref[...] * (_rms(acc, d) * pg_ref[...])
    xn_ref[...] = xn
    h2 = _rms(xn, d) * fg_ref[...] * (1.0 + sc2_ref[...]) + sh2_ref[...]
    h2_ref[...] = h2.T.astype(BF16)
    logits_t = _dot_nt_hi(rwt_ref[...], h2)
    mx = jnp.max(logits_t, axis=0, keepdims=True)
    ex = jnp.exp(logits_t - mx)
    afft_ref[...] = ex / jnp.sum(ex, axis=0, keepdims=True)


def out_proj(x, hf, hb, o, yb, yc, yd, out_g, w_out, post_g, g1, ffn_g, sc2, sh2, router_w):
    b, s, d = x.shape
    tm = min(512, s)
    e = router_w.shape[1]
    hm1 = np.kron(np.eye(GROUP_W // HEAD_DIM), np.ones((HEAD_DIM, HEAD_DIM))) / HEAD_DIM
    hm = jnp.asarray(np.concatenate([hm1, hm1], axis=0), BF16)
    row = lambda w: pl.BlockSpec((None, tm, w), lambda i, j: (i, j, 0))
    full = lambda a: pl.BlockSpec(a.shape, lambda i, j: (0,) * a.ndim)
    vec = lambda: pl.BlockSpec((None, 1, d), lambda i, j: (i, 0, 0))
    r1 = lambda a: a.reshape(1, -1)
    wob = w_out.astype(BF16)
    rwt = router_w.T
    return pl.pallas_call(
        _outproj_kernel,
        grid=(b, s // tm),
        in_specs=[row(d)] + [row(GROUP_W)] * 6 + [full(r1(out_g)), full(hm), full(wob),
                                                  full(r1(post_g)), vec(), full(r1(ffn_g)), vec(), vec(),
                                                  full(rwt)],
        out_specs=[row(d), pl.BlockSpec((None, d, tm), lambda i, j: (i, 0, j)),
                   pl.BlockSpec((None, e, tm), lambda i, j: (i, 0, j))],
        out_shape=[jax.ShapeDtypeStruct((b, s, d), F32), jax.ShapeDtypeStruct((b, d, s), BF16),
                   jax.ShapeDtypeStruct((b, e, s), F32)],
        compiler_params=_params("parallel", "parallel"),
        name="out_proj",
    )(x, hf, hb, o, yb, yc, yd, r1(out_g), hm, wob, r1(post_g), g1.reshape(b, 1, d), r1(ffn_g),
      sc2.reshape(b, 1, d), sh2.reshape(b, 1, d), rwt)


def _lane_cumsum(x):
    n = x.shape[-1]
    lane = lax.broadcasted_iota(jnp.int32, x.shape, x.ndim - 1)
    shift = 1
    while shift < n:
        x = x + jnp.where(lane >= shift, pltpu.roll(x, shift, x.ndim - 1), 0.0)
        shift *= 2
    return x


def _select_kernel(aff_ref, u_ref, ps_ref, cb_ref, *, cap):
    aff = aff_ref[...]
    capf = float(cap)

    def body(i, bits):
        cand = bits | (jnp.int32(1) << (30 - i))
        cnt = jnp.sum(jnp.where(aff >= pltpu.bitcast(cand, F32), 1.0, 0.0), axis=-1, keepdims=True)
        return jnp.where(cnt >= capf, cand, bits)

    bits = lax.fori_loop(0, 31, body, jnp.zeros((aff.shape[0], 1), jnp.int32))
    thr = pltpu.bitcast(bits, F32)
    gt = aff > thr
    eq = aff == thr
    n_gt = jnp.sum(jnp.where(gt, 1.0, 0.0), axis=-1, keepdims=True)
    eqf = jnp.where(eq, 1.0, 0.0)
    rank_eq = _lane_cumsum(eqf) - eqf
    sel = gt | (eq & (rank_eq < capf - n_gt))
    self_ = jnp.where(sel, 1.0, 0.0)
    pos = _lane_cumsum(self_) - self_
    ps_ref[...] = jnp.where(sel, pos, -1.0).astype(jnp.int32)
    cb_ref[...] = _dot(self_.astype(BF16), u_ref[...]).astype(jnp.int32)


def ec_select(aff_t, cap):
    b, e, s = aff_t.shape
    nt = s // TOKEN_TILE
    assert nt + 1 <= LANES
    tok = np.arange(s)[:, None]
    u = jnp.asarray(tok < (np.arange(LANES)[None, :] * TOKEN_TILE), BF16)
    return pl.pallas_call(
        functools.partial(_select_kernel, cap=cap),
        grid=(b,),
        in_specs=[pl.BlockSpec((None, e, s), lambda i: (i, 0, 0)),
                  pl.BlockSpec((s, LANES), lambda i: (0, 0))],
        out_specs=[pl.BlockSpec((None, e, s), lambda i: (i, 0, 0)),
                   pl.BlockSpec((None, e, LANES), lambda i: (i, 0, 0))],
        out_shape=[jax.ShapeDtypeStruct((b, e, s), jnp.int32),
                   jax.ShapeDtypeStruct((b, e, LANES), jnp.int32)],
        compiler_params=_params("parallel"),
        name="ec_select",
    )(aff_t, u)


def _gather_kernel(cb_ref, ps_ref, ht_ref, o_ref, *, n_tiles, n_blocks, block, chunk):
    bi, ei = pl.program_id(0), pl.program_id(1)
    base = (bi * N_EXPERTS + ei) * LANES
    d, s = ht_ref.shape
    slot = lax.broadcasted_iota(jnp.int32, (block, chunk), 0)
    tok = lax.broadcasted_iota(jnp.int32, (1, chunk), 1)

    def one_block(j, carry):
        first = j * block

        def bounds(t, c):
            return (c[0] + (cb_ref[base + t + 1] <= first).astype(jnp.int32),
                    c[1] + (cb_ref[base + t] < first + block).astype(jnp.int32))

        t_lo, t_hi = lax.fori_loop(0, n_tiles, bounds, (jnp.int32(0), jnp.int32(0)))
        lo_tok = t_lo * TOKEN_TILE

        def part(i, acc):
            lower = lo_tok + i * chunk
            start = pl.multiple_of(jnp.minimum(lower, s - chunk), TOKEN_TILE)
            ps = jnp.where(tok + start >= lower, ps_ref[:, pl.ds(start, chunk)], -1)
            onehot = jnp.where(ps == slot + first, 1.0, 0.0).astype(BF16)
            return acc + _dot_nt(ht_ref[:, pl.ds(start, chunk)], onehot)

        n_parts = ((t_hi - t_lo) * TOKEN_TILE + chunk - 1) // chunk
        acc = lax.fori_loop(0, n_parts, part, jnp.zeros((d, block), F32))
        o_ref[pl.ds(pl.multiple_of(first, block), block), :] = acc.T.astype(o_ref.dtype)
        return carry

    lax.fori_loop(0, n_blocks, one_block, 0)


def ec_gather(ht, ps, cb, cap):
    b, d, s = ht.shape
    e = ps.shape[1]
    nt = s // TOKEN_TILE
    block = min(2 * SLOT_BLOCK, cap)
    chunk = min(block * s // cap + 2 * TOKEN_TILE, s)
    grid_spec = pltpu.PrefetchScalarGridSpec(
        num_scalar_prefetch=1,
        grid=(b, e),
        in_specs=[pl.BlockSpec((None, None, 1, s), lambda i, j, cb: (i, j, 0, 0)),
                  pl.BlockSpec((None, d, s), lambda i, j, cb: (i, 0, 0))],
        out_specs=pl.BlockSpec((None, None, cap, d), lambda i, j, cb: (i, j, 0, 0)),
    )
    return pl.pallas_call(
        functools.partial(_gather_kernel, n_tiles=nt, n_blocks=cap // block, block=block, chunk=chunk),
        grid_spec=grid_spec,
        out_shape=jax.ShapeDtypeStruct((b, e, cap, d), BF16),
        compiler_params=_params("parallel", "arbitrary"),
        name="ec_gather",
    )(cb.reshape(-1), ps.reshape(b, e, 1, s), ht)


def _ffn_kernel(x_ref, wg_ref, wu_ref, wd_ref, o_ref, acc_ref, *, n_f):
    f = pl.program_id(1)

    @pl.when(f == 0)
    def _():
        acc_ref[...] = jnp.zeros_like(acc_ref)

    bsz, cap, d = x_ref.shape
    x = x_ref[...].reshape(bsz * cap, d)
    a = _dot(x, wg_ref[...].astype(BF16))
    up = _dot(x, wu_ref[...].astype(BF16))
    act = (a * _sigmoid(a) * up).astype(BF16)
    acc_ref[...] += _dot(act, wd_ref[...].astype(BF16))

    @pl.when(f == n_f - 1)
    def _():
        o_ref[...] = acc_ref[...].reshape(bsz, cap, d).astype(o_ref.dtype)


def expert_ffn(xe, w_gate, w_up, w_down, layer):
    b, e, cap, d = xe.shape
    ff = w_gate.shape[-1]
    tf = min(512, ff)
    n_f = ff // tf
    return pl.pallas_call(
        functools.partial(_ffn_kernel, n_f=n_f),
        grid=(e, n_f),
        in_specs=[pl.BlockSpec((b, None, cap, d), lambda j, f: (0, j, 0, 0)),
                  pl.BlockSpec((None, None, d, tf), lambda j, f: (layer, j, 0, f)),
                  pl.BlockSpec((None, None, d, tf), lambda j, f: (layer, j, 0, f)),
                  pl.BlockSpec((None, None, tf, d), lambda j, f: (layer, j, f, 0))],
        out_specs=pl.BlockSpec((b, None, cap, d), lambda j, f: (0, j, 0, 0)),
        out_shape=jax.ShapeDtypeStruct((b, e, cap, d), BF16),
        scratch_shapes=[pltpu.VMEM((b * cap, d), F32)],
        compiler_params=_params("parallel", "arbitrary"),
        name="expert_ffn",
    )(xe, w_gate, w_up, w_down)


def _scatter_kernel(cb_ref, ps_ref, aff_ref, ye_ref, x_ref, pg_ref, g2_ref, o_ref, acc_ref,
                    *, n_sub, window):
    bi, ti, ei = pl.program_id(0), pl.program_id(1), pl.program_id(2)

    @pl.when(ei == 0)
    def _():
        acc_ref[...] = jnp.zeros_like(acc_ref)

    tt = acc_ref.shape[0]
    base = (bi * N_EXPERTS + ei) * LANES + ti * n_sub
    lo = cb_ref[base]
    hi = cb_ref[base + n_sub]

    @pl.when(hi > lo)
    def _():
        ps = ps_ref[...]
        gate = aff_ref[...]
        slot = lax.broadcasted_iota(jnp.int32, (window, tt), 0)
        cap = ye_ref.shape[0]
        first = (lo // BF16_ROWS) * BF16_ROWS

        def body(j, carry):
            lower = first + j * window
            start = pl.multiple_of(jnp.minimum(lower, cap - window), BF16_ROWS)
            hit = jnp.where(ps >= lower, ps, -1) == slot + start
            onehot = jnp.where(hit, 1.0, 0.0).astype(BF16)
            gate_slot = jnp.sum(jnp.where(hit, gate, 0.0), axis=1, keepdims=True)
            ye = (ye_ref[pl.ds(start, window), :].astype(F32) * gate_slot).astype(BF16)
            acc_ref[...] += lax.dot_general(onehot, ye, (((0,), (0,)), ((), ())),
                                            preferred_element_type=F32)
            return carry

        lax.fori_loop(0, (hi - first + window - 1) // window, body, 0)

    @pl.when(ei == N_EXPERTS - 1)
    def _():
        y = acc_ref[...]
        o_ref[...] = x_ref[...] + g2_ref[...] * (_rms(y, y.shape[-1]) * pg_ref[...])


def ec_scatter(ye, ps, aff_t, cb, x, post_g, g2):
    b, e, cap, d = ye.shape
    s = x.shape[1]
    n_sub = min(4, s // TOKEN_TILE)
    tt = n_sub * TOKEN_TILE
    grid_spec = pltpu.PrefetchScalarGridSpec(
        num_scalar_prefetch=1,
        grid=(b, s // tt, e),
        in_specs=[pl.BlockSpec((None, None, 1, tt), lambda i, t, j, cb: (i, j, 0, t)),
                  pl.BlockSpec((None, None, 1, tt), lambda i, t, j, cb: (i, j, 0, t)),
                  pl.BlockSpec((None, None, cap, d), lambda i, t, j, cb: (i, j, 0, 0)),
                  pl.BlockSpec((None, tt, d), lambda i, t, j, cb: (i, t, 0)),
                  pl.BlockSpec((1, d), lambda i, t, j, cb: (0, 0)),
                  pl.BlockSpec((None, 1, d), lambda i, t, j, cb: (i, 0, 0))],
        out_specs=pl.BlockSpec((None, tt, d), lambda i, t, j, cb: (i, t, 0)),
        scratch_shapes=[pltpu.VMEM((tt, d), F32)],
    )
    return pl.pallas_call(
        functools.partial(_scatter_kernel, n_sub=n_sub, window=min(2 * SLOT_BLOCK, cap)),
        grid_spec=grid_spec,
        out_shape=jax.ShapeDtypeStruct((b, s, d), F32),
        compiler_params=_params("parallel", "parallel", "arbitrary"),
        name="ec_scatter",
    )(cb.reshape(-1), ps.reshape(b, e, 1, s), aff_t.reshape(b, e, 1, s), ye, x, post_g.reshape(1, d),
      g2.reshape(b, 1, d))


def kernel(x, c, positions, ada_w, ada_b, mix_pre_g, mix_post_g, ffn_pre_g, ffn_post_g, w_in, ml_gate_b, mla_q_norm, mla_kv_norm, mla_w_uq, mla_w_uk, mla_w_uv, hy_conv_w, hy_conv_b, hy_w1, hy_b1, hy_fr1, hy_w2, hy_b2, hy_fr2, hy_w3, hy_b3, hy_log_decay, hy_bias, sc_conv_w, mix_out_g, w_out, router_w, exp_w_gate, exp_w_up, exp_w_down):
    depth = ada_w.shape[0]
    b, s, d = x.shape
    cap = EC_CAPACITY * s // N_EXPERTS
    mod = ada_mod(c, ada_w, ada_b)
    tables = rope_tables(positions)
    for l in range(depth):
        sh1, sc1, g1, sh2, sc2, g2 = (mod[l, :, i * d:(i + 1) * d] for i in range(6))
        u = in_proj(x, mix_pre_g[l], sc1, sh1, w_in[l])
        hf, hb = mlstm(u["q"], u["v"], u["kT"], u["gT"], u["g"], ml_gate_b[l])
        qa, ka, va = mla_proj(u["cq"], u["ckv"], u["kr"], tables, mla_q_norm[l], mla_kv_norm[l],
                              mla_w_uq[l], mla_w_uk[l], mla_w_uv[l])
        y_b = flash_attention(qa, ka, va)
        x1, x2, z, y_d = conv_mixers(u["hy"], u["sc"], hy_conv_w[l], hy_conv_b[l], sc_conv_w[l])
        y_c = hyena_mixer(x1, x2, z, hy_w1[l], hy_b1[l], hy_fr1[l], hy_w2[l], hy_b2[l], hy_fr2[l],
                          hy_w3[l], hy_b3[l], hy_log_decay[l], hy_bias[l])
        xn, h2, aff_t = out_proj(x, hf, hb, u["o"], y_b, y_c, y_d, mix_out_g[l], w_out[l],
                                      mix_post_g[l], g1, ffn_pre_g[l], sc2, sh2, router_w[l])
        ps, cb = ec_select(aff_t, cap)
        xe = ec_gather(h2, ps, cb, cap)
        ye = expert_ffn(xe, exp_w_gate, exp_w_up, exp_w_down, l)
        x = ec_scatter(ye, ps, aff_t, cb, xn, ffn_post_g[l], g2)
    return x
```

```python
import functools
import math

import numpy as np
import jax
import jax.numpy as jnp
from jax import lax
from jax.experimental import pallas as pl
from jax.experimental.pallas import tpu as pltpu

F32 = jnp.float32
BF16 = jnp.bfloat16
HIGHEST = lax.Precision.HIGHEST

GROUP_W = 256
HEAD_DIM = 64
N_HEADS = 4
ML_CHUNK = 128
MLA_Q_RANK = 224
MLA_KV_RANK = 128
MLA_NOPE = 64
MLA_ROPE = 32
ROPE_HALF = MLA_ROPE // 2
ROPE_THETA = 10000.0
HY_BANDS = 8
HY_FFN = 64
N_EXPERTS = 16
EC_CAPACITY = 2
NORM_EPS = 1e-6
LANES = 128
BF16_ROWS = 16
V_ROWS = HEAD_DIM + BF16_ROWS
FLASH_HEADS = 2
SLOT_BLOCK = 128
TOKEN_TILE = 256
VMEM_LIMIT = 56 * 1024 * 1024


def _params(*sem):
    return pltpu.CompilerParams(dimension_semantics=sem, vmem_limit_bytes=VMEM_LIMIT)


def _dot(a, b):
    return jnp.dot(a, b, preferred_element_type=F32)


def _dot_hi(a, b):
    return jnp.dot(a, b, precision=HIGHEST, preferred_element_type=F32)


def _dot_nt(a, b):
    return lax.dot_general(a, b, (((1,), (1,)), ((), ())), preferred_element_type=F32)


def _dot_nt_hi(a, b):
    return lax.dot_general(a, b, (((1,), (1,)), ((), ())), precision=HIGHEST,
                           preferred_element_type=F32)


def _split2(x):
    hi = x.astype(BF16)
    lo = (x - hi.astype(F32)).astype(BF16)
    return hi, lo


def _split3(x):
    hi = x.astype(BF16)
    r = x - hi.astype(F32)
    mid = r.astype(BF16)
    lo = (r - mid.astype(F32)).astype(BF16)
    return hi, mid, lo


def _lhs_x3(f):
    hi, lo = _split2(f)
    return jnp.concatenate([hi, lo, hi], axis=1)


def _dot_x3(f3, x):
    hi, lo = _split2(x)
    return _dot(f3, jnp.concatenate([hi, hi, lo], axis=0))


def _dot3(a, w_hi, w_lo):
    a_hi, a_lo = _split2(a)
    return _dot(a_hi, w_hi) + _dot(a_hi, w_lo) + _dot(a_lo, w_hi)


def _rms(x, n):
    ms = jnp.sum(x * x, axis=-1, keepdims=True) * (1.0 / n)
    return x * lax.rsqrt(ms + NORM_EPS)


def _log_sigmoid(x):
    return jnp.minimum(x, 0.0) - jnp.log(1.0 + jnp.exp(-jnp.abs(x)))


def _sigmoid(x):
    return 1.0 / (1.0 + jnp.exp(-x))


def _ada_kernel(c_ref, w_ref, b_ref, o_ref):
    c = c_ref[...]
    cs = c * _sigmoid(c)
    o_ref[...] = _dot_hi(cs, w_ref[...]) + b_ref[...]


def ada_mod(c, ada_w, ada_b):
    depth, d, n6 = ada_w.shape
    b = c.shape[0]
    bp = 8
    cp = jnp.zeros((bp, d), F32).at[:b].set(c)
    tn = 1536
    out = pl.pallas_call(
        _ada_kernel,
        grid=(depth, n6 // tn),
        in_specs=[pl.BlockSpec((bp, d), lambda l, j: (0, 0)),
                  pl.BlockSpec((None, d, tn), lambda l, j: (l, 0, j)),
                  pl.BlockSpec((None, 1, tn), lambda l, j: (l, 0, j))],
        out_specs=pl.BlockSpec((None, bp, tn), lambda l, j: (l, 0, j)),
        out_shape=jax.ShapeDtypeStruct((depth, bp, n6), F32),
        compiler_params=_params("parallel", "parallel"),
        name="ada_mod",
    )(cp, ada_w, ada_b.reshape(depth, 1, n6))
    return out[:, :b]


_U_COLS = (("q", 256), ("v", 256), ("o", 256), ("g", 128), ("cq", 256), ("ckv", 128),
           ("kr", 128), ("hy", 768), ("sc", 768))
_U_TOTAL = sum(w for _, w in _U_COLS)
_UT_ROWS = 256 + 16


def _inproj_kernel(x_ref, gain_ref, sc_ref, sh_ref, w_ref, wt_ref,
                   q_ref, v_ref, o_ref, g_ref, cq_ref, ckv_ref, kr_ref, hy_ref, scu_ref,
                   kt_ref, gt_ref):
    x = x_ref[...]
    d = x.shape[-1]
    h = _rms(x, d) * gain_ref[...] * (1.0 + sc_ref[...]) + sh_ref[...]
    hb = h.astype(BF16)
    u = _dot(hb, w_ref[...])
    off = 0
    for ref, (_, width) in zip((q_ref, v_ref, o_ref, g_ref, cq_ref, ckv_ref, kr_ref, hy_ref, scu_ref),
                               _U_COLS):
        ref[...] = u[:, off:off + width]
        off += width
    ut = _dot_nt(wt_ref[...], hb)
    kt_ref[...] = ut[:256]
    gt_ref[...] = ut[256:]


def in_proj(x, gain, scale, shift, w_in):
    b, s, d = x.shape
    tm = min(512, s)
    cuts = np.cumsum([0, 256, 256, 256, 256, 16, MLA_Q_RANK, MLA_KV_RANK, MLA_ROPE, 768, 768])
    wq, wk, wv, wo, wg, wcq, wckv, wkr, why, wsc = (w_in[:, cuts[i]:cuts[i + 1]] for i in range(10))
    pad = lambda w, n: jnp.pad(w, ((0, 0), (0, n - w.shape[1])))
    wkr_p = jnp.pad(wkr, ((0, 0), (MLA_NOPE, LANES - MLA_NOPE - MLA_ROPE)))
    w1 = jnp.concatenate([wq, wv, wo, pad(wg, 128), pad(wcq, 256), wckv, wkr_p, why, wsc],
                         axis=1).astype(BF16)
    w2t = jnp.concatenate([wk, wg], axis=1).T.astype(BF16)
    row = lambda nm, w: pl.BlockSpec((None, tm, w), lambda i, j: (i, j, 0))
    out_shapes = [jax.ShapeDtypeStruct((b, s, w), F32) for _, w in _U_COLS]
    out_shapes += [jax.ShapeDtypeStruct((b, 256, s), F32), jax.ShapeDtypeStruct((b, 16, s), F32)]
    out_specs = [row(nm, w) for nm, w in _U_COLS]
    out_specs += [pl.BlockSpec((None, 256, tm), lambda i, j: (i, 0, j)),
                  pl.BlockSpec((None, 16, tm), lambda i, j: (i, 0, j))]
    vec = lambda: pl.BlockSpec((None, 1, d), lambda i, j: (i, 0, 0))
    outs = pl.pallas_call(
        _inproj_kernel,
        grid=(b, s // tm),
        in_specs=[pl.BlockSpec((None, tm, d), lambda i, j: (i, j, 0)),
                  pl.BlockSpec((1, d), lambda i, j: (0, 0)),
                  vec(), vec(),
                  pl.BlockSpec((d, _U_TOTAL), lambda i, j: (0, 0)),
                  pl.BlockSpec((_UT_ROWS, d), lambda i, j: (0, 0))],
        out_specs=out_specs,
        out_shape=out_shapes,
        compiler_params=_params("parallel", "parallel"),
        name="in_proj",
    )(x, gain.reshape(1, d), scale.reshape(b, 1, d), shift.reshape(b, 1, d), w1, w2t)
    names = [nm for nm, _ in _U_COLS] + ["kT", "gT"]
    return dict(zip(names, outs))


def _mlstm_dir(q, v, kt, gt, gc, c_ref, m_ref, base, rev):
    L = q.shape[0]
    r = lax.broadcasted_iota(jnp.int32, (L, L), 0)
    c = lax.broadcasted_iota(jnp.int32, (L, L), 1)
    tri = (c >= r) if rev else (c <= r)
    io, fo = (8, 12) if rev else (0, 4)
    logf_rows = _log_sigmoid(gt[fo:fo + 4, :])
    logf_cols = _log_sigmoid(gc)
    tri_b = jnp.where(tri, 1.0, 0.0).astype(BF16)
    a_cols = _dot(jnp.concatenate([tri_b] * 3, axis=1),
                  jnp.concatenate(_split3(logf_cols), axis=0))
    tri_t = (r >= c) if rev else (r <= c)
    tri_tb = jnp.where(tri_t, 1.0, 0.0).astype(BF16)
    a_rows = _dot(jnp.concatenate(_split3(_log_sigmoid(gt)), axis=1),
                  jnp.concatenate([tri_tb] * 3, axis=0))[fo:fo + 4, :]
    lane = lax.broadcasted_iota(jnp.int32, (L, LANES), 1)
    row128 = lax.broadcasted_iota(jnp.int32, (LANES, L), 0)
    scale = HEAD_DIM ** -0.5
    outs = []
    for pair in range(2):
        qp = q[:, pair * LANES:(pair + 1) * LANES]
        vp = v[:, pair * LANES:(pair + 1) * LANES]
        ktp = kt[pair * LANES:(pair + 1) * LANES, :] * scale
        pair_out = None
        for sub in range(2):
            h = pair * 2 + sub
            in_head = (row128 >= sub * HEAD_DIM) & (row128 < (sub + 1) * HEAD_DIM)
            kth = jnp.where(in_head, ktp, 0.0)
            own = (lane < HEAD_DIM) if sub == 0 else (lane >= HEAD_DIM)
            ncol = HEAD_DIM if sub == 0 else 0
            v_aug = jnp.where(own, vp, jnp.where(lane == ncol, 1.0, 0.0))
            v_aug_b = v_aug.astype(BF16)
            a_c = a_cols[:, fo + h:fo + h + 1]
            a_r = a_rows[h:h + 1, :]
            ig_r = gt[io + h:io + h + 1, :]
            a_end = jnp.sum(logf_rows[h:h + 1, :], axis=-1, keepdims=True)
            c_st = c_ref[base + h]
            m_st = m_ref[base + h][:, 0:1]
            r_log = jnp.where(tri, ig_r - a_r, -jnp.inf)
            inter = a_c + m_st
            m_t = jnp.maximum(inter, a_c + jnp.max(r_log, axis=-1, keepdims=True))
            qb = qp.astype(BF16)
            p = jnp.exp(r_log + (a_c - m_t)) * _dot(qb, kth.astype(BF16))
            sci = jnp.exp(inter - m_t)
            nd = _dot(p.astype(BF16), v_aug_b) + sci * _dot(qb, c_st.astype(BF16))
            den = nd[:, ncol:ncol + 1]
            out = nd / jnp.maximum(jnp.abs(den), jnp.exp(-m_t))
            w_st = a_end - a_r + ig_r
            m_loc = jnp.max(w_st, axis=-1, keepdims=True)
            ke = kth * jnp.exp(w_st - m_loc)
            c_loc = _dot(ke.astype(BF16), v_aug_b)
            m_new = jnp.maximum(a_end + m_st, m_loc)
            sp = jnp.exp(a_end + m_st - m_new)
            sl = jnp.exp(m_loc - m_new)
            c_ref[base + h] = sp * c_st + sl * c_loc
            m_ref[base + h] = jnp.broadcast_to(m_new, (1, LANES))
            pair_out = out if sub == 0 else jnp.where(lane < HEAD_DIM, pair_out, out)
        outs.append(pair_out)
    return jnp.concatenate(outs, axis=1)


def _mlstm_kernel(qf_ref, vf_ref, ktf_ref, gtf_ref, gcf_ref,
                  qb_ref, vb_ref, ktb_ref, gtb_ref, gcb_ref, brow_ref, bcol_ref,
                  hf_ref, hb_ref, c_ref, m_ref, *, n_sub):
    @pl.when(pl.program_id(1) == 0)
    def _():
        c_ref[...] = jnp.zeros_like(c_ref)
        m_ref[...] = jnp.zeros_like(m_ref)

    bcol = bcol_ref[...]
    brow = brow_ref[...]
    L = ML_CHUNK
    for ci in range(n_sub):
        rf = slice(ci * L, (ci + 1) * L)
        hf_ref[rf, :] = _mlstm_dir(qf_ref[rf, :], vf_ref[rf, :], ktf_ref[:, rf], gtf_ref[:, rf] + bcol,
                                   gcf_ref[rf, :] + brow, c_ref, m_ref, 0, False)
        rb = slice((n_sub - 1 - ci) * L, (n_sub - ci) * L)
        hb_ref[rb, :] = _mlstm_dir(qb_ref[rb, :], vb_ref[rb, :], ktb_ref[:, rb], gtb_ref[:, rb] + bcol,
                                   gcb_ref[rb, :] + brow, c_ref, m_ref, N_HEADS, True)


def mlstm(q, v, kt, gt, g, gate_b):
    b, s, w = q.shape
    n_sub = 2
    L = n_sub * ML_CHUNK
    nc = s // L
    bflat = gate_b.reshape(16)
    brow = jnp.zeros((1, LANES), F32).at[0, :16].set(bflat)
    bcol = bflat.reshape(16, 1)
    fw = lambda i, j: (i, j, 0)
    bw = lambda i, j: (i, nc - 1 - j, 0)
    fwt = lambda i, j: (i, 0, j)
    bwt = lambda i, j: (i, 0, nc - 1 - j)

    def specs(m, mt):
        return [pl.BlockSpec((None, L, w), m), pl.BlockSpec((None, L, w), m),
                pl.BlockSpec((None, w, L), mt), pl.BlockSpec((None, 16, L), mt),
                pl.BlockSpec((None, L, LANES), m)]

    return pl.pallas_call(
        functools.partial(_mlstm_kernel, n_sub=n_sub),
        grid=(b, nc),
        in_specs=specs(fw, fwt) + specs(bw, bwt) + [
            pl.BlockSpec((1, LANES), lambda i, j: (0, 0)),
            pl.BlockSpec((16, 1), lambda i, j: (0, 0))],
        out_specs=[pl.BlockSpec((None, L, w), fw), pl.BlockSpec((None, L, w), bw)],
        out_shape=[jax.ShapeDtypeStruct((b, s, w), F32)] * 2,
        scratch_shapes=[pltpu.VMEM((2 * N_HEADS, LANES, LANES), F32),
                        pltpu.VMEM((2 * N_HEADS, 1, LANES), F32)],
        compiler_params=_params("parallel", "arbitrary"),
        name="mlstm",
    )(q, v, kt, gt, g, q, v, kt, gt, g, brow, bcol)


def _rope_table_kernel(pos_ref, post_ref, inv_ref, invt_ref, cos_ref, sin_ref, cost_ref, sint_ref):
    ang = pos_ref[...].astype(F32) * inv_ref[...]
    cos = jnp.cos(ang)
    sin = jnp.sin(ang)
    cos_ref[...] = cos
    sin_ref[...] = sin
    cost_ref[...] = cos.T
    sint_ref[...] = sin.T


def rope_tables(positions):
    b, s = positions.shape
    tm = min(512, s)
    inv = ROPE_THETA ** (-jnp.arange(ROPE_HALF, dtype=F32) / ROPE_HALF)
    inv_row = jnp.zeros((1, LANES), F32).at[0, MLA_NOPE:MLA_NOPE + ROPE_HALF].set(inv)
    inv_row = inv_row.at[0, MLA_NOPE + ROPE_HALF:MLA_NOPE + MLA_ROPE].set(inv)
    spec = pl.BlockSpec((None, tm, LANES), lambda i, j: (i, j, 0))
    spec_t = pl.BlockSpec((None, LANES, tm), lambda i, j: (i, 0, j))
    return pl.pallas_call(
        _rope_table_kernel,
        grid=(b, s // tm),
        in_specs=[pl.BlockSpec((None, tm, 1), lambda i, j: (i, j, 0)),
                  pl.BlockSpec((None, 1, tm), lambda i, j: (i, 0, j)),
                  pl.BlockSpec((1, LANES), lambda i, j: (0, 0)),
                  pl.BlockSpec((LANES, 1), lambda i, j: (0, 0))],
        out_specs=[spec, spec, spec_t, spec_t],
        out_shape=[jax.ShapeDtypeStruct((b, s, LANES), F32)] * 2
        + [jax.ShapeDtypeStruct((b, LANES, s), F32)] * 2,
        compiler_params=_params("parallel", "parallel"),
        name="rope_tables",
    )(positions.reshape(b, s, 1), positions.reshape(b, 1, s), inv_row, inv_row.reshape(LANES, 1))


def _mla_proj_kernel(cq_ref, ckv_ref, kr_ref, cos_ref, sin_ref, cost_ref, sint_ref, qg_ref, kvg_ref,
                     wqt_ref, wk_ref, wvt_ref, qt_ref, k_ref, vt_ref):
    cos = cos_ref[...]
    sin = sin_ref[...]
    lane = lax.broadcasted_iota(jnp.int32, cos.shape, 1)
    x1 = (lane >= MLA_NOPE) & (lane < MLA_NOPE + ROPE_HALF)
    x2 = (lane >= MLA_NOPE + ROPE_HALF) & (lane < MLA_NOPE + MLA_ROPE)
    kr = kr_ref[...]
    krr = (kr * cos + pltpu.roll(kr, LANES - ROPE_HALF, 1) * jnp.where(x1, -sin, 0.0)
           + pltpu.roll(kr, ROPE_HALF, 1) * jnp.where(x2, sin, 0.0))

    cqn = (_rms(cq_ref[...], MLA_Q_RANK) * qg_ref[...]).astype(BF16)
    ckvn = (_rms(ckv_ref[...], MLA_KV_RANK) * kvg_ref[...]).astype(BF16)
    kn = _dot(ckvn, wk_ref[...])
    for h in range(N_HEADS):
        sl = slice(h * LANES, (h + 1) * LANES)
        k_ref[:, sl] = (kn[:, sl] + krr).astype(BF16)
    vrow = lax.broadcasted_iota(jnp.int32, vt_ref.shape, 0)
    ones_row = jnp.where(vrow % V_ROWS == HEAD_DIM, 1.0, 0.0)
    vt_ref[...] = (_dot_nt(wvt_ref[...], ckvn) + ones_row).astype(BF16)

    cos_t = cost_ref[...]
    sin_t = sint_ref[...]
    row = lax.broadcasted_iota(jnp.int32, cos_t.shape, 0)
    x1t = (row >= MLA_NOPE) & (row < MLA_NOPE + ROPE_HALF)
    x2t = (row >= MLA_NOPE + ROPE_HALF) & (row < MLA_NOPE + MLA_ROPE)
    sin_at = jnp.where(x1t, -sin_t, 0.0)
    sin_bt = jnp.where(x2t, sin_t, 0.0)
    qt = _dot_nt(wqt_ref[...], cqn)
    scale = (MLA_NOPE + MLA_ROPE) ** -0.5 * math.log2(math.e)
    for h in range(N_HEADS):
        sl = slice(h * LANES, (h + 1) * LANES)
        xs = qt[sl, :]
        roped = (xs * cos_t + pltpu.roll(xs, LANES - ROPE_HALF, 0) * sin_at
                 + pltpu.roll(xs, ROPE_HALF, 0) * sin_bt)
        qt_ref[sl, :] = (roped * scale).astype(BF16)


def mla_proj(cq, ckv, kr, tables, q_norm, kv_norm, w_uq, w_uk, w_uv):
    b, s, _ = cq.shape
    tm = min(512, s)
    dqk = MLA_NOPE + MLA_ROPE
    cos, sin, cos_t, sin_t = tables
    wq = w_uq.reshape(MLA_Q_RANK, N_HEADS, dqk)
    wq = jnp.pad(wq, ((0, 256 - MLA_Q_RANK), (0, 0), (0, LANES - dqk))).reshape(256, N_HEADS * LANES)
    wk = w_uk.reshape(MLA_KV_RANK, N_HEADS, MLA_NOPE)
    wk = jnp.pad(wk, ((0, 0), (0, 0), (0, LANES - MLA_NOPE))).reshape(MLA_KV_RANK, N_HEADS * LANES)
    qg = jnp.pad(q_norm, (0, 256 - MLA_Q_RANK)).reshape(1, 256)
    kvg = kv_norm.reshape(1, MLA_KV_RANK)
    row = lambda w: pl.BlockSpec((None, tm, w), lambda i, j: (i, j, 0))
    col = lambda w: pl.BlockSpec((None, w, tm), lambda i, j: (i, 0, j))
    full = lambda a: pl.BlockSpec(a.shape, lambda i, j: (0,) * a.ndim)
    wv = jnp.pad(w_uv.reshape(MLA_KV_RANK, N_HEADS, HEAD_DIM), ((0, 0), (0, 0), (0, V_ROWS - HEAD_DIM)))
    wqt, wkb = wq.T.astype(BF16), wk.astype(BF16)
    wvt = wv.reshape(MLA_KV_RANK, N_HEADS * V_ROWS).T.astype(BF16)
    return pl.pallas_call(
        _mla_proj_kernel,
        grid=(b, s // tm),
        in_specs=[row(256), row(128), row(128), row(128), row(128), col(128), col(128),
                  full(qg), full(kvg), full(wqt), full(wkb), full(wvt)],
        out_specs=[col(512), row(512), col(N_HEADS * V_ROWS)],
        out_shape=[jax.ShapeDtypeStruct((b, 512, s), BF16), jax.ShapeDtypeStruct((b, s, 512), BF16),
                   jax.ShapeDtypeStruct((b, N_HEADS * V_ROWS, s), BF16)],
        compiler_params=_params("parallel", "parallel"),
        name="mla_proj",
    )(cq, ckv, kr, cos, sin, cos_t, sin_t, qg, kvg, wqt, wkb, wvt)


def _flash_kernel(qt_ref, k_ref, vt_ref, o_ref, sc0_ref, sc1_ref, p0_ref, p1_ref, *, tk, nh):
    s = k_ref.shape[0]
    tq = qt_ref.shape[1]
    nk = s // tk
    sc_bufs = (sc0_ref, sc1_ref)
    p_bufs = (p0_ref, p1_ref)

    def scores(t, par):
        off = pl.multiple_of(t * tk, tk)
        for sub in range(nh):
            sc_bufs[par][sub] = _dot(k_ref[pl.ds(off, tk), sub * LANES:(sub + 1) * LANES],
                                     qt_ref[sub * LANES:(sub + 1) * LANES, :])

    def soften(par, ms):
        m_out, alphas = [], []
        for sub in range(nh):
            sc = sc_bufs[par][sub]
            m_new = jnp.maximum(ms[sub], jnp.max(sc, axis=0, keepdims=True))
            alphas.append(jnp.exp2(ms[sub] - m_new))
            p_bufs[par][sub] = jnp.exp2((sc - m_new).astype(BF16))
            m_out.append(m_new)
        return m_out, alphas

    def accumulate(t, par, alphas, accs):
        off = pl.multiple_of(t * tk, tk)
        new = []
        for sub in range(nh):
            vt = vt_ref[sub * V_ROWS:(sub + 1) * V_ROWS, pl.ds(off, tk)]
            new.append(alphas[sub] * accs[sub] + _dot(vt, p_bufs[par][sub]))
        return new

    def step(t, par, ms, alphas, accs, with_scores=True):
        if with_scores:
            scores(t + 2, par)
        ms, alphas_next = soften(1 - par, ms)
        accs = accumulate(t, par, alphas, accs)
        return ms, alphas_next, accs

    def body(u, carry):
        ms, alphas, accs = carry
        ms, alphas, accs = step(2 * u, 0, ms, alphas, accs)
        ms, alphas, accs = step(2 * u + 1, 1, ms, alphas, accs)
        return ms, alphas, accs

    scores(0, 0)
    scores(1, 1)
    ms, alphas = soften(0, [jnp.full((1, tq), -jnp.inf, F32) for _ in range(nh)])
    accs = [jnp.zeros((V_ROWS, tq), F32) for _ in range(nh)]
    ms, alphas, accs = lax.fori_loop(0, nk // 2 - 1, body, (ms, alphas, accs))
    ms, alphas, accs = step(nk - 2, 0, ms, alphas, accs, with_scores=False)
    accs = accumulate(nk - 1, 1, alphas, accs)
    out_t = jnp.concatenate([acc[:HEAD_DIM] / acc[HEAD_DIM:HEAD_DIM + 1] for acc in accs],
                            axis=0)
    o_ref[...] = out_t.T


def flash_attention(qt, k, vt):
    b, _, s = qt.shape
    tq = min(256, s)
    tk = min(512, s // 2)
    nh = FLASH_HEADS
    ng = N_HEADS // nh
    return pl.pallas_call(
        functools.partial(_flash_kernel, tk=tk, nh=nh),
        grid=(b, ng, s // tq),
        in_specs=[pl.BlockSpec((None, nh * LANES, tq), lambda i, p, j: (i, p, j)),
                  pl.BlockSpec((None, s, nh * LANES), lambda i, p, j: (i, 0, p)),
                  pl.BlockSpec((None, nh * V_ROWS, s), lambda i, p, j: (i, p, 0))],
        out_specs=pl.BlockSpec((None, tq, nh * HEAD_DIM), lambda i, p, j: (i, j, p)),
        out_shape=jax.ShapeDtypeStruct((b, s, N_HEADS * HEAD_DIM), F32),
        scratch_shapes=[pltpu.VMEM((nh, tk, tq), F32)] * 2 + [pltpu.VMEM((nh, tk, tq), BF16)] * 2,
        compiler_params=_params("parallel", "parallel", "parallel"),
        name="flash_attention",
    )(qt, k, vt)


def _shifted(x, prev_row, next_row, first, last):
    tm = x.shape[0]
    row = lax.broadcasted_iota(jnp.int32, x.shape, 0)
    prev_row = jnp.where(first, 0.0, prev_row)
    next_row = jnp.where(last, 0.0, next_row)
    xm1 = jnp.where(row == 0, prev_row, pltpu.roll(x, 1, 0))
    xp1 = jnp.where(row == tm - 1, next_row, pltpu.roll(x, tm - 1, 0))
    return xm1, xp1


def _conv_kernel(hy_ref, hyp_ref, hyn_ref, sc_ref, scp_ref, scn_ref, hw_ref, hb_ref, sw_ref,
                 x1_ref, x2_ref, z_ref, yd_ref):
    j = pl.program_id(1)
    first = j == 0
    last = j == pl.num_programs(1) - 1
    x = hy_ref[...]
    xm1, xp1 = _shifted(x, hyp_ref[7:8, :], hyn_ref[0:1, :], first, last)
    hw = hw_ref[...]
    proj = xm1 * hw[0:1] + x * hw[1:2] + xp1 * hw[2:3] + hb_ref[...]
    x1_ref[...] = proj[:, 0:GROUP_W]
    x2_ref[...] = proj[:, GROUP_W:2 * GROUP_W]
    z_ref[...] = proj[:, 2 * GROUP_W:]
    su = sc_ref[...]
    g = GROUP_W
    prod = su[:, g:2 * g] * su[:, 2 * g:]
    pprev = scp_ref[7:8, g:2 * g] * scp_ref[7:8, 2 * g:]
    pnext = scn_ref[0:1, g:2 * g] * scn_ref[0:1, 2 * g:]
    pm1, pp1 = _shifted(prod, pprev, pnext, first, last)
    sw = sw_ref[...]
    yd_ref[...] = su[:, :g] * (pm1 * sw[0:1] + prod * sw[1:2] + pp1 * sw[2:3])


def conv_mixers(hy_u, sc_u, hy_conv_w, hy_conv_b, sc_conv_w):
    b, s, w3 = hy_u.shape
    tm = min(512, s)
    nb8 = s // 8
    r8 = tm // 8
    main = pl.BlockSpec((None, tm, w3), lambda i, j: (i, j, 0))
    prev = pl.BlockSpec((None, 8, w3), lambda i, j: (i, jnp.maximum(j * r8 - 1, 0), 0))
    nxt = pl.BlockSpec((None, 8, w3), lambda i, j: (i, jnp.minimum((j + 1) * r8, nb8 - 1), 0))
    full = lambda a: pl.BlockSpec(a.shape, lambda i, j: (0,) * a.ndim)
    hw = hy_conv_w.T
    hb = hy_conv_b.reshape(1, w3)
    sw = sc_conv_w.T
    out = pl.BlockSpec((None, tm, GROUP_W), lambda i, j: (i, j, 0))
    return pl.pallas_call(
        _conv_kernel,
        grid=(b, s // tm),
        in_specs=[main, prev, nxt, main, prev, nxt, full(hw), full(hb), full(sw)],
        out_specs=[out] * 4,
        out_shape=[jax.ShapeDtypeStruct((b, s, GROUP_W), F32)] * 4,
        compiler_params=_params("parallel", "parallel"),
        name="conv_mixers",
    )(hy_u, hy_u, hy_u, sc_u, sc_u, sc_u, hw, hb, sw)


def _fft_dims(s):
    n = 2 * s
    lg = int(round(math.log2(n)))
    assert 1 << lg == n
    n1 = 1 << ((lg + 1) // 2)
    return n1, n // n1


def _filter_kernel(frow_ref, w1h_ref, w1l_ref, b1_ref, fr1_ref, w2h_ref, w2l_ref, b2_ref, fr2_ref,
                   w3h_ref, w3l_ref, b3_ref, ld_ref, k_ref, norm_ref, *, length):
    i = pl.program_id(0)
    tm = k_ref.shape[0]
    hm = tm // 2
    half = LANES // 2

    def times(first):
        n = first + lax.broadcasted_iota(jnp.int32, (hm, 1), 0)
        tt = jnp.where(n < length, n, 2 * length - 1 - n)
        return tt.astype(F32) / length

    ta, tb = times(i * tm), times(i * tm + hm)
    lane = lax.broadcasted_iota(jnp.int32, (hm, LANES), 1)
    sub = lane % half
    t2 = jnp.where(lane < half, ta, tb)
    ang = t2 * frow_ref[...] + jnp.where(sub > HY_BANDS, 0.5 * math.pi, 0.0)
    z = jnp.where(sub == 0, t2, jnp.where(sub <= 2 * HY_BANDS, jnp.sin(ang), 0.0))
    hid = jnp.sin(fr1_ref[...] * (_dot3(z, w1h_ref[...], w1l_ref[...]) + b1_ref[...]))
    hid = jnp.sin(fr2_ref[...] * (_dot3(hid, w2h_ref[...], w2l_ref[...]) + b2_ref[...]))
    decay = jnp.exp(ld_ref[...])
    total = None
    for part, t in enumerate((ta, tb)):
        filt = ((_dot3(hid, w3h_ref[part], w3l_ref[part]) + b3_ref[...]) * jnp.exp(-t * decay))
        k_ref[part * hm:(part + 1) * hm, :] = filt
        psum = jnp.sum(jnp.abs(filt), axis=0, keepdims=True)
        total = psum if total is None else total + psum

    @pl.when(i == 0)
    def _():
        norm_ref[...] = jnp.zeros_like(norm_ref)

    norm_ref[...] += total


def hyena_filter_taps(length, w1, b1, fr1, w2, b2, fr2, w3, b3, log_decay):
    tm = min(512, length)
    n_half = length // tm
    oc = 2 * GROUP_W
    half = LANES // 2
    assert HY_FFN == half
    bands = jnp.arange(1, HY_BANDS + 1, dtype=F32) * (2.0 * math.pi)
    fhalf = jnp.zeros((half,), F32).at[1:1 + HY_BANDS].set(bands).at[1 + HY_BANDS:1 + 2 * HY_BANDS].set(bands)
    frow = jnp.concatenate([fhalf, fhalf]).reshape(1, LANES)
    w1half = jnp.zeros((half, HY_FFN), F32).at[:1 + 2 * HY_BANDS].set(w1)
    blockdiag = lambda a: jnp.kron(jnp.eye(2, dtype=F32), a)
    w1p, w2p = blockdiag(w1half), blockdiag(w2)
    dup = lambda a: jnp.concatenate([a, a]).reshape(1, LANES)
    bydir = lambda a: jnp.moveaxis(a.reshape(a.shape[0], 2, 2, GROUP_W), 2, 0).reshape(2, a.shape[0], oc)
    w3d, b3d, ldd = bydir(w3), bydir(b3.reshape(1, -1)), bydir(log_decay.reshape(1, -1))
    zeros = jnp.zeros_like(w3d)
    w3p = jnp.stack([jnp.concatenate([w3d, zeros], axis=1), jnp.concatenate([zeros, w3d], axis=1)],
                    axis=1)
    full = lambda a: pl.BlockSpec(a.shape, lambda i: (0,) * a.ndim)
    dirspec = lambda a: pl.BlockSpec((None,) + a.shape[1:], lambda i: (i // n_half,) + (0,) * (a.ndim - 1))
    w1h, w1l = _split2(w1p)
    w2h, w2l = _split2(w2p)
    w3h, w3l = _split2(w3p)
    return pl.pallas_call(
        functools.partial(_filter_kernel, length=length),
        grid=(2 * n_half,),
        in_specs=[full(frow), full(w1h), full(w1l), full(dup(b1)), full(dup(fr1)), full(w2h), full(w2l),
                  full(dup(b2)), full(dup(fr2)), dirspec(w3h), dirspec(w3l), dirspec(b3d), dirspec(ldd)],
        out_specs=[pl.BlockSpec((tm, oc), lambda i: (i, 0)), pl.BlockSpec((1, oc), lambda i: (0, 0))],
        out_shape=[jax.ShapeDtypeStruct((2 * length, oc), F32), jax.ShapeDtypeStruct((1, oc), F32)],
        compiler_params=_params("arbitrary"),
        name="hyena_filter",
    )(frow, w1h, w1l, dup(b1), dup(fr1), w2h, w2l, dup(b2), dup(fr2), w3h, w3l, b3d, ldd)


def _dft_consts(s):
    n1, n2 = _fft_dims(s)
    n = n1 * n2
    kh = n1 // 2 + 1
    kp = -(-kh // 4) * 4
    pad_rows = lambda a: np.concatenate([a, np.zeros((kp - kh, a.shape[1]))], axis=0)
    a1 = 2.0 * np.pi * np.outer(np.arange(kh), np.arange(n1)) / n1
    f1 = np.concatenate([pad_rows(np.cos(a1)), pad_rows(-np.sin(a1))], axis=0)
    a2 = 2.0 * np.pi * np.outer(np.arange(n2), np.arange(n2)) / n2
    c2, s2 = np.cos(a2), np.sin(a2)
    f2_fwd = np.block([[c2, s2], [-s2, c2]])
    f2_inv = np.block([[c2, -s2], [s2, c2]])
    at = 2.0 * np.pi * np.outer(np.arange(kp), np.arange(n2)) / n
    tw = np.stack([np.cos(at), np.sin(at)], axis=0)[..., None]
    m1 = np.arange(n1 // 2)
    a3 = 2.0 * np.pi * np.outer(m1, np.arange(kh)) / n1
    wgt = np.where((np.arange(kh) == 0) | (np.arange(kh) == n1 // 2), 1.0, 2.0) / n
    f3 = np.concatenate([pad_rows((np.cos(a3) * wgt).T).T, pad_rows((-np.sin(a3) * wgt).T).T],
                        axis=1)
    f = lambda a: jnp.asarray(a, F32)
    x3 = lambda a: _lhs_x3(f(a))
    return dict(n1=n1, n2=n2, kh=kh, kp=kp, f1=x3(f1), f1_half=x3(f1[:, :n1 // 2]), f2_fwd=x3(f2_fwd),
                f2_inv=x3(f2_inv), tw=f(tw), f3=x3(f3))


_N2_TILE = 8


def _dft1_kernel(x_ref, f_ref, s_ref, o_ref, xs_ref):
    f = f_ref[...]
    scale = s_ref[...]
    for m in range(_N2_TILE):
        xs_ref[...] = x_ref[:, m, :]
        o_ref[:, m, :] = _dot_x3(f, xs_ref[...] * scale)


def dft_stage1(x, f1, scale_row):
    g, r, n2, ch = x.shape
    m = f1.shape[0]
    return pl.pallas_call(
        _dft1_kernel,
        grid=(g, n2 // _N2_TILE),
        in_specs=[pl.BlockSpec((None, r, _N2_TILE, ch), lambda i, j: (i, 0, j, 0)),
                  pl.BlockSpec(f1.shape, lambda i, j: (0, 0)),
                  pl.BlockSpec((1, ch), lambda i, j: (0, 0))],
        out_specs=pl.BlockSpec((None, m, _N2_TILE, ch), lambda i, j: (i, 0, j, 0)),
        out_shape=jax.ShapeDtypeStruct((g, m, n2, ch), F32),
        scratch_shapes=[pltpu.VMEM((r, ch), F32)],
        compiler_params=_params("parallel", "parallel"),
        name="dft_stage1",
    )(x, f1, scale_row)


def _twiddle(re, im, tc, ts, conj):
    if conj:
        return re * tc - im * ts, im * tc + re * ts
    return re * tc + im * ts, im * tc - re * ts


_K1_STEP = 4


def _dft2_filter_kernel(a_ref, tw_ref, ff_ref, o_ref):
    n2 = a_ref.shape[2]
    for i in range(_K1_STEP):
        br, bi = _twiddle(a_ref[0, i], a_ref[1, i], tw_ref[0, i], tw_ref[1, i], False)
        zz = _dot_x3(ff_ref[...], jnp.concatenate([br, bi], axis=0))
        o_ref[0, i] = zz[:n2]
        o_ref[1, i] = zz[n2:]


def dft_stage2_filter(a, consts):
    _, kp, n2, ch = a.shape
    return pl.pallas_call(
        _dft2_filter_kernel,
        grid=(kp // _K1_STEP,),
        input_output_aliases={0: 0},
        in_specs=[pl.BlockSpec((2, _K1_STEP, n2, ch), lambda k: (0, k, 0, 0)),
                  pl.BlockSpec((2, _K1_STEP, n2, 1), lambda k: (0, k, 0, 0)),
                  pl.BlockSpec(consts["f2_fwd"].shape, lambda k: (0, 0))],
        out_specs=pl.BlockSpec((2, _K1_STEP, n2, ch), lambda k: (0, k, 0, 0)),
        out_shape=jax.ShapeDtypeStruct(a.shape, F32),
        compiler_params=_params("parallel"),
        name="dft_stage2_filter",
    )(a, consts["tw"], consts["f2_fwd"])


def _dft2_conv_kernel(a_ref, tw_ref, kf_ref, ff_ref, fi_ref, o_ref):
    n2 = a_ref.shape[2]
    for i in range(_K1_STEP):
        tc, ts = tw_ref[0, i], tw_ref[1, i]
        br, bi = _twiddle(a_ref[0, i], a_ref[1, i], tc, ts, False)
        zz = _dot_x3(ff_ref[...], jnp.concatenate([br, bi], axis=0))
        zr, zi = zz[:n2], zz[n2:]
        kr, ki = kf_ref[0, i], kf_ref[1, i]
        pr = zr * kr - zi * ki
        pi = zr * ki + zi * kr
        vv = _dot_x3(fi_ref[...], jnp.concatenate([pr, pi], axis=0))
        vr, vi = _twiddle(vv[:n2], vv[n2:], tc, ts, True)
        o_ref[0, i] = vr
        o_ref[1, i] = vi


def dft_stage2_conv(a, kf, order, consts):
    b, _, kp, n2, ch = a.shape
    blk = pl.BlockSpec((None, 2, _K1_STEP, n2, ch), lambda i, k: (i, 0, k, 0, 0))
    mat = pl.BlockSpec(consts["f2_fwd"].shape, lambda i, k: (0, 0))
    return pl.pallas_call(
        _dft2_conv_kernel,
        grid=(b, kp // _K1_STEP),
        input_output_aliases={0: 0},
        in_specs=[blk,
                  pl.BlockSpec((2, _K1_STEP, n2, 1), lambda i, k: (0, k, 0, 0)),
                  pl.BlockSpec((2, _K1_STEP, n2, ch), lambda i, k: (0, k, 0, order)),
                  mat, mat],
        out_specs=blk,
        out_shape=jax.ShapeDtypeStruct(a.shape, F32),
        compiler_params=_params("parallel", "parallel"),
        name="dft_stage2_conv",
    )(a, consts["tw"], kf, consts["f2_fwd"], consts["f2_inv"])


def _dft3_kernel(v_ref, f_ref, gate_ref, z_ref, bias_ref, o_ref, vs_ref, ys_ref):
    f = f_ref[...]
    for m in range(_N2_TILE):
        vs_ref[...] = v_ref[:, m, :]
        ys_ref[:, m, :] = _dot_x3(f, vs_ref[...])
    o_ref[...] = gate_ref[...] * (ys_ref[...] + z_ref[...] * bias_ref[...])


def dft_stage3_gate(v, f3, gate, z, bias_row):
    b, m, n2, ch = v.shape
    r = f3.shape[0]
    row = pl.BlockSpec((None, r, _N2_TILE, ch), lambda i, j: (i, 0, j, 0))
    return pl.pallas_call(
        _dft3_kernel,
        grid=(b, n2 // _N2_TILE),
        in_specs=[pl.BlockSpec((None, m, _N2_TILE, ch), lambda i, j: (i, 0, j, 0)),
                  pl.BlockSpec(f3.shape, lambda i, j: (0, 0)),
                  row, row, pl.BlockSpec((1, ch), lambda i, j: (0, 0))],
        out_specs=row,
        out_shape=jax.ShapeDtypeStruct((b, r, n2, ch), F32),
        scratch_shapes=[pltpu.VMEM((m, ch), F32), pltpu.VMEM((r, _N2_TILE, ch), F32)],
        compiler_params=_params("parallel", "parallel"),
        name="dft_stage3_gate",
    )(v, f3, gate, z, bias_row)


def hyena_mixer(x1, x2, z, hy_w1, hy_b1, hy_fr1, hy_w2, hy_b2, hy_fr2, hy_w3, hy_b3, hy_log_decay,
                hy_bias):
    b, s, ch = z.shape
    consts = _dft_consts(s)
    n1, n2 = consts["n1"], consts["n2"]
    oc = 2 * ch
    taps, norm = hyena_filter_taps(s, hy_w1, hy_b1, hy_fr1, hy_w2, hy_b2, hy_fr2, hy_w3, hy_b3,
                                   hy_log_decay)
    kp = consts["kp"]
    ka = dft_stage1(taps.reshape(1, n1, n2, oc), consts["f1"], 1.0 / norm)
    kf = dft_stage2_filter(ka.reshape(2, kp, n2, oc), consts)
    half = n1 // 2
    view = lambda t: t.reshape(b, half, n2, ch)
    ones = jnp.ones((1, ch), F32)
    f1_half = consts["f1_half"]
    cur = view(z)
    for order, gate in enumerate((x1, x2)):
        a = dft_stage1(cur, f1_half, ones)
        v = dft_stage2_conv(a.reshape(b, 2, kp, n2, ch), kf, order, consts)
        cur = dft_stage3_gate(v.reshape(b, 2 * kp, n2, ch), consts["f3"], view(gate), cur,
                              hy_bias[order].reshape(1, ch))
    return cur.reshape(b, s, ch)


def _outproj_kernel(x_ref, hf_ref, hb_ref, o_ref, yb_ref, yc_ref, yd_ref, og_ref, hm_ref, wout_ref,
                    pg_ref, g1_ref, fg_ref, sc2_ref, sh2_ref, rwt_ref,
                    xn_ref, h2_ref, afft_ref):
    hm = hm_ref[...]
    y_a = _sigmoid(o_ref[...]) * (hf_ref[...] + hb_ref[...])
    acc = None
    for idx, y in enumerate((y_a, yb_ref[...], yc_ref[...], yd_ref[...])):
        ms = _dot(jnp.concatenate(_split2(y * y), axis=1), hm)
        yn = y * lax.rsqrt(ms + NORM_EPS) * og_ref[:, idx * GROUP_W:(idx + 1) * GROUP_W]
        part = _dot(yn.astype(BF16), wout_ref[idx * GROUP_W:(idx + 1) * GROUP_W, :])
        acc = part if acc is None else acc + part
    d = acc.shape[-1]
    xn = x_ref[...] + g1_ref[...] * (_rms(acc, d) * pg_ref[...])
    xn_ref[...] = xn
    h2 = _rms(xn, d) * fg_ref[...] * (1.0 + sc2_ref[...]) + sh2_ref[...]
    h2_ref[...] = h2.T.astype(BF16)
    logits_t = _dot_nt_hi(rwt_ref[...], h2)
    mx = jnp.max(logits_t, axis=0, keepdims=True)
    ex = jnp.exp(logits_t - mx)
    afft_ref[...] = ex / jnp.sum(ex, axis=0, keepdims=True)


def out_proj(x, hf, hb, o, yb, yc, yd, out_g, w_out, post_g, g1, ffn_g, sc2, sh2, router_w):
    b, s, d = x.shape
    tm = min(512, s)
    e = router_w.shape[1]
    hm1 = np.kron(np.eye(GROUP_W // HEAD_DIM), np.ones((HEAD_DIM, HEAD_DIM))) / HEAD_DIM
    hm = jnp.asarray(np.concatenate([hm1, hm1], axis=0), BF16)
    row = lambda w: pl.BlockSpec((None, tm, w), lambda i, j: (i, j, 0))
    full = lambda a: pl.BlockSpec(a.shape, lambda i, j: (0,) * a.ndim)
    vec = lambda: pl.BlockSpec((None, 1, d), lambda i, j: (i, 0, 0))
    r1 = lambda a: a.reshape(1, -1)
    wob = w_out.astype(BF16)
    rwt = router_w.T
    return pl.pallas_call(
        _outproj_kernel,
        grid=(b, s // tm),
        in_specs=[row(d)] + [row(GROUP_W)] * 6 + [full(r1(out_g)), full(hm), full(wob),
                                                  full(r1(post_g)), vec(), full(r1(ffn_g)), vec(), vec(),
                                                  full(rwt)],
        out_specs=[row(d), pl.BlockSpec((None, d, tm), lambda i, j: (i, 0, j)),
                   pl.BlockSpec((None, e, tm), lambda i, j: (i, 0, j))],
        out_shape=[jax.ShapeDtypeStruct((b, s, d), F32), jax.ShapeDtypeStruct((b, d, s), BF16),
                   jax.ShapeDtypeStruct((b, e, s), F32)],
        compiler_params=_params("parallel", "parallel"),
        name="out_proj",
    )(x, hf, hb, o, yb, yc, yd, r1(out_g), hm, wob, r1(post_g), g1.reshape(b, 1, d), r1(ffn_g),
      sc2.reshape(b, 1, d), sh2.reshape(b, 1, d), rwt)


def _lane_cumsum(x):
    n = x.shape[-1]
    lane = lax.broadcasted_iota(jnp.int32, x.shape, x.ndim - 1)
    shift = 1
    while shift < n:
        x = x + jnp.where(lane >= shift, pltpu.roll(x, shift, x.ndim - 1), 0.0)
        shift *= 2
    return x


def _select_kernel(aff_ref, u_ref, ps_ref, cb_ref, *, cap):
    aff = aff_ref[...]
    capf = float(cap)

    def body(i, bits):
        cand = bits | (jnp.int32(1) << (30 - i))
        cnt = jnp.sum(jnp.where(aff >= pltpu.bitcast(cand, F32), 1.0, 0.0), axis=-1, keepdims=True)
        return jnp.where(cnt >= capf, cand, bits)

    bits = lax.fori_loop(0, 31, body, jnp.zeros((aff.shape[0], 1), jnp.int32))
    thr = pltpu.bitcast(bits, F32)
    gt = aff > thr
    eq = aff == thr
    n_gt = jnp.sum(jnp.where(gt, 1.0, 0.0), axis=-1, keepdims=True)
    eqf = jnp.where(eq, 1.0, 0.0)
    rank_eq = _lane_cumsum(eqf) - eqf
    sel = gt | (eq & (rank_eq < capf - n_gt))
    self_ = jnp.where(sel, 1.0, 0.0)
    pos = _lane_cumsum(self_) - self_
    ps_ref[...] = jnp.where(sel, pos, -1.0).astype(jnp.int32)
    cb_ref[...] = _dot(self_.astype(BF16), u_ref[...]).astype(jnp.int32)


def ec_select(aff_t, cap):
    b, e, s = aff_t.shape
    nt = s // TOKEN_TILE
    assert nt + 1 <= LANES
    tok = np.arange(s)[:, None]
    u = jnp.asarray(tok < (np.arange(LANES)[None, :] * TOKEN_TILE), BF16)
    return pl.pallas_call(
        functools.partial(_select_kernel, cap=cap),
        grid=(b,),
        in_specs=[pl.BlockSpec((None, e, s), lambda i: (i, 0, 0)),
                  pl.BlockSpec((s, LANES), lambda i: (0, 0))],
        out_specs=[pl.BlockSpec((None, e, s), lambda i: (i, 0, 0)),
                   pl.BlockSpec((None, e, LANES), lambda i: (i, 0, 0))],
        out_shape=[jax.ShapeDtypeStruct((b, e, s), jnp.int32),
                   jax.ShapeDtypeStruct((b, e, LANES), jnp.int32)],
        compiler_params=_params("parallel"),
        name="ec_select",
    )(aff_t, u)


def _gather_kernel(cb_ref, ps_ref, ht_ref, o_ref, *, n_tiles, n_blocks, block, chunk):
    bi, ei = pl.program_id(0), pl.program_id(1)
    base = (bi * N_EXPERTS + ei) * LANES
    d, s = ht_ref.shape
    slot = lax.broadcasted_iota(jnp.int32, (block, chunk), 0)
    tok = lax.broadcasted_iota(jnp.int32, (1, chunk), 1)

    def one_block(j, carry):
        first = j * block

        def bounds(t, c):
            return (c[0] + (cb_ref[base + t + 1] <= first).astype(jnp.int32),
                    c[1] + (cb_ref[base + t] < first + block).astype(jnp.int32))

        t_lo, t_hi = lax.fori_loop(0, n_tiles, bounds, (jnp.int32(0), jnp.int32(0)))
        lo_tok = t_lo * TOKEN_TILE

        def part(i, acc):
            lower = lo_tok + i * chunk
            start = pl.multiple_of(jnp.minimum(lower, s - chunk), TOKEN_TILE)
            ps = jnp.where(tok + start >= lower, ps_ref[:, pl.ds(start, chunk)], -1)
            onehot = jnp.where(ps == slot + first, 1.0, 0.0).astype(BF16)
            return acc + _dot_nt(ht_ref[:, pl.ds(start, chunk)], onehot)

        n_parts = ((t_hi - t_lo) * TOKEN_TILE + chunk - 1) // chunk
        acc = lax.fori_loop(0, n_parts, part, jnp.zeros((d, block), F32))
        o_ref[pl.ds(pl.multiple_of(first, block), block), :] = acc.T.astype(o_ref.dtype)
        return carry

    lax.fori_loop(0, n_blocks, one_block, 0)


def ec_gather(ht, ps, cb, cap):
    b, d, s = ht.shape
    e = ps.shape[1]
    nt = s // TOKEN_TILE
    block = min(2 * SLOT_BLOCK, cap)
    chunk = min(block * s // cap + 2 * TOKEN_TILE, s)
    grid_spec = pltpu.PrefetchScalarGridSpec(
        num_scalar_prefetch=1,
        grid=(b, e),
        in_specs=[pl.BlockSpec((None, None, 1, s), lambda i, j, cb: (i, j, 0, 0)),
                  pl.BlockSpec((None, d, s), lambda i, j, cb: (i, 0, 0))],
        out_specs=pl.BlockSpec((None, None, cap, d), lambda i, j, cb: (i, j, 0, 0)),
    )
    return pl.pallas_call(
        functools.partial(_gather_kernel, n_tiles=nt, n_blocks=cap // block, block=block, chunk=chunk),
        grid_spec=grid_spec,
        out_shape=jax.ShapeDtypeStruct((b, e, cap, d), BF16),
        compiler_params=_params("parallel", "arbitrary"),
        name="ec_gather",
    )(cb.reshape(-1), ps.reshape(b, e, 1, s), ht)


def _ffn_kernel(x_ref, wg_ref, wu_ref, wd_ref, o_ref, acc_ref, *, n_f):
    f = pl.program_id(1)

    @pl.when(f == 0)
    def _():
        acc_ref[...] = jnp.zeros_like(acc_ref)

    bsz, cap, d = x_ref.shape
    x = x_ref[...].reshape(bsz * cap, d)
    a = _dot(x, wg_ref[...].astype(BF16))
    up = _dot(x, wu_ref[...].astype(BF16))
    act = (a * _sigmoid(a) * up).astype(BF16)
    acc_ref[...] += _dot(act, wd_ref[...].astype(BF16))

    @pl.when(f == n_f - 1)
    def _():
        o_ref[...] = acc_ref[...].reshape(bsz, cap, d).astype(o_ref.dtype)


def expert_ffn(xe, w_gate, w_up, w_down, layer):
    b, e, cap, d = xe.shape
    ff = w_gate.shape[-1]
    tf = min(512, ff)
    n_f = ff // tf
    return pl.pallas_call(
        functools.partial(_ffn_kernel, n_f=n_f),
        grid=(e, n_f),
        in_specs=[pl.BlockSpec((b, None, cap, d), lambda j, f: (0, j, 0, 0)),
                  pl.BlockSpec((None, None, d, tf), lambda j, f: (layer, j, 0, f)),
                  pl.BlockSpec((None, None, d, tf), lambda j, f: (layer, j, 0, f)),
                  pl.BlockSpec((None, None, tf, d), lambda j, f: (layer, j, f, 0))],
        out_specs=pl.BlockSpec((b, None, cap, d), lambda j, f: (0, j, 0, 0)),
        out_shape=jax.ShapeDtypeStruct((b, e, cap, d), BF16),
        scratch_shapes=[pltpu.VMEM((b * cap, d), F32)],
        compiler_params=_params("parallel", "arbitrary"),
        name="expert_ffn",
    )(xe, w_gate, w_up, w_down)


def _scatter_kernel(cb_ref, ps_ref, aff_ref, ye_ref, x_ref, pg_ref, g2_ref, o_ref, acc_ref,
                    *, n_sub, window):
    bi, ti, ei = pl.program_id(0), pl.program_id(1), pl.program_id(2)

    @pl.when(ei == 0)
    def _():
        acc_ref[...] = jnp.zeros_like(acc_ref)

    tt = acc_ref.shape[0]
    base = (bi * N_EXPERTS + ei) * LANES + ti * n_sub
    lo = cb_ref[base]
    hi = cb_ref[base + n_sub]

    @pl.when(hi > lo)
    def _():
        ps = ps_ref[...]
        gate = aff_ref[...]
        slot = lax.broadcasted_iota(jnp.int32, (window, tt), 0)
        cap = ye_ref.shape[0]
        first = (lo // BF16_ROWS) * BF16_ROWS

        def body(j, carry):
            lower = first + j * window
            start = pl.multiple_of(jnp.minimum(lower, cap - window), BF16_ROWS)
            hit = jnp.where(ps >= lower, ps, -1) == slot + start
            onehot = jnp.where(hit, 1.0, 0.0).astype(BF16)
            gate_slot = jnp.sum(jnp.where(hit, gate, 0.0), axis=1, keepdims=True)
            ye = (ye_ref[pl.ds(start, window), :].astype(F32) * gate_slot).astype(BF16)
            acc_ref[...] += lax.dot_general(onehot, ye, (((0,), (0,)), ((), ())),
                                            preferred_element_type=F32)
            return carry

        lax.fori_loop(0, (hi - first + window - 1) // window, body, 0)

    @pl.when(ei == N_EXPERTS - 1)
    def _():
        y = acc_ref[...]
        o_ref[...] = x_ref[...] + g2_ref[...] * (_rms(y, y.shape[-1]) * pg_ref[...])


def ec_scatter(ye, ps, aff_t, cb, x, post_g, g2):
    b, e, cap, d = ye.shape
    s = x.shape[1]
    n_sub = min(4, s // TOKEN_TILE)
    tt = n_sub * TOKEN_TILE
    grid_spec = pltpu.PrefetchScalarGridSpec(
        num_scalar_prefetch=1,
        grid=(b, s // tt, e),
        in_specs=[pl.BlockSpec((None, None, 1, tt), lambda i, t, j, cb: (i, j, 0, t)),
                  pl.BlockSpec((None, None, 1, tt), lambda i, t, j, cb: (i, j, 0, t)),
                  pl.BlockSpec((None, None, cap, d), lambda i, t, j, cb: (i, j, 0, 0)),
                  pl.BlockSpec((None, tt, d), lambda i, t, j, cb: (i, t, 0)),
                  pl.BlockSpec((1, d), lambda i, t, j, cb: (0, 0)),
                  pl.BlockSpec((None, 1, d), lambda i, t, j, cb: (i, 0, 0))],
        out_specs=pl.BlockSpec((None, tt, d), lambda i, t, j, cb: (i, t, 0)),
        scratch_shapes=[pltpu.VMEM((tt, d), F32)],
    )
    return pl.pallas_call(
        functools.partial(_scatter_kernel, n_sub=n_sub, window=min(2 * SLOT_BLOCK, cap)),
        grid_spec=grid_spec,
        out_shape=jax.ShapeDtypeStruct((b, s, d), F32),
        compiler_params=_params("parallel", "parallel", "arbitrary"),
        name="ec_scatter",
    )(cb.reshape(-1), ps.reshape(b, e, 1, s), aff_t.reshape(b, e, 1, s), ye, x, post_g.reshape(1, d),
      g2.reshape(b, 1, d))


def kernel(x, c, positions, ada_w, ada_b, mix_pre_g, mix_post_g, ffn_pre_g, ffn_post_g, w_in, ml_gate_b, mla_q_norm, mla_kv_norm, mla_w_uq, mla_w_uk, mla_w_uv, hy_conv_w, hy_conv_b, hy_w1, hy_b1, hy_fr1, hy_w2, hy_b2, hy_fr2, hy_w3, hy_b3, hy_log_decay, hy_bias, sc_conv_w, mix_out_g, w_out, router_w, exp_w_gate, exp_w_up, exp_w_down):
    depth = ada_w.shape[0]
    b, s, d = x.shape
    cap = EC_CAPACITY * s // N_EXPERTS
    mod = ada_mod(c, ada_w, ada_b)
    tables = rope_tables(positions)
    for l in range(depth):
        sh1, sc1, g1, sh2, sc2, g2 = (mod[l, :, i * d:(i + 1) * d] for i in range(6))
        u = in_proj(x, mix_pre_g[l], sc1, sh1, w_in[l])
        hf, hb = mlstm(u["q"], u["v"], u["kT"], u["gT"], u["g"], ml_gate_b[l])
        qa, ka, va = mla_proj(u["cq"], u["ckv"], u["kr"], tables, mla_q_norm[l], mla_kv_norm[l],
                              mla_w_uq[l], mla_w_uk[l], mla_w_uv[l])
        y_b = flash_attention(qa, ka, va)
        x1, x2, z, y_d = conv_mixers(u["hy"], u["sc"], hy_conv_w[l], hy_conv_b[l], sc_conv_w[l])
        y_c = hyena_mixer(x1, x2, z, hy_w1[l], hy_b1[l], hy_fr1[l], hy_w2[l], hy_b2[l], hy_fr2[l],
                          hy_w3[l], hy_b3[l], hy_log_decay[l], hy_bias[l])
        xn, h2, aff_t = out_proj(x, hf, hb, u["o"], y_b, y_c, y_d, mix_out_g[l], w_out[l],
                                      mix_post_g[l], g1, ffn_pre_g[l], sc2, sh2, router_w[l])
        ps, cb = ec_select(aff_t, cap)
        xe = ec_gather(h2, ps, cb, cap)
        ye = expert_ffn(xe, exp_w_gate, exp_w_up, exp_w_down, l)
        x = ec_scatter(ye, ps, aff_t, cb, xn, ffn_post_g[l], g2)
    return x
```

```python
import functools
import math

import numpy as np
import jax
import jax.numpy as jnp
from jax import lax
from jax.experimental import pallas as pl
from jax.experimental.pallas import tpu as pltpu

F32 = jnp.float32
BF16 = jnp.bfloat16
HIGHEST = lax.Precision.HIGHEST

GROUP_W = 256
HEAD_DIM = 64
N_HEADS = 4
ML_CHUNK = 128
MLA_Q_RANK = 224
MLA_KV_RANK = 128
MLA_NOPE = 64
MLA_ROPE = 32
ROPE_HALF = MLA_ROPE // 2
ROPE_THETA = 10000.0
HY_BANDS = 8
HY_FFN = 64
N_EXPERTS = 16
EC_CAPACITY = 2
NORM_EPS = 1e-6
LANES = 128
BF16_ROWS = 16
V_ROWS = HEAD_DIM + BF16_ROWS
FLASH_HEADS = 2
FLASH_Q_TILES = 2
SLOT_BLOCK = 128
TOKEN_TILE = 256
VMEM_LIMIT = 56 * 1024 * 1024


def _params(*sem):
    return pltpu.CompilerParams(dimension_semantics=sem, vmem_limit_bytes=VMEM_LIMIT)


def _dot(a, b):
    return jnp.dot(a, b, preferred_element_type=F32)


def _dot_hi(a, b):
    return jnp.dot(a, b, precision=HIGHEST, preferred_element_type=F32)


def _dot_nt(a, b):
    return lax.dot_general(a, b, (((1,), (1,)), ((), ())), preferred_element_type=F32)


def _dot_nt_hi(a, b):
    return lax.dot_general(a, b, (((1,), (1,)), ((), ())), precision=HIGHEST,
                           preferred_element_type=F32)


def _split2(x):
    hi = x.astype(BF16)
    lo = (x - hi.astype(F32)).astype(BF16)
    return hi, lo


def _split3(x):
    hi = x.astype(BF16)
    r = x - hi.astype(F32)
    mid = r.astype(BF16)
    lo = (r - mid.astype(F32)).astype(BF16)
    return hi, mid, lo


def _lhs_x3(f):
    hi, lo = _split2(f)
    return jnp.concatenate([hi, lo, hi], axis=1)


def _dot_x3(f3, x):
    hi, lo = _split2(x)
    return _dot(f3, jnp.concatenate([hi, hi, lo], axis=0))


def _dot3(a, w_hi, w_lo):
    a_hi, a_lo = _split2(a)
    return _dot(a_hi, w_hi) + _dot(a_hi, w_lo) + _dot(a_lo, w_hi)


def _rms(x, n):
    ms = jnp.sum(x * x, axis=-1, keepdims=True) * (1.0 / n)
    return x * lax.rsqrt(ms + NORM_EPS)


def _log_sigmoid(x):
    return jnp.minimum(x, 0.0) - jnp.log(1.0 + jnp.exp(-jnp.abs(x)))


def _sigmoid(x):
    return 1.0 / (1.0 + jnp.exp(-x))


def _ada_kernel(c_ref, w_ref, b_ref, o_ref):
    c = c_ref[...]
    cs = c * _sigmoid(c)
    o_ref[...] = _dot_hi(cs, w_ref[...]) + b_ref[...]


def ada_mod(c, ada_w, ada_b):
    depth, d, n6 = ada_w.shape
    b = c.shape[0]
    bp = 8
    cp = jnp.zeros((bp, d), F32).at[:b].set(c)
    tn = 1536
    out = pl.pallas_call(
        _ada_kernel,
        grid=(depth, n6 // tn),
        in_specs=[pl.BlockSpec((bp, d), lambda l, j: (0, 0)),
                  pl.BlockSpec((None, d, tn), lambda l, j: (l, 0, j)),
                  pl.BlockSpec((None, 1, tn), lambda l, j: (l, 0, j))],
        out_specs=pl.BlockSpec((None, bp, tn), lambda l, j: (l, 0, j)),
        out_shape=jax.ShapeDtypeStruct((depth, bp, n6), F32),
        compiler_params=_params("parallel", "parallel"),
        name="ada_mod",
    )(cp, ada_w, ada_b.reshape(depth, 1, n6))
    return out[:, :b]


_U_COLS = (("q", 256), ("v", 256), ("o", 256), ("g", 128), ("cq", 256), ("ckv", 128),
           ("kr", 128), ("hy", 768), ("sc", 768))
_U_TOTAL = sum(w for _, w in _U_COLS)
_UT_ROWS = 256 + 16


def _inproj_kernel(x_ref, gain_ref, sc_ref, sh_ref, w_ref, wt_ref,
                   q_ref, v_ref, o_ref, g_ref, cq_ref, ckv_ref, kr_ref, hy_ref, scu_ref,
                   kt_ref, gt_ref):
    x = x_ref[...]
    d = x.shape[-1]
    h = _rms(x, d) * gain_ref[...] * (1.0 + sc_ref[...]) + sh_ref[...]
    hb = h.astype(BF16)
    u = _dot(hb, w_ref[...])
    off = 0
    for ref, (_, width) in zip((q_ref, v_ref, o_ref, g_ref, cq_ref, ckv_ref, kr_ref, hy_ref, scu_ref),
                               _U_COLS):
        ref[...] = u[:, off:off + width]
        off += width
    ut = _dot_nt(wt_ref[...], hb)
    kt_ref[...] = ut[:256]
    gt_ref[...] = ut[256:]


def in_proj(x, gain, scale, shift, w_in):
    b, s, d = x.shape
    tm = min(512, s)
    cuts = np.cumsum([0, 256, 256, 256, 256, 16, MLA_Q_RANK, MLA_KV_RANK, MLA_ROPE, 768, 768])
    wq, wk, wv, wo, wg, wcq, wckv, wkr, why, wsc = (w_in[:, cuts[i]:cuts[i + 1]] for i in range(10))
    pad = lambda w, n: jnp.pad(w, ((0, 0), (0, n - w.shape[1])))
    wkr_p = jnp.pad(wkr, ((0, 0), (MLA_NOPE, LANES - MLA_NOPE - MLA_ROPE)))
    w1 = jnp.concatenate([wq, wv, wo, pad(wg, 128), pad(wcq, 256), wckv, wkr_p, why, wsc],
                         axis=1).astype(BF16)
    w2t = jnp.concatenate([wk, wg], axis=1).T.astype(BF16)
    row = lambda nm, w: pl.BlockSpec((None, tm, w), lambda i, j: (i, j, 0))
    out_shapes = [jax.ShapeDtypeStruct((b, s, w), F32) for _, w in _U_COLS]
    out_shapes += [jax.ShapeDtypeStruct((b, 256, s), F32), jax.ShapeDtypeStruct((b, 16, s), F32)]
    out_specs = [row(nm, w) for nm, w in _U_COLS]
    out_specs += [pl.BlockSpec((None, 256, tm), lambda i, j: (i, 0, j)),
                  pl.BlockSpec((None, 16, tm), lambda i, j: (i, 0, j))]
    vec = lambda: pl.BlockSpec((None, 1, d), lambda i, j: (i, 0, 0))
    outs = pl.pallas_call(
        _inproj_kernel,
        grid=(b, s // tm),
        in_specs=[pl.BlockSpec((None, tm, d), lambda i, j: (i, j, 0)),
                  pl.BlockSpec((1, d), lambda i, j: (0, 0)),
                  vec(), vec(),
                  pl.BlockSpec((d, _U_TOTAL), lambda i, j: (0, 0)),
                  pl.BlockSpec((_UT_ROWS, d), lambda i, j: (0, 0))],
        out_specs=out_specs,
        out_shape=out_shapes,
        compiler_params=_params("parallel", "parallel"),
        name="in_proj",
    )(x, gain.reshape(1, d), scale.reshape(b, 1, d), shift.reshape(b, 1, d), w1, w2t)
    names = [nm for nm, _ in _U_COLS] + ["kT", "gT"]
    return dict(zip(names, outs))


def _mlstm_dir(q, v, kt, gt, gc, c_ref, m_ref, base, rev):
    L = q.shape[0]
    r = lax.broadcasted_iota(jnp.int32, (L, L), 0)
    c = lax.broadcasted_iota(jnp.int32, (L, L), 1)
    tri = (c >= r) if rev else (c <= r)
    io, fo = (8, 12) if rev else (0, 4)
    logf_rows = _log_sigmoid(gt[fo:fo + 4, :])
    logf_cols = _log_sigmoid(gc)
    tri_b = jnp.where(tri, 1.0, 0.0).astype(BF16)
    a_cols = _dot(jnp.concatenate([tri_b] * 3, axis=1),
                  jnp.concatenate(_split3(logf_cols), axis=0))
    tri_t = (r >= c) if rev else (r <= c)
    tri_tb = jnp.where(tri_t, 1.0, 0.0).astype(BF16)
    a_rows = _dot(jnp.concatenate(_split3(_log_sigmoid(gt)), axis=1),
                  jnp.concatenate([tri_tb] * 3, axis=0))[fo:fo + 4, :]
    lane = lax.broadcasted_iota(jnp.int32, (L, LANES), 1)
    row128 = lax.broadcasted_iota(jnp.int32, (LANES, L), 0)
    scale = HEAD_DIM ** -0.5
    outs = []
    for pair in range(2):
        qp = q[:, pair * LANES:(pair + 1) * LANES]
        vp = v[:, pair * LANES:(pair + 1) * LANES]
        ktp = kt[pair * LANES:(pair + 1) * LANES, :] * scale
        pair_out = None
        for sub in range(2):
            h = pair * 2 + sub
            in_head = (row128 >= sub * HEAD_DIM) & (row128 < (sub + 1) * HEAD_DIM)
            kth = jnp.where(in_head, ktp, 0.0)
            own = (lane < HEAD_DIM) if sub == 0 else (lane >= HEAD_DIM)
            ncol = HEAD_DIM if sub == 0 else 0
            v_aug = jnp.where(own, vp, jnp.where(lane == ncol, 1.0, 0.0))
            v_aug_b = v_aug.astype(BF16)
            a_c = a_cols[:, fo + h:fo + h + 1]
            a_r = a_rows[h:h + 1, :]
            ig_r = gt[io + h:io + h + 1, :]
            a_end = jnp.sum(logf_rows[h:h + 1, :], axis=-1, keepdims=True)
            c_st = c_ref[base + h]
            m_st = m_ref[base + h][:, 0:1]
            r_log = jnp.where(tri, ig_r - a_r, -jnp.inf)
            inter = a_c + m_st
            m_t = jnp.maximum(inter, a_c + jnp.max(r_log, axis=-1, keepdims=True))
            qb = qp.astype(BF16)
            p = jnp.exp(r_log + (a_c - m_t)) * _dot(qb, kth.astype(BF16))
            sci = jnp.exp(inter - m_t)
            nd = _dot(p.astype(BF16), v_aug_b) + sci * _dot(qb, c_st.astype(BF16))
            den = nd[:, ncol:ncol + 1]
            out = nd / jnp.maximum(jnp.abs(den), jnp.exp(-m_t))
            w_st = a_end - a_r + ig_r
            m_loc = jnp.max(w_st, axis=-1, keepdims=True)
            ke = kth * jnp.exp(w_st - m_loc)
            c_loc = _dot(ke.astype(BF16), v_aug_b)
            m_new = jnp.maximum(a_end + m_st, m_loc)
            sp = jnp.exp(a_end + m_st - m_new)
            sl = jnp.exp(m_loc - m_new)
            c_ref[base + h] = sp * c_st + sl * c_loc
            m_ref[base + h] = jnp.broadcast_to(m_new, (1, LANES))
            pair_out = out if sub == 0 else jnp.where(lane < HEAD_DIM, pair_out, out)
        outs.append(pair_out)
    return jnp.concatenate(outs, axis=1)


def _mlstm_kernel(qf_ref, vf_ref, ktf_ref, gtf_ref, gcf_ref,
                  qb_ref, vb_ref, ktb_ref, gtb_ref, gcb_ref, brow_ref, bcol_ref,
                  hf_ref, hb_ref, c_ref, m_ref, *, n_sub):
    @pl.when(pl.program_id(1) == 0)
    def _():
        c_ref[...] = jnp.zeros_like(c_ref)
        m_ref[...] = jnp.zeros_like(m_ref)

    bcol = bcol_ref[...]
    brow = brow_ref[...]
    L = ML_CHUNK
    for ci in range(n_sub):
        rf = slice(ci * L, (ci + 1) * L)
        hf_ref[rf, :] = _mlstm_dir(qf_ref[rf, :], vf_ref[rf, :], ktf_ref[:, rf], gtf_ref[:, rf] + bcol,
                                   gcf_ref[rf, :] + brow, c_ref, m_ref, 0, False)
        rb = slice((n_sub - 1 - ci) * L, (n_sub - ci) * L)
        hb_ref[rb, :] = _mlstm_dir(qb_ref[rb, :], vb_ref[rb, :], ktb_ref[:, rb], gtb_ref[:, rb] + bcol,
                                   gcb_ref[rb, :] + brow, c_ref, m_ref, N_HEADS, True)


def mlstm(q, v, kt, gt, g, gate_b):
    b, s, w = q.shape
    n_sub = 2
    L = n_sub * ML_CHUNK
    nc = s // L
    bflat = gate_b.reshape(16)
    brow = jnp.zeros((1, LANES), F32).at[0, :16].set(bflat)
    bcol = bflat.reshape(16, 1)
    fw = lambda i, j: (i, j, 0)
    bw = lambda i, j: (i, nc - 1 - j, 0)
    fwt = lambda i, j: (i, 0, j)
    bwt = lambda i, j: (i, 0, nc - 1 - j)

    def specs(m, mt):
        return [pl.BlockSpec((None, L, w), m), pl.BlockSpec((None, L, w), m),
                pl.BlockSpec((None, w, L), mt), pl.BlockSpec((None, 16, L), mt),
                pl.BlockSpec((None, L, LANES), m)]

    return pl.pallas_call(
        functools.partial(_mlstm_kernel, n_sub=n_sub),
        grid=(b, nc),
        in_specs=specs(fw, fwt) + specs(bw, bwt) + [
            pl.BlockSpec((1, LANES), lambda i, j: (0, 0)),
            pl.BlockSpec((16, 1), lambda i, j: (0, 0))],
        out_specs=[pl.BlockSpec((None, L, w), fw), pl.BlockSpec((None, L, w), bw)],
        out_shape=[jax.ShapeDtypeStruct((b, s, w), F32)] * 2,
        scratch_shapes=[pltpu.VMEM((2 * N_HEADS, LANES, LANES), F32),
                        pltpu.VMEM((2 * N_HEADS, 1, LANES), F32)],
        compiler_params=_params("parallel", "arbitrary"),
        name="mlstm",
    )(q, v, kt, gt, g, q, v, kt, gt, g, brow, bcol)


def _rope_table_kernel(pos_ref, post_ref, inv_ref, invt_ref, cos_ref, sin_ref, cost_ref, sint_ref):
    ang = pos_ref[...].astype(F32) * inv_ref[...]
    cos = jnp.cos(ang)
    sin = jnp.sin(ang)
    cos_ref[...] = cos
    sin_ref[...] = sin
    cost_ref[...] = cos.T
    sint_ref[...] = sin.T


def rope_tables(positions):
    b, s = positions.shape
    tm = min(512, s)
    inv = ROPE_THETA ** (-jnp.arange(ROPE_HALF, dtype=F32) / ROPE_HALF)
    inv_row = jnp.zeros((1, LANES), F32).at[0, MLA_NOPE:MLA_NOPE + ROPE_HALF].set(inv)
    inv_row = inv_row.at[0, MLA_NOPE + ROPE_HALF:MLA_NOPE + MLA_ROPE].set(inv)
    spec = pl.BlockSpec((None, tm, LANES), lambda i, j: (i, j, 0))
    spec_t = pl.BlockSpec((None, LANES, tm), lambda i, j: (i, 0, j))
    return pl.pallas_call(
        _rope_table_kernel,
        grid=(b, s // tm),
        in_specs=[pl.BlockSpec((None, tm, 1), lambda i, j: (i, j, 0)),
                  pl.BlockSpec((None, 1, tm), lambda i, j: (i, 0, j)),
                  pl.BlockSpec((1, LANES), lambda i, j: (0, 0)),
                  pl.BlockSpec((LANES, 1), lambda i, j: (0, 0))],
        out_specs=[spec, spec, spec_t, spec_t],
        out_shape=[jax.ShapeDtypeStruct((b, s, LANES), F32)] * 2
        + [jax.ShapeDtypeStruct((b, LANES, s), F32)] * 2,
        compiler_params=_params("parallel", "parallel"),
        name="rope_tables",
    )(positions.reshape(b, s, 1), positions.reshape(b, 1, s), inv_row, inv_row.reshape(LANES, 1))


def _mla_proj_kernel(cq_ref, ckv_ref, kr_ref, cos_ref, sin_ref, cost_ref, sint_ref, qg_ref, kvg_ref,
                     wqt_ref, wk_ref, wvt_ref, qt_ref, k_ref, vt_ref):
    cos = cos_ref[...]
    sin = sin_ref[...]
    lane = lax.broadcasted_iota(jnp.int32, cos.shape, 1)
    x1 = (lane >= MLA_NOPE) & (lane < MLA_NOPE + ROPE_HALF)
    x2 = (lane >= MLA_NOPE + ROPE_HALF) & (lane < MLA_NOPE + MLA_ROPE)
    kr = kr_ref[...]
    krr = (kr * cos + pltpu.roll(kr, LANES - ROPE_HALF, 1) * jnp.where(x1, -sin, 0.0)
           + pltpu.roll(kr, ROPE_HALF, 1) * jnp.where(x2, sin, 0.0))

    cqn = (_rms(cq_ref[...], MLA_Q_RANK) * qg_ref[...]).astype(BF16)
    ckvn = (_rms(ckv_ref[...], MLA_KV_RANK) * kvg_ref[...]).astype(BF16)
    kn = _dot(ckvn, wk_ref[...])
    for h in range(N_HEADS):
        sl = slice(h * LANES, (h + 1) * LANES)
        k_ref[:, sl] = (kn[:, sl] + krr).astype(BF16)
    vrow = lax.broadcasted_iota(jnp.int32, vt_ref.shape, 0)
    ones_row = jnp.where(vrow % V_ROWS == HEAD_DIM, 1.0, 0.0)
    vt_ref[...] = (_dot_nt(wvt_ref[...], ckvn) + ones_row).astype(BF16)

    cos_t = cost_ref[...]
    sin_t = sint_ref[...]
    row = lax.broadcasted_iota(jnp.int32, cos_t.shape, 0)
    x1t = (row >= MLA_NOPE) & (row < MLA_NOPE + ROPE_HALF)
    x2t = (row >= MLA_NOPE + ROPE_HALF) & (row < MLA_NOPE + MLA_ROPE)
    sin_at = jnp.where(x1t, -sin_t, 0.0)
    sin_bt = jnp.where(x2t, sin_t, 0.0)
    qt = _dot_nt(wqt_ref[...], cqn)
    scale = (MLA_NOPE + MLA_ROPE) ** -0.5 * math.log2(math.e)
    for h in range(N_HEADS):
        sl = slice(h * LANES, (h + 1) * LANES)
        xs = qt[sl, :]
        roped = (xs * cos_t + pltpu.roll(xs, LANES - ROPE_HALF, 0) * sin_at
                 + pltpu.roll(xs, ROPE_HALF, 0) * sin_bt)
        qt_ref[sl, :] = (roped * scale).astype(BF16)


def mla_proj(cq, ckv, kr, tables, q_norm, kv_norm, w_uq, w_uk, w_uv):
    b, s, _ = cq.shape
    tm = min(512, s)
    dqk = MLA_NOPE + MLA_ROPE
    cos, sin, cos_t, sin_t = tables
    wq = w_uq.reshape(MLA_Q_RANK, N_HEADS, dqk)
    wq = jnp.pad(wq, ((0, 256 - MLA_Q_RANK), (0, 0), (0, LANES - dqk))).reshape(256, N_HEADS * LANES)
    wk = w_uk.reshape(MLA_KV_RANK, N_HEADS, MLA_NOPE)
    wk = jnp.pad(wk, ((0, 0), (0, 0), (0, LANES - MLA_NOPE))).reshape(MLA_KV_RANK, N_HEADS * LANES)
    qg = jnp.pad(q_norm, (0, 256 - MLA_Q_RANK)).reshape(1, 256)
    kvg = kv_norm.reshape(1, MLA_KV_RANK)
    row = lambda w: pl.BlockSpec((None, tm, w), lambda i, j: (i, j, 0))
    col = lambda w: pl.BlockSpec((None, w, tm), lambda i, j: (i, 0, j))
    full = lambda a: pl.BlockSpec(a.shape, lambda i, j: (0,) * a.ndim)
    wv = jnp.pad(w_uv.reshape(MLA_KV_RANK, N_HEADS, HEAD_DIM), ((0, 0), (0, 0), (0, V_ROWS - HEAD_DIM)))
    wqt, wkb = wq.T.astype(BF16), wk.astype(BF16)
    wvt = wv.reshape(MLA_KV_RANK, N_HEADS * V_ROWS).T.astype(BF16)
    return pl.pallas_call(
        _mla_proj_kernel,
        grid=(b, s // tm),
        in_specs=[row(256), row(128), row(128), row(128), row(128), col(128), col(128),
                  full(qg), full(kvg), full(wqt), full(wkb), full(wvt)],
        out_specs=[col(512), row(512), col(N_HEADS * V_ROWS)],
        out_shape=[jax.ShapeDtypeStruct((b, 512, s), BF16), jax.ShapeDtypeStruct((b, s, 512), BF16),
                   jax.ShapeDtypeStruct((b, N_HEADS * V_ROWS, s), BF16)],
        compiler_params=_params("parallel", "parallel"),
        name="mla_proj",
    )(cq, ckv, kr, cos, sin, cos_t, sin_t, qg, kvg, wqt, wkb, wvt)


def _flash_kernel(qt_ref, k_ref, vt_ref, o_ref, *bufs, tk, tq, nh):
    for qi in range(qt_ref.shape[1] // tq):
        cols = slice(qi * tq, (qi + 1) * tq)
        _flash_tile(qt_ref, cols, k_ref, vt_ref, o_ref, bufs[4 * qi:4 * qi + 4], tk=tk, tq=tq, nh=nh)


def _flash_tile(qt_ref, cols, k_ref, vt_ref, o_ref, bufs, *, tk, tq, nh):
    s = k_ref.shape[0]
    nk = s // tk
    sc_bufs = bufs[:2]
    p_bufs = bufs[2:]

    def scores(t, par):
        off = pl.multiple_of(t * tk, tk)
        for sub in range(nh):
            sc_bufs[par][sub] = _dot(k_ref[pl.ds(off, tk), sub * LANES:(sub + 1) * LANES],
                                     qt_ref[sub * LANES:(sub + 1) * LANES, cols])

    def soften(par, ms):
        m_out, alphas = [], []
        for sub in range(nh):
            sc = sc_bufs[par][sub]
            m_new = jnp.maximum(ms[sub], jnp.max(sc, axis=0, keepdims=True))
            alphas.append(jnp.exp2(ms[sub] - m_new))
            p_bufs[par][sub] = jnp.exp2((sc - m_new).astype(BF16))
            m_out.append(m_new)
        return m_out, alphas

    def accumulate(t, par, alphas, accs):
        off = pl.multiple_of(t * tk, tk)
        new = []
        for sub in range(nh):
            vt = vt_ref[sub * V_ROWS:(sub + 1) * V_ROWS, pl.ds(off, tk)]
            new.append(alphas[sub] * accs[sub] + _dot(vt, p_bufs[par][sub]))
        return new

    def step(t, par, ms, alphas, accs, with_scores=True):
        if with_scores:
            scores(t + 2, par)
        ms, alphas_next = soften(1 - par, ms)
        accs = accumulate(t, par, alphas, accs)
        return ms, alphas_next, accs

    def body(u, carry):
        ms, alphas, accs = carry
        ms, alphas, accs = step(2 * u, 0, ms, alphas, accs)
        ms, alphas, accs = step(2 * u + 1, 1, ms, alphas, accs)
        return ms, alphas, accs

    scores(0, 0)
    scores(1, 1)
    ms, alphas = soften(0, [jnp.full((1, tq), -jnp.inf, F32) for _ in range(nh)])
    accs = [jnp.zeros((V_ROWS, tq), F32) for _ in range(nh)]
    ms, alphas, accs = lax.fori_loop(0, nk // 2 - 1, body, (ms, alphas, accs))
    ms, alphas, accs = step(nk - 2, 0, ms, alphas, accs, with_scores=False)
    accs = accumulate(nk - 1, 1, alphas, accs)
    out_t = jnp.concatenate([acc[:HEAD_DIM] / acc[HEAD_DIM:HEAD_DIM + 1] for acc in accs],
                            axis=0)
    o_ref[cols, :] = out_t.T


def flash_attention(qt, k, vt):
    b, _, s = qt.shape
    tq = min(256, s)
    tk = min(512, s // 2)
    nh = FLASH_HEADS
    ng = N_HEADS // nh
    nq = FLASH_Q_TILES
    return pl.pallas_call(
        functools.partial(_flash_kernel, tk=tk, tq=tq, nh=nh),
        grid=(b, ng, s // (nq * tq)),
        in_specs=[pl.BlockSpec((None, nh * LANES, nq * tq), lambda i, p, j: (i, p, j)),
                  pl.BlockSpec((None, s, nh * LANES), lambda i, p, j: (i, 0, p)),
                  pl.BlockSpec((None, nh * V_ROWS, s), lambda i, p, j: (i, p, 0))],
        out_specs=pl.BlockSpec((None, nq * tq, nh * HEAD_DIM), lambda i, p, j: (i, j, p)),
        out_shape=jax.ShapeDtypeStruct((b, s, N_HEADS * HEAD_DIM), F32),
        scratch_shapes=([pltpu.VMEM((nh, tk, tq), F32)] * 2 + [pltpu.VMEM((nh, tk, tq), BF16)] * 2) * nq,
        compiler_params=_params("parallel", "parallel", "parallel"),
        name="flash_attention",
    )(qt, k, vt)


def _shifted(x, prev_row, next_row, first, last):
    tm = x.shape[0]
    row = lax.broadcasted_iota(jnp.int32, x.shape, 0)
    prev_row = jnp.where(first, 0.0, prev_row)
    next_row = jnp.where(last, 0.0, next_row)
    xm1 = jnp.where(row == 0, prev_row, pltpu.roll(x, 1, 0))
    xp1 = jnp.where(row == tm - 1, next_row, pltpu.roll(x, tm - 1, 0))
    return xm1, xp1


def _conv_kernel(hy_ref, hyp_ref, hyn_ref, sc_ref, scp_ref, scn_ref, hw_ref, hb_ref, sw_ref,
                 x1_ref, x2_ref, z_ref, yd_ref):
    j = pl.program_id(1)
    first = j == 0
    last = j == pl.num_programs(1) - 1
    x = hy_ref[...]
    xm1, xp1 = _shifted(x, hyp_ref[7:8, :], hyn_ref[0:1, :], first, last)
    hw = hw_ref[...]
    proj = xm1 * hw[0:1] + x * hw[1:2] + xp1 * hw[2:3] + hb_ref[...]
    x1_ref[...] = proj[:, 0:GROUP_W]
    x2_ref[...] = proj[:, GROUP_W:2 * GROUP_W]
    z_ref[...] = proj[:, 2 * GROUP_W:]
    su = sc_ref[...]
    g = GROUP_W
    prod = su[:, g:2 * g] * su[:, 2 * g:]
    pprev = scp_ref[7:8, g:2 * g] * scp_ref[7:8, 2 * g:]
    pnext = scn_ref[0:1, g:2 * g] * scn_ref[0:1, 2 * g:]
    pm1, pp1 = _shifted(prod, pprev, pnext, first, last)
    sw = sw_ref[...]
    yd_ref[...] = su[:, :g] * (pm1 * sw[0:1] + prod * sw[1:2] + pp1 * sw[2:3])


def conv_mixers(hy_u, sc_u, hy_conv_w, hy_conv_b, sc_conv_w):
    b, s, w3 = hy_u.shape
    tm = min(512, s)
    nb8 = s // 8
    r8 = tm // 8
    main = pl.BlockSpec((None, tm, w3), lambda i, j: (i, j, 0))
    prev = pl.BlockSpec((None, 8, w3), lambda i, j: (i, jnp.maximum(j * r8 - 1, 0), 0))
    nxt = pl.BlockSpec((None, 8, w3), lambda i, j: (i, jnp.minimum((j + 1) * r8, nb8 - 1), 0))
    full = lambda a: pl.BlockSpec(a.shape, lambda i, j: (0,) * a.ndim)
    hw = hy_conv_w.T
    hb = hy_conv_b.reshape(1, w3)
    sw = sc_conv_w.T
    out = pl.BlockSpec((None, tm, GROUP_W), lambda i, j: (i, j, 0))
    return pl.pallas_call(
        _conv_kernel,
        grid=(b, s // tm),
        in_specs=[main, prev, nxt, main, prev, nxt, full(hw), full(hb), full(sw)],
        out_specs=[out] * 4,
        out_shape=[jax.ShapeDtypeStruct((b, s, GROUP_W), F32)] * 4,
        compiler_params=_params("parallel", "parallel"),
        name="conv_mixers",
    )(hy_u, hy_u, hy_u, sc_u, sc_u, sc_u, hw, hb, sw)


def _fft_dims(s):
    n = 2 * s
    lg = int(round(math.log2(n)))
    assert 1 << lg == n
    n1 = 1 << ((lg + 1) // 2)
    return n1, n // n1


def _filter_kernel(frow_ref, w1h_ref, w1l_ref, b1_ref, fr1_ref, w2h_ref, w2l_ref, b2_ref, fr2_ref,
                   w3h_ref, w3l_ref, b3_ref, ld_ref, k_ref, norm_ref, *, length):
    i = pl.program_id(0)
    tm = k_ref.shape[0]
    hm = tm // 2
    half = LANES // 2

    def times(first):
        n = first + lax.broadcasted_iota(jnp.int32, (hm, 1), 0)
        tt = jnp.where(n < length, n, 2 * length - 1 - n)
        return tt.astype(F32) / length

    ta, tb = times(i * tm), times(i * tm + hm)
    lane = lax.broadcasted_iota(jnp.int32, (hm, LANES), 1)
    sub = lane % half
    t2 = jnp.where(lane < half, ta, tb)
    ang = t2 * frow_ref[...] + jnp.where(sub > HY_BANDS, 0.5 * math.pi, 0.0)
    z = jnp.where(sub == 0, t2, jnp.where(sub <= 2 * HY_BANDS, jnp.sin(ang), 0.0))
    hid = jnp.sin(fr1_ref[...] * (_dot3(z, w1h_ref[...], w1l_ref[...]) + b1_ref[...]))
    hid = jnp.sin(fr2_ref[...] * (_dot3(hid, w2h_ref[...], w2l_ref[...]) + b2_ref[...]))
    decay = jnp.exp(ld_ref[...])
    total = None
    for part, t in enumerate((ta, tb)):
        filt = ((_dot3(hid, w3h_ref[part], w3l_ref[part]) + b3_ref[...]) * jnp.exp(-t * decay))
        k_ref[part * hm:(part + 1) * hm, :] = filt
        psum = jnp.sum(jnp.abs(filt), axis=0, keepdims=True)
        total = psum if total is None else total + psum

    @pl.when(i == 0)
    def _():
        norm_ref[...] = jnp.zeros_like(norm_ref)

    norm_ref[...] += total


def hyena_filter_taps(length, w1, b1, fr1, w2, b2, fr2, w3, b3, log_decay):
    tm = min(512, length)
    n_half = length // tm
    oc = 2 * GROUP_W
    half = LANES // 2
    assert HY_FFN == half
    bands = jnp.arange(1, HY_BANDS + 1, dtype=F32) * (2.0 * math.pi)
    fhalf = jnp.zeros((half,), F32).at[1:1 + HY_BANDS].set(bands).at[1 + HY_BANDS:1 + 2 * HY_BANDS].set(bands)
    frow = jnp.concatenate([fhalf, fhalf]).reshape(1, LANES)
    w1half = jnp.zeros((half, HY_FFN), F32).at[:1 + 2 * HY_BANDS].set(w1)
    blockdiag = lambda a: jnp.kron(jnp.eye(2, dtype=F32), a)
    w1p, w2p = blockdiag(w1half), blockdiag(w2)
    dup = lambda a: jnp.concatenate([a, a]).reshape(1, LANES)
    bydir = lambda a: jnp.moveaxis(a.reshape(a.shape[0], 2, 2, GROUP_W), 2, 0).reshape(2, a.shape[0], oc)
    w3d, b3d, ldd = bydir(w3), bydir(b3.reshape(1, -1)), bydir(log_decay.reshape(1, -1))
    zeros = jnp.zeros_like(w3d)
    w3p = jnp.stack([jnp.concatenate([w3d, zeros], axis=1), jnp.concatenate([zeros, w3d], axis=1)],
                    axis=1)
    full = lambda a: pl.BlockSpec(a.shape, lambda i: (0,) * a.ndim)
    dirspec = lambda a: pl.BlockSpec((None,) + a.shape[1:], lambda i: (i // n_half,) + (0,) * (a.ndim - 1))
    w1h, w1l = _split2(w1p)
    w2h, w2l = _split2(w2p)
    w3h, w3l = _split2(w3p)
    return pl.pallas_call(
        functools.partial(_filter_kernel, length=length),
        grid=(2 * n_half,),
        in_specs=[full(frow), full(w1h), full(w1l), full(dup(b1)), full(dup(fr1)), full(w2h), full(w2l),
                  full(dup(b2)), full(dup(fr2)), dirspec(w3h), dirspec(w3l), dirspec(b3d), dirspec(ldd)],
        out_specs=[pl.BlockSpec((tm, oc), lambda i: (i, 0)), pl.BlockSpec((1, oc), lambda i: (0, 0))],
        out_shape=[jax.ShapeDtypeStruct((2 * length, oc), F32), jax.ShapeDtypeStruct((1, oc), F32)],
        compiler_params=_params("arbitrary"),
        name="hyena_filter",
    )(frow, w1h, w1l, dup(b1), dup(fr1), w2h, w2l, dup(b2), dup(fr2), w3h, w3l, b3d, ldd)


def _dft_consts(s):
    n1, n2 = _fft_dims(s)
    n = n1 * n2
    kh = n1 // 2 + 1
    kp = -(-kh // 4) * 4
    pad_rows = lambda a: np.concatenate([a, np.zeros((kp - kh, a.shape[1]))], axis=0)
    a1 = 2.0 * np.pi * np.outer(np.arange(kh), np.arange(n1)) / n1
    f1 = np.concatenate([pad_rows(np.cos(a1)), pad_rows(-np.sin(a1))], axis=0)
    a2 = 2.0 * np.pi * np.outer(np.arange(n2), np.arange(n2)) / n2
    c2, s2 = np.cos(a2), np.sin(a2)
    f2_fwd = np.block([[c2, s2], [-s2, c2]])
    f2_inv = np.block([[c2, -s2], [s2, c2]])
    at = 2.0 * np.pi * np.outer(np.arange(kp), np.arange(n2)) / n
    tw = np.stack([np.cos(at), np.sin(at)], axis=0)[..., None]
    m1 = np.arange(n1 // 2)
    a3 = 2.0 * np.pi * np.outer(m1, np.arange(kh)) / n1
    wgt = np.where((np.arange(kh) == 0) | (np.arange(kh) == n1 // 2), 1.0, 2.0) / n
    f3 = np.concatenate([pad_rows((np.cos(a3) * wgt).T).T, pad_rows((-np.sin(a3) * wgt).T).T],
                        axis=1)
    f = lambda a: jnp.asarray(a, F32)
    x3 = lambda a: _lhs_x3(f(a))
    return dict(n1=n1, n2=n2, kh=kh, kp=kp, f1=x3(f1), f1_half=x3(f1[:, :n1 // 2]), f2_fwd=x3(f2_fwd),
                f2_inv=x3(f2_inv), tw=f(tw), f3=x3(f3))


_N2_TILE = 8


def _dft1_kernel(x_ref, f_ref, s_ref, o_ref, xs_ref):
    f = f_ref[...]
    scale = s_ref[...]
    for m in range(_N2_TILE):
        xs_ref[...] = x_ref[:, m, :]
        o_ref[:, m, :] = _dot_x3(f, xs_ref[...] * scale)


def dft_stage1(x, f1, scale_row):
    g, r, n2, ch = x.shape
    m = f1.shape[0]
    return pl.pallas_call(
        _dft1_kernel,
        grid=(g, n2 // _N2_TILE),
        in_specs=[pl.BlockSpec((None, r, _N2_TILE, ch), lambda i, j: (i, 0, j, 0)),
                  pl.BlockSpec(f1.shape, lambda i, j: (0, 0)),
                  pl.BlockSpec((1, ch), lambda i, j: (0, 0))],
        out_specs=pl.BlockSpec((None, m, _N2_TILE, ch), lambda i, j: (i, 0, j, 0)),
        out_shape=jax.ShapeDtypeStruct((g, m, n2, ch), F32),
        scratch_shapes=[pltpu.VMEM((r, ch), F32)],
        compiler_params=_params("parallel", "parallel"),
        name="dft_stage1",
    )(x, f1, scale_row)


def _twiddle(re, im, tc, ts, conj):
    if conj:
        return re * tc - im * ts, im * tc + re * ts
    return re * tc + im * ts, im * tc - re * ts


_K1_STEP = 4


def _dft2_filter_kernel(a_ref, tw_ref, ff_ref, o_ref):
    n2 = a_ref.shape[2]
    for i in range(_K1_STEP):
        br, bi = _twiddle(a_ref[0, i], a_ref[1, i], tw_ref[0, i], tw_ref[1, i], False)
        zz = _dot_x3(ff_ref[...], jnp.concatenate([br, bi], axis=0))
        o_ref[0, i] = zz[:n2]
        o_ref[1, i] = zz[n2:]


def dft_stage2_filter(a, consts):
    _, kp, n2, ch = a.shape
    return pl.pallas_call(
        _dft2_filter_kernel,
        grid=(kp // _K1_STEP,),
        input_output_aliases={0: 0},
        in_specs=[pl.BlockSpec((2, _K1_STEP, n2, ch), lambda k: (0, k, 0, 0)),
                  pl.BlockSpec((2, _K1_STEP, n2, 1), lambda k: (0, k, 0, 0)),
                  pl.BlockSpec(consts["f2_fwd"].shape, lambda k: (0, 0))],
        out_specs=pl.BlockSpec((2, _K1_STEP, n2, ch), lambda k: (0, k, 0, 0)),
        out_shape=jax.ShapeDtypeStruct(a.shape, F32),
        compiler_params=_params("parallel"),
        name="dft_stage2_filter",
    )(a, consts["tw"], consts["f2_fwd"])


def _dft2_conv_kernel(a_ref, tw_ref, kf_ref, ff_ref, fi_ref, o_ref):
    n2 = a_ref.shape[2]
    for i in range(_K1_STEP):
        tc, ts = tw_ref[0, i], tw_ref[1, i]
        br, bi = _twiddle(a_ref[0, i], a_ref[1, i], tc, ts, False)
        zz = _dot_x3(ff_ref[...], jnp.concatenate([br, bi], axis=0))
        zr, zi = zz[:n2], zz[n2:]
        kr, ki = kf_ref[0, i], kf_ref[1, i]
        pr = zr * kr - zi * ki
        pi = zr * ki + zi * kr
        vv = _dot_x3(fi_ref[...], jnp.concatenate([pr, pi], axis=0))
        vr, vi = _twiddle(vv[:n2], vv[n2:], tc, ts, True)
        o_ref[0, i] = vr
        o_ref[1, i] = vi


def dft_stage2_conv(a, kf, order, consts):
    b, _, kp, n2, ch = a.shape
    blk = pl.BlockSpec((None, 2, _K1_STEP, n2, ch), lambda i, k: (i, 0, k, 0, 0))
    mat = pl.BlockSpec(consts["f2_fwd"].shape, lambda i, k: (0, 0))
    return pl.pallas_call(
        _dft2_conv_kernel,
        grid=(b, kp // _K1_STEP),
        input_output_aliases={0: 0},
        in_specs=[blk,
                  pl.BlockSpec((2, _K1_STEP, n2, 1), lambda i, k: (0, k, 0, 0)),
                  pl.BlockSpec((2, _K1_STEP, n2, ch), lambda i, k: (0, k, 0, order)),
                  mat, mat],
        out_specs=blk,
        out_shape=jax.ShapeDtypeStruct(a.shape, F32),
        compiler_params=_params("parallel", "parallel"),
        name="dft_stage2_conv",
    )(a, consts["tw"], kf, consts["f2_fwd"], consts["f2_inv"])


def _dft3_kernel(v_ref, f_ref, gate_ref, z_ref, bias_ref, o_ref, vs_ref, ys_ref):
    f = f_ref[...]
    for m in range(_N2_TILE):
        vs_ref[...] = v_ref[:, m, :]
        ys_ref[:, m, :] = _dot_x3(f, vs_ref[...])
    o_ref[...] = gate_ref[...] * (ys_ref[...] + z_ref[...] * bias_ref[...])


def dft_stage3_gate(v, f3, gate, z, bias_row):
    b, m, n2, ch = v.shape
    r = f3.shape[0]
    row = pl.BlockSpec((None, r, _N2_TILE, ch), lambda i, j: (i, 0, j, 0))
    return pl.pallas_call(
        _dft3_kernel,
        grid=(b, n2 // _N2_TILE),
        in_specs=[pl.BlockSpec((None, m, _N2_TILE, ch), lambda i, j: (i, 0, j, 0)),
                  pl.BlockSpec(f3.shape, lambda i, j: (0, 0)),
                  row, row, pl.BlockSpec((1, ch), lambda i, j: (0, 0))],
        out_specs=row,
        out_shape=jax.ShapeDtypeStruct((b, r, n2, ch), F32),
        scratch_shapes=[pltpu.VMEM((m, ch), F32), pltpu.VMEM((r, _N2_TILE, ch), F32)],
        compiler_params=_params("parallel", "parallel"),
        name="dft_stage3_gate",
    )(v, f3, gate, z, bias_row)


def hyena_mixer(x1, x2, z, hy_w1, hy_b1, hy_fr1, hy_w2, hy_b2, hy_fr2, hy_w3, hy_b3, hy_log_decay,
                hy_bias):
    b, s, ch = z.shape
    consts = _dft_consts(s)
    n1, n2 = consts["n1"], consts["n2"]
    oc = 2 * ch
    taps, norm = hyena_filter_taps(s, hy_w1, hy_b1, hy_fr1, hy_w2, hy_b2, hy_fr2, hy_w3, hy_b3,
                                   hy_log_decay)
    kp = consts["kp"]
    ka = dft_stage1(taps.reshape(1, n1, n2, oc), consts["f1"], 1.0 / norm)
    kf = dft_stage2_filter(ka.reshape(2, kp, n2, oc), consts)
    half = n1 // 2
    view = lambda t: t.reshape(b, half, n2, ch)
    ones = jnp.ones((1, ch), F32)
    f1_half = consts["f1_half"]
    cur = view(z)
    for order, gate in enumerate((x1, x2)):
        a = dft_stage1(cur, f1_half, ones)
        v = dft_stage2_conv(a.reshape(b, 2, kp, n2, ch), kf, order, consts)
        cur = dft_stage3_gate(v.reshape(b, 2 * kp, n2, ch), consts["f3"], view(gate), cur,
                              hy_bias[order].reshape(1, ch))
    return cur.reshape(b, s, ch)


def _outproj_kernel(x_ref, hf_ref, hb_ref, o_ref, yb_ref, yc_ref, yd_ref, og_ref, hm_ref, wout_ref,
                    pg_ref, g1_ref, fg_ref, sc2_ref, sh2_ref, rwt_ref,
                    xn_ref, h2_ref, afft_ref):
    hm = hm_ref[...]
    y_a = _sigmoid(o_ref[...]) * (hf_ref[...] + hb_ref[...])
    acc = None
    for idx, y in enumerate((y_a, yb_ref[...], yc_ref[...], yd_ref[...])):
        ms = _dot(jnp.concatenate(_split2(y * y), axis=1), hm)
        yn = y * lax.rsqrt(ms + NORM_EPS) * og_ref[:, idx * GROUP_W:(idx + 1) * GROUP_W]
        part = _dot(yn.astype(BF16), wout_ref[idx * GROUP_W:(idx + 1) * GROUP_W, :])
        acc = part if acc is None else acc + part
    d = acc.shape[-1]
    xn = x_ref[...] + g1_ref[...] * (_rms(acc, d) * pg_ref[...])
    xn_ref[...] = xn
    h2 = _rms(xn, d) * fg_ref[...] * (1.0 + sc2_ref[...]) + sh2_ref[...]
    h2_ref[...] = h2.T.astype(BF16)
    logits_t = _dot_nt_hi(rwt_ref[...], h2)
    mx = jnp.max(logits_t, axis=0, keepdims=True)
    ex = jnp.exp(logits_t - mx)
    afft_ref[...] = ex / jnp.sum(ex, axis=0, keepdims=True)


def out_proj(x, hf, hb, o, yb, yc, yd, out_g, w_out, post_g, g1, ffn_g, sc2, sh2, router_w):
    b, s, d = x.shape
    tm = min(512, s)
    e = router_w.shape[1]
    hm1 = np.kron(np.eye(GROUP_W // HEAD_DIM), np.ones((HEAD_DIM, HEAD_DIM))) / HEAD_DIM
    hm = jnp.asarray(np.concatenate([hm1, hm1], axis=0), BF16)
    row = lambda w: pl.BlockSpec((None, tm, w), lambda i, j: (i, j, 0))
    full = lambda a: pl.BlockSpec(a.shape, lambda i, j: (0,) * a.ndim)
    vec = lambda: pl.BlockSpec((None, 1, d), lambda i, j: (i, 0, 0))
    r1 = lambda a: a.reshape(1, -1)
    wob = w_out.astype(BF16)
    rwt = router_w.T
    return pl.pallas_call(
        _outproj_kernel,
        grid=(b, s // tm),
        in_specs=[row(d)] + [row(GROUP_W)] * 6 + [full(r1(out_g)), full(hm), full(wob),
                                                  full(r1(post_g)), vec(), full(r1(ffn_g)), vec(), vec(),
                                                  full(rwt)],
        out_specs=[row(d), pl.BlockSpec((None, d, tm), lambda i, j: (i, 0, j)),
                   pl.BlockSpec((None, e, tm), lambda i, j: (i, 0, j))],
        out_shape=[jax.ShapeDtypeStruct((b, s, d), F32), jax.ShapeDtypeStruct((b, d, s), BF16),
                   jax.ShapeDtypeStruct((b, e, s), F32)],
        compiler_params=_params("parallel", "parallel"),
        name="out_proj",
    )(x, hf, hb, o, yb, yc, yd, r1(out_g), hm, wob, r1(post_g), g1.reshape(b, 1, d), r1(ffn_g),
      sc2.reshape(b, 1, d), sh2.reshape(b, 1, d), rwt)


def _lane_cumsum(x):
    n = x.shape[-1]
    lane = lax.broadcasted_iota(jnp.int32, x.shape, x.ndim - 1)
    shift = 1
    while shift < n:
        x = x + jnp.where(lane >= shift, pltpu.roll(x, shift, x.ndim - 1), 0.0)
        shift *= 2
    return x


def _select_kernel(aff_ref, u_ref, ps_ref, cb_ref, *, cap):
    aff = aff_ref[...]
    capf = float(cap)

    def body(i, bits):
        cand = bits | (jnp.int32(1) << (30 - i))
        cnt = jnp.sum(jnp.where(aff >= pltpu.bitcast(cand, F32), 1.0, 0.0), axis=-1, keepdims=True)
        return jnp.where(cnt >= capf, cand, bits)

    bits = lax.fori_loop(0, 31, body, jnp.zeros((aff.shape[0], 1), jnp.int32))
    thr = pltpu.bitcast(bits, F32)
    gt = aff > thr
    eq = aff == thr
    n_gt = jnp.sum(jnp.where(gt, 1.0, 0.0), axis=-1, keepdims=True)
    eqf = jnp.where(eq, 1.0, 0.0)
    rank_eq = _lane_cumsum(eqf) - eqf
    sel = gt | (eq & (rank_eq < capf - n_gt))
    self_ = jnp.where(sel, 1.0, 0.0)
    pos = _lane_cumsum(self_) - self_
    ps_ref[...] = jnp.where(sel, pos, -1.0).astype(jnp.int32)
    cb_ref[...] = _dot(self_.astype(BF16), u_ref[...]).astype(jnp.int32)


def ec_select(aff_t, cap):
    b, e, s = aff_t.shape
    nt = s // TOKEN_TILE
    assert nt + 1 <= LANES
    tok = np.arange(s)[:, None]
    u = jnp.asarray(tok < (np.arange(LANES)[None, :] * TOKEN_TILE), BF16)
    return pl.pallas_call(
        functools.partial(_select_kernel, cap=cap),
        grid=(b,),
        in_specs=[pl.BlockSpec((None, e, s), lambda i: (i, 0, 0)),
                  pl.BlockSpec((s, LANES), lambda i: (0, 0))],
        out_specs=[pl.BlockSpec((None, e, s), lambda i: (i, 0, 0)),
                   pl.BlockSpec((None, e, LANES), lambda i: (i, 0, 0))],
        out_shape=[jax.ShapeDtypeStruct((b, e, s), jnp.int32),
                   jax.ShapeDtypeStruct((b, e, LANES), jnp.int32)],
        compiler_params=_params("parallel"),
        name="ec_select",
    )(aff_t, u)


def _gather_kernel(cb_ref, ps_ref, ht_ref, o_ref, *, n_tiles, n_blocks, block, chunk):
    bi, ei = pl.program_id(0), pl.program_id(1)
    base = (bi * N_EXPERTS + ei) * LANES
    d, s = ht_ref.shape
    slot = lax.broadcasted_iota(jnp.int32, (block, chunk), 0)
    tok = lax.broadcasted_iota(jnp.int32, (1, chunk), 1)

    def one_block(j, carry):
        first = j * block
        t_lo = lax.while_loop(lambda t: (t < n_tiles) & (cb_ref[base + t + 1] <= first),
                              lambda t: t + 1, carry[0])
        t_hi = lax.while_loop(lambda t: (t < n_tiles) & (cb_ref[base + t] < first + block),
                              lambda t: t + 1, carry[1])
        lo_tok = t_lo * TOKEN_TILE

        def part(i, acc):
            lower = lo_tok + i * chunk
            start = pl.multiple_of(jnp.minimum(lower, s - chunk), TOKEN_TILE)
            ps = jnp.where(tok + start >= lower, ps_ref[:, pl.ds(start, chunk)], -1)
            onehot = jnp.where(ps == slot + first, 1.0, 0.0).astype(BF16)
            return acc + _dot_nt(ht_ref[:, pl.ds(start, chunk)], onehot)

        n_parts = ((t_hi - t_lo) * TOKEN_TILE + chunk - 1) // chunk
        acc = lax.fori_loop(0, n_parts, part, jnp.zeros((d, block), F32))
        o_ref[pl.ds(pl.multiple_of(first, block), block), :] = acc.T.astype(o_ref.dtype)
        return t_lo, t_hi

    lax.fori_loop(0, n_blocks, one_block, (jnp.int32(0), jnp.int32(0)))


def ec_gather(ht, ps, cb, cap):
    b, d, s = ht.shape
    e = ps.shape[1]
    nt = s // TOKEN_TILE
    block = min(2 * SLOT_BLOCK, cap)
    chunk = min(block * s // cap + 2 * TOKEN_TILE, s)
    grid_spec = pltpu.PrefetchScalarGridSpec(
        num_scalar_prefetch=1,
        grid=(b, e),
        in_specs=[pl.BlockSpec((None, None, 1, s), lambda i, j, cb: (i, j, 0, 0)),
                  pl.BlockSpec((None, d, s), lambda i, j, cb: (i, 0, 0))],
        out_specs=pl.BlockSpec((None, None, cap, d), lambda i, j, cb: (i, j, 0, 0)),
    )
    return pl.pallas_call(
        functools.partial(_gather_kernel, n_tiles=nt, n_blocks=cap // block, block=block, chunk=chunk),
        grid_spec=grid_spec,
        out_shape=jax.ShapeDtypeStruct((b, e, cap, d), BF16),
        compiler_params=_params("parallel", "arbitrary"),
        name="ec_gather",
    )(cb.reshape(-1), ps.reshape(b, e, 1, s), ht)


def _ffn_kernel(x_ref, wg_ref, wu_ref, wd_ref, o_ref, acc_ref, *, n_f):
    f = pl.program_id(1)

    @pl.when(f == 0)
    def _():
        acc_ref[...] = jnp.zeros_like(acc_ref)

    bsz, cap, d = x_ref.shape
    x = x_ref[...].reshape(bsz * cap, d)
    a = _dot(x, wg_ref[...].astype(BF16))
    up = _dot(x, wu_ref[...].astype(BF16))
    act = (a * _sigmoid(a) * up).astype(BF16)
    acc_ref[...] += _dot(act, wd_ref[...].astype(BF16))

    @pl.when(f == n_f - 1)
    def _():
        o_ref[...] = acc_ref[...].reshape(bsz, cap, d).astype(o_ref.dtype)


def expert_ffn(xe, w_gate, w_up, w_down, layer):
    b, e, cap, d = xe.shape
    ff = w_gate.shape[-1]
    tf = min(512, ff)
    n_f = ff // tf
    return pl.pallas_call(
        functools.partial(_ffn_kernel, n_f=n_f),
        grid=(e, n_f),
        in_specs=[pl.BlockSpec((b, None, cap, d), lambda j, f: (0, j, 0, 0)),
                  pl.BlockSpec((None, None, d, tf), lambda j, f: (layer, j, 0, f)),
                  pl.BlockSpec((None, None, d, tf), lambda j, f: (layer, j, 0, f)),
                  pl.BlockSpec((None, None, tf, d), lambda j, f: (layer, j, f, 0))],
        out_specs=pl.BlockSpec((b, None, cap, d), lambda j, f: (0, j, 0, 0)),
        out_shape=jax.ShapeDtypeStruct((b, e, cap, d), BF16),
        scratch_shapes=[pltpu.VMEM((b * cap, d), F32)],
        compiler_params=_params("parallel", "arbitrary"),
        name="expert_ffn",
    )(xe, w_gate, w_up, w_down)


def _scatter_kernel(cb_ref, ps_ref, aff_ref, ye_ref, x_ref, pg_ref, g2_ref, o_ref, acc_ref,
                    *, n_sub, window):
    bi, ti, ei = pl.program_id(0), pl.program_id(1), pl.program_id(2)

    @pl.when(ei == 0)
    def _():
        acc_ref[...] = jnp.zeros_like(acc_ref)

    tt = acc_ref.shape[0]
    base = (bi * N_EXPERTS + ei) * LANES + ti * n_sub
    lo = cb_ref[base]
    hi = cb_ref[base + n_sub]

    @pl.when(hi > lo)
    def _():
        ps = ps_ref[...]
        gate = aff_ref[...]
        slot = lax.broadcasted_iota(jnp.int32, (window, tt), 0)
        cap = ye_ref.shape[0]
        first = (lo // BF16_ROWS) * BF16_ROWS

        def body(j, carry):
            lower = first + j * window
            start = pl.multiple_of(jnp.minimum(lower, cap - window), BF16_ROWS)
            hit = jnp.where(ps >= lower, ps, -1) == slot + start
            onehot = jnp.where(hit, 1.0, 0.0).astype(BF16)
            gate_slot = jnp.sum(jnp.where(hit, gate, 0.0), axis=1, keepdims=True)
            ye = (ye_ref[pl.ds(start, window), :].astype(F32) * gate_slot).astype(BF16)
            acc_ref[...] += lax.dot_general(onehot, ye, (((0,), (0,)), ((), ())),
                                            preferred_element_type=F32)
            return carry

        lax.fori_loop(0, (hi - first + window - 1) // window, body, 0)

    @pl.when(ei == N_EXPERTS - 1)
    def _():
        y = acc_ref[...]
        o_ref[...] = x_ref[...] + g2_ref[...] * (_rms(y, y.shape[-1]) * pg_ref[...])


def ec_scatter(ye, ps, aff_t, cb, x, post_g, g2):
    b, e, cap, d = ye.shape
    s = x.shape[1]
    n_sub = min(4, s // TOKEN_TILE)
    tt = n_sub * TOKEN_TILE
    grid_spec = pltpu.PrefetchScalarGridSpec(
        num_scalar_prefetch=1,
        grid=(b, s // tt, e),
        in_specs=[pl.BlockSpec((None, None, 1, tt), lambda i, t, j, cb: (i, j, 0, t)),
                  pl.BlockSpec((None, None, 1, tt), lambda i, t, j, cb: (i, j, 0, t)),
                  pl.BlockSpec((None, None, cap, d), lambda i, t, j, cb: (i, j, 0, 0)),
                  pl.BlockSpec((None, tt, d), lambda i, t, j, cb: (i, t, 0)),
                  pl.BlockSpec((1, d), lambda i, t, j, cb: (0, 0)),
                  pl.BlockSpec((None, 1, d), lambda i, t, j, cb: (i, 0, 0))],
        out_specs=pl.BlockSpec((None, tt, d), lambda i, t, j, cb: (i, t, 0)),
        scratch_shapes=[pltpu.VMEM((tt, d), F32)],
    )
    return pl.pallas_call(
        functools.partial(_scatter_kernel, n_sub=n_sub, window=min(2 * SLOT_BLOCK, cap)),
        grid_spec=grid_spec,
        out_shape=jax.ShapeDtypeStruct((b, s, d), F32),
        compiler_params=_params("parallel", "parallel", "arbitrary"),
        name="ec_scatter",
    )(cb.reshape(-1), ps.reshape(b, e, 1, s), aff_t.reshape(b, e, 1, s), ye, x, post_g.reshape(1, d),
      g2.reshape(b, 1, d))


def kernel(x, c, positions, ada_w, ada_b, mix_pre_g, mix_post_g, ffn_pre_g, ffn_post_g, w_in, ml_gate_b, mla_q_norm, mla_kv_norm, mla_w_uq, mla_w_uk, mla_w_uv, hy_conv_w, hy_conv_b, hy_w1, hy_b1, hy_fr1, hy_w2, hy_b2, hy_fr2, hy_w3, hy_b3, hy_log_decay, hy_bias, sc_conv_w, mix_out_g, w_out, router_w, exp_w_gate, exp_w_up, exp_w_down):
    depth = ada_w.shape[0]
    b, s, d = x.shape
    cap = EC_CAPACITY * s // N_EXPERTS
    mod = ada_mod(c, ada_w, ada_b)
    tables = rope_tables(positions)
    for l in range(depth):
        sh1, sc1, g1, sh2, sc2, g2 = (mod[l, :, i * d:(i + 1) * d] for i in range(6))
        u = in_proj(x, mix_pre_g[l], sc1, sh1, w_in[l])
        hf, hb = mlstm(u["q"], u["v"], u["kT"], u["gT"], u["g"], ml_gate_b[l])
        qa, ka, va = mla_proj(u["cq"], u["ckv"], u["kr"], tables, mla_q_norm[l], mla_kv_norm[l],
                              mla_w_uq[l], mla_w_uk[l], mla_w_uv[l])
        y_b = flash_attention(qa, ka, va)
        x1, x2, z, y_d = conv_mixers(u["hy"], u["sc"], hy_conv_w[l], hy_conv_b[l], sc_conv_w[l])
        y_c = hyena_mixer(x1, x2, z, hy_w1[l], hy_b1[l], hy_fr1[l], hy_w2[l], hy_b2[l], hy_fr2[l],
                          hy_w3[l], hy_b3[l], hy_log_decay[l], hy_bias[l])
        xn, h2, aff_t = out_proj(x, hf, hb, u["o"], y_b, y_c, y_d, mix_out_g[l], w_out[l],
                                      mix_post_g[l], g1, ffn_pre_g[l], sc2, sh2, router_w[l])
        ps, cb = ec_select(aff_t, cap)
        xe = ec_gather(h2, ps, cb, cap)
        ye = expert_ffn(xe, exp_w_gate, exp_w_up, exp_w_down, l)
        x = ec_scatter(ye, ps, aff_t, cb, xn, ffn_post_g[l], g2)
    return x
```

```python
import functools
import math

import numpy as np
import jax
import jax.numpy as jnp
from jax import lax
from jax.experimental import pallas as pl
from jax.experimental.pallas import tpu as pltpu

F32 = jnp.float32
BF16 = jnp.bfloat16
HIGHEST = lax.Precision.HIGHEST

GROUP_W = 256
HEAD_DIM = 64
N_HEADS = 4
ML_CHUNK = 128
MLA_Q_RANK = 224
MLA_KV_RANK = 128
MLA_NOPE = 64
MLA_ROPE = 32
ROPE_HALF = MLA_ROPE // 2
ROPE_THETA = 10000.0
HY_BANDS = 8
HY_FFN = 64
N_EXPERTS = 16
EC_CAPACITY = 2
NORM_EPS = 1e-6
LANES = 128
BF16_ROWS = 16
V_ROWS = HEAD_DIM + BF16_ROWS
FLASH_HEADS = 4
FLASH_Q_TILES = 2
SLOT_BLOCK = 128
TOKEN_TILE = 256
VMEM_LIMIT = 56 * 1024 * 1024


def _params(*sem):
    return pltpu.CompilerParams(dimension_semantics=sem, vmem_limit_bytes=VMEM_LIMIT)


def _dot(a, b):
    return jnp.dot(a, b, preferred_element_type=F32)


def _dot_hi(a, b):
    return jnp.dot(a, b, precision=HIGHEST, preferred_element_type=F32)


def _dot_nt(a, b):
    return lax.dot_general(a, b, (((1,), (1,)), ((), ())), preferred_element_type=F32)


def _dot_nt_hi(a, b):
    return lax.dot_general(a, b, (((1,), (1,)), ((), ())), precision=HIGHEST,
                           preferred_element_type=F32)


def _split2(x):
    hi = x.astype(BF16)
    lo = (x - hi.astype(F32)).astype(BF16)
    return hi, lo


def _split3(x):
    hi = x.astype(BF16)
    r = x - hi.astype(F32)
    mid = r.astype(BF16)
    lo = (r - mid.astype(F32)).astype(BF16)
    return hi, mid, lo


def _lhs_x3(f):
    hi, lo = _split2(f)
    return jnp.concatenate([hi, lo, hi], axis=1)


def _dot_x3(f3, x):
    hi, lo = _split2(x)
    return _dot(f3, jnp.concatenate([hi, hi, lo], axis=0))


def _dot3(a, w_hi, w_lo):
    a_hi, a_lo = _split2(a)
    return _dot(a_hi, w_hi) + _dot(a_hi, w_lo) + _dot(a_lo, w_hi)


def _rms(x, n):
    ms = jnp.sum(x * x, axis=-1, keepdims=True) * (1.0 / n)
    return x * lax.rsqrt(ms + NORM_EPS)


def _log_sigmoid(x):
    return jnp.minimum(x, 0.0) - jnp.log(1.0 + jnp.exp(-jnp.abs(x)))


def _sigmoid(x):
    return 1.0 / (1.0 + jnp.exp(-x))


def _ada_kernel(c_ref, w_ref, b_ref, o_ref):
    c = c_ref[...]
    cs = c * _sigmoid(c)
    o_ref[...] = _dot_hi(cs, w_ref[...]) + b_ref[...]


def ada_mod(c, ada_w, ada_b):
    depth, d, n6 = ada_w.shape
    b = c.shape[0]
    bp = 8
    cp = jnp.zeros((bp, d), F32).at[:b].set(c)
    tn = 1536
    out = pl.pallas_call(
        _ada_kernel,
        grid=(depth, n6 // tn),
        in_specs=[pl.BlockSpec((bp, d), lambda l, j: (0, 0)),
                  pl.BlockSpec((None, d, tn), lambda l, j: (l, 0, j)),
                  pl.BlockSpec((None, 1, tn), lambda l, j: (l, 0, j))],
        out_specs=pl.BlockSpec((None, bp, tn), lambda l, j: (l, 0, j)),
        out_shape=jax.ShapeDtypeStruct((depth, bp, n6), F32),
        compiler_params=_params("parallel", "parallel"),
        name="ada_mod",
    )(cp, ada_w, ada_b.reshape(depth, 1, n6))
    return out[:, :b]


_U_COLS = (("q", 256), ("v", 256), ("o", 256), ("g", 128), ("cq", 256), ("ckv", 128),
           ("kr", 128), ("hy", 768), ("sc", 768))
_U_TOTAL = sum(w for _, w in _U_COLS)
_UT_ROWS = 256 + 16


def _inproj_kernel(x_ref, gain_ref, sc_ref, sh_ref, w_ref, wt_ref,
                   q_ref, v_ref, o_ref, g_ref, cq_ref, ckv_ref, kr_ref, hy_ref, scu_ref,
                   kt_ref, gt_ref):
    x = x_ref[...]
    d = x.shape[-1]
    h = _rms(x, d) * gain_ref[...] * (1.0 + sc_ref[...]) + sh_ref[...]
    hb = h.astype(BF16)
    u = _dot(hb, w_ref[...])
    off = 0
    for ref, (_, width) in zip((q_ref, v_ref, o_ref, g_ref, cq_ref, ckv_ref, kr_ref, hy_ref, scu_ref),
                               _U_COLS):
        ref[...] = u[:, off:off + width]
        off += width
    ut = _dot_nt(wt_ref[...], hb)
    kt_ref[...] = ut[:256]
    gt_ref[...] = ut[256:]


def in_proj(x, gain, scale, shift, w_in):
    b, s, d = x.shape
    tm = min(512, s)
    cuts = np.cumsum([0, 256, 256, 256, 256, 16, MLA_Q_RANK, MLA_KV_RANK, MLA_ROPE, 768, 768])
    wq, wk, wv, wo, wg, wcq, wckv, wkr, why, wsc = (w_in[:, cuts[i]:cuts[i + 1]] for i in range(10))
    pad = lambda w, n: jnp.pad(w, ((0, 0), (0, n - w.shape[1])))
    wkr_p = jnp.pad(wkr, ((0, 0), (MLA_NOPE, LANES - MLA_NOPE - MLA_ROPE)))
    w1 = jnp.concatenate([wq, wv, wo, pad(wg, 128), pad(wcq, 256), wckv, wkr_p, why, wsc],
                         axis=1).astype(BF16)
    w2t = jnp.concatenate([wk, wg], axis=1).T.astype(BF16)
    row = lambda nm, w: pl.BlockSpec((None, tm, w), lambda i, j: (i, j, 0))
    out_shapes = [jax.ShapeDtypeStruct((b, s, w), F32) for _, w in _U_COLS]
    out_shapes += [jax.ShapeDtypeStruct((b, 256, s), F32), jax.ShapeDtypeStruct((b, 16, s), F32)]
    out_specs = [row(nm, w) for nm, w in _U_COLS]
    out_specs += [pl.BlockSpec((None, 256, tm), lambda i, j: (i, 0, j)),
                  pl.BlockSpec((None, 16, tm), lambda i, j: (i, 0, j))]
    vec = lambda: pl.BlockSpec((None, 1, d), lambda i, j: (i, 0, 0))
    outs = pl.pallas_call(
        _inproj_kernel,
        grid=(b, s // tm),
        in_specs=[pl.BlockSpec((None, tm, d), lambda i, j: (i, j, 0)),
                  pl.BlockSpec((1, d), lambda i, j: (0, 0)),
                  vec(), vec(),
                  pl.BlockSpec((d, _U_TOTAL), lambda i, j: (0, 0)),
                  pl.BlockSpec((_UT_ROWS, d), lambda i, j: (0, 0))],
        out_specs=out_specs,
        out_shape=out_shapes,
        compiler_params=_params("parallel", "parallel"),
        name="in_proj",
    )(x, gain.reshape(1, d), scale.reshape(b, 1, d), shift.reshape(b, 1, d), w1, w2t)
    names = [nm for nm, _ in _U_COLS] + ["kT", "gT"]
    return dict(zip(names, outs))


def _mlstm_dir(q, v, kt, gt, gc, c_ref, m_ref, base, rev):
    L = q.shape[0]
    r = lax.broadcasted_iota(jnp.int32, (L, L), 0)
    c = lax.broadcasted_iota(jnp.int32, (L, L), 1)
    tri = (c >= r) if rev else (c <= r)
    io, fo = (8, 12) if rev else (0, 4)
    logf_rows = _log_sigmoid(gt[fo:fo + 4, :])
    logf_cols = _log_sigmoid(gc)
    tri_b = jnp.where(tri, 1.0, 0.0).astype(BF16)
    a_cols = _dot(jnp.concatenate([tri_b] * 3, axis=1),
                  jnp.concatenate(_split3(logf_cols), axis=0))
    tri_t = (r >= c) if rev else (r <= c)
    tri_tb = jnp.where(tri_t, 1.0, 0.0).astype(BF16)
    a_rows = _dot(jnp.concatenate(_split3(_log_sigmoid(gt)), axis=1),
                  jnp.concatenate([tri_tb] * 3, axis=0))[fo:fo + 4, :]
    lane = lax.broadcasted_iota(jnp.int32, (L, LANES), 1)
    row128 = lax.broadcasted_iota(jnp.int32, (LANES, L), 0)
    scale = HEAD_DIM ** -0.5
    outs = []
    for pair in range(2):
        qp = q[:, pair * LANES:(pair + 1) * LANES]
        vp = v[:, pair * LANES:(pair + 1) * LANES]
        ktp = kt[pair * LANES:(pair + 1) * LANES, :] * scale
        pair_out = None
        for sub in range(2):
            h = pair * 2 + sub
            in_head = (row128 >= sub * HEAD_DIM) & (row128 < (sub + 1) * HEAD_DIM)
            kth = jnp.where(in_head, ktp, 0.0)
            own = (lane < HEAD_DIM) if sub == 0 else (lane >= HEAD_DIM)
            ncol = HEAD_DIM if sub == 0 else 0
            v_aug = jnp.where(own, vp, jnp.where(lane == ncol, 1.0, 0.0))
            v_aug_b = v_aug.astype(BF16)
            a_c = a_cols[:, fo + h:fo + h + 1]
            a_r = a_rows[h:h + 1, :]
            ig_r = gt[io + h:io + h + 1, :]
            a_end = jnp.sum(logf_rows[h:h + 1, :], axis=-1, keepdims=True)
            c_st = c_ref[base + h]
            m_st = m_ref[base + h][:, 0:1]
            r_log = jnp.where(tri, ig_r - a_r, -jnp.inf)
            inter = a_c + m_st
            m_t = jnp.maximum(inter, a_c + jnp.max(r_log, axis=-1, keepdims=True))
            qb = qp.astype(BF16)
            p = jnp.exp(r_log + (a_c - m_t)) * _dot(qb, kth.astype(BF16))
            sci = jnp.exp(inter - m_t)
            nd = _dot(p.astype(BF16), v_aug_b) + sci * _dot(qb, c_st.astype(BF16))
            den = nd[:, ncol:ncol + 1]
            out = nd / jnp.maximum(jnp.abs(den), jnp.exp(-m_t))
            w_st = a_end - a_r + ig_r
            m_loc = jnp.max(w_st, axis=-1, keepdims=True)
            ke = kth * jnp.exp(w_st - m_loc)
            c_loc = _dot(ke.astype(BF16), v_aug_b)
            m_new = jnp.maximum(a_end + m_st, m_loc)
            sp = jnp.exp(a_end + m_st - m_new)
            sl = jnp.exp(m_loc - m_new)
            c_ref[base + h] = sp * c_st + sl * c_loc
            m_ref[base + h] = jnp.broadcast_to(m_new, (1, LANES))
            pair_out = out if sub == 0 else jnp.where(lane < HEAD_DIM, pair_out, out)
        outs.append(pair_out)
    return jnp.concatenate(outs, axis=1)


def _mlstm_kernel(qf_ref, vf_ref, ktf_ref, gtf_ref, gcf_ref,
                  qb_ref, vb_ref, ktb_ref, gtb_ref, gcb_ref, brow_ref, bcol_ref,
                  hf_ref, hb_ref, c_ref, m_ref, *, n_sub):
    @pl.when(pl.program_id(1) == 0)
    def _():
        c_ref[...] = jnp.zeros_like(c_ref)
        m_ref[...] = jnp.zeros_like(m_ref)

    bcol = bcol_ref[...]
    brow = brow_ref[...]
    L = ML_CHUNK
    for ci in range(n_sub):
        rf = slice(ci * L, (ci + 1) * L)
        hf_ref[rf, :] = _mlstm_dir(qf_ref[rf, :], vf_ref[rf, :], ktf_ref[:, rf], gtf_ref[:, rf] + bcol,
                                   gcf_ref[rf, :] + brow, c_ref, m_ref, 0, False)
        rb = slice((n_sub - 1 - ci) * L, (n_sub - ci) * L)
        hb_ref[rb, :] = _mlstm_dir(qb_ref[rb, :], vb_ref[rb, :], ktb_ref[:, rb], gtb_ref[:, rb] + bcol,
                                   gcb_ref[rb, :] + brow, c_ref, m_ref, N_HEADS, True)


def mlstm(q, v, kt, gt, g, gate_b):
    b, s, w = q.shape
    n_sub = 2
    L = n_sub * ML_CHUNK
    nc = s // L
    bflat = gate_b.reshape(16)
    brow = jnp.zeros((1, LANES), F32).at[0, :16].set(bflat)
    bcol = bflat.reshape(16, 1)
    fw = lambda i, j: (i, j, 0)
    bw = lambda i, j: (i, nc - 1 - j, 0)
    fwt = lambda i, j: (i, 0, j)
    bwt = lambda i, j: (i, 0, nc - 1 - j)

    def specs(m, mt):
        return [pl.BlockSpec((None, L, w), m), pl.BlockSpec((None, L, w), m),
                pl.BlockSpec((None, w, L), mt), pl.BlockSpec((None, 16, L), mt),
                pl.BlockSpec((None, L, LANES), m)]

    return pl.pallas_call(
        functools.partial(_mlstm_kernel, n_sub=n_sub),
        grid=(b, nc),
        in_specs=specs(fw, fwt) + specs(bw, bwt) + [
            pl.BlockSpec((1, LANES), lambda i, j: (0, 0)),
            pl.BlockSpec((16, 1), lambda i, j: (0, 0))],
        out_specs=[pl.BlockSpec((None, L, w), fw), pl.BlockSpec((None, L, w), bw)],
        out_shape=[jax.ShapeDtypeStruct((b, s, w), F32)] * 2,
        scratch_shapes=[pltpu.VMEM((2 * N_HEADS, LANES, LANES), F32),
                        pltpu.VMEM((2 * N_HEADS, 1, LANES), F32)],
        compiler_params=_params("parallel", "arbitrary"),
        name="mlstm",
    )(q, v, kt, gt, g, q, v, kt, gt, g, brow, bcol)


def _rope_table_kernel(pos_ref, post_ref, inv_ref, invt_ref, cos_ref, sin_ref, cost_ref, sint_ref):
    ang = pos_ref[...].astype(F32) * inv_ref[...]
    cos = jnp.cos(ang)
    sin = jnp.sin(ang)
    cos_ref[...] = cos
    sin_ref[...] = sin
    cost_ref[...] = cos.T
    sint_ref[...] = sin.T


def rope_tables(positions):
    b, s = positions.shape
    tm = min(512, s)
    inv = ROPE_THETA ** (-jnp.arange(ROPE_HALF, dtype=F32) / ROPE_HALF)
    inv_row = jnp.zeros((1, LANES), F32).at[0, MLA_NOPE:MLA_NOPE + ROPE_HALF].set(inv)
    inv_row = inv_row.at[0, MLA_NOPE + ROPE_HALF:MLA_NOPE + MLA_ROPE].set(inv)
    spec = pl.BlockSpec((None, tm, LANES), lambda i, j: (i, j, 0))
    spec_t = pl.BlockSpec((None, LANES, tm), lambda i, j: (i, 0, j))
    return pl.pallas_call(
        _rope_table_kernel,
        grid=(b, s // tm),
        in_specs=[pl.BlockSpec((None, tm, 1), lambda i, j: (i, j, 0)),
                  pl.BlockSpec((None, 1, tm), lambda i, j: (i, 0, j)),
                  pl.BlockSpec((1, LANES), lambda i, j: (0, 0)),
                  pl.BlockSpec((LANES, 1), lambda i, j: (0, 0))],
        out_specs=[spec, spec, spec_t, spec_t],
        out_shape=[jax.ShapeDtypeStruct((b, s, LANES), F32)] * 2
        + [jax.ShapeDtypeStruct((b, LANES, s), F32)] * 2,
        compiler_params=_params("parallel", "parallel"),
        name="rope_tables",
    )(positions.reshape(b, s, 1), positions.reshape(b, 1, s), inv_row, inv_row.reshape(LANES, 1))


def _mla_proj_kernel(cq_ref, ckv_ref, kr_ref, cos_ref, sin_ref, cost_ref, sint_ref, qg_ref, kvg_ref,
                     wqt_ref, wk_ref, wvt_ref, qt_ref, k_ref, vt_ref):
    cos = cos_ref[...]
    sin = sin_ref[...]
    lane = lax.broadcasted_iota(jnp.int32, cos.shape, 1)
    x1 = (lane >= MLA_NOPE) & (lane < MLA_NOPE + ROPE_HALF)
    x2 = (lane >= MLA_NOPE + ROPE_HALF) & (lane < MLA_NOPE + MLA_ROPE)
    kr = kr_ref[...]
    krr = (kr * cos + pltpu.roll(kr, LANES - ROPE_HALF, 1) * jnp.where(x1, -sin, 0.0)
           + pltpu.roll(kr, ROPE_HALF, 1) * jnp.where(x2, sin, 0.0))

    cqn = (_rms(cq_ref[...], MLA_Q_RANK) * qg_ref[...]).astype(BF16)
    ckvn = (_rms(ckv_ref[...], MLA_KV_RANK) * kvg_ref[...]).astype(BF16)
    kn = _dot(ckvn, wk_ref[...])
    for h in range(N_HEADS):
        sl = slice(h * LANES, (h + 1) * LANES)
        k_ref[:, sl] = (kn[:, sl] + krr).astype(BF16)
    vrow = lax.broadcasted_iota(jnp.int32, vt_ref.shape, 0)
    ones_row = jnp.where(vrow % V_ROWS == HEAD_DIM, 1.0, 0.0)
    vt_ref[...] = (_dot_nt(wvt_ref[...], ckvn) + ones_row).astype(BF16)

    cos_t = cost_ref[...]
    sin_t = sint_ref[...]
    row = lax.broadcasted_iota(jnp.int32, cos_t.shape, 0)
    x1t = (row >= MLA_NOPE) & (row < MLA_NOPE + ROPE_HALF)
    x2t = (row >= MLA_NOPE + ROPE_HALF) & (row < MLA_NOPE + MLA_ROPE)
    sin_at = jnp.where(x1t, -sin_t, 0.0)
    sin_bt = jnp.where(x2t, sin_t, 0.0)
    qt = _dot_nt(wqt_ref[...], cqn)
    scale = (MLA_NOPE + MLA_ROPE) ** -0.5 * math.log2(math.e)
    for h in range(N_HEADS):
        sl = slice(h * LANES, (h + 1) * LANES)
        xs = qt[sl, :]
        roped = (xs * cos_t + pltpu.roll(xs, LANES - ROPE_HALF, 0) * sin_at
                 + pltpu.roll(xs, ROPE_HALF, 0) * sin_bt)
        qt_ref[sl, :] = (roped * scale).astype(BF16)


def mla_proj(cq, ckv, kr, tables, q_norm, kv_norm, w_uq, w_uk, w_uv):
    b, s, _ = cq.shape
    tm = min(512, s)
    dqk = MLA_NOPE + MLA_ROPE
    cos, sin, cos_t, sin_t = tables
    wq = w_uq.reshape(MLA_Q_RANK, N_HEADS, dqk)
    wq = jnp.pad(wq, ((0, 256 - MLA_Q_RANK), (0, 0), (0, LANES - dqk))).reshape(256, N_HEADS * LANES)
    wk = w_uk.reshape(MLA_KV_RANK, N_HEADS, MLA_NOPE)
    wk = jnp.pad(wk, ((0, 0), (0, 0), (0, LANES - MLA_NOPE))).reshape(MLA_KV_RANK, N_HEADS * LANES)
    qg = jnp.pad(q_norm, (0, 256 - MLA_Q_RANK)).reshape(1, 256)
    kvg = kv_norm.reshape(1, MLA_KV_RANK)
    row = lambda w: pl.BlockSpec((None, tm, w), lambda i, j: (i, j, 0))
    col = lambda w: pl.BlockSpec((None, w, tm), lambda i, j: (i, 0, j))
    full = lambda a: pl.BlockSpec(a.shape, lambda i, j: (0,) * a.ndim)
    wv = jnp.pad(w_uv.reshape(MLA_KV_RANK, N_HEADS, HEAD_DIM), ((0, 0), (0, 0), (0, V_ROWS - HEAD_DIM)))
    wqt, wkb = wq.T.astype(BF16), wk.astype(BF16)
    wvt = wv.reshape(MLA_KV_RANK, N_HEADS * V_ROWS).T.astype(BF16)
    return pl.pallas_call(
        _mla_proj_kernel,
        grid=(b, s // tm),
        in_specs=[row(256), row(128), row(128), row(128), row(128), col(128), col(128),
                  full(qg), full(kvg), full(wqt), full(wkb), full(wvt)],
        out_specs=[col(512), row(512), col(N_HEADS * V_ROWS)],
        out_shape=[jax.ShapeDtypeStruct((b, 512, s), BF16), jax.ShapeDtypeStruct((b, s, 512), BF16),
                   jax.ShapeDtypeStruct((b, N_HEADS * V_ROWS, s), BF16)],
        compiler_params=_params("parallel", "parallel"),
        name="mla_proj",
    )(cq, ckv, kr, cos, sin, cos_t, sin_t, qg, kvg, wqt, wkb, wvt)


def _flash_kernel(qt_ref, k_ref, vt_ref, o_ref, *bufs, tk, tq, nh):
    for qi in range(qt_ref.shape[1] // tq):
        cols = slice(qi * tq, (qi + 1) * tq)
        _flash_tile(qt_ref, cols, k_ref, vt_ref, o_ref, bufs[4 * qi:4 * qi + 4], tk=tk, tq=tq, nh=nh)


def _flash_tile(qt_ref, cols, k_ref, vt_ref, o_ref, bufs, *, tk, tq, nh):
    s = k_ref.shape[0]
    nk = s // tk
    sc_bufs = bufs[:2]
    p_bufs = bufs[2:]

    def scores(t, par):
        off = pl.multiple_of(t * tk, tk)
        for sub in range(nh):
            sc_bufs[par][sub] = _dot(k_ref[pl.ds(off, tk), sub * LANES:(sub + 1) * LANES],
                                     qt_ref[sub * LANES:(sub + 1) * LANES, cols])

    def soften(par, ms):
        m_out, alphas = [], []
        for sub in range(nh):
            sc = sc_bufs[par][sub]
            m_new = jnp.maximum(ms[sub], jnp.max(sc, axis=0, keepdims=True))
            alphas.append(jnp.exp2(ms[sub] - m_new))
            p_bufs[par][sub] = jnp.exp2((sc - m_new).astype(BF16))
            m_out.append(m_new)
        return m_out, alphas

    def accumulate(t, par, alphas, accs):
        off = pl.multiple_of(t * tk, tk)
        new = []
        for sub in range(nh):
            vt = vt_ref[sub * V_ROWS:(sub + 1) * V_ROWS, pl.ds(off, tk)]
            new.append(alphas[sub] * accs[sub] + _dot(vt, p_bufs[par][sub]))
        return new

    def step(t, par, ms, alphas, accs, with_scores=True):
        if with_scores:
            scores(t + 2, par)
        ms, alphas_next = soften(1 - par, ms)
        accs = accumulate(t, par, alphas, accs)
        return ms, alphas_next, accs

    def body(u, carry):
        ms, alphas, accs = carry
        ms, alphas, accs = step(2 * u, 0, ms, alphas, accs)
        ms, alphas, accs = step(2 * u + 1, 1, ms, alphas, accs)
        return ms, alphas, accs

    scores(0, 0)
    scores(1, 1)
    ms, alphas = soften(0, [jnp.full((1, tq), -jnp.inf, F32) for _ in range(nh)])
    accs = [jnp.zeros((V_ROWS, tq), F32) for _ in range(nh)]
    ms, alphas, accs = lax.fori_loop(0, nk // 2 - 1, body, (ms, alphas, accs))
    ms, alphas, accs = step(nk - 2, 0, ms, alphas, accs, with_scores=False)
    accs = accumulate(nk - 1, 1, alphas, accs)
    out_t = jnp.concatenate([acc[:HEAD_DIM] / acc[HEAD_DIM:HEAD_DIM + 1] for acc in accs],
                            axis=0)
    o_ref[cols, :] = out_t.T


def flash_attention(qt, k, vt):
    b, _, s = qt.shape
    tq = min(256, s)
    tk = min(512, s // 2)
    nh = FLASH_HEADS
    ng = N_HEADS // nh
    nq = FLASH_Q_TILES
    return pl.pallas_call(
        functools.partial(_flash_kernel, tk=tk, tq=tq, nh=nh),
        grid=(b, ng, s // (nq * tq)),
        in_specs=[pl.BlockSpec((None, nh * LANES, nq * tq), lambda i, p, j: (i, p, j)),
                  pl.BlockSpec((None, s, nh * LANES), lambda i, p, j: (i, 0, p)),
                  pl.BlockSpec((None, nh * V_ROWS, s), lambda i, p, j: (i, p, 0))],
        out_specs=pl.BlockSpec((None, nq * tq, nh * HEAD_DIM), lambda i, p, j: (i, j, p)),
        out_shape=jax.ShapeDtypeStruct((b, s, N_HEADS * HEAD_DIM), F32),
        scratch_shapes=([pltpu.VMEM((nh, tk, tq), F32)] * 2 + [pltpu.VMEM((nh, tk, tq), BF16)] * 2) * nq,
        compiler_params=_params("parallel", "parallel", "parallel"),
        name="flash_attention",
    )(qt, k, vt)


def _shifted(x, prev_row, next_row, first, last):
    tm = x.shape[0]
    row = lax.broadcasted_iota(jnp.int32, x.shape, 0)
    prev_row = jnp.where(first, 0.0, prev_row)
    next_row = jnp.where(last, 0.0, next_row)
    xm1 = jnp.where(row == 0, prev_row, pltpu.roll(x, 1, 0))
    xp1 = jnp.where(row == tm - 1, next_row, pltpu.roll(x, tm - 1, 0))
    return xm1, xp1


def _conv_kernel(hy_ref, hyp_ref, hyn_ref, sc_ref, scp_ref, scn_ref, hw_ref, hb_ref, sw_ref,
                 x1_ref, x2_ref, z_ref, yd_ref):
    j = pl.program_id(1)
    first = j == 0
    last = j == pl.num_programs(1) - 1
    x = hy_ref[...]
    xm1, xp1 = _shifted(x, hyp_ref[7:8, :], hyn_ref[0:1, :], first, last)
    hw = hw_ref[...]
    proj = xm1 * hw[0:1] + x * hw[1:2] + xp1 * hw[2:3] + hb_ref[...]
    x1_ref[...] = proj[:, 0:GROUP_W]
    x2_ref[...] = proj[:, GROUP_W:2 * GROUP_W]
    z_ref[...] = proj[:, 2 * GROUP_W:]
    su = sc_ref[...]
    g = GROUP_W
    prod = su[:, g:2 * g] * su[:, 2 * g:]
    pprev = scp_ref[7:8, g:2 * g] * scp_ref[7:8, 2 * g:]
    pnext = scn_ref[0:1, g:2 * g] * scn_ref[0:1, 2 * g:]
    pm1, pp1 = _shifted(prod, pprev, pnext, first, last)
    sw = sw_ref[...]
    yd_ref[...] = su[:, :g] * (pm1 * sw[0:1] + prod * sw[1:2] + pp1 * sw[2:3])


def conv_mixers(hy_u, sc_u, hy_conv_w, hy_conv_b, sc_conv_w):
    b, s, w3 = hy_u.shape
    tm = min(512, s)
    nb8 = s // 8
    r8 = tm // 8
    main = pl.BlockSpec((None, tm, w3), lambda i, j: (i, j, 0))
    prev = pl.BlockSpec((None, 8, w3), lambda i, j: (i, jnp.maximum(j * r8 - 1, 0), 0))
    nxt = pl.BlockSpec((None, 8, w3), lambda i, j: (i, jnp.minimum((j + 1) * r8, nb8 - 1), 0))
    full = lambda a: pl.BlockSpec(a.shape, lambda i, j: (0,) * a.ndim)
    hw = hy_conv_w.T
    hb = hy_conv_b.reshape(1, w3)
    sw = sc_conv_w.T
    out = pl.BlockSpec((None, tm, GROUP_W), lambda i, j: (i, j, 0))
    return pl.pallas_call(
        _conv_kernel,
        grid=(b, s // tm),
        in_specs=[main, prev, nxt, main, prev, nxt, full(hw), full(hb), full(sw)],
        out_specs=[out] * 4,
        out_shape=[jax.ShapeDtypeStruct((b, s, GROUP_W), F32)] * 4,
        compiler_params=_params("parallel", "parallel"),
        name="conv_mixers",
    )(hy_u, hy_u, hy_u, sc_u, sc_u, sc_u, hw, hb, sw)


def _fft_dims(s):
    n = 2 * s
    lg = int(round(math.log2(n)))
    assert 1 << lg == n
    n1 = 1 << ((lg + 1) // 2)
    return n1, n // n1


def _filter_kernel(frow_ref, w1h_ref, w1l_ref, b1_ref, fr1_ref, w2h_ref, w2l_ref, b2_ref, fr2_ref,
                   w3h_ref, w3l_ref, b3_ref, ld_ref, k_ref, norm_ref, *, length):
    i = pl.program_id(0)
    tm = k_ref.shape[0]
    hm = tm // 2
    half = LANES // 2

    def times(first):
        n = first + lax.broadcasted_iota(jnp.int32, (hm, 1), 0)
        tt = jnp.where(n < length, n, 2 * length - 1 - n)
        return tt.astype(F32) / length

    ta, tb = times(i * tm), times(i * tm + hm)
    lane = lax.broadcasted_iota(jnp.int32, (hm, LANES), 1)
    sub = lane % half
    t2 = jnp.where(lane < half, ta, tb)
    ang = t2 * frow_ref[...] + jnp.where(sub > HY_BANDS, 0.5 * math.pi, 0.0)
    z = jnp.where(sub == 0, t2, jnp.where(sub <= 2 * HY_BANDS, jnp.sin(ang), 0.0))
    hid = jnp.sin(fr1_ref[...] * (_dot3(z, w1h_ref[...], w1l_ref[...]) + b1_ref[...]))
    hid = jnp.sin(fr2_ref[...] * (_dot3(hid, w2h_ref[...], w2l_ref[...]) + b2_ref[...]))
    decay = jnp.exp(ld_ref[...])
    total = None
    for part, t in enumerate((ta, tb)):
        filt = ((_dot3(hid, w3h_ref[part], w3l_ref[part]) + b3_ref[...]) * jnp.exp(-t * decay))
        k_ref[part * hm:(part + 1) * hm, :] = filt
        psum = jnp.sum(jnp.abs(filt), axis=0, keepdims=True)
        total = psum if total is None else total + psum

    @pl.when(i == 0)
    def _():
        norm_ref[...] = jnp.zeros_like(norm_ref)

    norm_ref[...] += total


def hyena_filter_taps(length, w1, b1, fr1, w2, b2, fr2, w3, b3, log_decay):
    tm = min(512, length)
    n_half = length // tm
    oc = 2 * GROUP_W
    half = LANES // 2
    assert HY_FFN == half
    bands = jnp.arange(1, HY_BANDS + 1, dtype=F32) * (2.0 * math.pi)
    fhalf = jnp.zeros((half,), F32).at[1:1 + HY_BANDS].set(bands).at[1 + HY_BANDS:1 + 2 * HY_BANDS].set(bands)
    frow = jnp.concatenate([fhalf, fhalf]).reshape(1, LANES)
    w1half = jnp.zeros((half, HY_FFN), F32).at[:1 + 2 * HY_BANDS].set(w1)
    blockdiag = lambda a: jnp.kron(jnp.eye(2, dtype=F32), a)
    w1p, w2p = blockdiag(w1half), blockdiag(w2)
    dup = lambda a: jnp.concatenate([a, a]).reshape(1, LANES)
    bydir = lambda a: jnp.moveaxis(a.reshape(a.shape[0], 2, 2, GROUP_W), 2, 0).reshape(2, a.shape[0], oc)
    w3d, b3d, ldd = bydir(w3), bydir(b3.reshape(1, -1)), bydir(log_decay.reshape(1, -1))
    zeros = jnp.zeros_like(w3d)
    w3p = jnp.stack([jnp.concatenate([w3d, zeros], axis=1), jnp.concatenate([zeros, w3d], axis=1)],
                    axis=1)
    full = lambda a: pl.BlockSpec(a.shape, lambda i: (0,) * a.ndim)
    dirspec = lambda a: pl.BlockSpec((None,) + a.shape[1:], lambda i: (i // n_half,) + (0,) * (a.ndim - 1))
    w1h, w1l = _split2(w1p)
    w2h, w2l = _split2(w2p)
    w3h, w3l = _split2(w3p)
    return pl.pallas_call(
        functools.partial(_filter_kernel, length=length),
        grid=(2 * n_half,),
        in_specs=[full(frow), full(w1h), full(w1l), full(dup(b1)), full(dup(fr1)), full(w2h), full(w2l),
                  full(dup(b2)), full(dup(fr2)), dirspec(w3h), dirspec(w3l), dirspec(b3d), dirspec(ldd)],
        out_specs=[pl.BlockSpec((tm, oc), lambda i: (i, 0)), pl.BlockSpec((1, oc), lambda i: (0, 0))],
        out_shape=[jax.ShapeDtypeStruct((2 * length, oc), F32), jax.ShapeDtypeStruct((1, oc), F32)],
        compiler_params=_params("arbitrary"),
        name="hyena_filter",
    )(frow, w1h, w1l, dup(b1), dup(fr1), w2h, w2l, dup(b2), dup(fr2), w3h, w3l, b3d, ldd)


def _dft_consts(s):
    n1, n2 = _fft_dims(s)
    n = n1 * n2
    kh = n1 // 2 + 1
    kp = -(-kh // 4) * 4
    pad_rows = lambda a: np.concatenate([a, np.zeros((kp - kh, a.shape[1]))], axis=0)
    a1 = 2.0 * np.pi * np.outer(np.arange(kh), np.arange(n1)) / n1
    f1 = np.concatenate([pad_rows(np.cos(a1)), pad_rows(-np.sin(a1))], axis=0)
    a2 = 2.0 * np.pi * np.outer(np.arange(n2), np.arange(n2)) / n2
    c2, s2 = np.cos(a2), np.sin(a2)
    f2_fwd = np.block([[c2, s2], [-s2, c2]])
    f2_inv = np.block([[c2, -s2], [s2, c2]])
    at = 2.0 * np.pi * np.outer(np.arange(kp), np.arange(n2)) / n
    tw = np.stack([np.cos(at), np.sin(at)], axis=0)[..., None]
    m1 = np.arange(n1 // 2)
    a3 = 2.0 * np.pi * np.outer(m1, np.arange(kh)) / n1
    wgt = np.where((np.arange(kh) == 0) | (np.arange(kh) == n1 // 2), 1.0, 2.0) / n
    f3 = np.concatenate([pad_rows((np.cos(a3) * wgt).T).T, pad_rows((-np.sin(a3) * wgt).T).T],
                        axis=1)
    f = lambda a: jnp.asarray(a, F32)
    x3 = lambda a: _lhs_x3(f(a))
    return dict(n1=n1, n2=n2, kh=kh, kp=kp, f1=x3(f1), f1_half=x3(f1[:, :n1 // 2]), f2_fwd=x3(f2_fwd),
                f2_inv=x3(f2_inv), tw=f(tw), f3=x3(f3))


_N2_TILE = 8


def _dft1_kernel(x_ref, f_ref, s_ref, o_ref, xs_ref):
    f = f_ref[...]
    scale = s_ref[...]
    for m in range(_N2_TILE):
        xs_ref[...] = x_ref[:, m, :]
        o_ref[:, m, :] = _dot_x3(f, xs_ref[...] * scale)


def dft_stage1(x, f1, scale_row):
    g, r, n2, ch = x.shape
    m = f1.shape[0]
    return pl.pallas_call(
        _dft1_kernel,
        grid=(g, n2 // _N2_TILE),
        in_specs=[pl.BlockSpec((None, r, _N2_TILE, ch), lambda i, j: (i, 0, j, 0)),
                  pl.BlockSpec(f1.shape, lambda i, j: (0, 0)),
                  pl.BlockSpec((1, ch), lambda i, j: (0, 0))],
        out_specs=pl.BlockSpec((None, m, _N2_TILE, ch), lambda i, j: (i, 0, j, 0)),
        out_shape=jax.ShapeDtypeStruct((g, m, n2, ch), F32),
        scratch_shapes=[pltpu.VMEM((r, ch), F32)],
        compiler_params=_params("parallel", "parallel"),
        name="dft_stage1",
    )(x, f1, scale_row)


def _twiddle(re, im, tc, ts, conj):
    if conj:
        return re * tc - im * ts, im * tc + re * ts
    return re * tc + im * ts, im * tc - re * ts


_K1_STEP = 4


def _dft2_filter_kernel(a_ref, tw_ref, ff_ref, o_ref):
    n2 = a_ref.shape[2]
    for i in range(_K1_STEP):
        br, bi = _twiddle(a_ref[0, i], a_ref[1, i], tw_ref[0, i], tw_ref[1, i], False)
        zz = _dot_x3(ff_ref[...], jnp.concatenate([br, bi], axis=0))
        o_ref[0, i] = zz[:n2]
        o_ref[1, i] = zz[n2:]


def dft_stage2_filter(a, consts):
    _, kp, n2, ch = a.shape
    return pl.pallas_call(
        _dft2_filter_kernel,
        grid=(kp // _K1_STEP,),
        input_output_aliases={0: 0},
        in_specs=[pl.BlockSpec((2, _K1_STEP, n2, ch), lambda k: (0, k, 0, 0)),
                  pl.BlockSpec((2, _K1_STEP, n2, 1), lambda k: (0, k, 0, 0)),
                  pl.BlockSpec(consts["f2_fwd"].shape, lambda k: (0, 0))],
        out_specs=pl.BlockSpec((2, _K1_STEP, n2, ch), lambda k: (0, k, 0, 0)),
        out_shape=jax.ShapeDtypeStruct(a.shape, F32),
        compiler_params=_params("parallel"),
        name="dft_stage2_filter",
    )(a, consts["tw"], consts["f2_fwd"])


def _dft2_conv_kernel(a_ref, tw_ref, kf_ref, ff_ref, fi_ref, o_ref):
    n2 = a_ref.shape[2]
    for i in range(_K1_STEP):
        tc, ts = tw_ref[0, i], tw_ref[1, i]
        br, bi = _twiddle(a_ref[0, i], a_ref[1, i], tc, ts, False)
        zz = _dot_x3(ff_ref[...], jnp.concatenate([br, bi], axis=0))
        zr, zi = zz[:n2], zz[n2:]
        kr, ki = kf_ref[0, i], kf_ref[1, i]
        pr = zr * kr - zi * ki
        pi = zr * ki + zi * kr
        vv = _dot_x3(fi_ref[...], jnp.concatenate([pr, pi], axis=0))
        vr, vi = _twiddle(vv[:n2], vv[n2:], tc, ts, True)
        o_ref[0, i] = vr
        o_ref[1, i] = vi


def dft_stage2_conv(a, kf, order, consts):
    b, _, kp, n2, ch = a.shape
    blk = pl.BlockSpec((None, 2, _K1_STEP, n2, ch), lambda i, k: (i, 0, k, 0, 0))
    mat = pl.BlockSpec(consts["f2_fwd"].shape, lambda i, k: (0, 0))
    return pl.pallas_call(
        _dft2_conv_kernel,
        grid=(b, kp // _K1_STEP),
        input_output_aliases={0: 0},
        in_specs=[blk,
                  pl.BlockSpec((2, _K1_STEP, n2, 1), lambda i, k: (0, k, 0, 0)),
                  pl.BlockSpec((2, _K1_STEP, n2, ch), lambda i, k: (0, k, 0, order)),
                  mat, mat],
        out_specs=blk,
        out_shape=jax.ShapeDtypeStruct(a.shape, F32),
        compiler_params=_params("parallel", "parallel"),
        name="dft_stage2_conv",
    )(a, consts["tw"], kf, consts["f2_fwd"], consts["f2_inv"])


def _dft3_kernel(v_ref, f_ref, gate_ref, z_ref, bias_ref, o_ref, vs_ref, ys_ref):
    f = f_ref[...]
    for m in range(_N2_TILE):
        vs_ref[...] = v_ref[:, m, :]
        ys_ref[:, m, :] = _dot_x3(f, vs_ref[...])
    o_ref[...] = gate_ref[...] * (ys_ref[...] + z_ref[...] * bias_ref[...])


def dft_stage3_gate(v, f3, gate, z, bias_row):
    b, m, n2, ch = v.shape
    r = f3.shape[0]
    row = pl.BlockSpec((None, r, _N2_TILE, ch), lambda i, j: (i, 0, j, 0))
    return pl.pallas_call(
        _dft3_kernel,
        grid=(b, n2 // _N2_TILE),
        in_specs=[pl.BlockSpec((None, m, _N2_TILE, ch), lambda i, j: (i, 0, j, 0)),
                  pl.BlockSpec(f3.shape, lambda i, j: (0, 0)),
                  row, row, pl.BlockSpec((1, ch), lambda i, j: (0, 0))],
        out_specs=row,
        out_shape=jax.ShapeDtypeStruct((b, r, n2, ch), F32),
        scratch_shapes=[pltpu.VMEM((m, ch), F32), pltpu.VMEM((r, _N2_TILE, ch), F32)],
        compiler_params=_params("parallel", "parallel"),
        name="dft_stage3_gate",
    )(v, f3, gate, z, bias_row)


def hyena_mixer(x1, x2, z, hy_w1, hy_b1, hy_fr1, hy_w2, hy_b2, hy_fr2, hy_w3, hy_b3, hy_log_decay,
                hy_bias):
    b, s, ch = z.shape
    consts = _dft_consts(s)
    n1, n2 = consts["n1"], consts["n2"]
    oc = 2 * ch
    taps, norm = hyena_filter_taps(s, hy_w1, hy_b1, hy_fr1, hy_w2, hy_b2, hy_fr2, hy_w3, hy_b3,
                                   hy_log_decay)
    kp = consts["kp"]
    ka = dft_stage1(taps.reshape(1, n1, n2, oc), consts["f1"], 1.0 / norm)
    kf = dft_stage2_filter(ka.reshape(2, kp, n2, oc), consts)
    half = n1 // 2
    view = lambda t: t.reshape(b, half, n2, ch)
    ones = jnp.ones((1, ch), F32)
    f1_half = consts["f1_half"]
    cur = view(z)
    for order, gate in enumerate((x1, x2)):
        a = dft_stage1(cur, f1_half, ones)
        v = dft_stage2_conv(a.reshape(b, 2, kp, n2, ch), kf, order, consts)
        cur = dft_stage3_gate(v.reshape(b, 2 * kp, n2, ch), consts["f3"], view(gate), cur,
                              hy_bias[order].reshape(1, ch))
    return cur.reshape(b, s, ch)


def _outproj_kernel(x_ref, hf_ref, hb_ref, o_ref, yb_ref, yc_ref, yd_ref, og_ref, hm_ref, wout_ref,
                    pg_ref, g1_ref, fg_ref, sc2_ref, sh2_ref, rwt_ref,
                    xn_ref, h2_ref, afft_ref):
    hm = hm_ref[...]
    y_a = _sigmoid(o_ref[...]) * (hf_ref[...] + hb_ref[...])
    acc = None
    for idx, y in enumerate((y_a, yb_ref[...], yc_ref[...], yd_ref[...])):
        ms = _dot(jnp.concatenate(_split2(y * y), axis=1), hm)
        yn = y * lax.rsqrt(ms + NORM_EPS) * og_ref[:, idx * GROUP_W:(idx + 1) * GROUP_W]
        part = _dot(yn.astype(BF16), wout_ref[idx * GROUP_W:(idx + 1) * GROUP_W, :])
        acc = part if acc is None else acc + part
    d = acc.shape[-1]
    xn = x_ref[...] + g1_ref[...] * (_rms(acc, d) * pg_ref[...])
    xn_ref[...] = xn
    h2 = _rms(xn, d) * fg_ref[...] * (1.0 + sc2_ref[...]) + sh2_ref[...]
    h2_ref[...] = h2.T.astype(BF16)
    logits_t = _dot_nt_hi(rwt_ref[...], h2)
    mx = jnp.max(logits_t, axis=0, keepdims=True)
    ex = jnp.exp(logits_t - mx)
    afft_ref[...] = ex / jnp.sum(ex, axis=0, keepdims=True)


def out_proj(x, hf, hb, o, yb, yc, yd, out_g, w_out, post_g, g1, ffn_g, sc2, sh2, router_w):
    b, s, d = x.shape
    tm = min(512, s)
    e = router_w.shape[1]
    hm1 = np.kron(np.eye(GROUP_W // HEAD_DIM), np.ones((HEAD_DIM, HEAD_DIM))) / HEAD_DIM
    hm = jnp.asarray(np.concatenate([hm1, hm1], axis=0), BF16)
    row = lambda w: pl.BlockSpec((None, tm, w), lambda i, j: (i, j, 0))
    full = lambda a: pl.BlockSpec(a.shape, lambda i, j: (0,) * a.ndim)
    vec = lambda: pl.BlockSpec((None, 1, d), lambda i, j: (i, 0, 0))
    r1 = lambda a: a.reshape(1, -1)
    wob = w_out.astype(BF16)
    rwt = router_w.T
    return pl.pallas_call(
        _outproj_kernel,
        grid=(b, s // tm),
        in_specs=[row(d)] + [row(GROUP_W)] * 6 + [full(r1(out_g)), full(hm), full(wob),
                                                  full(r1(post_g)), vec(), full(r1(ffn_g)), vec(), vec(),
                                                  full(rwt)],
        out_specs=[row(d), pl.BlockSpec((None, d, tm), lambda i, j: (i, 0, j)),
                   pl.BlockSpec((None, e, tm), lambda i, j: (i, 0, j))],
        out_shape=[jax.ShapeDtypeStruct((b, s, d), F32), jax.ShapeDtypeStruct((b, d, s), BF16),
                   jax.ShapeDtypeStruct((b, e, s), F32)],
        compiler_params=_params("parallel", "parallel"),
        name="out_proj",
    )(x, hf, hb, o, yb, yc, yd, r1(out_g), hm, wob, r1(post_g), g1.reshape(b, 1, d), r1(ffn_g),
      sc2.reshape(b, 1, d), sh2.reshape(b, 1, d), rwt)


def _lane_cumsum(x):
    n = x.shape[-1]
    lane = lax.broadcasted_iota(jnp.int32, x.shape, x.ndim - 1)
    shift = 1
    while shift < n:
        x = x + jnp.where(lane >= shift, pltpu.roll(x, shift, x.ndim - 1), 0.0)
        shift *= 2
    return x


def _select_kernel(aff_ref, u_ref, ps_ref, cb_ref, *, cap):
    aff = aff_ref[...]
    capf = float(cap)

    def body(i, bits):
        cand = bits | (jnp.int32(1) << (30 - i))
        cnt = jnp.sum(jnp.where(aff >= pltpu.bitcast(cand, F32), 1.0, 0.0), axis=-1, keepdims=True)
        return jnp.where(cnt >= capf, cand, bits)

    bits = lax.fori_loop(0, 31, body, jnp.zeros((aff.shape[0], 1), jnp.int32))
    thr = pltpu.bitcast(bits, F32)
    gt = aff > thr
    eq = aff == thr
    n_gt = jnp.sum(jnp.where(gt, 1.0, 0.0), axis=-1, keepdims=True)
    eqf = jnp.where(eq, 1.0, 0.0)
    rank_eq = _lane_cumsum(eqf) - eqf
    sel = gt | (eq & (rank_eq < capf - n_gt))
    self_ = jnp.where(sel, 1.0, 0.0)
    pos = _lane_cumsum(self_) - self_
    ps_ref[...] = jnp.where(sel, pos, -1.0).astype(jnp.int32)
    cb_ref[...] = _dot(self_.astype(BF16), u_ref[...]).astype(jnp.int32)


def ec_select(aff_t, cap):
    b, e, s = aff_t.shape
    nt = s // TOKEN_TILE
    assert nt + 1 <= LANES
    tok = np.arange(s)[:, None]
    u = jnp.asarray(tok < (np.arange(LANES)[None, :] * TOKEN_TILE), BF16)
    return pl.pallas_call(
        functools.partial(_select_kernel, cap=cap),
        grid=(b,),
        in_specs=[pl.BlockSpec((None, e, s), lambda i: (i, 0, 0)),
                  pl.BlockSpec((s, LANES), lambda i: (0, 0))],
        out_specs=[pl.BlockSpec((None, e, s), lambda i: (i, 0, 0)),
                   pl.BlockSpec((None, e, LANES), lambda i: (i, 0, 0))],
        out_shape=[jax.ShapeDtypeStruct((b, e, s), jnp.int32),
                   jax.ShapeDtypeStruct((b, e, LANES), jnp.int32)],
        compiler_params=_params("parallel"),
        name="ec_select",
    )(aff_t, u)


def _gather_kernel(cb_ref, ps_ref, ht_ref, o_ref, *, n_tiles, n_blocks, block, chunk):
    bi, ei = pl.program_id(0), pl.program_id(1)
    base = (bi * N_EXPERTS + ei) * LANES
    d, s = ht_ref.shape
    slot = lax.broadcasted_iota(jnp.int32, (block, chunk), 0)
    tok = lax.broadcasted_iota(jnp.int32, (1, chunk), 1)

    def one_block(j, carry):
        first = j * block
        t_lo = lax.while_loop(lambda t: (t < n_tiles) & (cb_ref[base + t + 1] <= first),
                              lambda t: t + 1, carry[0])
        t_hi = lax.while_loop(lambda t: (t < n_tiles) & (cb_ref[base + t] < first + block),
                              lambda t: t + 1, carry[1])
        lo_tok = t_lo * TOKEN_TILE

        def part(i, acc):
            lower = lo_tok + i * chunk
            start = pl.multiple_of(jnp.minimum(lower, s - chunk), TOKEN_TILE)
            ps = jnp.where(tok + start >= lower, ps_ref[:, pl.ds(start, chunk)], -1)
            onehot = jnp.where(ps == slot + first, 1.0, 0.0).astype(BF16)
            return acc + _dot_nt(ht_ref[:, pl.ds(start, chunk)], onehot)

        n_parts = ((t_hi - t_lo) * TOKEN_TILE + chunk - 1) // chunk
        acc = lax.fori_loop(0, n_parts, part, jnp.zeros((d, block), F32))
        o_ref[pl.ds(pl.multiple_of(first, block), block), :] = acc.T.astype(o_ref.dtype)
        return t_lo, t_hi

    lax.fori_loop(0, n_blocks, one_block, (jnp.int32(0), jnp.int32(0)))


def ec_gather(ht, ps, cb, cap):
    b, d, s = ht.shape
    e = ps.shape[1]
    nt = s // TOKEN_TILE
    block = min(2 * SLOT_BLOCK, cap)
    chunk = min(block * s // cap + 2 * TOKEN_TILE, s)
    grid_spec = pltpu.PrefetchScalarGridSpec(
        num_scalar_prefetch=1,
        grid=(b, e),
        in_specs=[pl.BlockSpec((None, None, 1, s), lambda i, j, cb: (i, j, 0, 0)),
                  pl.BlockSpec((None, d, s), lambda i, j, cb: (i, 0, 0))],
        out_specs=pl.BlockSpec((None, None, cap, d), lambda i, j, cb: (i, j, 0, 0)),
    )
    return pl.pallas_call(
        functools.partial(_gather_kernel, n_tiles=nt, n_blocks=cap // block, block=block, chunk=chunk),
        grid_spec=grid_spec,
        out_shape=jax.ShapeDtypeStruct((b, e, cap, d), BF16),
        compiler_params=_params("parallel", "arbitrary"),
        name="ec_gather",
    )(cb.reshape(-1), ps.reshape(b, e, 1, s), ht)


def _ffn_kernel(x_ref, wg_ref, wu_ref, wd_ref, o_ref, acc_ref, *, n_f):
    f = pl.program_id(1)

    @pl.when(f == 0)
    def _():
        acc_ref[...] = jnp.zeros_like(acc_ref)

    bsz, cap, d = x_ref.shape
    x = x_ref[...].reshape(bsz * cap, d)
    a = _dot(x, wg_ref[...].astype(BF16))
    up = _dot(x, wu_ref[...].astype(BF16))
    act = (a * _sigmoid(a) * up).astype(BF16)
    acc_ref[...] += _dot(act, wd_ref[...].astype(BF16))

    @pl.when(f == n_f - 1)
    def _():
        o_ref[...] = acc_ref[...].reshape(bsz, cap, d).astype(o_ref.dtype)


def expert_ffn(xe, w_gate, w_up, w_down, layer):
    b, e, cap, d = xe.shape
    ff = w_gate.shape[-1]
    tf = min(512, ff)
    n_f = ff // tf
    return pl.pallas_call(
        functools.partial(_ffn_kernel, n_f=n_f),
        grid=(e, n_f),
        in_specs=[pl.BlockSpec((b, None, cap, d), lambda j, f: (0, j, 0, 0)),
                  pl.BlockSpec((None, None, d, tf), lambda j, f: (layer, j, 0, f)),
                  pl.BlockSpec((None, None, d, tf), lambda j, f: (layer, j, 0, f)),
                  pl.BlockSpec((None, None, tf, d), lambda j, f: (layer, j, f, 0))],
        out_specs=pl.BlockSpec((b, None, cap, d), lambda j, f: (0, j, 0, 0)),
        out_shape=jax.ShapeDtypeStruct((b, e, cap, d), BF16),
        scratch_shapes=[pltpu.VMEM((b * cap, d), F32)],
        compiler_params=_params("parallel", "arbitrary"),
        name="expert_ffn",
    )(xe, w_gate, w_up, w_down)


def _scatter_kernel(cb_ref, ps_ref, aff_ref, ye_ref, x_ref, pg_ref, g2_ref, o_ref, acc_ref,
                    *, n_sub, window):
    bi, ti, ei = pl.program_id(0), pl.program_id(1), pl.program_id(2)

    @pl.when(ei == 0)
    def _():
        acc_ref[...] = jnp.zeros_like(acc_ref)

    tt = acc_ref.shape[0]
    base = (bi * N_EXPERTS + ei) * LANES + ti * n_sub
    lo = cb_ref[base]
    hi = cb_ref[base + n_sub]

    @pl.when(hi > lo)
    def _():
        ps = ps_ref[...]
        gate = aff_ref[...]
        slot = lax.broadcasted_iota(jnp.int32, (window, tt), 0)
        cap = ye_ref.shape[0]
        first = (lo // BF16_ROWS) * BF16_ROWS

        def body(j, carry):
            lower = first + j * window
            start = pl.multiple_of(jnp.minimum(lower, cap - window), BF16_ROWS)
            hit = jnp.where(ps >= lower, ps, -1) == slot + start
            onehot = jnp.where(hit, 1.0, 0.0).astype(BF16)
            gate_slot = jnp.sum(jnp.where(hit, gate, 0.0), axis=1, keepdims=True)
            ye = (ye_ref[pl.ds(start, window), :].astype(F32) * gate_slot).astype(BF16)
            acc_ref[...] += lax.dot_general(onehot, ye, (((0,), (0,)), ((), ())),
                                            preferred_element_type=F32)
            return carry

        lax.fori_loop(0, (hi - first + window - 1) // window, body, 0)

    @pl.when(ei == N_EXPERTS - 1)
    def _():
        y = acc_ref[...]
        o_ref[...] = x_ref[...] + g2_ref[...] * (_rms(y, y.shape[-1]) * pg_ref[...])


def ec_scatter(ye, ps, aff_t, cb, x, post_g, g2):
    b, e, cap, d = ye.shape
    s = x.shape[1]
    n_sub = min(4, s // TOKEN_TILE)
    tt = n_sub * TOKEN_TILE
    grid_spec = pltpu.PrefetchScalarGridSpec(
        num_scalar_prefetch=1,
        grid=(b, s // tt, e),
        in_specs=[pl.BlockSpec((None, None, 1, tt), lambda i, t, j, cb: (i, j, 0, t)),
                  pl.BlockSpec((None, None, 1, tt), lambda i, t, j, cb: (i, j, 0, t)),
                  pl.BlockSpec((None, None, cap, d), lambda i, t, j, cb: (i, j, 0, 0)),
                  pl.BlockSpec((None, tt, d), lambda i, t, j, cb: (i, t, 0)),
                  pl.BlockSpec((1, d), lambda i, t, j, cb: (0, 0)),
                  pl.BlockSpec((None, 1, d), lambda i, t, j, cb: (i, 0, 0))],
        out_specs=pl.BlockSpec((None, tt, d), lambda i, t, j, cb: (i, t, 0)),
        scratch_shapes=[pltpu.VMEM((tt, d), F32)],
    )
    return pl.pallas_call(
        functools.partial(_scatter_kernel, n_sub=n_sub, window=min(2 * SLOT_BLOCK, cap)),
        grid_spec=grid_spec,
        out_shape=jax.ShapeDtypeStruct((b, s, d), F32),
        compiler_params=_params("parallel", "parallel", "arbitrary"),
        name="ec_scatter",
    )(cb.reshape(-1), ps.reshape(b, e, 1, s), aff_t.reshape(b, e, 1, s), ye, x, post_g.reshape(1, d),
      g2.reshape(b, 1, d))


def kernel(x, c, positions, ada_w, ada_b, mix_pre_g, mix_post_g, ffn_pre_g, ffn_post_g, w_in, ml_gate_b, mla_q_norm, mla_kv_norm, mla_w_uq, mla_w_uk, mla_w_uv, hy_conv_w, hy_conv_b, hy_w1, hy_b1, hy_fr1, hy_w2, hy_b2, hy_fr2, hy_w3, hy_b3, hy_log_decay, hy_bias, sc_conv_w, mix_out_g, w_out, router_w, exp_w_gate, exp_w_up, exp_w_down):
    depth = ada_w.shape[0]
    b, s, d = x.shape
    cap = EC_CAPACITY * s // N_EXPERTS
    mod = ada_mod(c, ada_w, ada_b)
    tables = rope_tables(positions)
    for l in range(depth):
        sh1, sc1, g1, sh2, sc2, g2 = (mod[l, :, i * d:(i + 1) * d] for i in range(6))
        u = in_proj(x, mix_pre_g[l], sc1, sh1, w_in[l])
        hf, hb = mlstm(u["q"], u["v"], u["kT"], u["gT"], u["g"], ml_gate_b[l])
        qa, ka, va = mla_proj(u["cq"], u["ckv"], u["kr"], tables, mla_q_norm[l], mla_kv_norm[l],
                              mla_w_uq[l], mla_w_uk[l], mla_w_uv[l])
        y_b = flash_attention(qa, ka, va)
        x1, x2, z, y_d = conv_mixers(u["hy"], u["sc"], hy_conv_w[l], hy_conv_b[l], sc_conv_w[l])
        y_c = hyena_mixer(x1, x2, z, hy_w1[l], hy_b1[l], hy_fr1[l], hy_w2[l], hy_b2[l], hy_fr2[l],
                          hy_w3[l], hy_b3[l], hy_log_decay[l], hy_bias[l])
        xn, h2, aff_t = out_proj(x, hf, hb, u["o"], y_b, y_c, y_d, mix_out_g[l], w_out[l],
                                      mix_post_g[l], g1, ffn_pre_g[l], sc2, sh2, router_w[l])
        ps, cb = ec_select(aff_t, cap)
        xe = ec_gather(h2, ps, cb, cap)
        ye = expert_ffn(xe, exp_w_gate, exp_w_up, exp_w_down, l)
        x = ec_scatter(ye, ps, aff_t, cb, xn, ffn_post_g[l], g2)
    return x
```
